```python
import jax, jax.numpy as jnp
from jax import lax
import numpy as np

D_MODEL = 1024
BATCH = 8
SEQ = 4096
DEPTH = 1

NSA_HEADS = 8
NSA_KV_HEADS = 2
NSA_HEAD_DIM = 64
NSA_GROUP = NSA_HEADS // NSA_KV_HEADS
CMP_BLOCK = 32
CMP_STRIDE = 16
CMP_HIDDEN = 256
SEL_BLOCK = 64
SEL_TOPN = 16
WINDOW = 512
Q_BLOCK = 128
MLSTM_HEADS = 4
MLSTM_HEAD_DIM = 128
MLSTM_CHUNK = 64
CONV_WIDTH = 4
NSA_WIDTH = NSA_HEADS * NSA_HEAD_DIM
MLSTM_WIDTH = MLSTM_HEADS * MLSTM_HEAD_DIM
MIX_WIDTH = NSA_WIDTH + MLSTM_WIDTH
KV_WIDTH = NSA_KV_HEADS * NSA_HEAD_DIM
IN_SPLITS = (NSA_WIDTH,) + (KV_WIDTH,) * 6 + (3 * NSA_HEADS,) + (MLSTM_WIDTH,) * 3 + (MLSTM_HEADS,) * 2
IN_WIDTH = sum(IN_SPLITS)
D_FF = -(-8 * D_MODEL // (3 * 256)) * 256
NORM_EPS = 1e-6
NEG = -1e30
FORCE = 1e9

kernel_name = 'hymba_nsa_mlstm_swiglu_alibi'


def rms_norm(x, g):
    xf = x.astype(jnp.float32)
    y = xf * lax.rsqrt(jnp.mean(xf * xf, axis=-1, keepdims=True) + NORM_EPS)
    return (y * g.astype(jnp.float32)).astype(x.dtype)


def alibi_slopes(n):
    return jnp.exp2(-8.0 * (jnp.arange(n, dtype=jnp.float32) + 1.0) / n)


def compress_blocks(a, pos, w1, w2):
    B, G, T, dh = a.shape
    r_seg = CMP_BLOCK // CMP_STRIDE
    nseg = T // CMP_STRIDE
    nc = nseg - r_seg + 1
    ab = a.reshape(B, G, nseg, CMP_STRIDE, dh)
    blocks = jnp.concatenate([ab[:, :, r:r + nc] for r in range(r_seg)], axis=3)
    flat = (blocks + pos).reshape(B, G, nc, CMP_BLOCK * dh)
    return jax.nn.gelu(flat @ w1) @ w2


def nsa_mixer(q, k_cmp, v_cmp, k_slc, v_slc, k_win, v_win, gate_logits,
              q_g, kc_g, ks_g, kw_g, pos, wk1, wk2, wv1, wv2):
    B, T, _ = q.shape
    G, R, dh = NSA_KV_HEADS, NSA_GROUP, NSA_HEAD_DIM
    f32 = jnp.float32
    scale = dh ** -0.5
    slopes = alibi_slopes(NSA_HEADS).reshape(G, R)
    t_pos = jnp.arange(T)
    qh = rms_norm(q.reshape(B, T, G, R, dh), q_g).transpose(0, 2, 3, 1, 4)

    def kv_heads(a):
        return a.reshape(B, T, G, dh).transpose(0, 2, 1, 3)

    kc = rms_norm(compress_blocks(kv_heads(k_cmp), pos, wk1, wk2), kc_g)
    vc = compress_blocks(kv_heads(v_cmp), pos, wv1, wv2)
    nc = kc.shape[2]
    blk_end = jnp.arange(nc) * CMP_STRIDE + CMP_BLOCK - 1
    dist_c = t_pos[:, None] - blk_end[None, :]
    valid_c = dist_c >= 0
    s_c = (jnp.einsum('bgrtd,bgcd->bgrtc', qh, kc).astype(f32) * scale
           - slopes[:, :, None, None] * dist_c.astype(f32))
    s_c = jnp.where(valid_c, s_c, NEG)
    p_c = jax.nn.softmax(s_c, axis=-1) * valid_c
    o_cmp = jnp.einsum('bgrtc,bgcd->bgrtd', p_c.astype(vc.dtype), vc)

    nsel = T // SEL_BLOCK
    topn = min(SEL_TOPN, nsel)
    cs = jnp.arange(nc) * CMP_STRIDE
    ss = jnp.arange(nsel) * SEL_BLOCK
    overlap = ((cs[:, None] < ss[None, :] + SEL_BLOCK) & (cs[:, None] + CMP_BLOCK > ss[None, :])).astype(f32)
    imp = jnp.einsum('bgtc,cn->bgtn', p_c.sum(axis=2), overlap)
    blk = jnp.arange(nsel)
    forced = (blk[None, :] == (t_pos // SEL_BLOCK)[:, None]) | (blk[None, :] == 0)
    future = ss[None, :] > t_pos[:, None]
    imp = jnp.where(forced, FORCE, jnp.where(future, -FORCE, imp))
    _, sel_idx = lax.top_k(imp, topn)

    ks_blocks = rms_norm(kv_heads(k_slc), ks_g).reshape(B, G, nsel, SEL_BLOCK, dh)
    vs_blocks = kv_heads(v_slc).reshape(B, G, nsel, SEL_BLOCK, dh)
    pad = ((0, 0), (0, 0), (WINDOW, 0), (0, 0))
    kw = jnp.pad(rms_norm(kv_heads(k_win), kw_g), pad)
    vw = jnp.pad(kv_heads(v_win), pad)
    bi = jnp.arange(B)[:, None, None, None]
    gi = jnp.arange(G)[None, :, None, None]
    sb_off = jnp.arange(SEL_BLOCK)
    w_off = jnp.arange(Q_BLOCK + WINDOW) - WINDOW
    slopes_sel = slopes[None, :, :, None, None, None]

    def block_fn(qb):
        t0 = qb * Q_BLOCK
        qq = lax.dynamic_slice_in_dim(qh, t0, Q_BLOCK, axis=3)
        tq = t0 + jnp.arange(Q_BLOCK)
        idx = lax.dynamic_slice_in_dim(sel_idx, t0, Q_BLOCK, axis=2)
        kg = ks_blocks[bi, gi, idx]
        vg = vs_blocks[bi, gi, idx]
        kpos = idx[..., None] * SEL_BLOCK + sb_off
        dist = (tq[None, None, :, None, None] - kpos)[:, :, None]
        s = (jnp.einsum('bgrqd,bgqnkd->bgrqnk', qq, kg).astype(f32) * scale
             - slopes_sel * dist.astype(f32))
        s = jnp.where(dist >= 0, s, NEG).reshape(B, G, R, Q_BLOCK, topn * SEL_BLOCK)
        p = jax.nn.softmax(s, axis=-1).astype(vg.dtype)
        o_s = jnp.einsum('bgrqk,bgqkd->bgrqd', p, vg.reshape(B, G, Q_BLOCK, topn * SEL_BLOCK, dh))
        kwb = lax.dynamic_slice_in_dim(kw, t0, Q_BLOCK + WINDOW, axis=2)
        vwb = lax.dynamic_slice_in_dim(vw, t0, Q_BLOCK + WINDOW, axis=2)
        wpos = t0 + w_off
        dist_w = tq[:, None] - wpos[None, :]
        valid_w = (dist_w >= 0) & (dist_w < WINDOW) & (wpos[None, :] >= 0)
        s_w = (jnp.einsum('bgrqd,bgkd->bgrqk', qq, kwb).astype(f32) * scale
               - slopes[:, :, None, None] * dist_w.astype(f32))
        s_w = jnp.where(valid_w, s_w, NEG)
        p_w = jax.nn.softmax(s_w, axis=-1).astype(vwb.dtype)
        o_w = jnp.einsum('bgrqk,bgkd->bgrqd', p_w, vwb)
        return o_s, o_w

    o_slc, o_win = lax.map(block_fn, jnp.arange(T // Q_BLOCK))

    def unblock(o):
        return o.transpose(1, 2, 3, 0, 4, 5).reshape(B, G, R, T, dh)

    gates = jax.nn.sigmoid(gate_logits.astype(f32).reshape(B, T, G, R, 3)).transpose(0, 2, 3, 1, 4)
    o = (gates[..., 0:1] * o_cmp.astype(f32) + gates[..., 1:2] * unblock(o_slc).astype(f32)
         + gates[..., 2:3] * unblock(o_win).astype(f32))
    return o.transpose(0, 3, 1, 2, 4).reshape(B, T, NSA_WIDTH).astype(q.dtype)


def causal_dwconv(x, w, b):
    C = x.shape[-1]
    y = lax.conv_general_dilated(x, w[:, None, :], window_strides=(1,), padding=[(CONV_WIDTH - 1, 0)],
                                 dimension_numbers=('NWC', 'WIO', 'NWC'), feature_group_count=C)
    return y + b


def mlstm_chunkwise(q, k, v, log_i, log_f):
    f32 = jnp.float32
    B, NH, T, dk = q.shape
    dv = v.shape[-1]
    L = MLSTM_CHUNK
    nch = T // L
    q = q.astype(f32).reshape(B, NH, nch, L, dk)
    k = k.astype(f32).reshape(B, NH, nch, L, dk)
    v = v.astype(f32).reshape(B, NH, nch, L, dv)
    log_i = log_i.reshape(B, NH, nch, L)
    log_f = log_f.reshape(B, NH, nch, L)
    b = jnp.cumsum(log_f, axis=-1)
    g = b[..., -1]
    w_end = g[..., None] - b + log_i

    def step(carry, xs):
        C, n, m = carry
        g_c, w_c, k_c, v_c = xs
        m_new = jnp.maximum(g_c + m, w_c.max(axis=-1))
        decay = jnp.exp(g_c + m - m_new)
        w = jnp.exp(w_c - m_new[..., None])
        C_new = decay[..., None, None] * C + jnp.einsum('bhl,bhlv,bhlk->bhvk', w, v_c, k_c)
        n_new = decay[..., None] * n + jnp.einsum('bhl,bhlk->bhk', w, k_c)
        return (C_new, n_new, m_new), (C, n, m)

    init = (jnp.zeros((B, NH, dv, dk), f32), jnp.zeros((B, NH, dk), f32), jnp.zeros((B, NH), f32))
    xs = (jnp.moveaxis(g, 2, 0), jnp.moveaxis(w_end, 2, 0), jnp.moveaxis(k, 2, 0), jnp.moveaxis(v, 2, 0))
    _, (C_prev, n_prev, m_prev) = lax.scan(step, init, xs)
    C_prev = jnp.moveaxis(C_prev, 0, 2)
    n_prev = jnp.moveaxis(n_prev, 0, 2)
    m_prev = jnp.moveaxis(m_prev, 0, 2)

    causal = jnp.tril(jnp.ones((L, L), dtype=bool))
    D = jnp.where(causal, b[..., :, None] - b[..., None, :] + log_i[..., None, :], NEG)
    m_inter = b + m_prev[..., None]
    m_out = jnp.maximum(m_inter, D.max(axis=-1))
    P = jnp.einsum('bhcld,bhcsd->bhcls', q, k) * jnp.exp(D - m_out[..., None])
    inter = jnp.exp(m_inter - m_out)
    num = (inter[..., None] * jnp.einsum('bhcld,bhcvd->bhclv', q, C_prev)
           + jnp.einsum('bhcls,bhcsv->bhclv', P, v))
    den = inter * jnp.einsum('bhcld,bhcd->bhcl', q, n_prev) + P.sum(axis=-1)
    h = num / jnp.maximum(jnp.abs(den), jnp.exp(-m_out))[..., None]
    return h.reshape(B, NH, T, dv)


def mlstm_mixer(u, v, o_pre, i_pre, f_pre, conv_w, conv_b, w_q, w_k, b_i, b_f, norm_g, skip):
    B, T, _ = u.shape
    NH, dm = MLSTM_HEADS, MLSTM_HEAD_DIM
    f32 = jnp.float32
    u_c = jax.nn.silu(causal_dwconv(u, conv_w, conv_b))
    uh = u_c.reshape(B, T, NH, dm)
    q = jnp.einsum('bthd,hde->bhte', uh, w_q)
    k = jnp.einsum('bthd,hde->bhte', uh, w_k) * (dm ** -0.5)
    vh = v.reshape(B, T, NH, dm).transpose(0, 2, 1, 3)
    log_i = (i_pre + b_i).astype(f32).transpose(0, 2, 1)
    log_f = jax.nn.log_sigmoid((f_pre + b_f).astype(f32)).transpose(0, 2, 1)
    h = mlstm_chunkwise(q, k, vh, log_i, log_f)
    h = h * jax.nn.sigmoid(o_pre.astype(f32).reshape(B, T, NH, dm).transpose(0, 2, 1, 3))
    h = rms_norm(h, norm_g[:, None, :])
    return h.transpose(0, 2, 1, 3).reshape(B, T, MLSTM_WIDTH).astype(u.dtype) + skip * u_c


def setup_inputs(seed: int = 0) -> dict:
    key = jax.random.key(seed)
    ks = jax.random.split(key, 25)
    L = DEPTH
    dh, dm, nh = NSA_HEAD_DIM, MLSTM_HEAD_DIM, MLSTM_HEADS

    def nrm(k, shape, scale):
        return jax.random.normal(k, shape, jnp.float32) * scale

    def gain(k, shape):
        return 1.0 + nrm(k, shape, 0.02)

    return {
        'x': nrm(ks[0], (BATCH, SEQ, D_MODEL), 1.0),
        'norm1_g': gain(ks[1], (L, D_MODEL)),
        'w_in': nrm(ks[2], (L, D_MODEL, IN_WIDTH), D_MODEL ** -0.5),
        'q_norm_g': gain(ks[3], (L, dh)),
        'kc_norm_g': gain(ks[4], (L, dh)),
        'ks_norm_g': gain(ks[5], (L, dh)),
        'kw_norm_g': gain(ks[6], (L, dh)),
        'cmp_pos': nrm(ks[7], (L, CMP_BLOCK, dh), 0.2),
        'w_ck1': nrm(ks[8], (L, CMP_BLOCK * dh, CMP_HIDDEN), (CMP_BLOCK * dh) ** -0.5),
        'w_ck2': nrm(ks[9], (L, CMP_HIDDEN, dh), CMP_HIDDEN ** -0.5),
        'w_cv1': nrm(ks[10], (L, CMP_BLOCK * dh, CMP_HIDDEN), (CMP_BLOCK * dh) ** -0.5),
        'w_cv2': nrm(ks[11], (L, CMP_HIDDEN, dh), CMP_HIDDEN ** -0.5),
        'conv_w': nrm(ks[12], (L, CONV_WIDTH, MLSTM_WIDTH), CONV_WIDTH ** -0.5),
        'conv_b': nrm(ks[13], (L, MLSTM_WIDTH), 0.01),
        'w_mq': nrm(ks[14], (L, nh, dm, dm), dm ** -0.5),
        'w_mk': nrm(ks[15], (L, nh, dm, dm), dm ** -0.5),
        'b_i': nrm(ks[16], (L, nh), 0.1),
        'b_f': jnp.linspace(3.0, 6.0, nh, dtype=jnp.float32)[None, :] + nrm(ks[17], (L, nh), 0.1),
        'mlstm_norm_g': gain(ks[18], (L, nh, dm)),
        'mlstm_skip': gain(ks[19], (L, MLSTM_WIDTH)),
        'w_out': nrm(ks[20], (L, MIX_WIDTH, D_MODEL), MIX_WIDTH ** -0.5),
        'norm2_g': gain(ks[21], (L, D_MODEL)),
        'w_gate': nrm(ks[22], (L, D_MODEL, D_FF), D_MODEL ** -0.5),
        'w_up': nrm(ks[23], (L, D_MODEL, D_FF), D_MODEL ** -0.5),
        'w_down': nrm(ks[24], (L, D_FF, D_MODEL), D_FF ** -0.5),
    }


def reference(x, norm1_g, w_in, q_norm_g, kc_norm_g, ks_norm_g, kw_norm_g, cmp_pos,
              w_ck1, w_ck2, w_cv1, w_cv2, conv_w, conv_b, w_mq, w_mk, b_i, b_f,
              mlstm_norm_g, mlstm_skip, w_out, norm2_g, w_gate, w_up, w_down):
    offsets = [int(o) for o in np.cumsum(IN_SPLITS)[:-1]]
    for l in range(DEPTH):
        h = rms_norm(x, norm1_g[l])
        proj = h @ w_in[l]
        (q, k_cmp, v_cmp, k_slc, v_slc, k_win, v_win, gate_logits,
         u, v_m, o_pre, i_pre, f_pre) = jnp.split(proj, offsets, axis=-1)
        y_nsa = nsa_mixer(q, k_cmp, v_cmp, k_slc, v_slc, k_win, v_win, gate_logits,
                          q_norm_g[l], kc_norm_g[l], ks_norm_g[l], kw_norm_g[l], cmp_pos[l],
                          w_ck1[l], w_ck2[l], w_cv1[l], w_cv2[l])
        y_mem = mlstm_mixer(u, v_m, o_pre, i_pre, f_pre, conv_w[l], conv_b[l], w_mq[l], w_mk[l],
                            b_i[l], b_f[l], mlstm_norm_g[l], mlstm_skip[l])
        x = x + jnp.concatenate([y_nsa, y_mem], axis=-1) @ w_out[l]
        h2 = rms_norm(x, norm2_g[l])
        x = x + (jax.nn.silu(h2 @ w_gate[l]) * (h2 @ w_up[l])) @ w_down[l]
    return x
```

```python
import functools
import math

import numpy as np
import jax
import jax.numpy as jnp
from jax import lax
from jax.experimental import pallas as pl
from jax.experimental.pallas import tpu as pltpu

f32 = jnp.float32
bf16 = jnp.bfloat16

D_MODEL = 1024
NSA_HEADS = 8
NSA_KV_HEADS = 2
NSA_HEAD_DIM = 64
NSA_GROUP = NSA_HEADS // NSA_KV_HEADS
CMP_BLOCK = 32
CMP_STRIDE = 16
CMP_HIDDEN = 256
SEL_BLOCK = 64
SEL_TOPN = 16
WINDOW = 512
MLSTM_HEADS = 4
MLSTM_HEAD_DIM = 128
CONV_WIDTH = 4
NSA_WIDTH = NSA_HEADS * NSA_HEAD_DIM
MLSTM_WIDTH = MLSTM_HEADS * MLSTM_HEAD_DIM
KV_WIDTH = NSA_KV_HEADS * NSA_HEAD_DIM
D_FF = -(-8 * D_MODEL // (3 * 256)) * 256
NORM_EPS = 1e-6
NEG = -1e30
FORCE = 1e9

LANES = 128
HALF = 64
VMEM_LIMIT = 56 * 1024 * 1024

TM_PROJ = 512
TM_PREP = 512
TQ = 128
TK = 256
TW = 128
MLSTM_L = 128
TM_FFN = 512
FF_CHUNK = 256

HI = lax.Precision.HIGHEST
NT = (((1,), (1,)), ((), ()))


def _cparams(sem):
    return pltpu.CompilerParams(dimension_semantics=sem, vmem_limit_bytes=VMEM_LIMIT)


PROJ_SPLITS = (NSA_WIDTH, 6 * KV_WIDTH, MLSTM_WIDTH, MLSTM_WIDTH, MLSTM_WIDTH, LANES)


def _inproj_kernel(x_ref, g_ref, w_ref, *out_refs):
    x = x_ref[...]
    h = x * lax.rsqrt(jnp.mean(x * x, axis=-1, keepdims=True) + NORM_EPS) * g_ref[...]
    hb = h.astype(bf16)
    off = 0
    for o_ref, width in zip(out_refs, PROJ_SPLITS):
        o_ref[...] = jnp.dot(hb, w_ref[:, off:off + width], preferred_element_type=f32)
        off += width


def _inproj(x2, g1, w_perm):
    n = x2.shape[0]
    wtot = sum(PROJ_SPLITS)
    return pl.pallas_call(
        _inproj_kernel,
        grid=(n // TM_PROJ,),
        in_specs=[
            pl.BlockSpec((TM_PROJ, D_MODEL), lambda i: (i, 0)),
            pl.BlockSpec((1, D_MODEL), lambda i: (0, 0)),
            pl.BlockSpec((D_MODEL, wtot), lambda i: (0, 0)),
        ],
        out_specs=[pl.BlockSpec((TM_PROJ, w), lambda i: (i, 0)) for w in PROJ_SPLITS],
        out_shape=[jax.ShapeDtypeStruct((n, w), f32) for w in PROJ_SPLITS],
        compiler_params=_cparams(("parallel",)),
        name="inproj",
    )(x2, g1, w_perm)


def _pair_norm(x, gain, lo):
    sq = x * x
    s_lo = jnp.sum(jnp.where(lo, sq, 0.0), axis=-1, keepdims=True)
    s_hi = jnp.sum(jnp.where(lo, 0.0, sq), axis=-1, keepdims=True)
    inv = jnp.where(lo, lax.rsqrt(s_lo / HALF + NORM_EPS), lax.rsqrt(s_hi / HALF + NORM_EPS))
    return x * inv * gain


def _kvprep_kernel(kv_ref, ksg_ref, kwg_ref, kse_ref, kso_ref, vse_ref, vso_ref, kwd_ref, vwe_ref, vwo_ref):
    i = pl.program_id(1)
    tm = kv_ref.shape[1]
    lane = lax.broadcasted_iota(jnp.int32, (tm, LANES), 1)
    lo = lane < HALF
    row = i * tm + lax.broadcasted_iota(jnp.int32, (tm, LANES), 0)
    code = jnp.where(lax.shift_right_logical(row, 6) == (lane & (HALF - 1)), 1.0, 0.0)

    ks = _pair_norm(kv_ref[0, :, 2 * KV_WIDTH:3 * KV_WIDTH], ksg_ref[...], lo)
    ksr = pltpu.roll(ks, HALF, 1)
    kse_ref[0, 0] = jnp.where(lo, ks, code).astype(bf16)
    kso_ref[0, 0] = jnp.where(lo, code, ksr).astype(bf16)
    kse_ref[0, 1] = jnp.where(lo, ksr, code).astype(bf16)
    kso_ref[0, 1] = jnp.where(lo, code, ks).astype(bf16)

    def put_values(v, e_ref, o_ref):
        vr = pltpu.roll(v, HALF, 1)
        e_ref[0, 0] = jnp.where(lo, v, 1.0).astype(bf16)
        o_ref[0, 0] = jnp.where(lo, 1.0, vr).astype(bf16)
        e_ref[0, 1] = jnp.where(lo, vr, 1.0).astype(bf16)
        o_ref[0, 1] = jnp.where(lo, 1.0, v).astype(bf16)

    put_values(kv_ref[0, :, 3 * KV_WIDTH:4 * KV_WIDTH], vse_ref, vso_ref)

    kw = _pair_norm(kv_ref[0, :, 4 * KV_WIDTH:5 * KV_WIDTH], kwg_ref[...], lo)
    kwr = pltpu.roll(kw, HALF, 1)
    kwd_ref[0, 0] = jnp.where(lo, kw, kwr).astype(bf16)
    kwd_ref[0, 1] = jnp.where(lo, kwr, kw).astype(bf16)

    put_values(kv_ref[0, :, 5 * KV_WIDTH:6 * KV_WIDTH], vwe_ref, vwo_ref)


def _kvprep(kv3, ksg2, kwg2):
    b, t, _ = kv3.shape
    tm = min(TM_PREP, t)
    out_spec = pl.BlockSpec((1, NSA_KV_HEADS, tm, LANES), lambda bi, i: (bi, 0, i, 0))
    out_shape = jax.ShapeDtypeStruct((b, NSA_KV_HEADS, t, LANES), bf16)
    return pl.pallas_call(
        _kvprep_kernel,
        grid=(b, t // tm),
        in_specs=[
            pl.BlockSpec((1, tm, 6 * KV_WIDTH), lambda bi, i: (bi, i, 0)),
            pl.BlockSpec((1, LANES), lambda bi, i: (0, 0)),
            pl.BlockSpec((1, LANES), lambda bi, i: (0, 0)),
        ],
        out_specs=[out_spec] * 7,
        out_shape=[out_shape] * 7,
        compiler_params=_cparams(("parallel", "parallel")),
        name="kvprep",
    )(kv3, ksg2, kwg2)


def _gelu_tanh(x):
    return 0.5 * x * (1.0 + jnp.tanh(math.sqrt(2.0 / math.pi) * (x + 0.044715 * (x * x * x))))


def _compress_kernel(ak_ref, av_ref, wk1_ref, wk2_ref, wv1_ref, wv2_ref, pos_ref, kcg_ref,
                     kcd_ref, vce_ref, vco_ref):
    nseg = ak_ref.shape[2]
    half_in = CMP_STRIDE * NSA_HEAD_DIM

    def branch(a, w1_ref, w2_ref):
        top = jnp.dot(a, w1_ref[0:half_in, :], precision=HI, preferred_element_type=f32)
        bot = jnp.dot(a, w1_ref[half_in:2 * half_in, :], precision=HI, preferred_element_type=f32)
        pos_term = jnp.dot(pos_ref[...], w1_ref[...], precision=HI, preferred_element_type=f32)[0:1]
        hid = top + pltpu.roll(bot, nseg - 1, 0) + pos_term
        return jnp.dot(_gelu_tanh(hid), w2_ref[...], precision=HI, preferred_element_type=f32)

    kc = branch(ak_ref[0, 0], wk1_ref, wk2_ref)
    kc = kc * lax.rsqrt(jnp.mean(kc * kc, axis=-1, keepdims=True) + NORM_EPS) * kcg_ref[...]
    kcd_ref[0, 0] = kc

    vc = branch(av_ref[0, 0], wv1_ref, wv2_ref)
    lane = lax.broadcasted_iota(jnp.int32, vc.shape, 1)
    vce_ref[0, 0] = jnp.where(lane < HALF, vc, 0.0).astype(bf16)
    vco_ref[0, 0] = jnp.where(lane < HALF, 0.0, vc).astype(bf16)


def _compress(ak, av, wk1, wk2d, wv1, wv2d, pos8, kcg2):
    b, g, nseg, seg_w = ak.shape
    a_spec = pl.BlockSpec((1, 1, nseg, seg_w), lambda bi, gi: (bi, gi, 0, 0))
    o_spec = pl.BlockSpec((1, 1, nseg, LANES), lambda bi, gi: (bi, gi, 0, 0))

    def full(shape):
        return pl.BlockSpec(shape, lambda bi, gi: (0,) * len(shape))

    return pl.pallas_call(
        _compress_kernel,
        grid=(b, g),
        in_specs=[a_spec, a_spec, full(wk1.shape), full(wk2d.shape), full(wv1.shape), full(wv2d.shape),
                  full(pos8.shape), full(kcg2.shape)],
        out_specs=[o_spec, o_spec, o_spec],
        out_shape=[jax.ShapeDtypeStruct((b, g, nseg, LANES), f32),
                   jax.ShapeDtypeStruct((b, g, nseg, LANES), bf16),
                   jax.ShapeDtypeStruct((b, g, nseg, LANES), bf16)],
        compiler_params=_cparams(("parallel", "parallel")),
        name="compress",
    )(ak, av, wk1, wk2d, wv1, wv2d, pos8, kcg2)


def _nsa_kernel(slopes_ref, q_ref, sm_ref, qg_ref, kcd_ref, vce_ref, vco_ref, ovt_ref,
                kse_ref, kso_ref, vse_ref, vso_ref, kwd_ref, vwe_ref, vwo_ref,
                y_ref, qa_scr, qw_scr, m_scr, acc_scr):
    g = pl.program_id(1)
    qi = pl.program_id(2)
    q0 = qi * TQ
    ncp = kcd_ref.shape[2]
    nsel = ncp // (SEL_BLOCK // CMP_STRIDE)
    scale = NSA_HEAD_DIM ** -0.5

    lane = lax.broadcasted_iota(jnp.int32, (TQ, LANES), 1)
    lo = lane < HALF
    rowi = lax.broadcasted_iota(jnp.int32, (TQ, LANES), 0)
    slopes = [slopes_ref[g * NSA_GROUP + r] for r in range(NSA_GROUP)]

    def own_half(r):
        return lo if r % 2 == 0 else jnp.logical_not(lo)

    q = q_ref[0]
    qn = [_pair_norm(q[:, c * LANES:(c + 1) * LANES], qg_ref[:, c * LANES:(c + 1) * LANES], lo) * scale
          for c in range(NSA_GROUP // 2)]

    cidx = lax.broadcasted_iota(jnp.int32, (TQ, ncp), 1)
    tpos_c = q0 + lax.broadcasted_iota(jnp.int32, (TQ, ncp), 0)
    dist_c = tpos_c - (cidx * CMP_STRIDE + (CMP_BLOCK - 1))
    valid_c = jnp.logical_and(dist_c >= 0, cidx < ncp - 1)
    dist_cf = dist_c.astype(f32)
    kcd = kcd_ref[0, 0]
    p_sum = jnp.zeros((TQ, ncp), f32)
    o_cmp = []
    for r in range(NSA_GROUP):
        qm = jnp.where(own_half(r), qn[r // 2], 0.0)
        s = lax.dot_general(qm, kcd, NT, precision=HI, preferred_element_type=f32)
        s = jnp.where(valid_c, s - slopes[r] * dist_cf, NEG)
        e = jnp.exp(s - jnp.max(s, axis=-1, keepdims=True))
        p = jnp.where(valid_c, e / jnp.sum(e, axis=-1, keepdims=True), 0.0)
        p_sum = p_sum + p
        vc_ref = vce_ref if r % 2 == 0 else vco_ref
        o_cmp.append(jnp.dot(p.astype(bf16), vc_ref[0, 0], preferred_element_type=f32))

    imp = lax.dot_general(ovt_ref[...], p_sum, NT, precision=HI, preferred_element_type=f32)[0:nsel]
    jrow = lax.broadcasted_iota(jnp.int32, (nsel, TQ), 0)
    tcol = q0 + lax.broadcasted_iota(jnp.int32, (nsel, TQ), 1)
    forced = jnp.logical_or(jrow == lax.shift_right_logical(tcol, 6), jrow == 0)
    future = jrow * SEL_BLOCK > tcol
    work = jnp.where(forced, FORCE, jnp.where(future, -FORCE, imp))
    jrow_f = jrow.astype(f32)
    bias_t = jnp.full((nsel, TQ), NEG, f32)
    for _ in range(min(SEL_TOPN, nsel)):
        best = jnp.max(work, axis=0, keepdims=True)
        first = jnp.min(jnp.where(work == best, jrow_f, float(nsel)), axis=0, keepdims=True)
        hit = jrow_f == first
        bias_t = jnp.where(hit, 0.0, bias_t)
        work = jnp.where(hit, -3e38, work)
    if nsel < HALF:
        bias_t = jnp.concatenate([bias_t, jnp.full((HALF - nsel, TQ), NEG, f32)], axis=0)
    bias = jnp.concatenate([bias_t, bias_t], axis=0).T

    for r in range(NSA_GROUP):
        qa_scr[r] = jnp.where(own_half(r), qn[r // 2], bias).astype(bf16)
        qw_scr[r] = jnp.where(own_half(r), qn[r // 2], 0.0).astype(bf16)

    def reset_state():
        m_scr[...] = jnp.full(m_scr.shape, NEG, f32)
        acc_scr[...] = jnp.zeros(acc_scr.shape, f32)

    def flash_step(r, q_scr, k_tile, v_tile, kposrel, valid):
        s = lax.dot_general(q_scr[r], k_tile, NT, preferred_element_type=f32)
        s = s + slopes[r] * kposrel
        if valid is not None:
            s = jnp.where(valid, s, NEG)
        m_old = m_scr[r]
        m_new = jnp.maximum(m_old, jnp.max(s, axis=-1, keepdims=True))
        p = jnp.exp(s - m_new[:, 0:1])
        acc_scr[r] = jnp.exp(m_old - m_new) * acc_scr[r] + jnp.dot(p.astype(bf16), v_tile,
                                                                    preferred_element_type=f32)
        m_scr[r] = m_new

    def finish(r):
        acc = acc_scr[r]
        denom = jnp.sum(jnp.where(own_half(r), 0.0, acc), axis=-1, keepdims=True) * (1.0 / HALF)
        return acc / denom

    reset_state()

    def sel_tile(kt, causal):
        k0 = pl.multiple_of(kt * TK, TK)
        kposrel = (k0 - q0 + lax.broadcasted_iota(jnp.int32, (1, TK), 1)).astype(f32)
        valid = None
        if causal:
            kpos = k0 + lax.broadcasted_iota(jnp.int32, (TQ, TK), 1)
            valid = kpos <= q0 + lax.broadcasted_iota(jnp.int32, (TQ, TK), 0)
        for r in range(NSA_GROUP):
            k_ref, v_ref = (kse_ref, vse_ref) if r % 2 == 0 else (kso_ref, vso_ref)
            flash_step(r, qa_scr, k_ref[0, 0, pl.ds(k0, TK), :], v_ref[0, 0, pl.ds(k0, TK), :], kposrel, valid)

    n_full = q0 // TK

    def sel_body(kt, carry):
        sel_tile(kt, False)
        return carry

    lax.fori_loop(0, n_full, sel_body, 0)
    sel_tile(n_full, True)
    o_slc = [finish(r) for r in range(NSA_GROUP)]

    reset_state()
    n_wt = WINDOW // TW + 1
    tpos_w = q0 + rowi
    for j in range(n_wt):
        wi = qi * (TQ // TW) - (n_wt - 1) + j

        @pl.when(wi >= 0)
        def _():
            k0 = pl.multiple_of(wi * TW, TW)
            kposrel = (k0 - q0 + lax.broadcasted_iota(jnp.int32, (1, TW), 1)).astype(f32)
            dist = tpos_w - (k0 + lane)
            valid = jnp.logical_and(dist >= 0, dist < WINDOW)
            for r in range(NSA_GROUP):
                v_ref = vwe_ref if r % 2 == 0 else vwo_ref
                flash_step(r, qw_scr, kwd_ref[0, 0, pl.ds(k0, TW), :], v_ref[0, 0, pl.ds(k0, TW), :],
                           kposrel, valid)

    o_win = [finish(r) for r in range(NSA_GROUP)]

    sig = jax.nn.sigmoid(sm_ref[0])

    def gate(r, branch):
        col = (g * NSA_GROUP + r) * 3 + branch
        return jnp.sum(jnp.where(lane == col, sig, 0.0), axis=-1, keepdims=True)

    tot = [gate(r, 0) * o_cmp[r] + gate(r, 1) * o_slc[r] + gate(r, 2) * o_win[r] for r in range(NSA_GROUP)]
    for c in range(NSA_GROUP // 2):
        y_ref[0, :, c * LANES:(c + 1) * LANES] = jnp.where(lo, tot[2 * c], tot[2 * c + 1])


def _nsa(slopes, q3, sm3, qg2, kcd, vce, vco, ovt, kse, kso, vse, vso, kwd, vwe, vwo):
    b, t, _ = q3.shape
    gw = NSA_GROUP * NSA_HEAD_DIM
    ncp = kcd.shape[2]

    def kv_spec(rows):
        return pl.BlockSpec((1, 1, rows, LANES), lambda bi, gi, qi: (bi, gi, 0, 0))

    return pl.pallas_call(
        _nsa_kernel,
        grid=(b, NSA_KV_HEADS, t // TQ),
        in_specs=[
            pl.BlockSpec(memory_space=pltpu.SMEM),
            pl.BlockSpec((1, TQ, gw), lambda bi, gi, qi: (bi, qi, gi)),
            pl.BlockSpec((1, TQ, LANES), lambda bi, gi, qi: (bi, qi, 0)),
            pl.BlockSpec((1, gw), lambda bi, gi, qi: (0, 0)),
            kv_spec(ncp), kv_spec(ncp), kv_spec(ncp),
            pl.BlockSpec(ovt.shape, lambda bi, gi, qi: (0, 0)),
            kv_spec(t), kv_spec(t), kv_spec(t), kv_spec(t), kv_spec(t), kv_spec(t), kv_spec(t),
        ],
        out_specs=pl.BlockSpec((1, TQ, gw), lambda bi, gi, qi: (bi, qi, gi)),
        out_shape=jax.ShapeDtypeStruct((b, t, NSA_WIDTH), f32),
        scratch_shapes=[
            pltpu.VMEM((NSA_GROUP, TQ, LANES), bf16),
            pltpu.VMEM((NSA_GROUP, TQ, LANES), bf16),
            pltpu.VMEM((NSA_GROUP, TQ, LANES), f32),
            pltpu.VMEM((NSA_GROUP, TQ, LANES), f32),
        ],
        compiler_params=_cparams(("parallel", "parallel", "arbitrary")),
        name="nsa",
    )(slopes, q3, sm3, qg2, kcd, vce, vco, ovt, kse, kso, vse, vso, kwd, vwe, vwo)


def _log_sigmoid(x):
    return jnp.minimum(x, 0.0) - jnp.log1p(jnp.exp(-jnp.abs(x)))


def _mlstm_kernel(bi_ref, bf_ref, u_ref, v_ref, op_ref, gi_ref, gf_ref, cw_ref, cb_ref, wq_ref, wk_ref,
                  ng_ref, sk_ref, y_ref, uc_scr, q_scr, kt_scr, ct_scr, m_scr):
    h = pl.program_id(1)
    t = u_ref.shape[1]
    L = MLSTM_L
    dm = MLSTM_HEAD_DIM

    x = u_ref[0]
    rows = lax.broadcasted_iota(jnp.int32, (t, dm), 0)
    acc = x * cw_ref[CONV_WIDTH - 1:CONV_WIDTH, :]
    for s in range(1, CONV_WIDTH):
        xs = jnp.where(rows >= s, pltpu.roll(x, s, 0), 0.0)
        acc = acc + xs * cw_ref[CONV_WIDTH - 1 - s:CONV_WIDTH - s, :]
    uc = acc + cb_ref[...]
    uc = uc * jax.nn.sigmoid(uc)
    uc_scr[...] = uc
    ucb = uc.astype(bf16)
    q_scr[...] = jnp.dot(ucb, wq_ref[0].astype(bf16), preferred_element_type=f32).astype(bf16)
    k = jnp.dot(ucb, wk_ref[0].astype(bf16), preferred_element_type=f32) * (dm ** -0.5)
    kt_scr[...] = k.T

    ct_scr[...] = jnp.zeros(ct_scr.shape, f32)
    m_scr[...] = jnp.zeros(m_scr.shape, f32)

    li_ = lax.broadcasted_iota(jnp.int32, (L, L), 0)
    si_ = lax.broadcasted_iota(jnp.int32, (L, L), 1)
    causal = si_ <= li_
    strict_lower = (li_ > si_).astype(f32)
    b_i = bi_ref[h]
    b_f = bf_ref[h]
    ones_v = jnp.ones((L, dm), f32)

    def chunk(c, carry):
        r0 = pl.multiple_of(c * L, L)
        qc = q_scr[pl.ds(r0, L), :]
        ktc = kt_scr[:, pl.ds(r0, L)]
        vaug = jnp.concatenate([v_ref[0, pl.ds(r0, L), :], ones_v], axis=1).astype(bf16)
        log_i = gi_ref[0, 0, :, pl.ds(r0, L)] + b_i
        log_f = _log_sigmoid(gf_ref[0, 0, :, pl.ds(r0, L)] + b_f)
        a = jnp.where(causal, log_f, 0.0)
        d0 = jnp.dot(a, strict_lower, precision=HI, preferred_element_type=f32)
        b_col = jnp.sum(a, axis=-1, keepdims=True)
        dmat = jnp.where(causal, d0 + log_i, NEG)
        m_prev = m_scr[0:1, 0:1]
        m_inter = b_col + m_prev
        m_out = jnp.maximum(m_inter, jnp.max(dmat, axis=-1, keepdims=True))
        p = jnp.dot(qc, ktc.astype(bf16), preferred_element_type=f32) * jnp.exp(dmat - m_out)
        inter = jnp.exp(m_inter - m_out)
        ct = ct_scr[...]
        xo = inter * jnp.dot(qc, ct.astype(bf16), preferred_element_type=f32) + jnp.dot(
            p.astype(bf16), vaug, preferred_element_type=f32)
        num = xo[:, 0:dm]
        den = xo[:, dm:2 * dm]
        hh = num / jnp.maximum(jnp.abs(den), jnp.exp(-m_out))
        hh = hh * jax.nn.sigmoid(op_ref[0, pl.ds(r0, L), :])
        hh = hh * lax.rsqrt(jnp.mean(hh * hh, axis=-1, keepdims=True) + NORM_EPS) * ng_ref[0]
        y_ref[0, pl.ds(r0, L), :] = hh + sk_ref[...] * uc_scr[pl.ds(r0, L), :]

        g_sum = jnp.sum(log_f, axis=-1, keepdims=True)
        w_end = d0[L - 1:L, :] + log_i
        m_new = jnp.maximum(g_sum + m_prev, jnp.max(w_end, axis=-1, keepdims=True))
        decay = jnp.exp(g_sum + m_prev - m_new)
        w = jnp.exp(w_end - m_new)
        ct_scr[...] = decay * ct + jnp.dot((ktc * w).astype(bf16), vaug, preferred_element_type=f32)
        m_scr[...] = jnp.broadcast_to(m_new, m_scr.shape)
        return carry

    lax.fori_loop(0, t // L, chunk, 0)


def _mlstm(b_i, b_f, u3, v3, op3, gi4, gf4, cw, cb2, wq, wk, ng3, sk2):
    b, t, _ = u3.shape
    dm = MLSTM_HEAD_DIM
    seq = pl.BlockSpec((1, t, dm), lambda bi, hi: (bi, 0, hi))
    gate = pl.BlockSpec((1, 1, 1, t), lambda bi, hi: (bi, hi, 0, 0))
    smem = pl.BlockSpec(memory_space=pltpu.SMEM)
    return pl.pallas_call(
        _mlstm_kernel,
        grid=(b, MLSTM_HEADS),
        in_specs=[
            smem, smem, seq, seq, seq, gate, gate,
            pl.BlockSpec((CONV_WIDTH, dm), lambda bi, hi: (0, hi)),
            pl.BlockSpec((1, dm), lambda bi, hi: (0, hi)),
            pl.BlockSpec((1, dm, dm), lambda bi, hi: (hi, 0, 0)),
            pl.BlockSpec((1, dm, dm), lambda bi, hi: (hi, 0, 0)),
            pl.BlockSpec((1, 1, dm), lambda bi, hi: (hi, 0, 0)),
            pl.BlockSpec((1, dm), lambda bi, hi: (0, hi)),
        ],
        out_specs=seq,
        out_shape=jax.ShapeDtypeStruct((b, t, MLSTM_WIDTH), f32),
        scratch_shapes=[
            pltpu.VMEM((t, dm), f32),
            pltpu.VMEM((t, dm), bf16),
            pltpu.VMEM((dm, t), f32),
            pltpu.VMEM((dm, 2 * dm), f32),
            pltpu.VMEM((8, LANES), f32),
        ],
        compiler_params=_cparams(("parallel", "parallel")),
        name="mlstm",
    )(b_i, b_f, u3, v3, op3, gi4, gf4, cw, cb2, wq, wk, ng3, sk2)


def _ffn_kernel(x_ref, ya_ref, yb_ref, wo_ref, g2_ref, wg_ref, wu_ref, wd_ref, o_ref, act_scr):
    x1 = (x_ref[...]
          + jnp.dot(ya_ref[...].astype(bf16), wo_ref[0:NSA_WIDTH, :], preferred_element_type=f32)
          + jnp.dot(yb_ref[...].astype(bf16), wo_ref[NSA_WIDTH:NSA_WIDTH + MLSTM_WIDTH, :],
                    preferred_element_type=f32))
    h2 = x1 * lax.rsqrt(jnp.mean(x1 * x1, axis=-1, keepdims=True) + NORM_EPS) * g2_ref[...]
    h2b = h2.astype(bf16)
    for c in range(D_FF // FF_CHUNK):
        cols = slice(c * FF_CHUNK, (c + 1) * FF_CHUNK)
        gt = jnp.dot(h2b, wg_ref[:, cols], preferred_element_type=f32)
        up = jnp.dot(h2b, wu_ref[:, cols], preferred_element_type=f32)
        act_scr[:, cols] = (gt * jax.nn.sigmoid(gt) * up).astype(bf16)
    o_ref[...] = x1 + jnp.dot(act_scr[...], wd_ref[...], preferred_element_type=f32)


def _ffn(x2, ya, yb, wo, g2, wg, wu, wd):
    n = x2.shape[0]
    tm = TM_FFN

    def const(shape):
        return pl.BlockSpec(shape, lambda i: (0, 0), pipeline_mode=pl.Buffered(1))

    return pl.pallas_call(
        _ffn_kernel,
        grid=(n // tm,),
        in_specs=[
            pl.BlockSpec((tm, D_MODEL), lambda i: (i, 0)),
            pl.BlockSpec((tm, NSA_WIDTH), lambda i: (i, 0)),
            pl.BlockSpec((tm, MLSTM_WIDTH), lambda i: (i, 0)),
            const(wo.shape), const(g2.shape), const(wg.shape), const(wu.shape), const(wd.shape),
        ],
        out_specs=pl.BlockSpec((tm, D_MODEL), lambda i: (i, 0)),
        out_shape=jax.ShapeDtypeStruct((n, D_MODEL), f32),
        scratch_shapes=[pltpu.VMEM((tm, D_FF), bf16)],
        compiler_params=_cparams(("parallel",)),
        name="ffn",
    )(x2, ya, yb, wo, g2, wg, wu, wd)


def _overlap_t(ncp):
    nsel = ncp // (SEL_BLOCK // CMP_STRIDE)
    cs = np.arange(ncp) * CMP_STRIDE
    ss = np.arange(nsel) * SEL_BLOCK
    ov = ((cs[None, :] < ss[:, None] + SEL_BLOCK) & (cs[None, :] + CMP_BLOCK > ss[:, None])).astype(np.float32)
    ov[:, ncp - 1] = 0.0
    out = np.zeros((LANES, ncp), np.float32)
    out[:nsel] = ov
    return jnp.asarray(out)


def _layer(x, norm1_g, w_in, q_g, kc_g, ks_g, kw_g, cmp_pos, w_ck1, w_ck2, w_cv1, w_cv2, conv_w, conv_b,
           w_mq, w_mk, b_i, b_f, mlstm_norm_g, mlstm_skip, w_out, norm2_g, w_gate, w_up, w_down):
    b, t, d = x.shape
    n = b * t
    x2 = x.reshape(n, d)

    o_gate = NSA_WIDTH + 6 * KV_WIDTH
    o_u = o_gate + 3 * NSA_HEADS
    o_if = o_u + 3 * MLSTM_WIDTH
    w_perm = jnp.concatenate([
        w_in[:, :o_gate], w_in[:, o_u:o_if], w_in[:, o_gate:o_u], w_in[:, o_if:],
        jnp.zeros((d, LANES - 3 * NSA_HEADS - 2 * MLSTM_HEADS), w_in.dtype)], axis=1).astype(bf16)
    q2, kv2, u2, vm2, op2, sm2 = _inproj(x2, norm1_g.reshape(1, d), w_perm)

    kv3 = kv2.reshape(b, t, 6 * KV_WIDTH)
    kse, kso, vse, vso, kwd, vwe, vwo = _kvprep(kv3, jnp.tile(ks_g, 2).reshape(1, LANES),
                                                jnp.tile(kw_g, 2).reshape(1, LANES))
    nseg = t // CMP_STRIDE
    seg_w = CMP_STRIDE * NSA_HEAD_DIM

    def segments(cols):
        return cols.reshape(b, t, NSA_KV_HEADS, NSA_HEAD_DIM).transpose(0, 2, 1, 3).reshape(
            b, NSA_KV_HEADS, nseg, seg_w)

    pos8 = jnp.broadcast_to(cmp_pos.reshape(1, CMP_BLOCK * NSA_HEAD_DIM), (8, CMP_BLOCK * NSA_HEAD_DIM))
    kcd, vce, vco = _compress(segments(kv3[..., 0:KV_WIDTH]), segments(kv3[..., KV_WIDTH:2 * KV_WIDTH]),
                              w_ck1, jnp.tile(w_ck2, (1, 2)), w_cv1, jnp.tile(w_cv2, (1, 2)), pos8,
                              jnp.tile(kc_g, 2).reshape(1, LANES))
    slopes = jnp.exp2(-8.0 * (jnp.arange(NSA_HEADS, dtype=f32) + 1.0) / NSA_HEADS)
    y_nsa = _nsa(slopes, q2.reshape(b, t, NSA_WIDTH), sm2.reshape(b, t, LANES),
                 jnp.tile(q_g, NSA_GROUP).reshape(1, NSA_GROUP * NSA_HEAD_DIM),
                 kcd, vce, vco, _overlap_t(nseg), kse, kso, vse, vso, kwd, vwe, vwo)

    gi4 = sm2[:, 3 * NSA_HEADS:3 * NSA_HEADS + MLSTM_HEADS].reshape(b, t, MLSTM_HEADS).transpose(0, 2, 1)
    gf4 = sm2[:, 3 * NSA_HEADS + MLSTM_HEADS:3 * NSA_HEADS + 2 * MLSTM_HEADS].reshape(
        b, t, MLSTM_HEADS).transpose(0, 2, 1)
    y_mem = _mlstm(b_i, b_f, u2.reshape(b, t, MLSTM_WIDTH), vm2.reshape(b, t, MLSTM_WIDTH),
                   op2.reshape(b, t, MLSTM_WIDTH), gi4.reshape(b, MLSTM_HEADS, 1, t),
                   gf4.reshape(b, MLSTM_HEADS, 1, t), conv_w, conv_b.reshape(1, MLSTM_WIDTH), w_mq, w_mk,
                   mlstm_norm_g.reshape(MLSTM_HEADS, 1, MLSTM_HEAD_DIM), mlstm_skip.reshape(1, MLSTM_WIDTH))

    out = _ffn(x2, y_nsa.reshape(n, NSA_WIDTH), y_mem.reshape(n, MLSTM_WIDTH), w_out.astype(bf16),
               norm2_g.reshape(1, d), w_gate.astype(bf16), w_up.astype(bf16), w_down.astype(bf16))
    return out.reshape(b, t, d)


def kernel(x, norm1_g, w_in, q_norm_g, kc_norm_g, ks_norm_g, kw_norm_g, cmp_pos, w_ck1, w_ck2, w_cv1, w_cv2,
           conv_w, conv_b, w_mq, w_mk, b_i, b_f, mlstm_norm_g, mlstm_skip, w_out, norm2_g, w_gate, w_up, w_down):
    depth = norm1_g.shape[0]
    for l in range(depth):
        x = _layer(x, norm1_g[l], w_in[l], q_norm_g[l], kc_norm_g[l], ks_norm_g[l], kw_norm_g[l], cmp_pos[l],
                   w_ck1[l], w_ck2[l], w_cv1[l], w_cv2[l], conv_w[l], conv_b[l], w_mq[l], w_mk[l], b_i[l], b_f[l],
                   mlstm_norm_g[l], mlstm_skip[l], w_out[l], norm2_g[l], w_gate[l], w_up[l], w_down[l])
    return x
```

```python
import functools
import math

import numpy as np
import jax
import jax.numpy as jnp
from jax import lax
from jax.experimental import pallas as pl
from jax.experimental.pallas import tpu as pltpu

f32 = jnp.float32
bf16 = jnp.bfloat16

D_MODEL = 1024
NSA_HEADS = 8
NSA_KV_HEADS = 2
NSA_HEAD_DIM = 64
NSA_GROUP = NSA_HEADS // NSA_KV_HEADS
CMP_BLOCK = 32
CMP_STRIDE = 16
CMP_HIDDEN = 256
SEL_BLOCK = 64
SEL_TOPN = 16
WINDOW = 512
MLSTM_HEADS = 4
MLSTM_HEAD_DIM = 128
CONV_WIDTH = 4
NSA_WIDTH = NSA_HEADS * NSA_HEAD_DIM
MLSTM_WIDTH = MLSTM_HEADS * MLSTM_HEAD_DIM
KV_WIDTH = NSA_KV_HEADS * NSA_HEAD_DIM
D_FF = -(-8 * D_MODEL // (3 * 256)) * 256
NORM_EPS = 1e-6
NEG = -1e30
FORCE = 1e9

LANES = 128
HALF = 64
VMEM_LIMIT = 56 * 1024 * 1024

TM_PROJ = 512
TM_PREP = 512
TQ = 128
TK = 512
MLSTM_L = 128
TM_FFN = 512
FF_CHUNK = 256

HI = lax.Precision.HIGHEST
NT = (((1,), (1,)), ((), ()))


def _cparams(sem):
    return pltpu.CompilerParams(dimension_semantics=sem, vmem_limit_bytes=VMEM_LIMIT)


PROJ_SPLITS = (NSA_WIDTH, 6 * KV_WIDTH, MLSTM_WIDTH, MLSTM_WIDTH, MLSTM_WIDTH, LANES)


def _inproj_kernel(x_ref, g_ref, w_ref, *out_refs):
    x = x_ref[...]
    h = x * lax.rsqrt(jnp.mean(x * x, axis=-1, keepdims=True) + NORM_EPS) * g_ref[...]
    hb = h.astype(bf16)
    off = 0
    for o_ref, width in zip(out_refs, PROJ_SPLITS):
        o_ref[...] = jnp.dot(hb, w_ref[:, off:off + width], preferred_element_type=f32)
        off += width


def _inproj(x2, g1, w_perm):
    n = x2.shape[0]
    wtot = sum(PROJ_SPLITS)
    return pl.pallas_call(
        _inproj_kernel,
        grid=(n // TM_PROJ,),
        in_specs=[
            pl.BlockSpec((TM_PROJ, D_MODEL), lambda i: (i, 0)),
            pl.BlockSpec((1, D_MODEL), lambda i: (0, 0)),
            pl.BlockSpec((D_MODEL, wtot), lambda i: (0, 0)),
        ],
        out_specs=[pl.BlockSpec((TM_PROJ, w), lambda i: (i, 0)) for w in PROJ_SPLITS],
        out_shape=[jax.ShapeDtypeStruct((n, w), f32) for w in PROJ_SPLITS],
        compiler_params=_cparams(("parallel",)),
        name="inproj",
    )(x2, g1, w_perm)


def _pair_norm(x, gain, lo):
    sq = x * x
    s_lo = jnp.sum(jnp.where(lo, sq, 0.0), axis=-1, keepdims=True)
    s_hi = jnp.sum(jnp.where(lo, 0.0, sq), axis=-1, keepdims=True)
    inv = jnp.where(lo, lax.rsqrt(s_lo / HALF + NORM_EPS), lax.rsqrt(s_hi / HALF + NORM_EPS))
    return x * inv * gain


def _kvprep_kernel(kv_ref, ksg_ref, kwg_ref, ks_ref, vs_ref, kw_ref, vw_ref):
    i = pl.program_id(1)
    tm = kv_ref.shape[1]
    lane = lax.broadcasted_iota(jnp.int32, (tm, LANES), 1)
    lo = lane < HALF
    row = i * tm + lax.broadcasted_iota(jnp.int32, (tm, LANES), 0)
    code = jnp.where(lax.shift_right_logical(row, 6) == (lane & (HALF - 1)), 1.0, 0.0)

    def put(x, o_ref, fill):
        o_ref[0, 0] = jnp.where(lo, x, fill).astype(bf16)
        o_ref[0, 1] = jnp.where(lo, pltpu.roll(x, HALF, 1), fill).astype(bf16)

    put(_pair_norm(kv_ref[0, :, 2 * KV_WIDTH:3 * KV_WIDTH], ksg_ref[...], lo), ks_ref, code)
    put(kv_ref[0, :, 3 * KV_WIDTH:4 * KV_WIDTH], vs_ref, 1.0)
    put(_pair_norm(kv_ref[0, :, 4 * KV_WIDTH:5 * KV_WIDTH], kwg_ref[...], lo), kw_ref, 0.0)
    put(kv_ref[0, :, 5 * KV_WIDTH:6 * KV_WIDTH], vw_ref, 1.0)


def _kvprep(kv3, ksg2, kwg2):
    b, t, _ = kv3.shape
    tm = min(TM_PREP, t)
    out_spec = pl.BlockSpec((1, NSA_KV_HEADS, tm, LANES), lambda bi, i: (bi, 0, i, 0))
    out_shape = jax.ShapeDtypeStruct((b, NSA_KV_HEADS, t, LANES), bf16)
    return pl.pallas_call(
        _kvprep_kernel,
        grid=(b, t // tm),
        in_specs=[
            pl.BlockSpec((1, tm, 6 * KV_WIDTH), lambda bi, i: (bi, i, 0)),
            pl.BlockSpec((1, LANES), lambda bi, i: (0, 0)),
            pl.BlockSpec((1, LANES), lambda bi, i: (0, 0)),
        ],
        out_specs=[out_spec] * 4,
        out_shape=[out_shape] * 4,
        compiler_params=_cparams(("parallel", "parallel")),
        name="kvprep",
    )(kv3, ksg2, kwg2)


def _gelu_tanh(x):
    return 0.5 * x * (1.0 + jnp.tanh(math.sqrt(2.0 / math.pi) * (x + 0.044715 * (x * x * x))))


def _compress_kernel(ak_ref, av_ref, wk1_ref, wk2_ref, wv1_ref, wv2_ref, pos_ref, kcg_ref,
                     kcd_ref, vcl_ref):
    nseg = ak_ref.shape[2]
    half_in = CMP_STRIDE * NSA_HEAD_DIM

    def branch(a, w1_ref, w2_ref):
        top = jnp.dot(a, w1_ref[0:half_in, :], precision=HI, preferred_element_type=f32)
        bot = jnp.dot(a, w1_ref[half_in:2 * half_in, :], precision=HI, preferred_element_type=f32)
        pos_term = jnp.dot(pos_ref[...], w1_ref[...], precision=HI, preferred_element_type=f32)[0:1]
        hid = top + pltpu.roll(bot, nseg - 1, 0) + pos_term
        return jnp.dot(_gelu_tanh(hid), w2_ref[...], precision=HI, preferred_element_type=f32)

    kc = branch(ak_ref[0, 0], wk1_ref, wk2_ref)
    kc = kc * lax.rsqrt(jnp.mean(kc * kc, axis=-1, keepdims=True) + NORM_EPS) * kcg_ref[...]
    kcd_ref[0, 0] = kc

    vc = branch(av_ref[0, 0], wv1_ref, wv2_ref)
    lane = lax.broadcasted_iota(jnp.int32, vc.shape, 1)
    vcl_ref[0, 0] = jnp.where(lane < HALF, vc, 0.0).astype(bf16)


def _compress(ak, av, wk1, wk2d, wv1, wv2d, pos8, kcg2):
    b, g, nseg, seg_w = ak.shape
    a_spec = pl.BlockSpec((1, 1, nseg, seg_w), lambda bi, gi: (bi, gi, 0, 0))
    o_spec = pl.BlockSpec((1, 1, nseg, LANES), lambda bi, gi: (bi, gi, 0, 0))

    def full(shape):
        return pl.BlockSpec(shape, lambda bi, gi: (0,) * len(shape))

    return pl.pallas_call(
        _compress_kernel,
        grid=(b, g),
        in_specs=[a_spec, a_spec, full(wk1.shape), full(wk2d.shape), full(wv1.shape), full(wv2d.shape),
                  full(pos8.shape), full(kcg2.shape)],
        out_specs=[o_spec, o_spec],
        out_shape=[jax.ShapeDtypeStruct((b, g, nseg, LANES), f32),
                   jax.ShapeDtypeStruct((b, g, nseg, LANES), bf16)],
        compiler_params=_cparams(("parallel", "parallel")),
        name="compress",
    )(ak, av, wk1, wk2d, wv1, wv2d, pos8, kcg2)


def _nsa_kernel(slopes_ref, q_ref, sm_ref, qg_ref, kcd_ref, vcl_ref, ovt_ref,
                ks_ref, vs_ref, kw_ref, vw_ref, y_ref, qa_scr, m_scr, acc_scr):
    g = pl.program_id(1)
    qi = pl.program_id(2)
    q0 = qi * TQ
    t_len = ks_ref.shape[2]
    ncp = kcd_ref.shape[2]
    nsel = ncp // (SEL_BLOCK // CMP_STRIDE)
    rows = NSA_GROUP * TQ
    log2e = math.log2(math.e)
    scale = NSA_HEAD_DIM ** -0.5 * log2e

    lane = lax.broadcasted_iota(jnp.int32, (TQ, LANES), 1)
    lo = lane < HALF
    slopes = [slopes_ref[g * NSA_GROUP + r] * log2e for r in range(NSA_GROUP)]

    def per_head(x, fn):
        return jnp.concatenate([fn(r, x[r * TQ:(r + 1) * TQ]) for r in range(NSA_GROUP)], axis=0)

    q = q_ref[0]
    qn = []
    for c in range(NSA_GROUP // 2):
        pair = _pair_norm(q[:, c * LANES:(c + 1) * LANES], qg_ref[:, c * LANES:(c + 1) * LANES], lo) * scale
        qn += [pair, pltpu.roll(pair, HALF, 1)]
    qz = jnp.concatenate([jnp.where(lo, x, 0.0) for x in qn], axis=0)

    cidx = lax.broadcasted_iota(jnp.int32, (TQ, ncp), 1)
    tpos_c = q0 + lax.broadcasted_iota(jnp.int32, (TQ, ncp), 0)
    dist_c = tpos_c - (cidx * CMP_STRIDE + (CMP_BLOCK - 1))
    valid_c = jnp.logical_and(dist_c >= 0, cidx < ncp - 1)
    dist_cf = dist_c.astype(f32)
    s_c = lax.dot_general(qz, kcd_ref[0, 0], NT, precision=HI, preferred_element_type=f32)
    s_c = per_head(s_c, lambda r, x: jnp.where(valid_c, x - slopes[r] * dist_cf, NEG))
    e_c = jnp.exp2(s_c - jnp.max(s_c, axis=-1, keepdims=True))
    p_c = e_c / jnp.sum(e_c, axis=-1, keepdims=True)
    p_c = per_head(p_c, lambda r, x: jnp.where(valid_c, x, 0.0))
    p_sum = p_c[0:TQ]
    for r in range(1, NSA_GROUP):
        p_sum = p_sum + p_c[r * TQ:(r + 1) * TQ]
    o_cmp = jnp.dot(p_c.astype(bf16), vcl_ref[0, 0], preferred_element_type=f32)

    band = WINDOW + TQ
    kb = pl.multiple_of(jnp.maximum(q0 - WINDOW, 0), TQ)
    s_w = lax.dot_general(qz.astype(bf16), kw_ref[0, 0, pl.ds(kb, band), :], NT, preferred_element_type=f32)
    krel_w = (kb - q0 + lax.broadcasted_iota(jnp.int32, (1, band), 1)).astype(f32)
    dist_w = (q0 - kb + lax.broadcasted_iota(jnp.int32, (TQ, band), 0)
              - lax.broadcasted_iota(jnp.int32, (TQ, band), 1))
    valid_w = jnp.logical_and(dist_w >= 0, dist_w < WINDOW)
    s_w = per_head(s_w, lambda r, x: jnp.where(valid_w, x + slopes[r] * krel_w, NEG))
    p_w = jnp.exp2(s_w - jnp.max(s_w, axis=-1, keepdims=True))
    acc_w = jnp.dot(p_w.astype(bf16), vw_ref[0, 0, pl.ds(kb, band), :], preferred_element_type=f32)
    o_win = acc_w / pltpu.roll(acc_w, HALF, 1)

    imp = lax.dot_general(ovt_ref[...], p_sum, NT, precision=HI, preferred_element_type=f32)[0:nsel]
    jrow = lax.broadcasted_iota(jnp.int32, (nsel, TQ), 0)
    tcol = q0 + lax.broadcasted_iota(jnp.int32, (nsel, TQ), 1)
    forced = jnp.logical_or(jrow == lax.shift_right_logical(tcol, 6), jrow == 0)
    future = jrow * SEL_BLOCK > tcol
    work = jnp.where(forced, FORCE, jnp.where(future, -FORCE, imp))
    jrow_f = jrow.astype(f32)
    bias_t = jnp.full((nsel, TQ), NEG, f32)
    for _ in range(min(SEL_TOPN, nsel)):
        best = jnp.max(work, axis=0, keepdims=True)
        first = jnp.min(jnp.where(work == best, jrow_f, float(nsel)), axis=0, keepdims=True)
        hit = jrow_f == first
        bias_t = jnp.where(hit, 0.0, bias_t)
        work = jnp.where(hit, -3e38, work)
    if nsel < HALF:
        bias_t = jnp.concatenate([bias_t, jnp.full((HALF - nsel, TQ), NEG, f32)], axis=0)
    bias = jnp.concatenate([bias_t, bias_t], axis=0).T

    qa_scr[...] = jnp.concatenate([jnp.where(lo, x, bias) for x in qn], axis=0).astype(bf16)

    m_scr[...] = jnp.full(m_scr.shape, NEG, f32)
    acc_scr[...] = jnp.zeros(acc_scr.shape, f32)

    def sel_tile(kt, causal):
        k0 = pl.multiple_of(kt * TK, TK)
        s = lax.dot_general(qa_scr[...], ks_ref[0, 0, pl.ds(k0, TK), :], NT, preferred_element_type=f32)
        krel = (k0 - q0 + lax.broadcasted_iota(jnp.int32, (1, TK), 1)).astype(f32)
        if causal:
            ahead = (k0 - q0 + lax.broadcasted_iota(jnp.int32, (TQ, TK), 1)
                     > lax.broadcasted_iota(jnp.int32, (TQ, TK), 0))
            s = per_head(s, lambda r, x: jnp.where(ahead, NEG, x + slopes[r] * krel))
        else:
            s = per_head(s, lambda r, x: x + slopes[r] * krel)
        m_old = m_scr[...]
        m_new = jnp.maximum(m_old, jnp.max(s, axis=-1, keepdims=True))
        p = jnp.exp2(s - m_new[:, 0:1])
        acc_scr[...] = jnp.exp2(m_old - m_new) * acc_scr[...] + jnp.dot(
            p.astype(bf16), vs_ref[0, 0, pl.ds(k0, TK), :], preferred_element_type=f32)
        m_scr[...] = m_new

    n_full = q0 // TK

    def sel_body(kt, carry):
        sel_tile(kt, False)
        return carry

    lax.fori_loop(0, n_full, sel_body, 0)
    sel_tile(n_full, True)
    acc_s = acc_scr[...]
    o_slc = acc_s / pltpu.roll(acc_s, HALF, 1)

    sig = jax.nn.sigmoid(sm_ref[0])

    def gate(r, branch):
        col = (g * NSA_GROUP + r) * 3 + branch
        return jnp.sum(jnp.where(lane == col, sig, 0.0), axis=-1, keepdims=True)

    def mix(r):
        sl = slice(r * TQ, (r + 1) * TQ)
        return gate(r, 0) * o_cmp[sl] + gate(r, 1) * o_slc[sl] + gate(r, 2) * o_win[sl]

    for c in range(NSA_GROUP // 2):
        y_ref[0, :, c * LANES:(c + 1) * LANES] = jnp.where(lo, mix(2 * c), pltpu.roll(mix(2 * c + 1), HALF, 1))


def _nsa(slopes, q3, sm3, qg2, kcd, vcl, ovt, ks, vs, kw, vw):
    b, t, _ = q3.shape
    gw = NSA_GROUP * NSA_HEAD_DIM
    ncp = kcd.shape[2]

    def kv_spec(rows):
        return pl.BlockSpec((1, 1, rows, LANES), lambda bi, gi, qi: (bi, gi, 0, 0))

    return pl.pallas_call(
        _nsa_kernel,
        grid=(b, NSA_KV_HEADS, t // TQ),
        in_specs=[
            pl.BlockSpec(memory_space=pltpu.SMEM),
            pl.BlockSpec((1, TQ, gw), lambda bi, gi, qi: (bi, qi, gi)),
            pl.BlockSpec((1, TQ, LANES), lambda bi, gi, qi: (bi, qi, 0)),
            pl.BlockSpec((1, gw), lambda bi, gi, qi: (0, 0)),
            kv_spec(ncp), kv_spec(ncp),
            pl.BlockSpec(ovt.shape, lambda bi, gi, qi: (0, 0)),
            kv_spec(t), kv_spec(t), kv_spec(t), kv_spec(t),
        ],
        out_specs=pl.BlockSpec((1, TQ, gw), lambda bi, gi, qi: (bi, qi, gi)),
        out_shape=jax.ShapeDtypeStruct((b, t, NSA_WIDTH), f32),
        scratch_shapes=[
            pltpu.VMEM((NSA_GROUP * TQ, LANES), bf16),
            pltpu.VMEM((NSA_GROUP * TQ, LANES), f32),
            pltpu.VMEM((NSA_GROUP * TQ, LANES), f32),
        ],
        compiler_params=_cparams(("parallel", "parallel", "arbitrary")),
        name="nsa",
    )(slopes, q3, sm3, qg2, kcd, vcl, ovt, ks, vs, kw, vw)


def _log_sigmoid(x):
    return jnp.minimum(x, 0.0) - jnp.log1p(jnp.exp(-jnp.abs(x)))


def _mlstm_kernel(bi_ref, bf_ref, u_ref, v_ref, op_ref, gi_ref, gf_ref, cw_ref, cb_ref, wq_ref, wk_ref,
                  ng_ref, sk_ref, y_ref, uc_scr, q_scr, kt_scr, ct_scr, m_scr):
    h = pl.program_id(1)
    t = u_ref.shape[1]
    L = MLSTM_L
    dm = MLSTM_HEAD_DIM

    x = u_ref[0]
    rows = lax.broadcasted_iota(jnp.int32, (t, dm), 0)
    acc = x * cw_ref[CONV_WIDTH - 1:CONV_WIDTH, :]
    for s in range(1, CONV_WIDTH):
        xs = jnp.where(rows >= s, pltpu.roll(x, s, 0), 0.0)
        acc = acc + xs * cw_ref[CONV_WIDTH - 1 - s:CONV_WIDTH - s, :]
    uc = acc + cb_ref[...]
    uc = uc * jax.nn.sigmoid(uc)
    uc_scr[...] = uc
    ucb = uc.astype(bf16)
    q_scr[...] = jnp.dot(ucb, wq_ref[0].astype(bf16), preferred_element_type=f32).astype(bf16)
    k = jnp.dot(ucb, wk_ref[0].astype(bf16), preferred_element_type=f32) * (dm ** -0.5)
    kt_scr[...] = k.T

    ct_scr[...] = jnp.zeros(ct_scr.shape, f32)
    m_scr[...] = jnp.zeros(m_scr.shape, f32)

    li_ = lax.broadcasted_iota(jnp.int32, (L, L), 0)
    si_ = lax.broadcasted_iota(jnp.int32, (L, L), 1)
    causal = si_ <= li_
    strict_lower = (li_ > si_).astype(f32)
    b_i = bi_ref[h]
    b_f = bf_ref[h]
    ones_v = jnp.ones((L, dm), f32)

    def chunk(c, carry):
        r0 = pl.multiple_of(c * L, L)
        qc = q_scr[pl.ds(r0, L), :]
        ktc = kt_scr[:, pl.ds(r0, L)]
        vaug = jnp.concatenate([v_ref[0, pl.ds(r0, L), :], ones_v], axis=1).astype(bf16)
        log_i = gi_ref[0, 0, :, pl.ds(r0, L)] + b_i
        log_f = _log_sigmoid(gf_ref[0, 0, :, pl.ds(r0, L)] + b_f)
        a = jnp.where(causal, log_f, 0.0)
        d0 = jnp.dot(a, strict_lower, precision=HI, preferred_element_type=f32)
        b_col = jnp.sum(a, axis=-1, keepdims=True)
        dmat = jnp.where(causal, d0 + log_i, NEG)
        m_prev = m_scr[0:1, 0:1]
        m_inter = b_col + m_prev
        m_out = jnp.maximum(m_inter, jnp.max(dmat, axis=-1, keepdims=True))
        p = jnp.dot(qc, ktc.astype(bf16), preferred_element_type=f32) * jnp.exp(dmat - m_out)
        inter = jnp.exp(m_inter - m_out)
        ct = ct_scr[...]
        xo = inter * jnp.dot(qc, ct.astype(bf16), preferred_element_type=f32) + jnp.dot(
            p.astype(bf16), vaug, preferred_element_type=f32)
        num = xo[:, 0:dm]
        den = xo[:, dm:2 * dm]
        hh = num / jnp.maximum(jnp.abs(den), jnp.exp(-m_out))
        hh = hh * jax.nn.sigmoid(op_ref[0, pl.ds(r0, L), :])
        hh = hh * lax.rsqrt(jnp.mean(hh * hh, axis=-1, keepdims=True) + NORM_EPS) * ng_ref[0]
        y_ref[0, pl.ds(r0, L), :] = hh + sk_ref[...] * uc_scr[pl.ds(r0, L), :]

        g_sum = jnp.sum(log_f, axis=-1, keepdims=True)
        w_end = d0[L - 1:L, :] + log_i
        m_new = jnp.maximum(g_sum + m_prev, jnp.max(w_end, axis=-1, keepdims=True))
        decay = jnp.exp(g_sum + m_prev - m_new)
        w = jnp.exp(w_end - m_new)
        ct_scr[...] = decay * ct + jnp.dot((ktc * w).astype(bf16), vaug, preferred_element_type=f32)
        m_scr[...] = jnp.broadcast_to(m_new, m_scr.shape)
        return carry

    lax.fori_loop(0, t // L, chunk, 0)


def _mlstm(b_i, b_f, u3, v3, op3, gi4, gf4, cw, cb2, wq, wk, ng3, sk2):
    b, t, _ = u3.shape
    dm = MLSTM_HEAD_DIM
    seq = pl.BlockSpec((1, t, dm), lambda bi, hi: (bi, 0, hi))
    gate = pl.BlockSpec((1, 1, 1, t), lambda bi, hi: (bi, hi, 0, 0))
    smem = pl.BlockSpec(memory_space=pltpu.SMEM)
    return pl.pallas_call(
        _mlstm_kernel,
        grid=(b, MLSTM_HEADS),
        in_specs=[
            smem, smem, seq, seq, seq, gate, gate,
            pl.BlockSpec((CONV_WIDTH, dm), lambda bi, hi: (0, hi)),
            pl.BlockSpec((1, dm), lambda bi, hi: (0, hi)),
            pl.BlockSpec((1, dm, dm), lambda bi, hi: (hi, 0, 0)),
            pl.BlockSpec((1, dm, dm), lambda bi, hi: (hi, 0, 0)),
            pl.BlockSpec((1, 1, dm), lambda bi, hi: (hi, 0, 0)),
            pl.BlockSpec((1, dm), lambda bi, hi: (0, hi)),
        ],
        out_specs=seq,
        out_shape=jax.ShapeDtypeStruct((b, t, MLSTM_WIDTH), f32),
        scratch_shapes=[
            pltpu.VMEM((t, dm), f32),
            pltpu.VMEM((t, dm), bf16),
            pltpu.VMEM((dm, t), f32),
            pltpu.VMEM((dm, 2 * dm), f32),
            pltpu.VMEM((8, LANES), f32),
        ],
        compiler_params=_cparams(("parallel", "parallel")),
        name="mlstm",
    )(b_i, b_f, u3, v3, op3, gi4, gf4, cw, cb2, wq, wk, ng3, sk2)


def _ffn_kernel(x_ref, ya_ref, yb_ref, wo_ref, g2_ref, wg_ref, wu_ref, wd_ref, o_ref, act_scr):
    x1 = (x_ref[...]
          + jnp.dot(ya_ref[...].astype(bf16), wo_ref[0:NSA_WIDTH, :], preferred_element_type=f32)
          + jnp.dot(yb_ref[...].astype(bf16), wo_ref[NSA_WIDTH:NSA_WIDTH + MLSTM_WIDTH, :],
                    preferred_element_type=f32))
    h2 = x1 * lax.rsqrt(jnp.mean(x1 * x1, axis=-1, keepdims=True) + NORM_EPS) * g2_ref[...]
    h2b = h2.astype(bf16)
    for c in range(D_FF // FF_CHUNK):
        cols = slice(c * FF_CHUNK, (c + 1) * FF_CHUNK)
        gt = jnp.dot(h2b, wg_ref[:, cols], preferred_element_type=f32)
        up = jnp.dot(h2b, wu_ref[:, cols], preferred_element_type=f32)
        act_scr[:, cols] = (gt * jax.nn.sigmoid(gt) * up).astype(bf16)
    o_ref[...] = x1 + jnp.dot(act_scr[...], wd_ref[...], preferred_element_type=f32)


def _ffn(x2, ya, yb, wo, g2, wg, wu, wd):
    n = x2.shape[0]
    tm = TM_FFN

    def const(shape):
        return pl.BlockSpec(shape, lambda i: (0, 0), pipeline_mode=pl.Buffered(1))

    return pl.pallas_call(
        _ffn_kernel,
        grid=(n // tm,),
        in_specs=[
            pl.BlockSpec((tm, D_MODEL), lambda i: (i, 0)),
            pl.BlockSpec((tm, NSA_WIDTH), lambda i: (i, 0)),
            pl.BlockSpec((tm, MLSTM_WIDTH), lambda i: (i, 0)),
            const(wo.shape), const(g2.shape), const(wg.shape), const(wu.shape), const(wd.shape),
        ],
        out_specs=pl.BlockSpec((tm, D_MODEL), lambda i: (i, 0)),
        out_shape=jax.ShapeDtypeStruct((n, D_MODEL), f32),
        scratch_shapes=[pltpu.VMEM((tm, D_FF), bf16)],
        compiler_params=_cparams(("parallel",)),
        name="ffn",
    )(x2, ya, yb, wo, g2, wg, wu, wd)


def _overlap_t(ncp):
    nsel = ncp // (SEL_BLOCK // CMP_STRIDE)
    cs = np.arange(ncp) * CMP_STRIDE
    ss = np.arange(nsel) * SEL_BLOCK
    ov = ((cs[None, :] < ss[:, None] + SEL_BLOCK) & (cs[None, :] + CMP_BLOCK > ss[:, None])).astype(np.float32)
    ov[:, ncp - 1] = 0.0
    out = np.zeros((LANES, ncp), np.float32)
    out[:nsel] = ov
    return jnp.asarray(out)


def _layer(x, norm1_g, w_in, q_g, kc_g, ks_g, kw_g, cmp_pos, w_ck1, w_ck2, w_cv1, w_cv2, conv_w, conv_b,
           w_mq, w_mk, b_i, b_f, mlstm_norm_g, mlstm_skip, w_out, norm2_g, w_gate, w_up, w_down):
    b, t, d = x.shape
    n = b * t
    x2 = x.reshape(n, d)

    o_gate = NSA_WIDTH + 6 * KV_WIDTH
    o_u = o_gate + 3 * NSA_HEADS
    o_if = o_u + 3 * MLSTM_WIDTH
    w_perm = jnp.concatenate([
        w_in[:, :o_gate], w_in[:, o_u:o_if], w_in[:, o_gate:o_u], w_in[:, o_if:],
        jnp.zeros((d, LANES - 3 * NSA_HEADS - 2 * MLSTM_HEADS), w_in.dtype)], axis=1).astype(bf16)
    q2, kv2, u2, vm2, op2, sm2 = _inproj(x2, norm1_g.reshape(1, d), w_perm)

    kv3 = kv2.reshape(b, t, 6 * KV_WIDTH)
    ks, vs, kw, vw = _kvprep(kv3, jnp.tile(ks_g, 2).reshape(1, LANES), jnp.tile(kw_g, 2).reshape(1, LANES))
    nseg = t // CMP_STRIDE
    seg_w = CMP_STRIDE * NSA_HEAD_DIM

    def segments(cols):
        return cols.reshape(b, t, NSA_KV_HEADS, NSA_HEAD_DIM).transpose(0, 2, 1, 3).reshape(
            b, NSA_KV_HEADS, nseg, seg_w)

    pos8 = jnp.broadcast_to(cmp_pos.reshape(1, CMP_BLOCK * NSA_HEAD_DIM), (8, CMP_BLOCK * NSA_HEAD_DIM))
    kcd, vcl = _compress(segments(kv3[..., 0:KV_WIDTH]), segments(kv3[..., KV_WIDTH:2 * KV_WIDTH]),
                         w_ck1, jnp.tile(w_ck2, (1, 2)), w_cv1, jnp.tile(w_cv2, (1, 2)), pos8,
                         jnp.tile(kc_g, 2).reshape(1, LANES))
    slopes = jnp.exp2(-8.0 * (jnp.arange(NSA_HEADS, dtype=f32) + 1.0) / NSA_HEADS)
    y_nsa = _nsa(slopes, q2.reshape(b, t, NSA_WIDTH), sm2.reshape(b, t, LANES),
                 jnp.tile(q_g, NSA_GROUP).reshape(1, NSA_GROUP * NSA_HEAD_DIM),
                 kcd, vcl, _overlap_t(nseg), ks, vs, kw, vw)

    gi4 = sm2[:, 3 * NSA_HEADS:3 * NSA_HEADS + MLSTM_HEADS].reshape(b, t, MLSTM_HEADS).transpose(0, 2, 1)
    gf4 = sm2[:, 3 * NSA_HEADS + MLSTM_HEADS:3 * NSA_HEADS + 2 * MLSTM_HEADS].reshape(
        b, t, MLSTM_HEADS).transpose(0, 2, 1)
    y_mem = _mlstm(b_i, b_f, u2.reshape(b, t, MLSTM_WIDTH), vm2.reshape(b, t, MLSTM_WIDTH),
                   op2.reshape(b, t, MLSTM_WIDTH), gi4.reshape(b, MLSTM_HEADS, 1, t),
                   gf4.reshape(b, MLSTM_HEADS, 1, t), conv_w, conv_b.reshape(1, MLSTM_WIDTH), w_mq, w_mk,
                   mlstm_norm_g.reshape(MLSTM_HEADS, 1, MLSTM_HEAD_DIM), mlstm_skip.reshape(1, MLSTM_WIDTH))

    out = _ffn(x2, y_nsa.reshape(n, NSA_WIDTH), y_mem.reshape(n, MLSTM_WIDTH), w_out.astype(bf16),
               norm2_g.reshape(1, d), w_gate.astype(bf16), w_up.astype(bf16), w_down.astype(bf16))
    return out.reshape(b, t, d)


def kernel(x, norm1_g, w_in, q_norm_g, kc_norm_g, ks_norm_g, kw_norm_g, cmp_pos, w_ck1, w_ck2, w_cv1, w_cv2,
           conv_w, conv_b, w_mq, w_mk, b_i, b_f, mlstm_norm_g, mlstm_skip, w_out, norm2_g, w_gate, w_up, w_down):
    depth = norm1_g.shape[0]
    for l in range(depth):
        x = _layer(x, norm1_g[l], w_in[l], q_norm_g[l], kc_norm_g[l], ks_norm_g[l], kw_norm_g[l], cmp_pos[l],
                   w_ck1[l], w_ck2[l], w_cv1[l], w_cv2[l], conv_w[l], conv_b[l], w_mq[l], w_mk[l], b_i[l], b_f[l],
                   mlstm_norm_g[l], mlstm_skip[l], w_out[l], norm2_g[l], w_gate[l], w_up[l], w_down[l])
    return x
```

```python
import functools
import math

import numpy as np
import jax
import jax.numpy as jnp
from jax import lax
from jax.experimental import pallas as pl
from jax.experimental.pallas import tpu as pltpu

f32 = jnp.float32
bf16 = jnp.bfloat16

D_MODEL = 1024
NSA_HEADS = 8
NSA_KV_HEADS = 2
NSA_HEAD_DIM = 64
NSA_GROUP = NSA_HEADS // NSA_KV_HEADS
CMP_BLOCK = 32
CMP_STRIDE = 16
CMP_HIDDEN = 256
SEL_BLOCK = 64
SEL_TOPN = 16
WINDOW = 512
MLSTM_HEADS = 4
MLSTM_HEAD_DIM = 128
CONV_WIDTH = 4
NSA_WIDTH = NSA_HEADS * NSA_HEAD_DIM
MLSTM_WIDTH = MLSTM_HEADS * MLSTM_HEAD_DIM
KV_WIDTH = NSA_KV_HEADS * NSA_HEAD_DIM
D_FF = -(-8 * D_MODEL // (3 * 256)) * 256
NORM_EPS = 1e-6
NEG = -1e30
FORCE = 1e9

LANES = 128
HALF = 64
VMEM_LIMIT = 56 * 1024 * 1024

TM_PROJ = 512
TM_PREP = 512
TQ = 256
TK = 512
MLSTM_L = 128
TM_FFN = 512
FF_CHUNK = 256

HI = lax.Precision.HIGHEST
NT = (((1,), (1,)), ((), ()))


def _cparams(sem):
    return pltpu.CompilerParams(dimension_semantics=sem, vmem_limit_bytes=VMEM_LIMIT)


def _split3(x):
    x1 = x.astype(bf16)
    r1 = x - x1.astype(f32)
    x2 = r1.astype(bf16)
    x3 = (r1 - x2.astype(f32)).astype(bf16)
    return x1, x2, x3


PROJ_SPLITS = (NSA_WIDTH, 6 * KV_WIDTH, MLSTM_WIDTH, MLSTM_WIDTH, MLSTM_WIDTH, LANES)


def _inproj_kernel(x_ref, g_ref, w_ref, *out_refs):
    x = x_ref[...]
    h = x * lax.rsqrt(jnp.mean(x * x, axis=-1, keepdims=True) + NORM_EPS) * g_ref[...]
    hb = h.astype(bf16)
    off = 0
    for o_ref, width in zip(out_refs, PROJ_SPLITS):
        o_ref[...] = jnp.dot(hb, w_ref[:, off:off + width], preferred_element_type=f32)
        off += width


def _inproj(x2, g1, w_perm):
    n = x2.shape[0]
    wtot = sum(PROJ_SPLITS)
    return pl.pallas_call(
        _inproj_kernel,
        grid=(n // TM_PROJ,),
        in_specs=[
            pl.BlockSpec((TM_PROJ, D_MODEL), lambda i: (i, 0)),
            pl.BlockSpec((1, D_MODEL), lambda i: (0, 0)),
            pl.BlockSpec((D_MODEL, wtot), lambda i: (0, 0)),
        ],
        out_specs=[pl.BlockSpec((TM_PROJ, w), lambda i: (i, 0)) for w in PROJ_SPLITS],
        out_shape=[jax.ShapeDtypeStruct((n, w), f32) for w in PROJ_SPLITS],
        compiler_params=_cparams(("parallel",)),
        name="inproj",
    )(x2, g1, w_perm)


def _pair_norm(x, gain, lo):
    sq = x * x
    s_lo = jnp.sum(jnp.where(lo, sq, 0.0), axis=-1, keepdims=True)
    s_hi = jnp.sum(jnp.where(lo, 0.0, sq), axis=-1, keepdims=True)
    inv = jnp.where(lo, lax.rsqrt(s_lo / HALF + NORM_EPS), lax.rsqrt(s_hi / HALF + NORM_EPS))
    return x * inv * gain


def _kvprep_kernel(kv_ref, ksg_ref, kwg_ref, ks_ref, vs_ref, kw_ref, vw_ref):
    i = pl.program_id(1)
    tm = kv_ref.shape[1]
    lane = lax.broadcasted_iota(jnp.int32, (tm, LANES), 1)
    lo = lane < HALF
    row = i * tm + lax.broadcasted_iota(jnp.int32, (tm, LANES), 0)
    code = jnp.where(lax.shift_right_logical(row, 6) == (lane & (HALF - 1)), 1.0, 0.0)

    def put(x, o_ref, fill):
        o_ref[0, 0] = jnp.where(lo, x, fill).astype(bf16)
        o_ref[0, 1] = jnp.where(lo, pltpu.roll(x, HALF, 1), fill).astype(bf16)

    put(_pair_norm(kv_ref[0, :, 2 * KV_WIDTH:3 * KV_WIDTH], ksg_ref[...], lo), ks_ref, code)
    put(kv_ref[0, :, 3 * KV_WIDTH:4 * KV_WIDTH], vs_ref, 1.0)
    put(_pair_norm(kv_ref[0, :, 4 * KV_WIDTH:5 * KV_WIDTH], kwg_ref[...], lo), kw_ref, 0.0)
    put(kv_ref[0, :, 5 * KV_WIDTH:6 * KV_WIDTH], vw_ref, 1.0)


def _kvprep(kv3, ksg2, kwg2):
    b, t, _ = kv3.shape
    tm = min(TM_PREP, t)
    out_spec = pl.BlockSpec((1, NSA_KV_HEADS, tm, LANES), lambda bi, i: (bi, 0, i, 0))
    out_shape = jax.ShapeDtypeStruct((b, NSA_KV_HEADS, t, LANES), bf16)
    return pl.pallas_call(
        _kvprep_kernel,
        grid=(b, t // tm),
        in_specs=[
            pl.BlockSpec((1, tm, 6 * KV_WIDTH), lambda bi, i: (bi, i, 0)),
            pl.BlockSpec((1, LANES), lambda bi, i: (0, 0)),
            pl.BlockSpec((1, LANES), lambda bi, i: (0, 0)),
        ],
        out_specs=[out_spec] * 4,
        out_shape=[out_shape] * 4,
        compiler_params=_cparams(("parallel", "parallel")),
        name="kvprep",
    )(kv3, ksg2, kwg2)


def _gelu_tanh(x):
    return 0.5 * x * (1.0 + jnp.tanh(math.sqrt(2.0 / math.pi) * (x + 0.044715 * (x * x * x))))


def _compress_kernel(ak_ref, av_ref, wk1_ref, wk2_ref, wv1_ref, wv2_ref, pos_ref, kcg_ref,
                     kcd_ref, vcl_ref):
    nseg = ak_ref.shape[2]
    half_in = CMP_STRIDE * NSA_HEAD_DIM

    def branch(a, w1_ref, w2_ref):
        top = jnp.dot(a, w1_ref[0:half_in, :], precision=HI, preferred_element_type=f32)
        bot = jnp.dot(a, w1_ref[half_in:2 * half_in, :], precision=HI, preferred_element_type=f32)
        pos_term = jnp.dot(pos_ref[...], w1_ref[...], precision=HI, preferred_element_type=f32)[0:1]
        hid = top + pltpu.roll(bot, nseg - 1, 0) + pos_term
        return jnp.dot(_gelu_tanh(hid), w2_ref[...], precision=HI, preferred_element_type=f32)

    kc = branch(ak_ref[0, 0], wk1_ref, wk2_ref)
    kc = kc * lax.rsqrt(jnp.mean(kc * kc, axis=-1, keepdims=True) + NORM_EPS) * kcg_ref[...]
    lane = lax.broadcasted_iota(jnp.int32, kc.shape, 1)
    k_hi = kc.astype(bf16)
    k_lo = jnp.where(lane < HALF, kc - k_hi.astype(f32), 0.0).astype(bf16)
    kcd_ref[0, 0] = jnp.concatenate([k_hi, k_lo], axis=1)

    vc = branch(av_ref[0, 0], wv1_ref, wv2_ref)
    vcl_ref[0, 0] = jnp.where(lane < HALF, vc, 0.0).astype(bf16)


def _compress(ak, av, wk1, wk2d, wv1, wv2d, pos8, kcg2):
    b, g, nseg, seg_w = ak.shape
    a_spec = pl.BlockSpec((1, 1, nseg, seg_w), lambda bi, gi: (bi, gi, 0, 0))
    o_spec = pl.BlockSpec((1, 1, nseg, LANES), lambda bi, gi: (bi, gi, 0, 0))

    def full(shape):
        return pl.BlockSpec(shape, lambda bi, gi: (0,) * len(shape))

    return pl.pallas_call(
        _compress_kernel,
        grid=(b, g),
        in_specs=[a_spec, a_spec, full(wk1.shape), full(wk2d.shape), full(wv1.shape), full(wv2d.shape),
                  full(pos8.shape), full(kcg2.shape)],
        out_specs=[pl.BlockSpec((1, 1, nseg, 2 * LANES), lambda bi, gi: (bi, gi, 0, 0)), o_spec],
        out_shape=[jax.ShapeDtypeStruct((b, g, nseg, 2 * LANES), bf16),
                   jax.ShapeDtypeStruct((b, g, nseg, LANES), bf16)],
        compiler_params=_cparams(("parallel", "parallel")),
        name="compress",
    )(ak, av, wk1, wk2d, wv1, wv2d, pos8, kcg2)


def _nsa_kernel(slopes_ref, q_ref, sm_ref, qg_ref, kcd_ref, vcl_ref, ovt_ref,
                ks_ref, vs_ref, kw_ref, vw_ref, y_ref, qa_scr, m_scr, acc_scr):
    g = pl.program_id(1)
    qi = pl.program_id(2)
    q0 = qi * TQ
    t_len = ks_ref.shape[2]
    ncp = kcd_ref.shape[2]
    nsel = ncp // (SEL_BLOCK // CMP_STRIDE)
    rows = NSA_GROUP * TQ
    log2e = math.log2(math.e)
    scale = NSA_HEAD_DIM ** -0.5 * log2e

    lane = lax.broadcasted_iota(jnp.int32, (TQ, LANES), 1)
    lo = lane < HALF
    slopes = [slopes_ref[g * NSA_GROUP + r] * log2e for r in range(NSA_GROUP)]

    def per_head(x, fn):
        return jnp.concatenate([fn(r, x[r * TQ:(r + 1) * TQ]) for r in range(NSA_GROUP)], axis=0)

    q = q_ref[0]
    qn = []
    for c in range(NSA_GROUP // 2):
        pair = _pair_norm(q[:, c * LANES:(c + 1) * LANES], qg_ref[:, c * LANES:(c + 1) * LANES], lo) * scale
        qn += [pair, pltpu.roll(pair, HALF, 1)]
    qz = jnp.concatenate([jnp.where(lo, x, 0.0) for x in qn], axis=0)

    cidx = lax.broadcasted_iota(jnp.int32, (TQ, ncp), 1)
    tpos_c = q0 + lax.broadcasted_iota(jnp.int32, (TQ, ncp), 0)
    dist_c = tpos_c - (cidx * CMP_STRIDE + (CMP_BLOCK - 1))
    valid_c = jnp.logical_and(dist_c >= 0, cidx < ncp - 1)
    dist_cf = dist_c.astype(f32)
    q_hi = qz.astype(bf16)
    q_lo = pltpu.roll(qz - q_hi.astype(f32), HALF, 1).astype(bf16)
    q3 = jnp.concatenate([q_hi + q_lo, q_hi], axis=1)
    s_c = lax.dot_general(q3, kcd_ref[0, 0], NT, preferred_element_type=f32)
    s_c = per_head(s_c, lambda r, x: jnp.where(valid_c, x - slopes[r] * dist_cf, NEG))
    e_c = jnp.exp2(s_c - jnp.max(s_c, axis=-1, keepdims=True))
    p_c = e_c / jnp.sum(e_c, axis=-1, keepdims=True)
    p_c = per_head(p_c, lambda r, x: jnp.where(valid_c, x, 0.0))
    p_sum = p_c[0:TQ]
    for r in range(1, NSA_GROUP):
        p_sum = p_sum + p_c[r * TQ:(r + 1) * TQ]
    o_cmp = jnp.dot(p_c.astype(bf16), vcl_ref[0, 0], preferred_element_type=f32)

    band = WINDOW + TQ
    kb = pl.multiple_of(jnp.maximum(q0 - WINDOW, 0), TQ)
    s_w = lax.dot_general(qz.astype(bf16), kw_ref[0, 0, pl.ds(kb, band), :], NT, preferred_element_type=f32)
    krel_w = (kb - q0 + lax.broadcasted_iota(jnp.int32, (1, band), 1)).astype(f32)
    dist_w = (q0 - kb + lax.broadcasted_iota(jnp.int32, (TQ, band), 0)
              - lax.broadcasted_iota(jnp.int32, (TQ, band), 1))
    valid_w = jnp.logical_and(dist_w >= 0, dist_w < WINDOW)
    s_w = per_head(s_w, lambda r, x: jnp.where(valid_w, x + slopes[r] * krel_w, NEG))
    p_w = jnp.exp2(s_w - jnp.max(s_w, axis=-1, keepdims=True))
    acc_w = jnp.dot(p_w.astype(bf16), vw_ref[0, 0, pl.ds(kb, band), :], preferred_element_type=f32)
    o_win = acc_w / pltpu.roll(acc_w, HALF, 1)

    imp = lax.dot_general(ovt_ref[...], jnp.concatenate(_split3(p_sum), axis=1), NT,
                          preferred_element_type=f32)[0:nsel]
    jrow = lax.broadcasted_iota(jnp.int32, (nsel, TQ), 0)
    tcol = q0 + lax.broadcasted_iota(jnp.int32, (nsel, TQ), 1)
    forced = jnp.logical_or(jrow == lax.shift_right_logical(tcol, 6), jrow == 0)
    future = jrow * SEL_BLOCK > tcol
    work = jnp.where(forced, FORCE, jnp.where(future, -FORCE, imp))
    jrow_f = jrow.astype(f32)
    bias_t = jnp.full((nsel, TQ), NEG, f32)
    for _ in range(min(SEL_TOPN, nsel)):
        best = jnp.max(work, axis=0, keepdims=True)
        first = jnp.min(jnp.where(work == best, jrow_f, float(nsel)), axis=0, keepdims=True)
        hit = jrow_f == first
        bias_t = jnp.where(hit, 0.0, bias_t)
        work = jnp.where(hit, -3e38, work)
    if nsel < HALF:
        bias_t = jnp.concatenate([bias_t, jnp.full((HALF - nsel, TQ), NEG, f32)], axis=0)
    bias = jnp.concatenate([bias_t, bias_t], axis=0).T

    qa_scr[...] = jnp.concatenate([jnp.where(lo, x, bias) for x in qn], axis=0).astype(bf16)

    m_scr[...] = jnp.full(m_scr.shape, NEG, f32)
    acc_scr[...] = jnp.zeros(acc_scr.shape, f32)

    def sel_tile(kt, causal):
        k0 = pl.multiple_of(kt * TK, TK)
        s = lax.dot_general(qa_scr[...], ks_ref[0, 0, pl.ds(k0, TK), :], NT, preferred_element_type=f32)
        krel = (k0 - q0 + lax.broadcasted_iota(jnp.int32, (1, TK), 1)).astype(f32)
        if causal:
            ahead = (k0 - q0 + lax.broadcasted_iota(jnp.int32, (TQ, TK), 1)
                     > lax.broadcasted_iota(jnp.int32, (TQ, TK), 0))
            s = per_head(s, lambda r, x: jnp.where(ahead, NEG, x + slopes[r] * krel))
        else:
            s = per_head(s, lambda r, x: x + slopes[r] * krel)
        m_old = m_scr[...]
        m_new = jnp.maximum(m_old, jnp.max(s, axis=-1, keepdims=True))
        p = jnp.exp2(s - m_new[:, 0:1])
        acc_scr[...] = jnp.exp2(m_old - m_new) * acc_scr[...] + jnp.dot(
            p.astype(bf16), vs_ref[0, 0, pl.ds(k0, TK), :], preferred_element_type=f32)
        m_scr[...] = m_new

    n_full = q0 // TK

    def sel_body(kt, carry):
        sel_tile(kt, False)
        return carry

    lax.fori_loop(0, n_full, sel_body, 0)
    sel_tile(n_full, True)
    acc_s = acc_scr[...]
    o_slc = acc_s / pltpu.roll(acc_s, HALF, 1)

    sig = jax.nn.sigmoid(sm_ref[0])

    def gate(r, branch):
        col = (g * NSA_GROUP + r) * 3 + branch
        return jnp.sum(jnp.where(lane == col, sig, 0.0), axis=-1, keepdims=True)

    def mix(r):
        sl = slice(r * TQ, (r + 1) * TQ)
        return gate(r, 0) * o_cmp[sl] + gate(r, 1) * o_slc[sl] + gate(r, 2) * o_win[sl]

    for c in range(NSA_GROUP // 2):
        y_ref[0, :, c * LANES:(c + 1) * LANES] = jnp.where(lo, mix(2 * c), pltpu.roll(mix(2 * c + 1), HALF, 1))


def _nsa(slopes, q3, sm3, qg2, kcd, vcl, ovt, ks, vs, kw, vw):
    b, t, _ = q3.shape
    gw = NSA_GROUP * NSA_HEAD_DIM
    ncp = kcd.shape[2]

    def kv_spec(rows):
        return pl.BlockSpec((1, 1, rows, LANES), lambda bi, gi, qi: (bi, gi, 0, 0))

    return pl.pallas_call(
        _nsa_kernel,
        grid=(b, NSA_KV_HEADS, t // TQ),
        in_specs=[
            pl.BlockSpec(memory_space=pltpu.SMEM),
            pl.BlockSpec((1, TQ, gw), lambda bi, gi, qi: (bi, qi, gi)),
            pl.BlockSpec((1, TQ, LANES), lambda bi, gi, qi: (bi, qi, 0)),
            pl.BlockSpec((1, gw), lambda bi, gi, qi: (0, 0)),
            pl.BlockSpec((1, 1, ncp, 2 * LANES), lambda bi, gi, qi: (bi, gi, 0, 0)), kv_spec(ncp),
            pl.BlockSpec(ovt.shape, lambda bi, gi, qi: (0, 0)),
            kv_spec(t), kv_spec(t), kv_spec(t), kv_spec(t),
        ],
        out_specs=pl.BlockSpec((1, TQ, gw), lambda bi, gi, qi: (bi, qi, gi)),
        out_shape=jax.ShapeDtypeStruct((b, t, NSA_WIDTH), f32),
        scratch_shapes=[
            pltpu.VMEM((NSA_GROUP * TQ, LANES), bf16),
            pltpu.VMEM((NSA_GROUP * TQ, LANES), f32),
            pltpu.VMEM((NSA_GROUP * TQ, LANES), f32),
        ],
        compiler_params=_cparams(("parallel", "parallel", "arbitrary")),
        name="nsa",
    )(slopes, q3, sm3, qg2, kcd, vcl, ovt, ks, vs, kw, vw)


def _log_sigmoid(x):
    return jnp.minimum(x, 0.0) - jnp.log1p(jnp.exp(-jnp.abs(x)))


def _mlstm_kernel(bi_ref, bf_ref, u_ref, v_ref, op_ref, gi_ref, gf_ref, cw_ref, cb_ref, wq_ref, wk_ref,
                  ng_ref, sk_ref, y_ref, uc_scr, q_scr, kt_scr, ct_scr, m_scr):
    h = pl.program_id(1)
    t = u_ref.shape[1]
    L = MLSTM_L
    dm = MLSTM_HEAD_DIM

    x = u_ref[0]
    rows = lax.broadcasted_iota(jnp.int32, (t, dm), 0)
    acc = x * cw_ref[CONV_WIDTH - 1:CONV_WIDTH, :]
    for s in range(1, CONV_WIDTH):
        xs = jnp.where(rows >= s, pltpu.roll(x, s, 0), 0.0)
        acc = acc + xs * cw_ref[CONV_WIDTH - 1 - s:CONV_WIDTH - s, :]
    uc = acc + cb_ref[...]
    uc = uc * jax.nn.sigmoid(uc)
    uc_scr[...] = uc
    ucb = uc.astype(bf16)
    q_scr[...] = jnp.dot(ucb, wq_ref[0].astype(bf16), preferred_element_type=f32).astype(bf16)
    k = jnp.dot(ucb, wk_ref[0].astype(bf16), preferred_element_type=f32) * (dm ** -0.5)
    kt_scr[...] = k.T

    ct_scr[...] = jnp.zeros(ct_scr.shape, f32)
    m_scr[...] = jnp.zeros(m_scr.shape, f32)

    li_ = lax.broadcasted_iota(jnp.int32, (L, L), 0)
    si_ = lax.broadcasted_iota(jnp.int32, (L, L), 1)
    causal = si_ <= li_
    strict_lower = jnp.where(li_ > si_, 1.0, 0.0).astype(bf16)
    strict_lower3 = jnp.concatenate([strict_lower] * 3, axis=0)
    b_i = bi_ref[h]
    b_f = bf_ref[h]
    ones_v = jnp.ones((L, dm), f32)

    def chunk(c, carry):
        r0 = pl.multiple_of(c * L, L)
        qc = q_scr[pl.ds(r0, L), :]
        ktc = kt_scr[:, pl.ds(r0, L)]
        vaug = jnp.concatenate([v_ref[0, pl.ds(r0, L), :], ones_v], axis=1).astype(bf16)
        log_i = gi_ref[0, 0, :, pl.ds(r0, L)] + b_i
        log_f = _log_sigmoid(gf_ref[0, 0, :, pl.ds(r0, L)] + b_f)
        a = jnp.where(causal, log_f, 0.0)
        d0 = jnp.dot(jnp.concatenate(_split3(a), axis=1), strict_lower3, preferred_element_type=f32)
        b_col = jnp.sum(a, axis=-1, keepdims=True)
        dmat = jnp.where(causal, d0 + log_i, NEG)
        m_prev = m_scr[0:1, 0:1]
        m_inter = b_col + m_prev
        m_out = jnp.maximum(m_inter, jnp.max(dmat, axis=-1, keepdims=True))
        p = jnp.dot(qc, ktc.astype(bf16), preferred_element_type=f32) * jnp.exp(dmat - m_out)
        inter = jnp.exp(m_inter - m_out)
        ct = ct_scr[...]
        xo = inter * jnp.dot(qc, ct.astype(bf16), preferred_element_type=f32) + jnp.dot(
            p.astype(bf16), vaug, preferred_element_type=f32)
        num = xo[:, 0:dm]
        den = xo[:, dm:2 * dm]
        hh = num / jnp.maximum(jnp.abs(den), jnp.exp(-m_out))
        hh = hh * jax.nn.sigmoid(op_ref[0, pl.ds(r0, L), :])
        hh = hh * lax.rsqrt(jnp.mean(hh * hh, axis=-1, keepdims=True) + NORM_EPS) * ng_ref[0]
        y_ref[0, pl.ds(r0, L), :] = hh + sk_ref[...] * uc_scr[pl.ds(r0, L), :]

        g_sum = jnp.sum(log_f, axis=-1, keepdims=True)
        w_end = d0[L - 1:L, :] + log_i
        m_new = jnp.maximum(g_sum + m_prev, jnp.max(w_end, axis=-1, keepdims=True))
        decay = jnp.exp(g_sum + m_prev - m_new)
        w = jnp.exp(w_end - m_new)
        ct_scr[...] = decay * ct + jnp.dot((ktc * w).astype(bf16), vaug, preferred_element_type=f32)
        m_scr[...] = jnp.broadcast_to(m_new, m_scr.shape)
        return carry

    lax.fori_loop(0, t // L, chunk, 0)


def _mlstm(b_i, b_f, u3, v3, op3, gi4, gf4, cw, cb2, wq, wk, ng3, sk2):
    b, t, _ = u3.shape
    dm = MLSTM_HEAD_DIM
    seq = pl.BlockSpec((1, t, dm), lambda bi, hi: (bi, 0, hi))
    gate = pl.BlockSpec((1, 1, 1, t), lambda bi, hi: (bi, hi, 0, 0))
    smem = pl.BlockSpec(memory_space=pltpu.SMEM)
    return pl.pallas_call(
        _mlstm_kernel,
        grid=(b, MLSTM_HEADS),
        in_specs=[
            smem, smem, seq, seq, seq, gate, gate,
            pl.BlockSpec((CONV_WIDTH, dm), lambda bi, hi: (0, hi)),
            pl.BlockSpec((1, dm), lambda bi, hi: (0, hi)),
            pl.BlockSpec((1, dm, dm), lambda bi, hi: (hi, 0, 0)),
            pl.BlockSpec((1, dm, dm), lambda bi, hi: (hi, 0, 0)),
            pl.BlockSpec((1, 1, dm), lambda bi, hi: (hi, 0, 0)),
            pl.BlockSpec((1, dm), lambda bi, hi: (0, hi)),
        ],
        out_specs=seq,
        out_shape=jax.ShapeDtypeStruct((b, t, MLSTM_WIDTH), f32),
        scratch_shapes=[
            pltpu.VMEM((t, dm), f32),
            pltpu.VMEM((t, dm), bf16),
            pltpu.VMEM((dm, t), f32),
            pltpu.VMEM((dm, 2 * dm), f32),
            pltpu.VMEM((8, LANES), f32),
        ],
        compiler_params=_cparams(("parallel", "parallel")),
        name="mlstm",
    )(b_i, b_f, u3, v3, op3, gi4, gf4, cw, cb2, wq, wk, ng3, sk2)


def _ffn_kernel(x_ref, ya_ref, yb_ref, wo_ref, g2_ref, wg_ref, wu_ref, wd_ref, o_ref, act_scr):
    x1 = (x_ref[...]
          + jnp.dot(ya_ref[...].astype(bf16), wo_ref[0:NSA_WIDTH, :], preferred_element_type=f32)
          + jnp.dot(yb_ref[...].astype(bf16), wo_ref[NSA_WIDTH:NSA_WIDTH + MLSTM_WIDTH, :],
                    preferred_element_type=f32))
    h2 = x1 * lax.rsqrt(jnp.mean(x1 * x1, axis=-1, keepdims=True) + NORM_EPS) * g2_ref[...]
    h2b = h2.astype(bf16)
    for c in range(D_FF // FF_CHUNK):
        cols = slice(c * FF_CHUNK, (c + 1) * FF_CHUNK)
        gt = jnp.dot(h2b, wg_ref[:, cols], preferred_element_type=f32)
        up = jnp.dot(h2b, wu_ref[:, cols], preferred_element_type=f32)
        act_scr[:, cols] = (gt * jax.nn.sigmoid(gt) * up).astype(bf16)
    o_ref[...] = x1 + jnp.dot(act_scr[...], wd_ref[...], preferred_element_type=f32)


def _ffn(x2, ya, yb, wo, g2, wg, wu, wd):
    n = x2.shape[0]
    tm = TM_FFN

    def const(shape):
        return pl.BlockSpec(shape, lambda i: (0, 0), pipeline_mode=pl.Buffered(1))

    return pl.pallas_call(
        _ffn_kernel,
        grid=(n // tm,),
        in_specs=[
            pl.BlockSpec((tm, D_MODEL), lambda i: (i, 0)),
            pl.BlockSpec((tm, NSA_WIDTH), lambda i: (i, 0)),
            pl.BlockSpec((tm, MLSTM_WIDTH), lambda i: (i, 0)),
            const(wo.shape), const(g2.shape), const(wg.shape), const(wu.shape), const(wd.shape),
        ],
        out_specs=pl.BlockSpec((tm, D_MODEL), lambda i: (i, 0)),
        out_shape=jax.ShapeDtypeStruct((n, D_MODEL), f32),
        scratch_shapes=[pltpu.VMEM((tm, D_FF), bf16)],
        compiler_params=_cparams(("parallel",)),
        name="ffn",
    )(x2, ya, yb, wo, g2, wg, wu, wd)


def _overlap_t(ncp):
    nsel = ncp // (SEL_BLOCK // CMP_STRIDE)
    cs = np.arange(ncp) * CMP_STRIDE
    ss = np.arange(nsel) * SEL_BLOCK
    ov = ((cs[None, :] < ss[:, None] + SEL_BLOCK) & (cs[None, :] + CMP_BLOCK > ss[:, None])).astype(np.float32)
    ov[:, ncp - 1] = 0.0
    out = np.zeros((LANES, ncp), np.float32)
    out[:nsel] = ov
    return jnp.asarray(np.tile(out, (1, 3)), dtype=bf16)


def _layer(x, norm1_g, w_in, q_g, kc_g, ks_g, kw_g, cmp_pos, w_ck1, w_ck2, w_cv1, w_cv2, conv_w, conv_b,
           w_mq, w_mk, b_i, b_f, mlstm_norm_g, mlstm_skip, w_out, norm2_g, w_gate, w_up, w_down):
    b, t, d = x.shape
    n = b * t
    x2 = x.reshape(n, d)

    o_gate = NSA_WIDTH + 6 * KV_WIDTH
    o_u = o_gate + 3 * NSA_HEADS
    o_if = o_u + 3 * MLSTM_WIDTH
    w_perm = jnp.concatenate([
        w_in[:, :o_gate], w_in[:, o_u:o_if], w_in[:, o_gate:o_u], w_in[:, o_if:],
        jnp.zeros((d, LANES - 3 * NSA_HEADS - 2 * MLSTM_HEADS), w_in.dtype)], axis=1).astype(bf16)
    q2, kv2, u2, vm2, op2, sm2 = _inproj(x2, norm1_g.reshape(1, d), w_perm)

    kv3 = kv2.reshape(b, t, 6 * KV_WIDTH)
    ks, vs, kw, vw = _kvprep(kv3, jnp.tile(ks_g, 2).reshape(1, LANES), jnp.tile(kw_g, 2).reshape(1, LANES))
    nseg = t // CMP_STRIDE
    seg_w = CMP_STRIDE * NSA_HEAD_DIM

    def segments(cols):
        return cols.reshape(b, t, NSA_KV_HEADS, NSA_HEAD_DIM).transpose(0, 2, 1, 3).reshape(
            b, NSA_KV_HEADS, nseg, seg_w)

    pos8 = jnp.broadcast_to(cmp_pos.reshape(1, CMP_BLOCK * NSA_HEAD_DIM), (8, CMP_BLOCK * NSA_HEAD_DIM))
    kcd, vcl = _compress(segments(kv3[..., 0:KV_WIDTH]), segments(kv3[..., KV_WIDTH:2 * KV_WIDTH]),
                         w_ck1, jnp.tile(w_ck2, (1, 2)), w_cv1, jnp.tile(w_cv2, (1, 2)), pos8,
                         jnp.tile(kc_g, 2).reshape(1, LANES))
    slopes = jnp.exp2(-8.0 * (jnp.arange(NSA_HEADS, dtype=f32) + 1.0) / NSA_HEADS)
    y_nsa = _nsa(slopes, q2.reshape(b, t, NSA_WIDTH), sm2.reshape(b, t, LANES),
                 jnp.tile(q_g, NSA_GROUP).reshape(1, NSA_GROUP * NSA_HEAD_DIM),
                 kcd, vcl, _overlap_t(nseg), ks, vs, kw, vw)

    gi4 = sm2[:, 3 * NSA_HEADS:3 * NSA_HEADS + MLSTM_HEADS].reshape(b, t, MLSTM_HEADS).transpose(0, 2, 1)
    gf4 = sm2[:, 3 * NSA_HEADS + MLSTM_HEADS:3 * NSA_HEADS + 2 * MLSTM_HEADS].reshape(
        b, t, MLSTM_HEADS).transpose(0, 2, 1)
    y_mem = _mlstm(b_i, b_f, u2.reshape(b, t, MLSTM_WIDTH), vm2.reshape(b, t, MLSTM_WIDTH),
                   op2.reshape(b, t, MLSTM_WIDTH), gi4.reshape(b, MLSTM_HEADS, 1, t),
                   gf4.reshape(b, MLSTM_HEADS, 1, t), conv_w, conv_b.reshape(1, MLSTM_WIDTH), w_mq, w_mk,
                   mlstm_norm_g.reshape(MLSTM_HEADS, 1, MLSTM_HEAD_DIM), mlstm_skip.reshape(1, MLSTM_WIDTH))

    out = _ffn(x2, y_nsa.reshape(n, NSA_WIDTH), y_mem.reshape(n, MLSTM_WIDTH), w_out.astype(bf16),
               norm2_g.reshape(1, d), w_gate.astype(bf16), w_up.astype(bf16), w_down.astype(bf16))
    return out.reshape(b, t, d)


def kernel(x, norm1_g, w_in, q_norm_g, kc_norm_g, ks_norm_g, kw_norm_g, cmp_pos, w_ck1, w_ck2, w_cv1, w_cv2,
           conv_w, conv_b, w_mq, w_mk, b_i, b_f, mlstm_norm_g, mlstm_skip, w_out, norm2_g, w_gate, w_up, w_down):
    depth = norm1_g.shape[0]
    for l in range(depth):
        x = _layer(x, norm1_g[l], w_in[l], q_norm_g[l], kc_norm_g[l], ks_norm_g[l], kw_norm_g[l], cmp_pos[l],
                   w_ck1[l], w_ck2[l], w_cv1[l], w_cv2[l], conv_w[l], conv_b[l], w_mq[l], w_mk[l], b_i[l], b_f[l],
                   mlstm_norm_g[l], mlstm_skip[l], w_out[l], norm2_g[l], w_gate[l], w_up[l], w_down[l])
    return x
```

```python
import functools
import math

import numpy as np
import jax
import jax.numpy as jnp
from jax import lax
from jax.experimental import pallas as pl
from jax.experimental.pallas import tpu as pltpu

f32 = jnp.float32
bf16 = jnp.bfloat16

D_MODEL = 1024
NSA_HEADS = 8
NSA_KV_HEADS = 2
NSA_HEAD_DIM = 64
NSA_GROUP = NSA_HEADS // NSA_KV_HEADS
CMP_BLOCK = 32
CMP_STRIDE = 16
CMP_HIDDEN = 256
SEL_BLOCK = 64
SEL_TOPN = 16
WINDOW = 512
MLSTM_HEADS = 4
MLSTM_HEAD_DIM = 128
CONV_WIDTH = 4
NSA_WIDTH = NSA_HEADS * NSA_HEAD_DIM
MLSTM_WIDTH = MLSTM_HEADS * MLSTM_HEAD_DIM
KV_WIDTH = NSA_KV_HEADS * NSA_HEAD_DIM
D_FF = -(-8 * D_MODEL // (3 * 256)) * 256
NORM_EPS = 1e-6
NEG = -1e30
FORCE = 1e9

LANES = 128
HALF = 64
VMEM_LIMIT = 56 * 1024 * 1024

TM_PROJ = 512
TM_PREP = 512
TQ = 256
TK = 512
MLSTM_L = 128
MLSTM_UNROLL = 2
TM_FFN = 512
FF_CHUNK = 256

HI = lax.Precision.HIGHEST
NT = (((1,), (1,)), ((), ()))


def _cparams(sem):
    return pltpu.CompilerParams(dimension_semantics=sem, vmem_limit_bytes=VMEM_LIMIT)


def _split3(x):
    x1 = x.astype(bf16)
    r1 = x - x1.astype(f32)
    x2 = r1.astype(bf16)
    x3 = (r1 - x2.astype(f32)).astype(bf16)
    return x1, x2, x3


PROJ_SPLITS = (NSA_WIDTH, 6 * KV_WIDTH, MLSTM_WIDTH, MLSTM_WIDTH, MLSTM_WIDTH, LANES)


def _inproj_kernel(x_ref, g_ref, w_ref, *out_refs):
    x = x_ref[...]
    h = x * lax.rsqrt(jnp.mean(x * x, axis=-1, keepdims=True) + NORM_EPS) * g_ref[...]
    hb = h.astype(bf16)
    off = 0
    for o_ref, width in zip(out_refs, PROJ_SPLITS):
        o_ref[...] = jnp.dot(hb, w_ref[:, off:off + width], preferred_element_type=f32)
        off += width


def _inproj(x2, g1, w_perm):
    n = x2.shape[0]
    wtot = sum(PROJ_SPLITS)
    return pl.pallas_call(
        _inproj_kernel,
        grid=(n // TM_PROJ,),
        in_specs=[
            pl.BlockSpec((TM_PROJ, D_MODEL), lambda i: (i, 0)),
            pl.BlockSpec((1, D_MODEL), lambda i: (0, 0)),
            pl.BlockSpec((D_MODEL, wtot), lambda i: (0, 0)),
        ],
        out_specs=[pl.BlockSpec((TM_PROJ, w), lambda i: (i, 0)) for w in PROJ_SPLITS],
        out_shape=[jax.ShapeDtypeStruct((n, w), f32) for w in PROJ_SPLITS],
        compiler_params=_cparams(("parallel",)),
        name="inproj",
    )(x2, g1, w_perm)


def _pair_norm(x, gain, lo):
    sq = x * x
    s_lo = jnp.sum(jnp.where(lo, sq, 0.0), axis=-1, keepdims=True)
    s_hi = jnp.sum(jnp.where(lo, 0.0, sq), axis=-1, keepdims=True)
    inv = jnp.where(lo, lax.rsqrt(s_lo / HALF + NORM_EPS), lax.rsqrt(s_hi / HALF + NORM_EPS))
    return x * inv * gain


def _kvprep_kernel(kv_ref, ksg_ref, kwg_ref, ks_ref, vs_ref, kw_ref, vw_ref):
    i = pl.program_id(1)
    tm = kv_ref.shape[1]
    lane = lax.broadcasted_iota(jnp.int32, (tm, LANES), 1)
    lo = lane < HALF
    row = i * tm + lax.broadcasted_iota(jnp.int32, (tm, LANES), 0)
    code = jnp.where(lax.shift_right_logical(row, 6) == (lane & (HALF - 1)), 1.0, 0.0)

    def put(x, o_ref, fill):
        o_ref[0, 0] = jnp.where(lo, x, fill).astype(bf16)
        o_ref[0, 1] = jnp.where(lo, pltpu.roll(x, HALF, 1), fill).astype(bf16)

    put(_pair_norm(kv_ref[0, :, 2 * KV_WIDTH:3 * KV_WIDTH], ksg_ref[...], lo), ks_ref, code)
    put(kv_ref[0, :, 3 * KV_WIDTH:4 * KV_WIDTH], vs_ref, 1.0)
    put(_pair_norm(kv_ref[0, :, 4 * KV_WIDTH:5 * KV_WIDTH], kwg_ref[...], lo), kw_ref, 0.0)
    put(kv_ref[0, :, 5 * KV_WIDTH:6 * KV_WIDTH], vw_ref, 1.0)


def _kvprep(kv3, ksg2, kwg2):
    b, t, _ = kv3.shape
    tm = min(TM_PREP, t)
    out_spec = pl.BlockSpec((1, NSA_KV_HEADS, tm, LANES), lambda bi, i: (bi, 0, i, 0))
    out_shape = jax.ShapeDtypeStruct((b, NSA_KV_HEADS, t, LANES), bf16)
    return pl.pallas_call(
        _kvprep_kernel,
        grid=(b, t // tm),
        in_specs=[
            pl.BlockSpec((1, tm, 6 * KV_WIDTH), lambda bi, i: (bi, i, 0)),
            pl.BlockSpec((1, LANES), lambda bi, i: (0, 0)),
            pl.BlockSpec((1, LANES), lambda bi, i: (0, 0)),
        ],
        out_specs=[out_spec] * 4,
        out_shape=[out_shape] * 4,
        compiler_params=_cparams(("parallel", "parallel")),
        name="kvprep",
    )(kv3, ksg2, kwg2)


def _gelu_tanh(x):
    return 0.5 * x * (1.0 + jnp.tanh(math.sqrt(2.0 / math.pi) * (x + 0.044715 * (x * x * x))))


def _compress_kernel(ak_ref, av_ref, wk1_ref, wk2_ref, wv1_ref, wv2_ref, pos_ref, kcg_ref,
                     kcd_ref, vcl_ref):
    nseg = ak_ref.shape[2]
    half_in = CMP_STRIDE * NSA_HEAD_DIM

    def branch(a, w1_ref, w2_ref):
        top = jnp.dot(a, w1_ref[0:half_in, :], precision=HI, preferred_element_type=f32)
        bot = jnp.dot(a, w1_ref[half_in:2 * half_in, :], precision=HI, preferred_element_type=f32)
        pos_term = jnp.dot(pos_ref[...], w1_ref[...], precision=HI, preferred_element_type=f32)[0:1]
        hid = top + pltpu.roll(bot, nseg - 1, 0) + pos_term
        return jnp.dot(_gelu_tanh(hid), w2_ref[...], precision=HI, preferred_element_type=f32)

    kc = branch(ak_ref[0, 0], wk1_ref, wk2_ref)
    kc = kc * lax.rsqrt(jnp.mean(kc * kc, axis=-1, keepdims=True) + NORM_EPS) * kcg_ref[...]
    lane = lax.broadcasted_iota(jnp.int32, kc.shape, 1)
    k_hi = kc.astype(bf16)
    k_lo = jnp.where(lane < HALF, kc - k_hi.astype(f32), 0.0).astype(bf16)
    kcd_ref[0, 0] = jnp.concatenate([k_hi, k_lo], axis=1)

    vc = branch(av_ref[0, 0], wv1_ref, wv2_ref)
    vcl_ref[0, 0] = jnp.where(lane < HALF, vc, 0.0).astype(bf16)


def _compress(ak, av, wk1, wk2d, wv1, wv2d, pos8, kcg2):
    b, g, nseg, seg_w = ak.shape
    a_spec = pl.BlockSpec((1, 1, nseg, seg_w), lambda bi, gi: (bi, gi, 0, 0))
    o_spec = pl.BlockSpec((1, 1, nseg, LANES), lambda bi, gi: (bi, gi, 0, 0))

    def full(shape):
        return pl.BlockSpec(shape, lambda bi, gi: (0,) * len(shape))

    return pl.pallas_call(
        _compress_kernel,
        grid=(b, g),
        in_specs=[a_spec, a_spec, full(wk1.shape), full(wk2d.shape), full(wv1.shape), full(wv2d.shape),
                  full(pos8.shape), full(kcg2.shape)],
        out_specs=[pl.BlockSpec((1, 1, nseg, 2 * LANES), lambda bi, gi: (bi, gi, 0, 0)), o_spec],
        out_shape=[jax.ShapeDtypeStruct((b, g, nseg, 2 * LANES), bf16),
                   jax.ShapeDtypeStruct((b, g, nseg, LANES), bf16)],
        compiler_params=_cparams(("parallel", "parallel")),
        name="compress",
    )(ak, av, wk1, wk2d, wv1, wv2d, pos8, kcg2)


def _nsa_kernel(slopes_ref, q_ref, sm_ref, qg_ref, kcd_ref, vcl_ref, ovt_ref,
                ks_ref, vs_ref, kw_ref, vw_ref, y_ref, qa_scr, m_scr, acc_scr):
    g = pl.program_id(1)
    qi = pl.program_id(2)
    q0 = qi * TQ
    t_len = ks_ref.shape[2]
    ncp = kcd_ref.shape[2]
    nsel = ncp // (SEL_BLOCK // CMP_STRIDE)
    rows = NSA_GROUP * TQ
    log2e = math.log2(math.e)
    scale = NSA_HEAD_DIM ** -0.5 * log2e

    lane = lax.broadcasted_iota(jnp.int32, (TQ, LANES), 1)
    lo = lane < HALF
    slopes = [slopes_ref[g * NSA_GROUP + r] * log2e for r in range(NSA_GROUP)]

    def per_head(x, fn):
        return jnp.concatenate([fn(r, x[r * TQ:(r + 1) * TQ]) for r in range(NSA_GROUP)], axis=0)

    q = q_ref[0]
    qn = []
    for c in range(NSA_GROUP // 2):
        pair = _pair_norm(q[:, c * LANES:(c + 1) * LANES], qg_ref[:, c * LANES:(c + 1) * LANES], lo) * scale
        qn += [pair, pltpu.roll(pair, HALF, 1)]
    qz = jnp.concatenate([jnp.where(lo, x, 0.0) for x in qn], axis=0)

    cidx = lax.broadcasted_iota(jnp.int32, (TQ, ncp), 1)
    tpos_c = q0 + lax.broadcasted_iota(jnp.int32, (TQ, ncp), 0)
    dist_c = tpos_c - (cidx * CMP_STRIDE + (CMP_BLOCK - 1))
    valid_c = jnp.logical_and(dist_c >= 0, cidx < ncp - 1)
    dist_cf = dist_c.astype(f32)
    q_hi = qz.astype(bf16)
    q_lo = pltpu.roll(qz - q_hi.astype(f32), HALF, 1).astype(bf16)
    q3 = jnp.concatenate([q_hi + q_lo, q_hi], axis=1)
    s_c = lax.dot_general(q3, kcd_ref[0, 0], NT, preferred_element_type=f32)
    s_c = per_head(s_c, lambda r, x: jnp.where(valid_c, x - slopes[r] * dist_cf, NEG))
    e_c = jnp.exp2(s_c - jnp.max(s_c, axis=-1, keepdims=True))
    p_c = e_c / jnp.sum(e_c, axis=-1, keepdims=True)
    p_c = per_head(p_c, lambda r, x: jnp.where(valid_c, x, 0.0))
    p_sum = p_c[0:TQ]
    for r in range(1, NSA_GROUP):
        p_sum = p_sum + p_c[r * TQ:(r + 1) * TQ]
    o_cmp = jnp.dot(p_c.astype(bf16), vcl_ref[0, 0], preferred_element_type=f32)

    band = WINDOW + TQ
    kb = pl.multiple_of(jnp.maximum(q0 - WINDOW, 0), TQ)
    s_w = lax.dot_general(qz.astype(bf16), kw_ref[0, 0, pl.ds(kb, band), :], NT, preferred_element_type=f32)
    krel_w = (kb - q0 + lax.broadcasted_iota(jnp.int32, (1, band), 1)).astype(f32)
    dist_w = (q0 - kb + lax.broadcasted_iota(jnp.int32, (TQ, band), 0)
              - lax.broadcasted_iota(jnp.int32, (TQ, band), 1))
    valid_w = jnp.logical_and(dist_w >= 0, dist_w < WINDOW)
    s_w = per_head(s_w, lambda r, x: jnp.where(valid_w, x + slopes[r] * krel_w, NEG))
    p_w = jnp.exp2(s_w - jnp.max(s_w, axis=-1, keepdims=True))
    acc_w = jnp.dot(p_w.astype(bf16), vw_ref[0, 0, pl.ds(kb, band), :], preferred_element_type=f32)
    o_win = acc_w / pltpu.roll(acc_w, HALF, 1)

    imp = lax.dot_general(ovt_ref[...], jnp.concatenate(_split3(p_sum), axis=1), NT,
                          preferred_element_type=f32)[0:nsel]
    jrow = lax.broadcasted_iota(jnp.int32, (nsel, TQ), 0)
    tcol = q0 + lax.broadcasted_iota(jnp.int32, (nsel, TQ), 1)
    forced = jnp.logical_or(jrow == lax.shift_right_logical(tcol, 6), jrow == 0)
    future = jrow * SEL_BLOCK > tcol
    work = jnp.where(forced, FORCE, jnp.where(future, -FORCE, imp))
    jrow_f = jrow.astype(f32)
    bias_t = jnp.full((nsel, TQ), NEG, f32)
    for _ in range(min(SEL_TOPN, nsel)):
        best = jnp.max(work, axis=0, keepdims=True)
        first = jnp.min(jnp.where(work == best, jrow_f, float(nsel)), axis=0, keepdims=True)
        hit = jrow_f == first
        bias_t = jnp.where(hit, 0.0, bias_t)
        work = jnp.where(hit, -3e38, work)
    if nsel < HALF:
        bias_t = jnp.concatenate([bias_t, jnp.full((HALF - nsel, TQ), NEG, f32)], axis=0)
    bias = jnp.concatenate([bias_t, bias_t], axis=0).T

    qa_scr[...] = jnp.concatenate([jnp.where(lo, x, bias) for x in qn], axis=0).astype(bf16)

    m_scr[...] = jnp.full(m_scr.shape, NEG, f32)
    acc_scr[...] = jnp.zeros(acc_scr.shape, f32)

    def sel_tile(state, kt, causal):
        m_old, acc = state
        k0 = pl.multiple_of(kt * TK, TK)
        s = lax.dot_general(qa_scr[...], ks_ref[0, 0, pl.ds(k0, TK), :], NT, preferred_element_type=f32)
        krel = (k0 - q0 + lax.broadcasted_iota(jnp.int32, (1, TK), 1)).astype(f32)
        if causal:
            ahead = (k0 - q0 + lax.broadcasted_iota(jnp.int32, (TQ, TK), 1)
                     > lax.broadcasted_iota(jnp.int32, (TQ, TK), 0))
            s = per_head(s, lambda r, x: jnp.where(ahead, NEG, x + slopes[r] * krel))
        else:
            s = per_head(s, lambda r, x: x + slopes[r] * krel)
        m_new = jnp.maximum(m_old, jnp.max(s, axis=-1, keepdims=True))
        p = jnp.exp2(s - m_new[:, 0:1])
        acc = jnp.exp2(m_old - m_new) * acc + jnp.dot(p.astype(bf16), vs_ref[0, 0, pl.ds(k0, TK), :],
                                                      preferred_element_type=f32)
        return m_new, acc

    def sel_tiles(tiles):
        state = (m_scr[...], acc_scr[...])
        for kt, causal in tiles:
            state = sel_tile(state, kt, causal)
        m_scr[...] = state[0]
        acc_scr[...] = state[1]

    n_full = q0 // TK

    def sel_body(i, carry):
        sel_tiles([(2 * i, False), (2 * i + 1, False)])
        return carry

    lax.fori_loop(0, n_full // 2, sel_body, 0)

    @pl.when(n_full % 2 == 1)
    def _():
        sel_tiles([(n_full - 1, False), (n_full, True)])

    @pl.when(n_full % 2 == 0)
    def _():
        sel_tiles([(n_full, True)])

    acc_s = acc_scr[...]
    o_slc = acc_s / pltpu.roll(acc_s, HALF, 1)

    sig = jax.nn.sigmoid(sm_ref[0])

    def gate(r, branch):
        col = (g * NSA_GROUP + r) * 3 + branch
        return jnp.sum(jnp.where(lane == col, sig, 0.0), axis=-1, keepdims=True)

    def mix(r):
        sl = slice(r * TQ, (r + 1) * TQ)
        return gate(r, 0) * o_cmp[sl] + gate(r, 1) * o_slc[sl] + gate(r, 2) * o_win[sl]

    for c in range(NSA_GROUP // 2):
        y_ref[0, :, c * LANES:(c + 1) * LANES] = jnp.where(lo, mix(2 * c), pltpu.roll(mix(2 * c + 1), HALF, 1))


def _nsa(slopes, q3, sm3, qg2, kcd, vcl, ovt, ks, vs, kw, vw):
    b, t, _ = q3.shape
    gw = NSA_GROUP * NSA_HEAD_DIM
    ncp = kcd.shape[2]

    def kv_spec(rows):
        return pl.BlockSpec((1, 1, rows, LANES), lambda bi, gi, qi: (bi, gi, 0, 0))

    return pl.pallas_call(
        _nsa_kernel,
        grid=(b, NSA_KV_HEADS, t // TQ),
        in_specs=[
            pl.BlockSpec(memory_space=pltpu.SMEM),
            pl.BlockSpec((1, TQ, gw), lambda bi, gi, qi: (bi, qi, gi)),
            pl.BlockSpec((1, TQ, LANES), lambda bi, gi, qi: (bi, qi, 0)),
            pl.BlockSpec((1, gw), lambda bi, gi, qi: (0, 0)),
            pl.BlockSpec((1, 1, ncp, 2 * LANES), lambda bi, gi, qi: (bi, gi, 0, 0)), kv_spec(ncp),
            pl.BlockSpec(ovt.shape, lambda bi, gi, qi: (0, 0)),
            kv_spec(t), kv_spec(t), kv_spec(t), kv_spec(t),
        ],
        out_specs=pl.BlockSpec((1, TQ, gw), lambda bi, gi, qi: (bi, qi, gi)),
        out_shape=jax.ShapeDtypeStruct((b, t, NSA_WIDTH), f32),
        scratch_shapes=[
            pltpu.VMEM((NSA_GROUP * TQ, LANES), bf16),
            pltpu.VMEM((NSA_GROUP * TQ, LANES), f32),
            pltpu.VMEM((NSA_GROUP * TQ, LANES), f32),
        ],
        compiler_params=_cparams(("parallel", "parallel", "arbitrary")),
        name="nsa",
    )(slopes, q3, sm3, qg2, kcd, vcl, ovt, ks, vs, kw, vw)


def _log_sigmoid(x):
    return jnp.minimum(x, 0.0) - jnp.log1p(jnp.exp(-jnp.abs(x)))


def _mlstm_kernel(bi_ref, bf_ref, u_ref, v_ref, op_ref, gi_ref, gf_ref, cw_ref, cb_ref, wq_ref, wk_ref,
                  ng_ref, sk_ref, y_ref, uc_scr, q_scr, kt_scr, ct_scr, m_scr):
    h = pl.program_id(1)
    t = u_ref.shape[1]
    L = MLSTM_L
    dm = MLSTM_HEAD_DIM

    x = u_ref[0]
    rows = lax.broadcasted_iota(jnp.int32, (t, dm), 0)
    acc = x * cw_ref[CONV_WIDTH - 1:CONV_WIDTH, :]
    for s in range(1, CONV_WIDTH):
        xs = jnp.where(rows >= s, pltpu.roll(x, s, 0), 0.0)
        acc = acc + xs * cw_ref[CONV_WIDTH - 1 - s:CONV_WIDTH - s, :]
    uc = acc + cb_ref[...]
    uc = uc * jax.nn.sigmoid(uc)
    uc_scr[...] = uc
    ucb = uc.astype(bf16)
    q_scr[...] = jnp.dot(ucb, wq_ref[0].astype(bf16), preferred_element_type=f32).astype(bf16)
    k = jnp.dot(ucb, wk_ref[0].astype(bf16), preferred_element_type=f32) * (dm ** -0.5)
    kt_scr[...] = k.T

    ct_scr[...] = jnp.zeros(ct_scr.shape, f32)
    m_scr[...] = jnp.zeros(m_scr.shape, f32)

    li_ = lax.broadcasted_iota(jnp.int32, (L, L), 0)
    si_ = lax.broadcasted_iota(jnp.int32, (L, L), 1)
    causal = si_ <= li_
    strict_lower = jnp.where(li_ > si_, 1.0, 0.0).astype(bf16)
    strict_lower3 = jnp.concatenate([strict_lower] * 3, axis=0)
    b_i = bi_ref[h]
    b_f = bf_ref[h]
    ones_v = jnp.ones((L, dm), f32)

    def chunk(c, ct, m_prev):
        r0 = pl.multiple_of(c * L, L)
        qc = q_scr[pl.ds(r0, L), :]
        ktc = kt_scr[:, pl.ds(r0, L)]
        vaug = jnp.concatenate([v_ref[0, pl.ds(r0, L), :], ones_v], axis=1).astype(bf16)
        log_i = gi_ref[0, 0, :, pl.ds(r0, L)] + b_i
        log_f = _log_sigmoid(gf_ref[0, 0, :, pl.ds(r0, L)] + b_f)
        a = jnp.where(causal, log_f, 0.0)
        d0 = jnp.dot(jnp.concatenate(_split3(a), axis=1), strict_lower3, preferred_element_type=f32)
        b_col = jnp.sum(a, axis=-1, keepdims=True)
        dmat = jnp.where(causal, d0 + log_i, NEG)
        m_loc = jnp.max(dmat, axis=-1, keepdims=True)
        p = jnp.dot(qc, ktc.astype(bf16), preferred_element_type=f32) * jnp.exp(dmat - m_loc)
        intra = jnp.dot(p.astype(bf16), vaug, preferred_element_type=f32)
        m_inter = b_col + m_prev
        m_out = jnp.maximum(m_inter, m_loc)
        xo = (jnp.exp(m_inter - m_out) * jnp.dot(qc, ct.astype(bf16), preferred_element_type=f32)
              + jnp.exp(m_loc - m_out) * intra)
        num = xo[:, 0:dm]
        den = xo[:, dm:2 * dm]
        hh = num / jnp.maximum(jnp.abs(den), jnp.exp(-m_out))
        hh = hh * jax.nn.sigmoid(op_ref[0, pl.ds(r0, L), :])
        hh = hh * lax.rsqrt(jnp.mean(hh * hh, axis=-1, keepdims=True) + NORM_EPS) * ng_ref[0]
        y_ref[0, pl.ds(r0, L), :] = hh + sk_ref[...] * uc_scr[pl.ds(r0, L), :]

        g_sum = jnp.sum(log_f, axis=-1, keepdims=True)
        w_end = d0[L - 1:L, :] + log_i
        m_new = jnp.maximum(g_sum + m_prev, jnp.max(w_end, axis=-1, keepdims=True))
        decay = jnp.exp(g_sum + m_prev - m_new)
        w = jnp.exp(w_end - m_new)
        return decay * ct + jnp.dot((ktc * w).astype(bf16), vaug, preferred_element_type=f32), m_new

    def chunk_group(i, carry):
        ct, m_prev = ct_scr[...], m_scr[0:1, 0:1]
        for j in range(MLSTM_UNROLL):
            ct, m_prev = chunk(i * MLSTM_UNROLL + j, ct, m_prev)
        ct_scr[...] = ct
        m_scr[...] = jnp.broadcast_to(m_prev, m_scr.shape)
        return carry

    lax.fori_loop(0, t // (L * MLSTM_UNROLL), chunk_group, 0)


def _mlstm(b_i, b_f, u3, v3, op3, gi4, gf4, cw, cb2, wq, wk, ng3, sk2):
    b, t, _ = u3.shape
    dm = MLSTM_HEAD_DIM
    seq = pl.BlockSpec((1, t, dm), lambda bi, hi: (bi, 0, hi))
    gate = pl.BlockSpec((1, 1, 1, t), lambda bi, hi: (bi, hi, 0, 0))
    smem = pl.BlockSpec(memory_space=pltpu.SMEM)
    return pl.pallas_call(
        _mlstm_kernel,
        grid=(b, MLSTM_HEADS),
        in_specs=[
            smem, smem, seq, seq, seq, gate, gate,
            pl.BlockSpec((CONV_WIDTH, dm), lambda bi, hi: (0, hi)),
            pl.BlockSpec((1, dm), lambda bi, hi: (0, hi)),
            pl.BlockSpec((1, dm, dm), lambda bi, hi: (hi, 0, 0)),
            pl.BlockSpec((1, dm, dm), lambda bi, hi: (hi, 0, 0)),
            pl.BlockSpec((1, 1, dm), lambda bi, hi: (hi, 0, 0)),
            pl.BlockSpec((1, dm), lambda bi, hi: (0, hi)),
        ],
        out_specs=seq,
        out_shape=jax.ShapeDtypeStruct((b, t, MLSTM_WIDTH), f32),
        scratch_shapes=[
            pltpu.VMEM((t, dm), f32),
            pltpu.VMEM((t, dm), bf16),
            pltpu.VMEM((dm, t), f32),
            pltpu.VMEM((dm, 2 * dm), f32),
            pltpu.VMEM((8, LANES), f32),
        ],
        compiler_params=_cparams(("parallel", "parallel")),
        name="mlstm",
    )(b_i, b_f, u3, v3, op3, gi4, gf4, cw, cb2, wq, wk, ng3, sk2)


def _ffn_kernel(x_ref, ya_ref, yb_ref, wo_ref, g2_ref, wg_ref, wu_ref, wd_ref, o_ref, act_scr):
    x1 = (x_ref[...]
          + jnp.dot(ya_ref[...].astype(bf16), wo_ref[0:NSA_WIDTH, :], preferred_element_type=f32)
          + jnp.dot(yb_ref[...].astype(bf16), wo_ref[NSA_WIDTH:NSA_WIDTH + MLSTM_WIDTH, :],
                    preferred_element_type=f32))
    h2 = x1 * lax.rsqrt(jnp.mean(x1 * x1, axis=-1, keepdims=True) + NORM_EPS) * g2_ref[...]
    h2b = h2.astype(bf16)
    for c in range(D_FF // FF_CHUNK):
        cols = slice(c * FF_CHUNK, (c + 1) * FF_CHUNK)
        gt = jnp.dot(h2b, wg_ref[:, cols], preferred_element_type=f32)
        up = jnp.dot(h2b, wu_ref[:, cols], preferred_element_type=f32)
        act_scr[:, cols] = (gt * jax.nn.sigmoid(gt) * up).astype(bf16)
    o_ref[...] = x1 + jnp.dot(act_scr[...], wd_ref[...], preferred_element_type=f32)


def _ffn(x2, ya, yb, wo, g2, wg, wu, wd):
    n = x2.shape[0]
    tm = TM_FFN

    def const(shape):
        return pl.BlockSpec(shape, lambda i: (0, 0), pipeline_mode=pl.Buffered(1))

    return pl.pallas_call(
        _ffn_kernel,
        grid=(n // tm,),
        in_specs=[
            pl.BlockSpec((tm, D_MODEL), lambda i: (i, 0)),
            pl.BlockSpec((tm, NSA_WIDTH), lambda i: (i, 0)),
            pl.BlockSpec((tm, MLSTM_WIDTH), lambda i: (i, 0)),
            const(wo.shape), const(g2.shape), const(wg.shape), const(wu.shape), const(wd.shape),
        ],
        out_specs=pl.BlockSpec((tm, D_MODEL), lambda i: (i, 0)),
        out_shape=jax.ShapeDtypeStruct((n, D_MODEL), f32),
        scratch_shapes=[pltpu.VMEM((tm, D_FF), bf16)],
        compiler_params=_cparams(("parallel",)),
        name="ffn",
    )(x2, ya, yb, wo, g2, wg, wu, wd)


def _overlap_t(ncp):
    nsel = ncp // (SEL_BLOCK // CMP_STRIDE)
    cs = np.arange(ncp) * CMP_STRIDE
    ss = np.arange(nsel) * SEL_BLOCK
    ov = ((cs[None, :] < ss[:, None] + SEL_BLOCK) & (cs[None, :] + CMP_BLOCK > ss[:, None])).astype(np.float32)
    ov[:, ncp - 1] = 0.0
    out = np.zeros((LANES, ncp), np.float32)
    out[:nsel] = ov
    return jnp.asarray(np.tile(out, (1, 3)), dtype=bf16)


def _layer(x, norm1_g, w_in, q_g, kc_g, ks_g, kw_g, cmp_pos, w_ck1, w_ck2, w_cv1, w_cv2, conv_w, conv_b,
           w_mq, w_mk, b_i, b_f, mlstm_norm_g, mlstm_skip, w_out, norm2_g, w_gate, w_up, w_down):
    b, t, d = x.shape
    n = b * t
    x2 = x.reshape(n, d)

    o_gate = NSA_WIDTH + 6 * KV_WIDTH
    o_u = o_gate + 3 * NSA_HEADS
    o_if = o_u + 3 * MLSTM_WIDTH
    w_perm = jnp.concatenate([
        w_in[:, :o_gate], w_in[:, o_u:o_if], w_in[:, o_gate:o_u], w_in[:, o_if:],
        jnp.zeros((d, LANES - 3 * NSA_HEADS - 2 * MLSTM_HEADS), w_in.dtype)], axis=1).astype(bf16)
    q2, kv2, u2, vm2, op2, sm2 = _inproj(x2, norm1_g.reshape(1, d), w_perm)

    kv3 = kv2.reshape(b, t, 6 * KV_WIDTH)
    ks, vs, kw, vw = _kvprep(kv3, jnp.tile(ks_g, 2).reshape(1, LANES), jnp.tile(kw_g, 2).reshape(1, LANES))
    nseg = t // CMP_STRIDE
    seg_w = CMP_STRIDE * NSA_HEAD_DIM

    def segments(cols):
        return cols.reshape(b, t, NSA_KV_HEADS, NSA_HEAD_DIM).transpose(0, 2, 1, 3).reshape(
            b, NSA_KV_HEADS, nseg, seg_w)

    pos8 = jnp.broadcast_to(cmp_pos.reshape(1, CMP_BLOCK * NSA_HEAD_DIM), (8, CMP_BLOCK * NSA_HEAD_DIM))
    kcd, vcl = _compress(segments(kv3[..., 0:KV_WIDTH]), segments(kv3[..., KV_WIDTH:2 * KV_WIDTH]),
                         w_ck1, jnp.tile(w_ck2, (1, 2)), w_cv1, jnp.tile(w_cv2, (1, 2)), pos8,
                         jnp.tile(kc_g, 2).reshape(1, LANES))
    slopes = jnp.exp2(-8.0 * (jnp.arange(NSA_HEADS, dtype=f32) + 1.0) / NSA_HEADS)
    y_nsa = _nsa(slopes, q2.reshape(b, t, NSA_WIDTH), sm2.reshape(b, t, LANES),
                 jnp.tile(q_g, NSA_GROUP).reshape(1, NSA_GROUP * NSA_HEAD_DIM),
                 kcd, vcl, _overlap_t(nseg), ks, vs, kw, vw)

    gi4 = sm2[:, 3 * NSA_HEADS:3 * NSA_HEADS + MLSTM_HEADS].reshape(b, t, MLSTM_HEADS).transpose(0, 2, 1)
    gf4 = sm2[:, 3 * NSA_HEADS + MLSTM_HEADS:3 * NSA_HEADS + 2 * MLSTM_HEADS].reshape(
        b, t, MLSTM_HEADS).transpose(0, 2, 1)
    y_mem = _mlstm(b_i, b_f, u2.reshape(b, t, MLSTM_WIDTH), vm2.reshape(b, t, MLSTM_WIDTH),
                   op2.reshape(b, t, MLSTM_WIDTH), gi4.reshape(b, MLSTM_HEADS, 1, t),
                   gf4.reshape(b, MLSTM_HEADS, 1, t), conv_w, conv_b.reshape(1, MLSTM_WIDTH), w_mq, w_mk,
                   mlstm_norm_g.reshape(MLSTM_HEADS, 1, MLSTM_HEAD_DIM), mlstm_skip.reshape(1, MLSTM_WIDTH))

    out = _ffn(x2, y_nsa.reshape(n, NSA_WIDTH), y_mem.reshape(n, MLSTM_WIDTH), w_out.astype(bf16),
               norm2_g.reshape(1, d), w_gate.astype(bf16), w_up.astype(bf16), w_down.astype(bf16))
    return out.reshape(b, t, d)


def kernel(x, norm1_g, w_in, q_norm_g, kc_norm_g, ks_norm_g, kw_norm_g, cmp_pos, w_ck1, w_ck2, w_cv1, w_cv2,
           conv_w, conv_b, w_mq, w_mk, b_i, b_f, mlstm_norm_g, mlstm_skip, w_out, norm2_g, w_gate, w_up, w_down):
    depth = norm1_g.shape[0]
    for l in range(depth):
        x = _layer(x, norm1_g[l], w_in[l], q_norm_g[l], kc_norm_g[l], ks_norm_g[l], kw_norm_g[l], cmp_pos[l],
                   w_ck1[l], w_ck2[l], w_cv1[l], w_cv2[l], conv_w[l], conv_b[l], w_mq[l], w_mk[l], b_i[l], b_f[l],
                   mlstm_norm_g[l], mlstm_skip[l], w_out[l], norm2_g[l], w_gate[l], w_up[l], w_down[l])
    return x
```

```python
import functools
import math

import numpy as np
import jax
import jax.numpy as jnp
from jax import lax
from jax.experimental import pallas as pl
from jax.experimental.pallas import tpu as pltpu

f32 = jnp.float32
bf16 = jnp.bfloat16

D_MODEL = 1024
NSA_HEADS = 8
NSA_KV_HEADS = 2
NSA_HEAD_DIM = 64
NSA_GROUP = NSA_HEADS // NSA_KV_HEADS
CMP_BLOCK = 32
CMP_STRIDE = 16
CMP_HIDDEN = 256
SEL_BLOCK = 64
SEL_TOPN = 16
WINDOW = 512
MLSTM_HEADS = 4
MLSTM_HEAD_DIM = 128
CONV_WIDTH = 4
NSA_WIDTH = NSA_HEADS * NSA_HEAD_DIM
MLSTM_WIDTH = MLSTM_HEADS * MLSTM_HEAD_DIM
KV_WIDTH = NSA_KV_HEADS * NSA_HEAD_DIM
D_FF = -(-8 * D_MODEL // (3 * 256)) * 256
NORM_EPS = 1e-6
NEG = -1e30
FORCE = 1e9

LANES = 128
HALF = 64
VMEM_LIMIT = 56 * 1024 * 1024

TM_PROJ = 512
TM_PREP = 512
TQ = 256
TK = 512
MLSTM_L = 256
MLSTM_UNROLL = 4
TM_FFN = 512
FF_CHUNK = 256

HI = lax.Precision.HIGHEST
NT = (((1,), (1,)), ((), ()))


def _cparams(sem):
    return pltpu.CompilerParams(dimension_semantics=sem, vmem_limit_bytes=VMEM_LIMIT)


def _split3(x):
    x1 = x.astype(bf16)
    r1 = x - x1.astype(f32)
    x2 = r1.astype(bf16)
    x3 = (r1 - x2.astype(f32)).astype(bf16)
    return x1, x2, x3


PROJ_SPLITS = (NSA_WIDTH, 6 * KV_WIDTH, MLSTM_WIDTH, MLSTM_WIDTH, MLSTM_WIDTH, LANES)


def _inproj_kernel(x_ref, g_ref, w_ref, *out_refs):
    x = x_ref[...]
    h = x * lax.rsqrt(jnp.mean(x * x, axis=-1, keepdims=True) + NORM_EPS) * g_ref[...]
    hb = h.astype(bf16)
    off = 0
    for o_ref, width in zip(out_refs, PROJ_SPLITS):
        o_ref[...] = jnp.dot(hb, w_ref[:, off:off + width], preferred_element_type=f32)
        off += width


def _inproj(x2, g1, w_perm):
    n = x2.shape[0]
    wtot = sum(PROJ_SPLITS)
    return pl.pallas_call(
        _inproj_kernel,
        grid=(n // TM_PROJ,),
        in_specs=[
            pl.BlockSpec((TM_PROJ, D_MODEL), lambda i: (i, 0)),
            pl.BlockSpec((1, D_MODEL), lambda i: (0, 0)),
            pl.BlockSpec((D_MODEL, wtot), lambda i: (0, 0)),
        ],
        out_specs=[pl.BlockSpec((TM_PROJ, w), lambda i: (i, 0)) for w in PROJ_SPLITS],
        out_shape=[jax.ShapeDtypeStruct((n, w), f32) for w in PROJ_SPLITS],
        compiler_params=_cparams(("parallel",)),
        name="inproj",
    )(x2, g1, w_perm)


def _pair_norm(x, gain, lo):
    sq = x * x
    s_lo = jnp.sum(jnp.where(lo, sq, 0.0), axis=-1, keepdims=True)
    s_hi = jnp.sum(jnp.where(lo, 0.0, sq), axis=-1, keepdims=True)
    inv = jnp.where(lo, lax.rsqrt(s_lo / HALF + NORM_EPS), lax.rsqrt(s_hi / HALF + NORM_EPS))
    return x * inv * gain


def _kvprep_kernel(kv_ref, ksg_ref, kwg_ref, ks_ref, vs_ref, kw_ref, vw_ref):
    i = pl.program_id(1)
    tm = kv_ref.shape[1]
    lane = lax.broadcasted_iota(jnp.int32, (tm, LANES), 1)
    lo = lane < HALF
    row = i * tm + lax.broadcasted_iota(jnp.int32, (tm, LANES), 0)
    code = jnp.where(lax.shift_right_logical(row, 6) == (lane & (HALF - 1)), 1.0, 0.0)

    def put(x, o_ref, fill):
        o_ref[0, 0] = jnp.where(lo, x, fill).astype(bf16)
        o_ref[0, 1] = jnp.where(lo, pltpu.roll(x, HALF, 1), fill).astype(bf16)

    put(_pair_norm(kv_ref[0, :, 2 * KV_WIDTH:3 * KV_WIDTH], ksg_ref[...], lo), ks_ref, code)
    put(kv_ref[0, :, 3 * KV_WIDTH:4 * KV_WIDTH], vs_ref, 1.0)
    put(_pair_norm(kv_ref[0, :, 4 * KV_WIDTH:5 * KV_WIDTH], kwg_ref[...], lo), kw_ref, 0.0)
    put(kv_ref[0, :, 5 * KV_WIDTH:6 * KV_WIDTH], vw_ref, 1.0)


def _kvprep(kv3, ksg2, kwg2):
    b, t, _ = kv3.shape
    tm = min(TM_PREP, t)
    out_spec = pl.BlockSpec((1, NSA_KV_HEADS, tm, LANES), lambda bi, i: (bi, 0, i, 0))
    out_shape = jax.ShapeDtypeStruct((b, NSA_KV_HEADS, t, LANES), bf16)
    return pl.pallas_call(
        _kvprep_kernel,
        grid=(b, t // tm),
        in_specs=[
            pl.BlockSpec((1, tm, 6 * KV_WIDTH), lambda bi, i: (bi, i, 0)),
            pl.BlockSpec((1, LANES), lambda bi, i: (0, 0)),
            pl.BlockSpec((1, LANES), lambda bi, i: (0, 0)),
        ],
        out_specs=[out_spec] * 4,
        out_shape=[out_shape] * 4,
        compiler_params=_cparams(("parallel", "parallel")),
        name="kvprep",
    )(kv3, ksg2, kwg2)


def _gelu_tanh(x):
    return 0.5 * x * (1.0 + jnp.tanh(math.sqrt(2.0 / math.pi) * (x + 0.044715 * (x * x * x))))


def _compress_kernel(ak_ref, av_ref, wk1_ref, wk2_ref, wv1_ref, wv2_ref, pos_ref, kcg_ref,
                     kcd_ref, vcl_ref):
    nseg = ak_ref.shape[2]
    half_in = CMP_STRIDE * NSA_HEAD_DIM

    def branch(a, w1_ref, w2_ref):
        top = jnp.dot(a, w1_ref[0:half_in, :], precision=HI, preferred_element_type=f32)
        bot = jnp.dot(a, w1_ref[half_in:2 * half_in, :], precision=HI, preferred_element_type=f32)
        pos_term = jnp.dot(pos_ref[...], w1_ref[...], precision=HI, preferred_element_type=f32)[0:1]
        hid = top + pltpu.roll(bot, nseg - 1, 0) + pos_term
        return jnp.dot(_gelu_tanh(hid), w2_ref[...], precision=HI, preferred_element_type=f32)

    kc = branch(ak_ref[0, 0], wk1_ref, wk2_ref)
    kc = kc * lax.rsqrt(jnp.mean(kc * kc, axis=-1, keepdims=True) + NORM_EPS) * kcg_ref[...]
    lane = lax.broadcasted_iota(jnp.int32, kc.shape, 1)
    k_hi = kc.astype(bf16)
    k_lo = jnp.where(lane < HALF, kc - k_hi.astype(f32), 0.0).astype(bf16)
    kcd_ref[0, 0] = jnp.concatenate([k_hi, k_lo], axis=1)

    vc = branch(av_ref[0, 0], wv1_ref, wv2_ref)
    vcl_ref[0, 0] = jnp.where(lane < HALF, vc, 0.0).astype(bf16)


def _compress(ak, av, wk1, wk2d, wv1, wv2d, pos8, kcg2):
    b, g, nseg, seg_w = ak.shape
    a_spec = pl.BlockSpec((1, 1, nseg, seg_w), lambda bi, gi: (bi, gi, 0, 0))
    o_spec = pl.BlockSpec((1, 1, nseg, LANES), lambda bi, gi: (bi, gi, 0, 0))

    def full(shape):
        return pl.BlockSpec(shape, lambda bi, gi: (0,) * len(shape))

    return pl.pallas_call(
        _compress_kernel,
        grid=(b, g),
        in_specs=[a_spec, a_spec, full(wk1.shape), full(wk2d.shape), full(wv1.shape), full(wv2d.shape),
                  full(pos8.shape), full(kcg2.shape)],
        out_specs=[pl.BlockSpec((1, 1, nseg, 2 * LANES), lambda bi, gi: (bi, gi, 0, 0)), o_spec],
        out_shape=[jax.ShapeDtypeStruct((b, g, nseg, 2 * LANES), bf16),
                   jax.ShapeDtypeStruct((b, g, nseg, LANES), bf16)],
        compiler_params=_cparams(("parallel", "parallel")),
        name="compress",
    )(ak, av, wk1, wk2d, wv1, wv2d, pos8, kcg2)


def _nsa_kernel(slopes_ref, q_ref, sm_ref, qg_ref, kcd_ref, vcl_ref, ovt_ref,
                ks_ref, vs_ref, kw_ref, vw_ref, y_ref, qa_scr, m_scr, acc_scr):
    g = pl.program_id(1)
    qi = pl.program_id(2)
    q0 = qi * TQ
    t_len = ks_ref.shape[2]
    ncp = kcd_ref.shape[2]
    nsel = ncp // (SEL_BLOCK // CMP_STRIDE)
    rows = NSA_GROUP * TQ
    log2e = math.log2(math.e)
    scale = NSA_HEAD_DIM ** -0.5 * log2e

    lane = lax.broadcasted_iota(jnp.int32, (TQ, LANES), 1)
    lo = lane < HALF
    slopes = [slopes_ref[g * NSA_GROUP + r] * log2e for r in range(NSA_GROUP)]

    def per_head(x, fn):
        return jnp.concatenate([fn(r, x[r * TQ:(r + 1) * TQ]) for r in range(NSA_GROUP)], axis=0)

    q = q_ref[0]
    qn = []
    for c in range(NSA_GROUP // 2):
        pair = _pair_norm(q[:, c * LANES:(c + 1) * LANES], qg_ref[:, c * LANES:(c + 1) * LANES], lo) * scale
        qn += [pair, pltpu.roll(pair, HALF, 1)]
    qz = jnp.concatenate([jnp.where(lo, x, 0.0) for x in qn], axis=0)

    cidx = lax.broadcasted_iota(jnp.int32, (TQ, ncp), 1)
    tpos_c = q0 + lax.broadcasted_iota(jnp.int32, (TQ, ncp), 0)
    dist_c = tpos_c - (cidx * CMP_STRIDE + (CMP_BLOCK - 1))
    valid_c = jnp.logical_and(dist_c >= 0, cidx < ncp - 1)
    dist_cf = dist_c.astype(f32)
    q_hi = qz.astype(bf16)
    q_lo = pltpu.roll(qz - q_hi.astype(f32), HALF, 1).astype(bf16)
    q3 = jnp.concatenate([q_hi + q_lo, q_hi], axis=1)
    s_c = lax.dot_general(q3, kcd_ref[0, 0], NT, preferred_element_type=f32)
    s_c = per_head(s_c, lambda r, x: jnp.where(valid_c, x - slopes[r] * dist_cf, NEG))
    e_c = jnp.exp2(s_c - jnp.max(s_c, axis=-1, keepdims=True))
    p_c = e_c / jnp.sum(e_c, axis=-1, keepdims=True)
    p_c = per_head(p_c, lambda r, x: jnp.where(valid_c, x, 0.0))
    p_sum = p_c[0:TQ]
    for r in range(1, NSA_GROUP):
        p_sum = p_sum + p_c[r * TQ:(r + 1) * TQ]
    o_cmp = jnp.dot(p_c.astype(bf16), vcl_ref[0, 0], preferred_element_type=f32)

    band = WINDOW + TQ
    kb = pl.multiple_of(jnp.maximum(q0 - WINDOW, 0), TQ)
    s_w = lax.dot_general(qz.astype(bf16), kw_ref[0, 0, pl.ds(kb, band), :], NT, preferred_element_type=f32)
    krel_w = (kb - q0 + lax.broadcasted_iota(jnp.int32, (1, band), 1)).astype(f32)
    dist_w = (q0 - kb + lax.broadcasted_iota(jnp.int32, (TQ, band), 0)
              - lax.broadcasted_iota(jnp.int32, (TQ, band), 1))
    valid_w = jnp.logical_and(dist_w >= 0, dist_w < WINDOW)
    s_w = per_head(s_w, lambda r, x: jnp.where(valid_w, x + slopes[r] * krel_w, NEG))
    p_w = jnp.exp2(s_w - jnp.max(s_w, axis=-1, keepdims=True))
    acc_w = jnp.dot(p_w.astype(bf16), vw_ref[0, 0, pl.ds(kb, band), :], preferred_element_type=f32)
    o_win = acc_w / pltpu.roll(acc_w, HALF, 1)

    imp = lax.dot_general(ovt_ref[...], jnp.concatenate(_split3(p_sum), axis=1), NT,
                          preferred_element_type=f32)[0:nsel]
    jrow = lax.broadcasted_iota(jnp.int32, (nsel, TQ), 0)
    tcol = q0 + lax.broadcasted_iota(jnp.int32, (nsel, TQ), 1)
    forced = jnp.logical_or(jrow == lax.shift_right_logical(tcol, 6), jrow == 0)
    future = jrow * SEL_BLOCK > tcol
    work = jnp.where(forced, FORCE, jnp.where(future, -FORCE, imp))
    jrow_f = jrow.astype(f32)
    bias_t = jnp.full((nsel, TQ), NEG, f32)
    for _ in range(min(SEL_TOPN, nsel)):
        best = jnp.max(work, axis=0, keepdims=True)
        first = jnp.min(jnp.where(work == best, jrow_f, float(nsel)), axis=0, keepdims=True)
        hit = jrow_f == first
        bias_t = jnp.where(hit, 0.0, bias_t)
        work = jnp.where(hit, -3e38, work)
    if nsel < HALF:
        bias_t = jnp.concatenate([bias_t, jnp.full((HALF - nsel, TQ), NEG, f32)], axis=0)
    bias = jnp.concatenate([bias_t, bias_t], axis=0).T

    qa_scr[...] = jnp.concatenate([jnp.where(lo, x, bias) for x in qn], axis=0).astype(bf16)

    m_scr[...] = jnp.full(m_scr.shape, NEG, f32)
    acc_scr[...] = jnp.zeros(acc_scr.shape, f32)

    def sel_tile(state, kt, causal):
        m_old, acc = state
        k0 = pl.multiple_of(kt * TK, TK)
        s = lax.dot_general(qa_scr[...], ks_ref[0, 0, pl.ds(k0, TK), :], NT, preferred_element_type=f32)
        krel = (k0 - q0 + lax.broadcasted_iota(jnp.int32, (1, TK), 1)).astype(f32)
        if causal:
            ahead = (k0 - q0 + lax.broadcasted_iota(jnp.int32, (TQ, TK), 1)
                     > lax.broadcasted_iota(jnp.int32, (TQ, TK), 0))
            s = per_head(s, lambda r, x: jnp.where(ahead, NEG, x + slopes[r] * krel))
        else:
            s = per_head(s, lambda r, x: x + slopes[r] * krel)
        m_new = jnp.maximum(m_old, jnp.max(s, axis=-1, keepdims=True))
        p = jnp.exp2(s - m_new[:, 0:1])
        acc = jnp.exp2(m_old - m_new) * acc + jnp.dot(p.astype(bf16), vs_ref[0, 0, pl.ds(k0, TK), :],
                                                      preferred_element_type=f32)
        return m_new, acc

    def sel_tiles(tiles):
        state = (m_scr[...], acc_scr[...])
        for kt, causal in tiles:
            state = sel_tile(state, kt, causal)
        m_scr[...] = state[0]
        acc_scr[...] = state[1]

    n_full = q0 // TK

    def sel_body(i, carry):
        sel_tiles([(2 * i, False), (2 * i + 1, False)])
        return carry

    lax.fori_loop(0, n_full // 2, sel_body, 0)

    @pl.when(n_full % 2 == 1)
    def _():
        sel_tiles([(n_full - 1, False), (n_full, True)])

    @pl.when(n_full % 2 == 0)
    def _():
        sel_tiles([(n_full, True)])

    acc_s = acc_scr[...]
    o_slc = acc_s / pltpu.roll(acc_s, HALF, 1)

    sig = jax.nn.sigmoid(sm_ref[0])

    def gate(r, branch):
        col = (g * NSA_GROUP + r) * 3 + branch
        return jnp.sum(jnp.where(lane == col, sig, 0.0), axis=-1, keepdims=True)

    def mix(r):
        sl = slice(r * TQ, (r + 1) * TQ)
        return gate(r, 0) * o_cmp[sl] + gate(r, 1) * o_slc[sl] + gate(r, 2) * o_win[sl]

    for c in range(NSA_GROUP // 2):
        y_ref[0, :, c * LANES:(c + 1) * LANES] = jnp.where(lo, mix(2 * c), pltpu.roll(mix(2 * c + 1), HALF, 1))


def _nsa(slopes, q3, sm3, qg2, kcd, vcl, ovt, ks, vs, kw, vw):
    b, t, _ = q3.shape
    gw = NSA_GROUP * NSA_HEAD_DIM
    ncp = kcd.shape[2]

    def kv_spec(rows):
        return pl.BlockSpec((1, 1, rows, LANES), lambda bi, gi, qi: (bi, gi, 0, 0))

    return pl.pallas_call(
        _nsa_kernel,
        grid=(b, NSA_KV_HEADS, t // TQ),
        in_specs=[
            pl.BlockSpec(memory_space=pltpu.SMEM),
            pl.BlockSpec((1, TQ, gw), lambda bi, gi, qi: (bi, qi, gi)),
            pl.BlockSpec((1, TQ, LANES), lambda bi, gi, qi: (bi, qi, 0)),
            pl.BlockSpec((1, gw), lambda bi, gi, qi: (0, 0)),
            pl.BlockSpec((1, 1, ncp, 2 * LANES), lambda bi, gi, qi: (bi, gi, 0, 0)), kv_spec(ncp),
            pl.BlockSpec(ovt.shape, lambda bi, gi, qi: (0, 0)),
            kv_spec(t), kv_spec(t), kv_spec(t), kv_spec(t),
        ],
        out_specs=pl.BlockSpec((1, TQ, gw), lambda bi, gi, qi: (bi, qi, gi)),
        out_shape=jax.ShapeDtypeStruct((b, t, NSA_WIDTH), f32),
        scratch_shapes=[
            pltpu.VMEM((NSA_GROUP * TQ, LANES), bf16),
            pltpu.VMEM((NSA_GROUP * TQ, LANES), f32),
            pltpu.VMEM((NSA_GROUP * TQ, LANES), f32),
        ],
        compiler_params=_cparams(("parallel", "parallel", "arbitrary")),
        name="nsa",
    )(slopes, q3, sm3, qg2, kcd, vcl, ovt, ks, vs, kw, vw)


def _log_sigmoid(x):
    return jnp.minimum(x, 0.0) - jnp.log1p(jnp.exp(-jnp.abs(x)))


def _mlstm_kernel(bi_ref, bf_ref, u_ref, v_ref, op_ref, gi_ref, gf_ref, cw_ref, cb_ref, wq_ref, wk_ref,
                  ng_ref, sk_ref, y_ref, uc_scr, q_scr, kt_scr, ct_scr, m_scr, b_scr, li_scr, xp_scr):
    h = pl.program_id(1)
    t = u_ref.shape[1]
    L = MLSTM_L
    dm = MLSTM_HEAD_DIM

    x = u_ref[0]
    xp_scr[0:8, :] = jnp.zeros((8, dm), f32)
    xp_scr[8:, :] = x
    acc = x * cw_ref[CONV_WIDTH - 1:CONV_WIDTH, :]
    for s in range(1, CONV_WIDTH):
        acc = acc + xp_scr[8 - s:8 - s + t, :] * cw_ref[CONV_WIDTH - 1 - s:CONV_WIDTH - s, :]
    uc = acc + cb_ref[...]
    uc = uc * jax.nn.sigmoid(uc)
    uc_scr[...] = uc
    ucb = uc.astype(bf16)
    q_scr[...] = jnp.dot(ucb, wq_ref[0].astype(bf16), preferred_element_type=f32).astype(bf16)
    k = jnp.dot(ucb, wk_ref[0].astype(bf16), preferred_element_type=f32) * (dm ** -0.5)
    kt_scr[...] = k.T

    ct_scr[...] = jnp.zeros(ct_scr.shape, f32)
    m_scr[...] = jnp.zeros(m_scr.shape, f32)

    li_ = lax.broadcasted_iota(jnp.int32, (L, L), 0)
    si_ = lax.broadcasted_iota(jnp.int32, (L, L), 1)
    causal = si_ <= li_
    diag = si_ == li_
    ones_v = jnp.ones((L, dm), f32)

    log_f = _log_sigmoid(gf_ref[0, 0] + bf_ref[h])
    upper = jnp.where(li_ <= si_, 1.0, 0.0).astype(bf16)
    b_scr[...] = jnp.dot(jnp.concatenate(_split3(log_f), axis=1), jnp.concatenate([upper] * 3, axis=0),
                         preferred_element_type=f32)
    li_scr[...] = gi_ref[0, 0] + bi_ref[h]

    def chunk(c, ct, m_prev):
        r0 = pl.multiple_of(c * L, L)
        qc = q_scr[pl.ds(r0, L), :]
        ktc = kt_scr[:, pl.ds(r0, L)]
        vaug = jnp.concatenate([v_ref[0, pl.ds(r0, L), :], ones_v], axis=1).astype(bf16)
        log_i = li_scr[pl.ds(c, 1), :]
        b_row = b_scr[pl.ds(c, 1), :]
        b_col = jnp.sum(jnp.where(diag, b_row, 0.0), axis=-1, keepdims=True)
        g_sum = b_row[:, L - 1:L]
        dmat = jnp.where(causal, (b_col - b_row) + log_i, NEG)
        m_loc = jnp.max(dmat, axis=-1, keepdims=True)
        p = jnp.dot(qc, ktc.astype(bf16), preferred_element_type=f32) * jnp.exp(dmat - m_loc)
        intra = jnp.dot(p.astype(bf16), vaug, preferred_element_type=f32)
        m_inter = b_col + m_prev
        m_out = jnp.maximum(m_inter, m_loc)
        xo = (jnp.exp(m_inter - m_out) * jnp.dot(qc, ct.astype(bf16), preferred_element_type=f32)
              + jnp.exp(m_loc - m_out) * intra)
        num = xo[:, 0:dm]
        den = xo[:, dm:2 * dm]
        hh = num / jnp.maximum(jnp.abs(den), jnp.exp(-m_out))
        hh = hh * jax.nn.sigmoid(op_ref[0, pl.ds(r0, L), :])
        hh = hh * lax.rsqrt(jnp.mean(hh * hh, axis=-1, keepdims=True) + NORM_EPS) * ng_ref[0]
        y_ref[0, pl.ds(r0, L), :] = hh + sk_ref[...] * uc_scr[pl.ds(r0, L), :]

        w_end = (g_sum - b_row) + log_i
        m_new = jnp.maximum(g_sum + m_prev, jnp.max(w_end, axis=-1, keepdims=True))
        decay = jnp.exp(g_sum + m_prev - m_new)
        w = jnp.exp(w_end - m_new)
        return decay * ct + jnp.dot((ktc * w).astype(bf16), vaug, preferred_element_type=f32), m_new

    def chunk_group(i, carry):
        ct, m_prev = ct_scr[...], m_scr[0:1, 0:1]
        for j in range(MLSTM_UNROLL):
            ct, m_prev = chunk(i * MLSTM_UNROLL + j, ct, m_prev)
        ct_scr[...] = ct
        m_scr[...] = jnp.broadcast_to(m_prev, m_scr.shape)
        return carry

    lax.fori_loop(0, t // (L * MLSTM_UNROLL), chunk_group, 0)


def _mlstm(b_i, b_f, u3, v3, op3, gi4, gf4, cw, cb2, wq, wk, ng3, sk2):
    b, t, _ = u3.shape
    dm = MLSTM_HEAD_DIM
    seq = pl.BlockSpec((1, t, dm), lambda bi, hi: (bi, 0, hi))
    gate = pl.BlockSpec((1, 1, t // MLSTM_L, MLSTM_L), lambda bi, hi: (bi, hi, 0, 0))
    smem = pl.BlockSpec(memory_space=pltpu.SMEM)
    return pl.pallas_call(
        _mlstm_kernel,
        grid=(b, MLSTM_HEADS),
        in_specs=[
            smem, smem, seq, seq, seq, gate, gate,
            pl.BlockSpec((CONV_WIDTH, dm), lambda bi, hi: (0, hi)),
            pl.BlockSpec((1, dm), lambda bi, hi: (0, hi)),
            pl.BlockSpec((1, dm, dm), lambda bi, hi: (hi, 0, 0)),
            pl.BlockSpec((1, dm, dm), lambda bi, hi: (hi, 0, 0)),
            pl.BlockSpec((1, 1, dm), lambda bi, hi: (hi, 0, 0)),
            pl.BlockSpec((1, dm), lambda bi, hi: (0, hi)),
        ],
        out_specs=seq,
        out_shape=jax.ShapeDtypeStruct((b, t, MLSTM_WIDTH), f32),
        scratch_shapes=[
            pltpu.VMEM((t, dm), f32),
            pltpu.VMEM((t, dm), bf16),
            pltpu.VMEM((dm, t), f32),
            pltpu.VMEM((dm, 2 * dm), f32),
            pltpu.VMEM((8, LANES), f32),
            pltpu.VMEM((t // MLSTM_L, MLSTM_L), f32),
            pltpu.VMEM((t // MLSTM_L, MLSTM_L), f32),
            pltpu.VMEM((t + 8, dm), f32),
        ],
        compiler_params=_cparams(("parallel", "parallel")),
        name="mlstm",
    )(b_i, b_f, u3, v3, op3, gi4, gf4, cw, cb2, wq, wk, ng3, sk2)


def _ffn_kernel(x_ref, ya_ref, yb_ref, wo_ref, g2_ref, wg_ref, wu_ref, wd_ref, o_ref, act_scr):
    x1 = (x_ref[...]
          + jnp.dot(ya_ref[...].astype(bf16), wo_ref[0:NSA_WIDTH, :], preferred_element_type=f32)
          + jnp.dot(yb_ref[...].astype(bf16), wo_ref[NSA_WIDTH:NSA_WIDTH + MLSTM_WIDTH, :],
                    preferred_element_type=f32))
    h2 = x1 * lax.rsqrt(jnp.mean(x1 * x1, axis=-1, keepdims=True) + NORM_EPS) * g2_ref[...]
    h2b = h2.astype(bf16)
    for c in range(D_FF // FF_CHUNK):
        cols = slice(c * FF_CHUNK, (c + 1) * FF_CHUNK)
        gt = jnp.dot(h2b, wg_ref[:, cols], preferred_element_type=f32)
        up = jnp.dot(h2b, wu_ref[:, cols], preferred_element_type=f32)
        act_scr[:, cols] = (gt * jax.nn.sigmoid(gt) * up).astype(bf16)
    o_ref[...] = x1 + jnp.dot(act_scr[...], wd_ref[...], preferred_element_type=f32)


def _ffn(x2, ya, yb, wo, g2, wg, wu, wd):
    n = x2.shape[0]
    tm = TM_FFN

    def const(shape):
        return pl.BlockSpec(shape, lambda i: (0, 0), pipeline_mode=pl.Buffered(1))

    return pl.pallas_call(
        _ffn_kernel,
        grid=(n // tm,),
        in_specs=[
            pl.BlockSpec((tm, D_MODEL), lambda i: (i, 0)),
            pl.BlockSpec((tm, NSA_WIDTH), lambda i: (i, 0)),
            pl.BlockSpec((tm, MLSTM_WIDTH), lambda i: (i, 0)),
            const(wo.shape), const(g2.shape), const(wg.shape), const(wu.shape), const(wd.shape),
        ],
        out_specs=pl.BlockSpec((tm, D_MODEL), lambda i: (i, 0)),
        out_shape=jax.ShapeDtypeStruct((n, D_MODEL), f32),
        scratch_shapes=[pltpu.VMEM((tm, D_FF), bf16)],
        compiler_params=_cparams(("parallel",)),
        name="ffn",
    )(x2, ya, yb, wo, g2, wg, wu, wd)


def _overlap_t(ncp):
    nsel = ncp // (SEL_BLOCK // CMP_STRIDE)
    cs = np.arange(ncp) * CMP_STRIDE
    ss = np.arange(nsel) * SEL_BLOCK
    ov = ((cs[None, :] < ss[:, None] + SEL_BLOCK) & (cs[None, :] + CMP_BLOCK > ss[:, None])).astype(np.float32)
    ov[:, ncp - 1] = 0.0
    out = np.zeros((LANES, ncp), np.float32)
    out[:nsel] = ov
    return jnp.asarray(np.tile(out, (1, 3)), dtype=bf16)


def _layer(x, norm1_g, w_in, q_g, kc_g, ks_g, kw_g, cmp_pos, w_ck1, w_ck2, w_cv1, w_cv2, conv_w, conv_b,
           w_mq, w_mk, b_i, b_f, mlstm_norm_g, mlstm_skip, w_out, norm2_g, w_gate, w_up, w_down):
    b, t, d = x.shape
    n = b * t
    x2 = x.reshape(n, d)

    o_gate = NSA_WIDTH + 6 * KV_WIDTH
    o_u = o_gate + 3 * NSA_HEADS
    o_if = o_u + 3 * MLSTM_WIDTH
    w_perm = jnp.concatenate([
        w_in[:, :o_gate], w_in[:, o_u:o_if], w_in[:, o_gate:o_u], w_in[:, o_if:],
        jnp.zeros((d, LANES - 3 * NSA_HEADS - 2 * MLSTM_HEADS), w_in.dtype)], axis=1).astype(bf16)
    q2, kv2, u2, vm2, op2, sm2 = _inproj(x2, norm1_g.reshape(1, d), w_perm)

    kv3 = kv2.reshape(b, t, 6 * KV_WIDTH)
    ks, vs, kw, vw = _kvprep(kv3, jnp.tile(ks_g, 2).reshape(1, LANES), jnp.tile(kw_g, 2).reshape(1, LANES))
    nseg = t // CMP_STRIDE
    seg_w = CMP_STRIDE * NSA_HEAD_DIM

    def segments(cols):
        return cols.reshape(b, t, NSA_KV_HEADS, NSA_HEAD_DIM).transpose(0, 2, 1, 3).reshape(
            b, NSA_KV_HEADS, nseg, seg_w)

    pos8 = jnp.broadcast_to(cmp_pos.reshape(1, CMP_BLOCK * NSA_HEAD_DIM), (8, CMP_BLOCK * NSA_HEAD_DIM))
    kcd, vcl = _compress(segments(kv3[..., 0:KV_WIDTH]), segments(kv3[..., KV_WIDTH:2 * KV_WIDTH]),
                         w_ck1, jnp.tile(w_ck2, (1, 2)), w_cv1, jnp.tile(w_cv2, (1, 2)), pos8,
                         jnp.tile(kc_g, 2).reshape(1, LANES))
    slopes = jnp.exp2(-8.0 * (jnp.arange(NSA_HEADS, dtype=f32) + 1.0) / NSA_HEADS)
    y_nsa = _nsa(slopes, q2.reshape(b, t, NSA_WIDTH), sm2.reshape(b, t, LANES),
                 jnp.tile(q_g, NSA_GROUP).reshape(1, NSA_GROUP * NSA_HEAD_DIM),
                 kcd, vcl, _overlap_t(nseg), ks, vs, kw, vw)

    gi4 = sm2[:, 3 * NSA_HEADS:3 * NSA_HEADS + MLSTM_HEADS].reshape(b, t, MLSTM_HEADS).transpose(0, 2, 1)
    gf4 = sm2[:, 3 * NSA_HEADS + MLSTM_HEADS:3 * NSA_HEADS + 2 * MLSTM_HEADS].reshape(
        b, t, MLSTM_HEADS).transpose(0, 2, 1)
    y_mem = _mlstm(b_i, b_f, u2.reshape(b, t, MLSTM_WIDTH), vm2.reshape(b, t, MLSTM_WIDTH),
                   op2.reshape(b, t, MLSTM_WIDTH), gi4.reshape(b, MLSTM_HEADS, t // MLSTM_L, MLSTM_L),
                   gf4.reshape(b, MLSTM_HEADS, t // MLSTM_L, MLSTM_L), conv_w, conv_b.reshape(1, MLSTM_WIDTH),
                   w_mq, w_mk,
                   mlstm_norm_g.reshape(MLSTM_HEADS, 1, MLSTM_HEAD_DIM), mlstm_skip.reshape(1, MLSTM_WIDTH))

    out = _ffn(x2, y_nsa.reshape(n, NSA_WIDTH), y_mem.reshape(n, MLSTM_WIDTH), w_out.astype(bf16),
               norm2_g.reshape(1, d), w_gate.astype(bf16), w_up.astype(bf16), w_down.astype(bf16))
    return out.reshape(b, t, d)


def kernel(x, norm1_g, w_in, q_norm_g, kc_norm_g, ks_norm_g, kw_norm_g, cmp_pos, w_ck1, w_ck2, w_cv1, w_cv2,
           conv_w, conv_b, w_mq, w_mk, b_i, b_f, mlstm_norm_g, mlstm_skip, w_out, norm2_g, w_gate, w_up, w_down):
    depth = norm1_g.shape[0]
    for l in range(depth):
        x = _layer(x, norm1_g[l], w_in[l], q_norm_g[l], kc_norm_g[l], ks_norm_g[l], kw_norm_g[l], cmp_pos[l],
                   w_ck1[l], w_ck2[l], w_cv1[l], w_cv2[l], conv_w[l], conv_b[l], w_mq[l], w_mk[l], b_i[l], b_f[l],
                   mlstm_norm_g[l], mlstm_skip[l], w_out[l], norm2_g[l], w_gate[l], w_up[l], w_down[l])
    return x
```

```python
import functools
import math

import numpy as np
import jax
import jax.numpy as jnp
from jax import lax
from jax.experimental import pallas as pl
from jax.experimental.pallas import tpu as pltpu

f32 = jnp.float32
bf16 = jnp.bfloat16

D_MODEL = 1024
NSA_HEADS = 8
NSA_KV_HEADS = 2
NSA_HEAD_DIM = 64
NSA_GROUP = NSA_HEADS // NSA_KV_HEADS
CMP_BLOCK = 32
CMP_STRIDE = 16
CMP_HIDDEN = 256
SEL_BLOCK = 64
SEL_TOPN = 16
WINDOW = 512
MLSTM_HEADS = 4
MLSTM_HEAD_DIM = 128
CONV_WIDTH = 4
NSA_WIDTH = NSA_HEADS * NSA_HEAD_DIM
MLSTM_WIDTH = MLSTM_HEADS * MLSTM_HEAD_DIM
KV_WIDTH = NSA_KV_HEADS * NSA_HEAD_DIM
D_FF = -(-8 * D_MODEL // (3 * 256)) * 256
NORM_EPS = 1e-6
NEG = -1e30
FORCE = 1e9

LANES = 128
HALF = 64
VMEM_LIMIT = 56 * 1024 * 1024

TM_PROJ = 512
TM_PREP = 512
TQ = 256
TK = 512
MLSTM_L = 256
MLSTM_UNROLL = 4
TM_FFN = 512
FF_CHUNK = 256

HI = lax.Precision.HIGHEST
NT = (((1,), (1,)), ((), ()))


def _cparams(sem):
    return pltpu.CompilerParams(dimension_semantics=sem, vmem_limit_bytes=VMEM_LIMIT)


def _split3(x):
    x1 = x.astype(bf16)
    r1 = x - x1.astype(f32)
    x2 = r1.astype(bf16)
    x3 = (r1 - x2.astype(f32)).astype(bf16)
    return x1, x2, x3


PROJ_SPLITS = (NSA_WIDTH, 6 * KV_WIDTH, MLSTM_WIDTH, MLSTM_WIDTH, MLSTM_WIDTH, LANES)


def _inproj_kernel(x_ref, g_ref, w_ref, *out_refs):
    x = x_ref[...]
    h = x * lax.rsqrt(jnp.mean(x * x, axis=-1, keepdims=True) + NORM_EPS) * g_ref[...]
    hb = h.astype(bf16)
    off = 0
    for o_ref, width in zip(out_refs, PROJ_SPLITS):
        o_ref[...] = jnp.dot(hb, w_ref[:, off:off + width], preferred_element_type=f32)
        off += width


def _inproj(x2, g1, w_perm):
    n = x2.shape[0]
    wtot = sum(PROJ_SPLITS)
    return pl.pallas_call(
        _inproj_kernel,
        grid=(n // TM_PROJ,),
        in_specs=[
            pl.BlockSpec((TM_PROJ, D_MODEL), lambda i: (i, 0)),
            pl.BlockSpec((1, D_MODEL), lambda i: (0, 0)),
            pl.BlockSpec((D_MODEL, wtot), lambda i: (0, 0)),
        ],
        out_specs=[pl.BlockSpec((TM_PROJ, w), lambda i: (i, 0)) for w in PROJ_SPLITS],
        out_shape=[jax.ShapeDtypeStruct((n, w), f32) for w in PROJ_SPLITS],
        compiler_params=_cparams(("parallel",)),
        name="inproj",
    )(x2, g1, w_perm)


def _pair_norm(x, gain, lo):
    sq = x * x
    s_lo = jnp.sum(jnp.where(lo, sq, 0.0), axis=-1, keepdims=True)
    s_hi = jnp.sum(jnp.where(lo, 0.0, sq), axis=-1, keepdims=True)
    inv = jnp.where(lo, lax.rsqrt(s_lo / HALF + NORM_EPS), lax.rsqrt(s_hi / HALF + NORM_EPS))
    return x * inv * gain


def _kvprep_kernel(kv_ref, q_ref, ksg_ref, kwg_ref, qg_ref, ks_ref, vs_ref, kw_ref, vw_ref, q3_ref):
    i = pl.program_id(1)
    tm = kv_ref.shape[1]
    lane = lax.broadcasted_iota(jnp.int32, (tm, LANES), 1)
    lo = lane < HALF
    row = i * tm + lax.broadcasted_iota(jnp.int32, (tm, LANES), 0)
    code = jnp.where(lax.shift_right_logical(row, 6) == (lane & (HALF - 1)), 1.0, 0.0)

    def put(x, o_ref, fill):
        o_ref[0, 0] = jnp.where(lo, x, fill).astype(bf16)
        o_ref[0, 1] = jnp.where(lo, pltpu.roll(x, HALF, 1), fill).astype(bf16)

    put(_pair_norm(kv_ref[0, :, 2 * KV_WIDTH:3 * KV_WIDTH], ksg_ref[...], lo), ks_ref, code)
    put(kv_ref[0, :, 3 * KV_WIDTH:4 * KV_WIDTH], vs_ref, 1.0)
    put(_pair_norm(kv_ref[0, :, 4 * KV_WIDTH:5 * KV_WIDTH], kwg_ref[...], lo), kw_ref, 0.0)
    put(kv_ref[0, :, 5 * KV_WIDTH:6 * KV_WIDTH], vw_ref, 1.0)

    scale = NSA_HEAD_DIM ** -0.5 * math.log2(math.e)
    for c in range(NSA_HEADS // 2):
        pair = _pair_norm(q_ref[0, :, c * LANES:(c + 1) * LANES], qg_ref[:, c * LANES:(c + 1) * LANES], lo) * scale
        for par, x in enumerate((pair, pltpu.roll(pair, HALF, 1))):
            hi = jnp.where(lo, x, 0.0).astype(bf16).astype(f32)
            res = jnp.where(lo, x - hi, 0.0)
            head = 2 * c + par
            q3_ref[0, head // NSA_GROUP, head % NSA_GROUP] = jnp.concatenate(
                [hi + pltpu.roll(res, HALF, 1), hi], axis=1).astype(bf16)


def _kvprep(kv3, q3d, ksg2, kwg2, qg2):
    b, t, _ = kv3.shape
    tm = min(TM_PREP, t)
    out_spec = pl.BlockSpec((1, NSA_KV_HEADS, tm, LANES), lambda bi, i: (bi, 0, i, 0))
    out_shape = jax.ShapeDtypeStruct((b, NSA_KV_HEADS, t, LANES), bf16)
    return pl.pallas_call(
        _kvprep_kernel,
        grid=(b, t // tm),
        in_specs=[
            pl.BlockSpec((1, tm, 6 * KV_WIDTH), lambda bi, i: (bi, i, 0)),
            pl.BlockSpec((1, tm, NSA_WIDTH), lambda bi, i: (bi, i, 0)),
            pl.BlockSpec((1, LANES), lambda bi, i: (0, 0)),
            pl.BlockSpec((1, LANES), lambda bi, i: (0, 0)),
            pl.BlockSpec((1, NSA_WIDTH), lambda bi, i: (0, 0)),
        ],
        out_specs=[out_spec] * 4 + [
            pl.BlockSpec((1, NSA_KV_HEADS, NSA_GROUP, tm, 2 * LANES), lambda bi, i: (bi, 0, 0, i, 0))],
        out_shape=[out_shape] * 4 + [
            jax.ShapeDtypeStruct((b, NSA_KV_HEADS, NSA_GROUP, t, 2 * LANES), bf16)],
        compiler_params=_cparams(("parallel", "parallel")),
        name="kvprep",
    )(kv3, q3d, ksg2, kwg2, qg2)


def _gelu_tanh(x):
    return 0.5 * x * (1.0 + jnp.tanh(math.sqrt(2.0 / math.pi) * (x + 0.044715 * (x * x * x))))


def _compress_kernel(ak_ref, av_ref, wk1_ref, wk2_ref, wv1_ref, wv2_ref, pos_ref, kcg_ref,
                     kcd_ref, vcl_ref):
    nseg = ak_ref.shape[2]
    half_in = CMP_STRIDE * NSA_HEAD_DIM

    def branch(a, w1_ref, w2_ref):
        top = jnp.dot(a, w1_ref[0:half_in, :], precision=HI, preferred_element_type=f32)
        bot = jnp.dot(a, w1_ref[half_in:2 * half_in, :], precision=HI, preferred_element_type=f32)
        pos_term = jnp.dot(pos_ref[...], w1_ref[...], precision=HI, preferred_element_type=f32)[0:1]
        hid = top + pltpu.roll(bot, nseg - 1, 0) + pos_term
        return jnp.dot(_gelu_tanh(hid), w2_ref[...], precision=HI, preferred_element_type=f32)

    kc = branch(ak_ref[0, 0], wk1_ref, wk2_ref)
    kc = kc * lax.rsqrt(jnp.mean(kc * kc, axis=-1, keepdims=True) + NORM_EPS) * kcg_ref[...]
    lane = lax.broadcasted_iota(jnp.int32, kc.shape, 1)
    k_hi = kc.astype(bf16)
    k_lo = jnp.where(lane < HALF, kc - k_hi.astype(f32), 0.0).astype(bf16)
    kcd_ref[0, 0] = jnp.concatenate([k_hi, k_lo], axis=1)

    vc = branch(av_ref[0, 0], wv1_ref, wv2_ref)
    vcl_ref[0, 0] = jnp.where(lane < HALF, vc, 0.0).astype(bf16)


def _compress(ak, av, wk1, wk2d, wv1, wv2d, pos8, kcg2):
    b, g, nseg, seg_w = ak.shape
    a_spec = pl.BlockSpec((1, 1, nseg, seg_w), lambda bi, gi: (bi, gi, 0, 0))
    o_spec = pl.BlockSpec((1, 1, nseg, LANES), lambda bi, gi: (bi, gi, 0, 0))

    def full(shape):
        return pl.BlockSpec(shape, lambda bi, gi: (0,) * len(shape))

    return pl.pallas_call(
        _compress_kernel,
        grid=(b, g),
        in_specs=[a_spec, a_spec, full(wk1.shape), full(wk2d.shape), full(wv1.shape), full(wv2d.shape),
                  full(pos8.shape), full(kcg2.shape)],
        out_specs=[pl.BlockSpec((1, 1, nseg, 2 * LANES), lambda bi, gi: (bi, gi, 0, 0)), o_spec],
        out_shape=[jax.ShapeDtypeStruct((b, g, nseg, 2 * LANES), bf16),
                   jax.ShapeDtypeStruct((b, g, nseg, LANES), bf16)],
        compiler_params=_cparams(("parallel", "parallel")),
        name="compress",
    )(ak, av, wk1, wk2d, wv1, wv2d, pos8, kcg2)


def _nsa_kernel(slopes_ref, q3_ref, sm_ref, kcd_ref, vcl_ref, ovt_ref,
                ks_ref, vs_ref, kw_ref, vw_ref, y_ref, qa_scr, m_scr, acc_scr):
    g = pl.program_id(1)
    qi = pl.program_id(2)
    q0 = qi * TQ
    ncp = kcd_ref.shape[2]
    nsel = ncp // (SEL_BLOCK // CMP_STRIDE)
    rows = NSA_GROUP * TQ
    log2e = math.log2(math.e)

    lane = lax.broadcasted_iota(jnp.int32, (TQ, LANES), 1)
    lo = lane < HALF
    slopes = [slopes_ref[g * NSA_GROUP + r] * log2e for r in range(NSA_GROUP)]

    def per_head(x, fn):
        return jnp.concatenate([fn(r, x[r * TQ:(r + 1) * TQ]) for r in range(NSA_GROUP)], axis=0)

    q3 = q3_ref[0, 0].reshape(rows, 2 * LANES)
    q1 = q3[:, 0:LANES]

    cidx = lax.broadcasted_iota(jnp.int32, (TQ, ncp), 1)
    tpos_c = q0 + lax.broadcasted_iota(jnp.int32, (TQ, ncp), 0)
    blk_end = cidx * CMP_STRIDE + (CMP_BLOCK - 1)
    valid_c = jnp.logical_and(tpos_c >= blk_end, cidx < ncp - 1)
    krel_c = (CMP_BLOCK - 1 - q0 + CMP_STRIDE * lax.broadcasted_iota(jnp.int32, (1, ncp), 1)).astype(f32)
    s_c = lax.dot_general(q3, kcd_ref[0, 0], NT, preferred_element_type=f32)
    s_c = per_head(s_c, lambda r, x: jnp.where(valid_c, x + slopes[r] * krel_c, NEG))
    e_c = jnp.exp2(s_c - jnp.max(s_c, axis=-1, keepdims=True))
    p_c = e_c / jnp.sum(e_c, axis=-1, keepdims=True)
    p_c = per_head(p_c, lambda r, x: jnp.where(valid_c, x, 0.0))
    p_sum = p_c[0:TQ]
    for r in range(1, NSA_GROUP):
        p_sum = p_sum + p_c[r * TQ:(r + 1) * TQ]
    o_cmp = jnp.dot(p_c.astype(bf16), vcl_ref[0, 0], preferred_element_type=f32)

    band = WINDOW + TQ
    kb = pl.multiple_of(jnp.maximum(q0 - WINDOW, 0), TQ)
    s_w = lax.dot_general(q1, kw_ref[0, 0, pl.ds(kb, band), :], NT, preferred_element_type=f32)
    krel_w = (kb - q0 + lax.broadcasted_iota(jnp.int32, (1, band), 1)).astype(f32)
    dist_w = (q0 - kb + lax.broadcasted_iota(jnp.int32, (TQ, band), 0)
              - lax.broadcasted_iota(jnp.int32, (TQ, band), 1))
    valid_w = jnp.logical_and(dist_w >= 0, dist_w < WINDOW)
    s_w = per_head(s_w, lambda r, x: jnp.where(valid_w, x + slopes[r] * krel_w, NEG))
    p_w = jnp.exp2(s_w - jnp.max(s_w, axis=-1, keepdims=True))
    acc_w = jnp.dot(p_w.astype(bf16), vw_ref[0, 0, pl.ds(kb, band), :], preferred_element_type=f32)
    o_win = acc_w / pltpu.roll(acc_w, HALF, 1)

    imp = lax.dot_general(ovt_ref[...], jnp.concatenate(_split3(p_sum), axis=1), NT,
                          preferred_element_type=f32)[0:nsel]
    jrow = lax.broadcasted_iota(jnp.int32, (nsel, TQ), 0)
    tcol = q0 + lax.broadcasted_iota(jnp.int32, (nsel, TQ), 1)
    forced = jnp.logical_or(jrow == lax.shift_right_logical(tcol, 6), jrow == 0)
    future = jrow * SEL_BLOCK > tcol
    work = jnp.where(forced, FORCE, jnp.where(future, -FORCE, imp))
    jrow_f = jrow.astype(f32)
    bias_t = jnp.full((nsel, TQ), NEG, f32)
    for _ in range(min(SEL_TOPN, nsel)):
        best = jnp.max(work, axis=0, keepdims=True)
        first = jnp.min(jnp.where(work == best, jrow_f, float(nsel)), axis=0, keepdims=True)
        hit = jrow_f == first
        bias_t = jnp.where(hit, 0.0, bias_t)
        work = jnp.where(hit, -3e38, work)
    if nsel < HALF:
        bias_t = jnp.concatenate([bias_t, jnp.full((HALF - nsel, TQ), NEG, f32)], axis=0)
    bias = jnp.concatenate([bias_t, bias_t], axis=0).T

    bias_b = bias.astype(bf16)
    qa_scr[...] = per_head(q1, lambda r, x: jnp.where(lo, x, bias_b))

    m_scr[...] = jnp.full(m_scr.shape, NEG, f32)
    acc_scr[...] = jnp.zeros(acc_scr.shape, f32)

    def sel_tile(state, kt, causal):
        m_old, acc = state
        k0 = pl.multiple_of(kt * TK, TK)
        s = lax.dot_general(qa_scr[...], ks_ref[0, 0, pl.ds(k0, TK), :], NT, preferred_element_type=f32)
        krel = (k0 - q0 + lax.broadcasted_iota(jnp.int32, (1, TK), 1)).astype(f32)
        if causal:
            ahead = (k0 - q0 + lax.broadcasted_iota(jnp.int32, (TQ, TK), 1)
                     > lax.broadcasted_iota(jnp.int32, (TQ, TK), 0))
            s = per_head(s, lambda r, x: jnp.where(ahead, NEG, x + slopes[r] * krel))
        else:
            s = per_head(s, lambda r, x: x + slopes[r] * krel)
        m_new = jnp.maximum(m_old, jnp.max(s, axis=-1, keepdims=True))
        p = jnp.exp2(s - m_new[:, 0:1])
        acc = jnp.exp2(m_old - m_new) * acc + jnp.dot(p.astype(bf16), vs_ref[0, 0, pl.ds(k0, TK), :],
                                                      preferred_element_type=f32)
        return m_new, acc

    def sel_tiles(tiles):
        state = (m_scr[...], acc_scr[...])
        for kt, causal in tiles:
            state = sel_tile(state, kt, causal)
        m_scr[...] = state[0]
        acc_scr[...] = state[1]

    n_full = q0 // TK

    def sel_body(i, carry):
        sel_tiles([(2 * i, False), (2 * i + 1, False)])
        return carry

    lax.fori_loop(0, n_full // 2, sel_body, 0)

    @pl.when(n_full % 2 == 1)
    def _():
        sel_tiles([(n_full - 1, False), (n_full, True)])

    @pl.when(n_full % 2 == 0)
    def _():
        sel_tiles([(n_full, True)])

    acc_s = acc_scr[...]
    o_slc = acc_s / pltpu.roll(acc_s, HALF, 1)

    sig = jax.nn.sigmoid(sm_ref[0])

    def gate(r, branch):
        col = (g * NSA_GROUP + r) * 3 + branch
        return jnp.sum(jnp.where(lane == col, sig, 0.0), axis=-1, keepdims=True)

    def mix(r):
        sl = slice(r * TQ, (r + 1) * TQ)
        return gate(r, 0) * o_cmp[sl] + gate(r, 1) * o_slc[sl] + gate(r, 2) * o_win[sl]

    for c in range(NSA_GROUP // 2):
        y_ref[0, :, c * LANES:(c + 1) * LANES] = jnp.where(lo, mix(2 * c), pltpu.roll(mix(2 * c + 1), HALF, 1))


def _nsa(slopes, q3, sm3, kcd, vcl, ovt, ks, vs, kw, vw):
    b, t = q3.shape[0], q3.shape[3]
    gw = NSA_GROUP * NSA_HEAD_DIM
    ncp = kcd.shape[2]

    def kv_spec(rows):
        return pl.BlockSpec((1, 1, rows, LANES), lambda bi, gi, qi: (bi, gi, 0, 0))

    return pl.pallas_call(
        _nsa_kernel,
        grid=(b, NSA_KV_HEADS, t // TQ),
        in_specs=[
            pl.BlockSpec(memory_space=pltpu.SMEM),
            pl.BlockSpec((1, 1, NSA_GROUP, TQ, 2 * LANES), lambda bi, gi, qi: (bi, gi, 0, qi, 0)),
            pl.BlockSpec((1, TQ, LANES), lambda bi, gi, qi: (bi, qi, 0)),
            pl.BlockSpec((1, 1, ncp, 2 * LANES), lambda bi, gi, qi: (bi, gi, 0, 0)), kv_spec(ncp),
            pl.BlockSpec(ovt.shape, lambda bi, gi, qi: (0, 0)),
            kv_spec(t), kv_spec(t), kv_spec(t), kv_spec(t),
        ],
        out_specs=pl.BlockSpec((1, TQ, gw), lambda bi, gi, qi: (bi, qi, gi)),
        out_shape=jax.ShapeDtypeStruct((b, t, NSA_WIDTH), f32),
        scratch_shapes=[
            pltpu.VMEM((NSA_GROUP * TQ, LANES), bf16),
            pltpu.VMEM((NSA_GROUP * TQ, LANES), f32),
            pltpu.VMEM((NSA_GROUP * TQ, LANES), f32),
        ],
        compiler_params=_cparams(("parallel", "parallel", "arbitrary")),
        name="nsa",
    )(slopes, q3, sm3, kcd, vcl, ovt, ks, vs, kw, vw)


def _log_sigmoid(x):
    return jnp.minimum(x, 0.0) - jnp.log1p(jnp.exp(-jnp.abs(x)))


def _mlstm_kernel(bi_ref, bf_ref, u_ref, v_ref, op_ref, gi_ref, gf_ref, cw_ref, cb_ref, wq_ref, wk_ref,
                  ng_ref, sk_ref, y_ref, uc_scr, q_scr, kt_scr, ct_scr, m_scr, b_scr, li_scr, xp_scr):
    h = pl.program_id(1)
    t = u_ref.shape[1]
    L = MLSTM_L
    dm = MLSTM_HEAD_DIM

    x = u_ref[0]
    xp_scr[0:8, :] = jnp.zeros((8, dm), f32)
    xp_scr[8:, :] = x
    acc = x * cw_ref[CONV_WIDTH - 1:CONV_WIDTH, :]
    for s in range(1, CONV_WIDTH):
        acc = acc + xp_scr[8 - s:8 - s + t, :] * cw_ref[CONV_WIDTH - 1 - s:CONV_WIDTH - s, :]
    uc = acc + cb_ref[...]
    uc = uc * jax.nn.sigmoid(uc)
    uc_scr[...] = uc
    ucb = uc.astype(bf16)
    q_scr[...] = jnp.dot(ucb, wq_ref[0].astype(bf16), preferred_element_type=f32).astype(bf16)
    k = jnp.dot(ucb, wk_ref[0].astype(bf16), preferred_element_type=f32) * (dm ** -0.5)
    kt_scr[...] = k.T

    ct_scr[...] = jnp.zeros(ct_scr.shape, f32)
    m_scr[...] = jnp.zeros(m_scr.shape, f32)

    li_ = lax.broadcasted_iota(jnp.int32, (L, L), 0)
    si_ = lax.broadcasted_iota(jnp.int32, (L, L), 1)
    causal = si_ <= li_
    diag = si_ == li_
    ones_v = jnp.ones((L, dm), f32)

    log_f = _log_sigmoid(gf_ref[0, 0] + bf_ref[h])
    upper = jnp.where(li_ <= si_, 1.0, 0.0).astype(bf16)
    b_scr[...] = jnp.dot(jnp.concatenate(_split3(log_f), axis=1), jnp.concatenate([upper] * 3, axis=0),
                         preferred_element_type=f32)
    li_scr[...] = gi_ref[0, 0] + bi_ref[h]

    def chunk(c, ct, m_prev):
        r0 = pl.multiple_of(c * L, L)
        qc = q_scr[pl.ds(r0, L), :]
        ktc = kt_scr[:, pl.ds(r0, L)]
        vaug = jnp.concatenate([v_ref[0, pl.ds(r0, L), :], ones_v], axis=1).astype(bf16)
        log_i = li_scr[pl.ds(c, 1), :]
        b_row = b_scr[pl.ds(c, 1), :]
        b_col = jnp.sum(jnp.where(diag, b_row, 0.0), axis=-1, keepdims=True)
        g_sum = b_row[:, L - 1:L]
        dmat = jnp.where(causal, (b_col - b_row) + log_i, NEG)
        m_loc = jnp.max(dmat, axis=-1, keepdims=True)
        p = jnp.dot(qc, ktc.astype(bf16), preferred_element_type=f32) * jnp.exp(dmat - m_loc)
        intra = jnp.dot(p.astype(bf16), vaug, preferred_element_type=f32)
        m_inter = b_col + m_prev
        m_out = jnp.maximum(m_inter, m_loc)
        xo = (jnp.exp(m_inter - m_out) * jnp.dot(qc, ct.astype(bf16), preferred_element_type=f32)
              + jnp.exp(m_loc - m_out) * intra)
        num = xo[:, 0:dm]
        den = xo[:, dm:2 * dm]
        hh = num / jnp.maximum(jnp.abs(den), jnp.exp(-m_out))
        hh = hh * jax.nn.sigmoid(op_ref[0, pl.ds(r0, L), :])
        hh = hh * lax.rsqrt(jnp.mean(hh * hh, axis=-1, keepdims=True) + NORM_EPS) * ng_ref[0]
        y_ref[0, pl.ds(r0, L), :] = hh + sk_ref[...] * uc_scr[pl.ds(r0, L), :]

        w_end = (g_sum - b_row) + log_i
        m_new = jnp.maximum(g_sum + m_prev, jnp.max(w_end, axis=-1, keepdims=True))
        decay = jnp.exp(g_sum + m_prev - m_new)
        w = jnp.exp(w_end - m_new)
        return decay * ct + jnp.dot((ktc * w).astype(bf16), vaug, preferred_element_type=f32), m_new

    def chunk_group(i, carry):
        ct, m_prev = ct_scr[...], m_scr[0:1, 0:1]
        for j in range(MLSTM_UNROLL):
            ct, m_prev = chunk(i * MLSTM_UNROLL + j, ct, m_prev)
        ct_scr[...] = ct
        m_scr[...] = jnp.broadcast_to(m_prev, m_scr.shape)
        return carry

    lax.fori_loop(0, t // (L * MLSTM_UNROLL), chunk_group, 0)


def _mlstm(b_i, b_f, u3, v3, op3, gi4, gf4, cw, cb2, wq, wk, ng3, sk2):
    b, t, _ = u3.shape
    dm = MLSTM_HEAD_DIM
    seq = pl.BlockSpec((1, t, dm), lambda bi, hi: (bi, 0, hi))
    gate = pl.BlockSpec((1, 1, t // MLSTM_L, MLSTM_L), lambda bi, hi: (bi, hi, 0, 0))
    smem = pl.BlockSpec(memory_space=pltpu.SMEM)
    return pl.pallas_call(
        _mlstm_kernel,
        grid=(b, MLSTM_HEADS),
        in_specs=[
            smem, smem, seq, seq, seq, gate, gate,
            pl.BlockSpec((CONV_WIDTH, dm), lambda bi, hi: (0, hi)),
            pl.BlockSpec((1, dm), lambda bi, hi: (0, hi)),
            pl.BlockSpec((1, dm, dm), lambda bi, hi: (hi, 0, 0)),
            pl.BlockSpec((1, dm, dm), lambda bi, hi: (hi, 0, 0)),
            pl.BlockSpec((1, 1, dm), lambda bi, hi: (hi, 0, 0)),
            pl.BlockSpec((1, dm), lambda bi, hi: (0, hi)),
        ],
        out_specs=seq,
        out_shape=jax.ShapeDtypeStruct((b, t, MLSTM_WIDTH), f32),
        scratch_shapes=[
            pltpu.VMEM((t, dm), f32),
            pltpu.VMEM((t, dm), bf16),
            pltpu.VMEM((dm, t), f32),
            pltpu.VMEM((dm, 2 * dm), f32),
            pltpu.VMEM((8, LANES), f32),
            pltpu.VMEM((t // MLSTM_L, MLSTM_L), f32),
            pltpu.VMEM((t // MLSTM_L, MLSTM_L), f32),
            pltpu.VMEM((t + 8, dm), f32),
        ],
        compiler_params=_cparams(("parallel", "parallel")),
        name="mlstm",
    )(b_i, b_f, u3, v3, op3, gi4, gf4, cw, cb2, wq, wk, ng3, sk2)


def _ffn_kernel(x_ref, ya_ref, yb_ref, wo_ref, g2_ref, wg_ref, wu_ref, wd_ref, o_ref, act_scr):
    x1 = (x_ref[...]
          + jnp.dot(ya_ref[...].astype(bf16), wo_ref[0:NSA_WIDTH, :], preferred_element_type=f32)
          + jnp.dot(yb_ref[...].astype(bf16), wo_ref[NSA_WIDTH:NSA_WIDTH + MLSTM_WIDTH, :],
                    preferred_element_type=f32))
    h2 = x1 * lax.rsqrt(jnp.mean(x1 * x1, axis=-1, keepdims=True) + NORM_EPS) * g2_ref[...]
    h2b = h2.astype(bf16)
    for c in range(D_FF // FF_CHUNK):
        cols = slice(c * FF_CHUNK, (c + 1) * FF_CHUNK)
        gt = jnp.dot(h2b, wg_ref[:, cols], preferred_element_type=f32)
        up = jnp.dot(h2b, wu_ref[:, cols], preferred_element_type=f32)
        act_scr[:, cols] = (gt * jax.nn.sigmoid(gt) * up).astype(bf16)
    o_ref[...] = x1 + jnp.dot(act_scr[...], wd_ref[...], preferred_element_type=f32)


def _ffn(x2, ya, yb, wo, g2, wg, wu, wd):
    n = x2.shape[0]
    tm = TM_FFN

    def const(shape):
        return pl.BlockSpec(shape, lambda i: (0, 0), pipeline_mode=pl.Buffered(1))

    return pl.pallas_call(
        _ffn_kernel,
        grid=(n // tm,),
        in_specs=[
            pl.BlockSpec((tm, D_MODEL), lambda i: (i, 0)),
            pl.BlockSpec((tm, NSA_WIDTH), lambda i: (i, 0)),
            pl.BlockSpec((tm, MLSTM_WIDTH), lambda i: (i, 0)),
            const(wo.shape), const(g2.shape), const(wg.shape), const(wu.shape), const(wd.shape),
        ],
        out_specs=pl.BlockSpec((tm, D_MODEL), lambda i: (i, 0)),
        out_shape=jax.ShapeDtypeStruct((n, D_MODEL), f32),
        scratch_shapes=[pltpu.VMEM((tm, D_FF), bf16)],
        compiler_params=_cparams(("parallel",)),
        name="ffn",
    )(x2, ya, yb, wo, g2, wg, wu, wd)


def _overlap_t(ncp):
    nsel = ncp // (SEL_BLOCK // CMP_STRIDE)
    cs = np.arange(ncp) * CMP_STRIDE
    ss = np.arange(nsel) * SEL_BLOCK
    ov = ((cs[None, :] < ss[:, None] + SEL_BLOCK) & (cs[None, :] + CMP_BLOCK > ss[:, None])).astype(np.float32)
    ov[:, ncp - 1] = 0.0
    out = np.zeros((LANES, ncp), np.float32)
    out[:nsel] = ov
    return jnp.asarray(np.tile(out, (1, 3)), dtype=bf16)


def _layer(x, norm1_g, w_in, q_g, kc_g, ks_g, kw_g, cmp_pos, w_ck1, w_ck2, w_cv1, w_cv2, conv_w, conv_b,
           w_mq, w_mk, b_i, b_f, mlstm_norm_g, mlstm_skip, w_out, norm2_g, w_gate, w_up, w_down):
    b, t, d = x.shape
    n = b * t
    x2 = x.reshape(n, d)

    o_gate = NSA_WIDTH + 6 * KV_WIDTH
    o_u = o_gate + 3 * NSA_HEADS
    o_if = o_u + 3 * MLSTM_WIDTH
    w_perm = jnp.concatenate([
        w_in[:, :o_gate], w_in[:, o_u:o_if], w_in[:, o_gate:o_u], w_in[:, o_if:],
        jnp.zeros((d, LANES - 3 * NSA_HEADS - 2 * MLSTM_HEADS), w_in.dtype)], axis=1).astype(bf16)
    q2, kv2, u2, vm2, op2, sm2 = _inproj(x2, norm1_g.reshape(1, d), w_perm)

    kv3 = kv2.reshape(b, t, 6 * KV_WIDTH)
    ks, vs, kw, vw, q3 = _kvprep(kv3, q2.reshape(b, t, NSA_WIDTH), jnp.tile(ks_g, 2).reshape(1, LANES),
                                 jnp.tile(kw_g, 2).reshape(1, LANES), jnp.tile(q_g, NSA_HEADS).reshape(1, NSA_WIDTH))
    nseg = t // CMP_STRIDE
    seg_w = CMP_STRIDE * NSA_HEAD_DIM

    def segments(cols):
        return cols.reshape(b, t, NSA_KV_HEADS, NSA_HEAD_DIM).transpose(0, 2, 1, 3).reshape(
            b, NSA_KV_HEADS, nseg, seg_w)

    pos8 = jnp.broadcast_to(cmp_pos.reshape(1, CMP_BLOCK * NSA_HEAD_DIM), (8, CMP_BLOCK * NSA_HEAD_DIM))
    kcd, vcl = _compress(segments(kv3[..., 0:KV_WIDTH]), segments(kv3[..., KV_WIDTH:2 * KV_WIDTH]),
                         w_ck1, jnp.tile(w_ck2, (1, 2)), w_cv1, jnp.tile(w_cv2, (1, 2)), pos8,
                         jnp.tile(kc_g, 2).reshape(1, LANES))
    slopes = jnp.exp2(-8.0 * (jnp.arange(NSA_HEADS, dtype=f32) + 1.0) / NSA_HEADS)
    y_nsa = _nsa(slopes, q3, sm2.reshape(b, t, LANES), kcd, vcl, _overlap_t(nseg), ks, vs, kw, vw)

    gi4 = sm2[:, 3 * NSA_HEADS:3 * NSA_HEADS + MLSTM_HEADS].reshape(b, t, MLSTM_HEADS).transpose(0, 2, 1)
    gf4 = sm2[:, 3 * NSA_HEADS + MLSTM_HEADS:3 * NSA_HEADS + 2 * MLSTM_HEADS].reshape(
        b, t, MLSTM_HEADS).transpose(0, 2, 1)
    y_mem = _mlstm(b_i, b_f, u2.reshape(b, t, MLSTM_WIDTH), vm2.reshape(b, t, MLSTM_WIDTH),
                   op2.reshape(b, t, MLSTM_WIDTH), gi4.reshape(b, MLSTM_HEADS, t // MLSTM_L, MLSTM_L),
                   gf4.reshape(b, MLSTM_HEADS, t // MLSTM_L, MLSTM_L), conv_w, conv_b.reshape(1, MLSTM_WIDTH),
                   w_mq, w_mk,
                   mlstm_norm_g.reshape(MLSTM_HEADS, 1, MLSTM_HEAD_DIM), mlstm_skip.reshape(1, MLSTM_WIDTH))

    out = _ffn(x2, y_nsa.reshape(n, NSA_WIDTH), y_mem.reshape(n, MLSTM_WIDTH), w_out.astype(bf16),
               norm2_g.reshape(1, d), w_gate.astype(bf16), w_up.astype(bf16), w_down.astype(bf16))
    return out.reshape(b, t, d)


def kernel(x, norm1_g, w_in, q_norm_g, kc_norm_g, ks_norm_g, kw_norm_g, cmp_pos, w_ck1, w_ck2, w_cv1, w_cv2,
           conv_w, conv_b, w_mq, w_mk, b_i, b_f, mlstm_norm_g, mlstm_skip, w_out, norm2_g, w_gate, w_up, w_down):
    depth = norm1_g.shape[0]
    for l in range(depth):
        x = _layer(x, norm1_g[l], w_in[l], q_norm_g[l], kc_norm_g[l], ks_norm_g[l], kw_norm_g[l], cmp_pos[l],
                   w_ck1[l], w_ck2[l], w_cv1[l], w_cv2[l], conv_w[l], conv_b[l], w_mq[l], w_mk[l], b_i[l], b_f[l],
                   mlstm_norm_g[l], mlstm_skip[l], w_out[l], norm2_g[l], w_gate[l], w_up[l], w_down[l])
    return x
```

```python
import functools
import math

import numpy as np
import jax
import jax.numpy as jnp
from jax import lax
from jax.experimental import pallas as pl
from jax.experimental.pallas import tpu as pltpu

f32 = jnp.float32
bf16 = jnp.bfloat16

D_MODEL = 1024
NSA_HEADS = 8
NSA_KV_HEADS = 2
NSA_HEAD_DIM = 64
NSA_GROUP = NSA_HEADS // NSA_KV_HEADS
CMP_BLOCK = 32
CMP_STRIDE = 16
CMP_HIDDEN = 256
SEL_BLOCK = 64
SEL_TOPN = 16
WINDOW = 512
MLSTM_HEADS = 4
MLSTM_HEAD_DIM = 128
CONV_WIDTH = 4
NSA_WIDTH = NSA_HEADS * NSA_HEAD_DIM
MLSTM_WIDTH = MLSTM_HEADS * MLSTM_HEAD_DIM
KV_WIDTH = NSA_KV_HEADS * NSA_HEAD_DIM
D_FF = -(-8 * D_MODEL // (3 * 256)) * 256
NORM_EPS = 1e-6
NEG = -1e30
FORCE = 1e9

LANES = 128
HALF = 64
VMEM_LIMIT = 56 * 1024 * 1024

TM_PROJ = 512
TM_PREP = 512
TQ = 256
TK = 512
MLSTM_L = 256
MLSTM_UNROLL = 4
TM_FFN = 512
FF_CHUNK = 256

HI = lax.Precision.HIGHEST
NT = (((1,), (1,)), ((), ()))


def _cparams(sem):
    return pltpu.CompilerParams(dimension_semantics=sem, vmem_limit_bytes=VMEM_LIMIT)


def _split3(x):
    x1 = x.astype(bf16)
    r1 = x - x1.astype(f32)
    x2 = r1.astype(bf16)
    x3 = (r1 - x2.astype(f32)).astype(bf16)
    return x1, x2, x3


PROJ_SPLITS = (NSA_WIDTH, 6 * KV_WIDTH, MLSTM_WIDTH, MLSTM_WIDTH, MLSTM_WIDTH, LANES)


GATE_ROWS = 16


def _inproj_kernel(x_ref, g_ref, w_ref, wgt_ref, *out_refs):
    x = x_ref[...]
    h = x * lax.rsqrt(jnp.mean(x * x, axis=-1, keepdims=True) + NORM_EPS) * g_ref[...]
    hb = h.astype(bf16)
    off = 0
    for o_ref, width in zip(out_refs[:-1], PROJ_SPLITS):
        o_ref[...] = jnp.dot(hb, w_ref[:, off:off + width], preferred_element_type=f32)
        off += width
    out_refs[-1][...] = lax.dot_general(wgt_ref[...], hb, NT, preferred_element_type=f32)


def _inproj(x2, g1, w_perm, w_gates_t):
    n = x2.shape[0]
    wtot = sum(PROJ_SPLITS)
    return pl.pallas_call(
        _inproj_kernel,
        grid=(n // TM_PROJ,),
        in_specs=[
            pl.BlockSpec((TM_PROJ, D_MODEL), lambda i: (i, 0)),
            pl.BlockSpec((1, D_MODEL), lambda i: (0, 0)),
            pl.BlockSpec((D_MODEL, wtot), lambda i: (0, 0)),
            pl.BlockSpec((GATE_ROWS, D_MODEL), lambda i: (0, 0)),
        ],
        out_specs=[pl.BlockSpec((TM_PROJ, w), lambda i: (i, 0)) for w in PROJ_SPLITS] + [
            pl.BlockSpec((GATE_ROWS, TM_PROJ), lambda i: (0, i))],
        out_shape=[jax.ShapeDtypeStruct((n, w), f32) for w in PROJ_SPLITS] + [
            jax.ShapeDtypeStruct((GATE_ROWS, n), f32)],
        compiler_params=_cparams(("parallel",)),
        name="inproj",
    )(x2, g1, w_perm, w_gates_t)


def _pair_norm(x, gain, lo):
    sq = x * x
    s_lo = jnp.sum(jnp.where(lo, sq, 0.0), axis=-1, keepdims=True)
    s_hi = jnp.sum(jnp.where(lo, 0.0, sq), axis=-1, keepdims=True)
    inv = jnp.where(lo, lax.rsqrt(s_lo / HALF + NORM_EPS), lax.rsqrt(s_hi / HALF + NORM_EPS))
    return x * inv * gain


def _kvprep_kernel(kv_ref, q_ref, ksg_ref, kwg_ref, qg_ref, ks_ref, vs_ref, kw_ref, vw_ref, q3_ref):
    i = pl.program_id(1)
    tm = kv_ref.shape[1]
    lane = lax.broadcasted_iota(jnp.int32, (tm, LANES), 1)
    lo = lane < HALF
    row = i * tm + lax.broadcasted_iota(jnp.int32, (tm, LANES), 0)
    code = jnp.where(lax.shift_right_logical(row, 6) == (lane & (HALF - 1)), 1.0, 0.0)

    def put(x, o_ref, fill):
        o_ref[0, 0] = jnp.where(lo, x, fill).astype(bf16)
        o_ref[0, 1] = jnp.where(lo, pltpu.roll(x, HALF, 1), fill).astype(bf16)

    put(_pair_norm(kv_ref[0, :, 2 * KV_WIDTH:3 * KV_WIDTH], ksg_ref[...], lo), ks_ref, code)
    put(kv_ref[0, :, 3 * KV_WIDTH:4 * KV_WIDTH], vs_ref, 1.0)
    put(_pair_norm(kv_ref[0, :, 4 * KV_WIDTH:5 * KV_WIDTH], kwg_ref[...], lo), kw_ref, 0.0)
    put(kv_ref[0, :, 5 * KV_WIDTH:6 * KV_WIDTH], vw_ref, 1.0)

    scale = NSA_HEAD_DIM ** -0.5 * math.log2(math.e)
    for c in range(NSA_HEADS // 2):
        pair = _pair_norm(q_ref[0, :, c * LANES:(c + 1) * LANES], qg_ref[:, c * LANES:(c + 1) * LANES], lo) * scale
        for par, x in enumerate((pair, pltpu.roll(pair, HALF, 1))):
            hi = jnp.where(lo, x, 0.0).astype(bf16).astype(f32)
            res = jnp.where(lo, x - hi, 0.0)
            head = 2 * c + par
            q3_ref[0, head // NSA_GROUP, head % NSA_GROUP] = jnp.concatenate(
                [hi + pltpu.roll(res, HALF, 1), hi], axis=1).astype(bf16)


def _kvprep(kv3, q3d, ksg2, kwg2, qg2):
    b, t, _ = kv3.shape
    tm = min(TM_PREP, t)
    out_spec = pl.BlockSpec((1, NSA_KV_HEADS, tm, LANES), lambda bi, i: (bi, 0, i, 0))
    out_shape = jax.ShapeDtypeStruct((b, NSA_KV_HEADS, t, LANES), bf16)
    return pl.pallas_call(
        _kvprep_kernel,
        grid=(b, t // tm),
        in_specs=[
            pl.BlockSpec((1, tm, 6 * KV_WIDTH), lambda bi, i: (bi, i, 0)),
            pl.BlockSpec((1, tm, NSA_WIDTH), lambda bi, i: (bi, i, 0)),
            pl.BlockSpec((1, LANES), lambda bi, i: (0, 0)),
            pl.BlockSpec((1, LANES), lambda bi, i: (0, 0)),
            pl.BlockSpec((1, NSA_WIDTH), lambda bi, i: (0, 0)),
        ],
        out_specs=[out_spec] * 4 + [
            pl.BlockSpec((1, NSA_KV_HEADS, NSA_GROUP, tm, 2 * LANES), lambda bi, i: (bi, 0, 0, i, 0))],
        out_shape=[out_shape] * 4 + [
            jax.ShapeDtypeStruct((b, NSA_KV_HEADS, NSA_GROUP, t, 2 * LANES), bf16)],
        compiler_params=_cparams(("parallel", "parallel")),
        name="kvprep",
    )(kv3, q3d, ksg2, kwg2, qg2)


def _gelu_tanh(x):
    return 0.5 * x * (1.0 + jnp.tanh(math.sqrt(2.0 / math.pi) * (x + 0.044715 * (x * x * x))))


def _hi_lo(x):
    hi = x.astype(bf16)
    return hi, (x - hi.astype(f32)).astype(bf16)


def _compress_kernel(kc_ref, vc_ref, wk1_ref, wk1s_ref, wk2s_ref, wv1_ref, wv1s_ref, wv2s_ref, pos_ref, kcg_ref,
                     kcd_ref, vcl_ref):
    nseg = kc_ref.shape[1] // CMP_STRIDE
    half = NSA_KV_HEADS * CMP_HIDDEN

    def branch(x_ref, w1_ref, w1s_ref, w2s_ref):
        terms = []
        for r in range(CMP_STRIDE):
            hi, lo = _hi_lo(x_ref[0, pl.ds(r, nseg, stride=CMP_STRIDE), :])
            terms += [hi, lo, hi]
        acc = jnp.dot(jnp.concatenate(terms, axis=1), w1s_ref[...], preferred_element_type=f32)
        pos_term = jnp.dot(pos_ref[...], w1_ref[...], precision=HI, preferred_element_type=f32)[0:1]
        hid = (acc[:, 0:half] + pltpu.roll(acc[:, half:2 * half], nseg - 1, 0)
               + jnp.concatenate([pos_term] * NSA_KV_HEADS, axis=1))
        hi, lo = _hi_lo(_gelu_tanh(hid))
        return jnp.dot(jnp.concatenate([hi, lo, hi], axis=1), w2s_ref[...], preferred_element_type=f32)

    kc2 = branch(kc_ref, wk1_ref, wk1s_ref, wk2s_ref)
    vc2 = branch(vc_ref, wv1_ref, wv1s_ref, wv2s_ref)
    lane = lax.broadcasted_iota(jnp.int32, (nseg, LANES), 1)
    for g in range(NSA_KV_HEADS):
        kc = kc2[:, g * LANES:(g + 1) * LANES]
        kc = kc * lax.rsqrt(jnp.mean(kc * kc, axis=-1, keepdims=True) + NORM_EPS) * kcg_ref[...]
        k_hi = kc.astype(bf16)
        k_lo = jnp.where(lane < HALF, kc - k_hi.astype(f32), 0.0).astype(bf16)
        kcd_ref[0, g] = jnp.concatenate([k_hi, k_lo], axis=1)
        vcl_ref[0, g] = jnp.where(lane < HALF, vc2[:, g * LANES:(g + 1) * LANES], 0.0).astype(bf16)


def _compress_weights(w1, w2):
    w1r = w1.reshape(2, CMP_STRIDE, NSA_HEAD_DIM, CMP_HIDDEN)
    z = jnp.zeros_like(w1r[0])

    def two_groups(w):
        return jnp.concatenate([jnp.concatenate([w, z], axis=2), jnp.concatenate([z, w], axis=2)], axis=1)

    wr = jnp.concatenate([two_groups(w1r[0]), two_groups(w1r[1])], axis=2)
    hi, lo = _hi_lo(wr)
    w1s = jnp.concatenate([hi, hi, lo], axis=1).reshape(CMP_STRIDE * 3 * LANES, 2 * NSA_KV_HEADS * CMP_HIDDEN)
    z2 = jnp.zeros_like(w2)
    w2d = jnp.concatenate([jnp.concatenate([w2, w2, z2, z2], axis=1),
                           jnp.concatenate([z2, z2, w2, w2], axis=1)], axis=0)
    hi2, lo2 = _hi_lo(w2d)
    return w1s, jnp.concatenate([hi2, hi2, lo2], axis=0)


def _compress(kv3, wk1, wk1s, wk2s, wv1, wv1s, wv2s, pos8, kcg2):
    b, t, _ = kv3.shape
    nseg = t // CMP_STRIDE

    def const(shape):
        return pl.BlockSpec(shape, lambda bi: (0,) * len(shape), pipeline_mode=pl.Buffered(1))

    return pl.pallas_call(
        _compress_kernel,
        grid=(b,),
        in_specs=[pl.BlockSpec((1, t, KV_WIDTH), lambda bi: (bi, 0, 0)),
                  pl.BlockSpec((1, t, KV_WIDTH), lambda bi: (bi, 0, 1)),
                  const(wk1.shape), const(wk1s.shape), const(wk2s.shape),
                  const(wv1.shape), const(wv1s.shape), const(wv2s.shape),
                  const(pos8.shape), const(kcg2.shape)],
        out_specs=[pl.BlockSpec((1, NSA_KV_HEADS, nseg, 2 * LANES), lambda bi: (bi, 0, 0, 0)),
                   pl.BlockSpec((1, NSA_KV_HEADS, nseg, LANES), lambda bi: (bi, 0, 0, 0))],
        out_shape=[jax.ShapeDtypeStruct((b, NSA_KV_HEADS, nseg, 2 * LANES), bf16),
                   jax.ShapeDtypeStruct((b, NSA_KV_HEADS, nseg, LANES), bf16)],
        compiler_params=_cparams(("parallel",)),
        name="compress",
    )(kv3, kv3, wk1, wk1s, wk2s, wv1, wv1s, wv2s, pos8, kcg2)


def _nsa_kernel(slopes_ref, q3_ref, sm_ref, kcd_ref, vcl_ref, ovt_ref,
                ks_ref, vs_ref, kw_ref, vw_ref, y_ref, qa_scr, m_scr, acc_scr):
    g = pl.program_id(1)
    qi = pl.program_id(2)
    q0 = qi * TQ
    ncp = kcd_ref.shape[2]
    nsel = ncp // (SEL_BLOCK // CMP_STRIDE)
    rows = NSA_GROUP * TQ
    log2e = math.log2(math.e)

    lane = lax.broadcasted_iota(jnp.int32, (TQ, LANES), 1)
    lo = lane < HALF
    slopes = [slopes_ref[g * NSA_GROUP + r] * log2e for r in range(NSA_GROUP)]

    def per_head(x, fn):
        return jnp.concatenate([fn(r, x[r * TQ:(r + 1) * TQ]) for r in range(NSA_GROUP)], axis=0)

    q3 = q3_ref[0, 0].reshape(rows, 2 * LANES)
    q1 = q3[:, 0:LANES]

    cidx = lax.broadcasted_iota(jnp.int32, (TQ, ncp), 1)
    tpos_c = q0 + lax.broadcasted_iota(jnp.int32, (TQ, ncp), 0)
    blk_end = cidx * CMP_STRIDE + (CMP_BLOCK - 1)
    valid_c = jnp.logical_and(tpos_c >= blk_end, cidx < ncp - 1)
    krel_c = (CMP_BLOCK - 1 - q0 + CMP_STRIDE * lax.broadcasted_iota(jnp.int32, (1, ncp), 1)).astype(f32)
    s_c = lax.dot_general(q3, kcd_ref[0, 0], NT, preferred_element_type=f32)
    s_c = per_head(s_c, lambda r, x: jnp.where(valid_c, x + slopes[r] * krel_c, NEG))
    e_c = jnp.exp2(s_c - jnp.max(s_c, axis=-1, keepdims=True))
    p_c = e_c / jnp.sum(e_c, axis=-1, keepdims=True)
    p_c = per_head(p_c, lambda r, x: jnp.where(valid_c, x, 0.0))
    p_sum = p_c[0:TQ]
    for r in range(1, NSA_GROUP):
        p_sum = p_sum + p_c[r * TQ:(r + 1) * TQ]
    o_cmp = jnp.dot(p_c.astype(bf16), vcl_ref[0, 0], preferred_element_type=f32)

    band = WINDOW + TQ
    kb = pl.multiple_of(jnp.maximum(q0 - WINDOW, 0), TQ)
    s_w = lax.dot_general(q1, kw_ref[0, 0, pl.ds(kb, band), :], NT, preferred_element_type=f32)
    krel_w = (kb - q0 + lax.broadcasted_iota(jnp.int32, (1, band), 1)).astype(f32)
    dist_w = (q0 - kb + lax.broadcasted_iota(jnp.int32, (TQ, band), 0)
              - lax.broadcasted_iota(jnp.int32, (TQ, band), 1))
    valid_w = jnp.logical_and(dist_w >= 0, dist_w < WINDOW)
    s_w = per_head(s_w, lambda r, x: jnp.where(valid_w, x + slopes[r] * krel_w, NEG))
    p_w = jnp.exp2(s_w - jnp.max(s_w, axis=-1, keepdims=True))
    acc_w = jnp.dot(p_w.astype(bf16), vw_ref[0, 0, pl.ds(kb, band), :], preferred_element_type=f32)
    o_win = acc_w / pltpu.roll(acc_w, HALF, 1)

    imp = lax.dot_general(ovt_ref[...], jnp.concatenate(_split3(p_sum), axis=1), NT,
                          preferred_element_type=f32)[0:nsel]
    jrow = lax.broadcasted_iota(jnp.int32, (nsel, TQ), 0)
    tcol = q0 + lax.broadcasted_iota(jnp.int32, (nsel, TQ), 1)
    forced = jnp.logical_or(jrow == lax.shift_right_logical(tcol, 6), jrow == 0)
    future = jrow * SEL_BLOCK > tcol
    work = jnp.where(forced, FORCE, jnp.where(future, -FORCE, imp))
    jrow_f = jrow.astype(f32)
    bias_t = jnp.full((nsel, TQ), NEG, f32)
    for _ in range(min(SEL_TOPN, nsel)):
        best = jnp.max(work, axis=0, keepdims=True)
        first = jnp.min(jnp.where(work == best, jrow_f, float(nsel)), axis=0, keepdims=True)
        hit = jrow_f == first
        bias_t = jnp.where(hit, 0.0, bias_t)
        work = jnp.where(hit, -3e38, work)
    if nsel < HALF:
        bias_t = jnp.concatenate([bias_t, jnp.full((HALF - nsel, TQ), NEG, f32)], axis=0)
    bias = jnp.concatenate([bias_t, bias_t], axis=0).T

    bias_b = bias.astype(bf16)
    qa_scr[...] = per_head(q1, lambda r, x: jnp.where(lo, x, bias_b))

    m_scr[...] = jnp.full(m_scr.shape, NEG, f32)
    acc_scr[...] = jnp.zeros(acc_scr.shape, f32)

    def sel_tile(state, kt, causal):
        m_old, acc = state
        k0 = pl.multiple_of(kt * TK, TK)
        s = lax.dot_general(qa_scr[...], ks_ref[0, 0, pl.ds(k0, TK), :], NT, preferred_element_type=f32)
        krel = (k0 - q0 + lax.broadcasted_iota(jnp.int32, (1, TK), 1)).astype(f32)
        if causal:
            ahead = (k0 - q0 + lax.broadcasted_iota(jnp.int32, (TQ, TK), 1)
                     > lax.broadcasted_iota(jnp.int32, (TQ, TK), 0))
            s = per_head(s, lambda r, x: jnp.where(ahead, NEG, x + slopes[r] * krel))
        else:
            s = per_head(s, lambda r, x: x + slopes[r] * krel)
        m_new = jnp.maximum(m_old, jnp.max(s, axis=-1, keepdims=True))
        p = jnp.exp2(s - m_new[:, 0:1])
        acc = jnp.exp2(m_old - m_new) * acc + jnp.dot(p.astype(bf16), vs_ref[0, 0, pl.ds(k0, TK), :],
                                                      preferred_element_type=f32)
        return m_new, acc

    def sel_tiles(tiles):
        state = (m_scr[...], acc_scr[...])
        for kt, causal in tiles:
            state = sel_tile(state, kt, causal)
        m_scr[...] = state[0]
        acc_scr[...] = state[1]

    n_full = q0 // TK

    def sel_body(i, carry):
        sel_tiles([(2 * i, False), (2 * i + 1, False)])
        return carry

    lax.fori_loop(0, n_full // 2, sel_body, 0)

    @pl.when(n_full % 2 == 1)
    def _():
        sel_tiles([(n_full - 1, False), (n_full, True)])

    @pl.when(n_full % 2 == 0)
    def _():
        sel_tiles([(n_full, True)])

    acc_s = acc_scr[...]
    o_slc = acc_s / pltpu.roll(acc_s, HALF, 1)

    sig = jax.nn.sigmoid(sm_ref[0])

    def gate(r, branch):
        col = (g * NSA_GROUP + r) * 3 + branch
        return jnp.sum(jnp.where(lane == col, sig, 0.0), axis=-1, keepdims=True)

    def mix(r):
        sl = slice(r * TQ, (r + 1) * TQ)
        return gate(r, 0) * o_cmp[sl] + gate(r, 1) * o_slc[sl] + gate(r, 2) * o_win[sl]

    for c in range(NSA_GROUP // 2):
        y_ref[0, :, c * LANES:(c + 1) * LANES] = jnp.where(lo, mix(2 * c), pltpu.roll(mix(2 * c + 1), HALF, 1))


def _nsa(slopes, q3, sm3, kcd, vcl, ovt, ks, vs, kw, vw):
    b, t = q3.shape[0], q3.shape[3]
    gw = NSA_GROUP * NSA_HEAD_DIM
    ncp = kcd.shape[2]

    def kv_spec(rows):
        return pl.BlockSpec((1, 1, rows, LANES), lambda bi, gi, qi: (bi, gi, 0, 0))

    return pl.pallas_call(
        _nsa_kernel,
        grid=(b, NSA_KV_HEADS, t // TQ),
        in_specs=[
            pl.BlockSpec(memory_space=pltpu.SMEM),
            pl.BlockSpec((1, 1, NSA_GROUP, TQ, 2 * LANES), lambda bi, gi, qi: (bi, gi, 0, qi, 0)),
            pl.BlockSpec((1, TQ, LANES), lambda bi, gi, qi: (bi, qi, 0)),
            pl.BlockSpec((1, 1, ncp, 2 * LANES), lambda bi, gi, qi: (bi, gi, 0, 0)), kv_spec(ncp),
            pl.BlockSpec(ovt.shape, lambda bi, gi, qi: (0, 0)),
            kv_spec(t), kv_spec(t), kv_spec(t), kv_spec(t),
        ],
        out_specs=pl.BlockSpec((1, TQ, gw), lambda bi, gi, qi: (bi, qi, gi)),
        out_shape=jax.ShapeDtypeStruct((b, t, NSA_WIDTH), f32),
        scratch_shapes=[
            pltpu.VMEM((NSA_GROUP * TQ, LANES), bf16),
            pltpu.VMEM((NSA_GROUP * TQ, LANES), f32),
            pltpu.VMEM((NSA_GROUP * TQ, LANES), f32),
        ],
        compiler_params=_cparams(("parallel", "parallel", "arbitrary")),
        name="nsa",
    )(slopes, q3, sm3, kcd, vcl, ovt, ks, vs, kw, vw)


def _log_sigmoid(x):
    return jnp.minimum(x, 0.0) - jnp.log1p(jnp.exp(-jnp.abs(x)))


def _mlstm_kernel(bi_ref, bf_ref, u_ref, v_ref, op_ref, gi_ref, gf_ref, cw_ref, cb_ref, wq_ref, wk_ref,
                  ng_ref, sk_ref, y_ref, uc_scr, q_scr, kt_scr, ct_scr, m_scr, b_scr, li_scr, xp_scr):
    h = pl.program_id(1)
    t = u_ref.shape[1]
    L = MLSTM_L
    dm = MLSTM_HEAD_DIM

    x = u_ref[0]
    xp_scr[0:8, :] = jnp.zeros((8, dm), f32)
    xp_scr[8:, :] = x
    acc = x * cw_ref[CONV_WIDTH - 1:CONV_WIDTH, :]
    for s in range(1, CONV_WIDTH):
        acc = acc + xp_scr[8 - s:8 - s + t, :] * cw_ref[CONV_WIDTH - 1 - s:CONV_WIDTH - s, :]
    uc = acc + cb_ref[...]
    uc = uc * jax.nn.sigmoid(uc)
    uc_scr[...] = uc
    ucb = uc.astype(bf16)
    q_scr[...] = jnp.dot(ucb, wq_ref[0].astype(bf16), preferred_element_type=f32).astype(bf16)
    k = jnp.dot(ucb, wk_ref[0].astype(bf16), preferred_element_type=f32) * (dm ** -0.5)
    kt_scr[...] = k.T

    ct_scr[...] = jnp.zeros(ct_scr.shape, f32)
    m_scr[...] = jnp.zeros(m_scr.shape, f32)

    li_ = lax.broadcasted_iota(jnp.int32, (L, L), 0)
    si_ = lax.broadcasted_iota(jnp.int32, (L, L), 1)
    causal = si_ <= li_
    diag = si_ == li_
    ones_v = jnp.ones((L, dm), f32)

    log_f = _log_sigmoid(gf_ref[0, 0] + bf_ref[h])
    upper = jnp.where(li_ <= si_, 1.0, 0.0).astype(bf16)
    b_scr[...] = jnp.dot(jnp.concatenate(_split3(log_f), axis=1), jnp.concatenate([upper] * 3, axis=0),
                         preferred_element_type=f32)
    li_scr[...] = gi_ref[0, 0] + bi_ref[h]

    def chunk(c, ct, m_prev):
        r0 = pl.multiple_of(c * L, L)
        qc = q_scr[pl.ds(r0, L), :]
        ktc = kt_scr[:, pl.ds(r0, L)]
        vaug = jnp.concatenate([v_ref[0, pl.ds(r0, L), :], ones_v], axis=1).astype(bf16)
        log_i = li_scr[pl.ds(c, 1), :]
        b_row = b_scr[pl.ds(c, 1), :]
        b_col = jnp.sum(jnp.where(diag, b_row, 0.0), axis=-1, keepdims=True)
        g_sum = b_row[:, L - 1:L]
        dmat = jnp.where(causal, (b_col - b_row) + log_i, NEG)
        m_loc = jnp.max(dmat, axis=-1, keepdims=True)
        p = jnp.dot(qc, ktc.astype(bf16), preferred_element_type=f32) * jnp.exp(dmat - m_loc)
        intra = jnp.dot(p.astype(bf16), vaug, preferred_element_type=f32)
        m_inter = b_col + m_prev
        m_out = jnp.maximum(m_inter, m_loc)
        xo = (jnp.exp(m_inter - m_out) * jnp.dot(qc, ct.astype(bf16), preferred_element_type=f32)
              + jnp.exp(m_loc - m_out) * intra)
        num = xo[:, 0:dm]
        den = xo[:, dm:2 * dm]
        hh = num / jnp.maximum(jnp.abs(den), jnp.exp(-m_out))
        hh = hh * jax.nn.sigmoid(op_ref[0, pl.ds(r0, L), :])
        hh = hh * lax.rsqrt(jnp.mean(hh * hh, axis=-1, keepdims=True) + NORM_EPS) * ng_ref[0]
        y_ref[0, pl.ds(r0, L), :] = hh + sk_ref[...] * uc_scr[pl.ds(r0, L), :]

        w_end = (g_sum - b_row) + log_i
        m_new = jnp.maximum(g_sum + m_prev, jnp.max(w_end, axis=-1, keepdims=True))
        decay = jnp.exp(g_sum + m_prev - m_new)
        w = jnp.exp(w_end - m_new)
        return decay * ct + jnp.dot((ktc * w).astype(bf16), vaug, preferred_element_type=f32), m_new

    def chunk_group(i, carry):
        ct, m_prev = ct_scr[...], m_scr[0:1, 0:1]
        for j in range(MLSTM_UNROLL):
            ct, m_prev = chunk(i * MLSTM_UNROLL + j, ct, m_prev)
        ct_scr[...] = ct
        m_scr[...] = jnp.broadcast_to(m_prev, m_scr.shape)
        return carry

    lax.fori_loop(0, t // (L * MLSTM_UNROLL), chunk_group, 0)


def _mlstm(b_i, b_f, u3, v3, op3, gi4, gf4, cw, cb2, wq, wk, ng3, sk2):
    b, t, _ = u3.shape
    dm = MLSTM_HEAD_DIM
    seq = pl.BlockSpec((1, t, dm), lambda bi, hi: (bi, 0, hi))
    gate = pl.BlockSpec((1, 1, t // MLSTM_L, MLSTM_L), lambda bi, hi: (bi, hi, 0, 0))
    smem = pl.BlockSpec(memory_space=pltpu.SMEM)
    return pl.pallas_call(
        _mlstm_kernel,
        grid=(b, MLSTM_HEADS),
        in_specs=[
            smem, smem, seq, seq, seq, gate, gate,
            pl.BlockSpec((CONV_WIDTH, dm), lambda bi, hi: (0, hi)),
            pl.BlockSpec((1, dm), lambda bi, hi: (0, hi)),
            pl.BlockSpec((1, dm, dm), lambda bi, hi: (hi, 0, 0)),
            pl.BlockSpec((1, dm, dm), lambda bi, hi: (hi, 0, 0)),
            pl.BlockSpec((1, 1, dm), lambda bi, hi: (hi, 0, 0)),
            pl.BlockSpec((1, dm), lambda bi, hi: (0, hi)),
        ],
        out_specs=seq,
        out_shape=jax.ShapeDtypeStruct((b, t, MLSTM_WIDTH), f32),
        scratch_shapes=[
            pltpu.VMEM((t, dm), f32),
            pltpu.VMEM((t, dm), bf16),
            pltpu.VMEM((dm, t), f32),
            pltpu.VMEM((dm, 2 * dm), f32),
            pltpu.VMEM((8, LANES), f32),
            pltpu.VMEM((t // MLSTM_L, MLSTM_L), f32),
            pltpu.VMEM((t // MLSTM_L, MLSTM_L), f32),
            pltpu.VMEM((t + 8, dm), f32),
        ],
        compiler_params=_cparams(("parallel", "parallel")),
        name="mlstm",
    )(b_i, b_f, u3, v3, op3, gi4, gf4, cw, cb2, wq, wk, ng3, sk2)


def _ffn_kernel(x_ref, ya_ref, yb_ref, wo_ref, g2_ref, wg_ref, wu_ref, wd_ref, o_ref, act_scr):
    x1 = (x_ref[...]
          + jnp.dot(ya_ref[...].astype(bf16), wo_ref[0:NSA_WIDTH, :], preferred_element_type=f32)
          + jnp.dot(yb_ref[...].astype(bf16), wo_ref[NSA_WIDTH:NSA_WIDTH + MLSTM_WIDTH, :],
                    preferred_element_type=f32))
    h2 = x1 * lax.rsqrt(jnp.mean(x1 * x1, axis=-1, keepdims=True) + NORM_EPS) * g2_ref[...]
    h2b = h2.astype(bf16)
    for c in range(D_FF // FF_CHUNK):
        cols = slice(c * FF_CHUNK, (c + 1) * FF_CHUNK)
        gt = jnp.dot(h2b, wg_ref[:, cols], preferred_element_type=f32)
        up = jnp.dot(h2b, wu_ref[:, cols], preferred_element_type=f32)
        act_scr[:, cols] = (gt * jax.nn.sigmoid(gt) * up).astype(bf16)
    o_ref[...] = x1 + jnp.dot(act_scr[...], wd_ref[...], preferred_element_type=f32)


def _ffn(x2, ya, yb, wo, g2, wg, wu, wd):
    n = x2.shape[0]
    tm = TM_FFN

    def const(shape):
        return pl.BlockSpec(shape, lambda i: (0, 0), pipeline_mode=pl.Buffered(1))

    return pl.pallas_call(
        _ffn_kernel,
        grid=(n // tm,),
        in_specs=[
            pl.BlockSpec((tm, D_MODEL), lambda i: (i, 0)),
            pl.BlockSpec((tm, NSA_WIDTH), lambda i: (i, 0)),
            pl.BlockSpec((tm, MLSTM_WIDTH), lambda i: (i, 0)),
            const(wo.shape), const(g2.shape), const(wg.shape), const(wu.shape), const(wd.shape),
        ],
        out_specs=pl.BlockSpec((tm, D_MODEL), lambda i: (i, 0)),
        out_shape=jax.ShapeDtypeStruct((n, D_MODEL), f32),
        scratch_shapes=[pltpu.VMEM((tm, D_FF), bf16)],
        compiler_params=_cparams(("parallel",)),
        name="ffn",
    )(x2, ya, yb, wo, g2, wg, wu, wd)


def _overlap_t(ncp):
    nsel = ncp // (SEL_BLOCK // CMP_STRIDE)
    cs = np.arange(ncp) * CMP_STRIDE
    ss = np.arange(nsel) * SEL_BLOCK
    ov = ((cs[None, :] < ss[:, None] + SEL_BLOCK) & (cs[None, :] + CMP_BLOCK > ss[:, None])).astype(np.float32)
    ov[:, ncp - 1] = 0.0
    out = np.zeros((LANES, ncp), np.float32)
    out[:nsel] = ov
    return jnp.asarray(np.tile(out, (1, 3)), dtype=bf16)


def _layer(x, norm1_g, w_in, q_g, kc_g, ks_g, kw_g, cmp_pos, w_ck1, w_ck2, w_cv1, w_cv2, conv_w, conv_b,
           w_mq, w_mk, b_i, b_f, mlstm_norm_g, mlstm_skip, w_out, norm2_g, w_gate, w_up, w_down):
    b, t, d = x.shape
    n = b * t
    x2 = x.reshape(n, d)

    o_gate = NSA_WIDTH + 6 * KV_WIDTH
    o_u = o_gate + 3 * NSA_HEADS
    o_if = o_u + 3 * MLSTM_WIDTH
    w_perm = jnp.concatenate([
        w_in[:, :o_gate], w_in[:, o_u:o_if], w_in[:, o_gate:o_u], w_in[:, o_if:],
        jnp.zeros((d, LANES - 3 * NSA_HEADS - 2 * MLSTM_HEADS), w_in.dtype)], axis=1).astype(bf16)
    w_gates_t = jnp.concatenate([w_in[:, o_if:].T, jnp.zeros((GATE_ROWS - 2 * MLSTM_HEADS, d), w_in.dtype)],
                                axis=0).astype(bf16)
    q2, kv2, u2, vm2, op2, sm2, gates_t = _inproj(x2, norm1_g.reshape(1, d), w_perm, w_gates_t)

    kv3 = kv2.reshape(b, t, 6 * KV_WIDTH)
    ks, vs, kw, vw, q3 = _kvprep(kv3, q2.reshape(b, t, NSA_WIDTH), jnp.tile(ks_g, 2).reshape(1, LANES),
                                 jnp.tile(kw_g, 2).reshape(1, LANES), jnp.tile(q_g, NSA_HEADS).reshape(1, NSA_WIDTH))
    nseg = t // CMP_STRIDE
    pos8 = jnp.broadcast_to(cmp_pos.reshape(1, CMP_BLOCK * NSA_HEAD_DIM), (8, CMP_BLOCK * NSA_HEAD_DIM))
    wk1s, wk2s = _compress_weights(w_ck1, w_ck2)
    wv1s, wv2s = _compress_weights(w_cv1, w_cv2)
    kcd, vcl = _compress(kv3, w_ck1, wk1s, wk2s, w_cv1, wv1s, wv2s, pos8, jnp.tile(kc_g, 2).reshape(1, LANES))
    slopes = jnp.exp2(-8.0 * (jnp.arange(NSA_HEADS, dtype=f32) + 1.0) / NSA_HEADS)
    y_nsa = _nsa(slopes, q3, sm2.reshape(b, t, LANES), kcd, vcl, _overlap_t(nseg), ks, vs, kw, vw)

    gi4 = gates_t[0:MLSTM_HEADS].reshape(MLSTM_HEADS, b, t).transpose(1, 0, 2)
    gf4 = gates_t[MLSTM_HEADS:2 * MLSTM_HEADS].reshape(MLSTM_HEADS, b, t).transpose(1, 0, 2)
    y_mem = _mlstm(b_i, b_f, u2.reshape(b, t, MLSTM_WIDTH), vm2.reshape(b, t, MLSTM_WIDTH),
                   op2.reshape(b, t, MLSTM_WIDTH), gi4.reshape(b, MLSTM_HEADS, t // MLSTM_L, MLSTM_L),
                   gf4.reshape(b, MLSTM_HEADS, t // MLSTM_L, MLSTM_L), conv_w, conv_b.reshape(1, MLSTM_WIDTH),
                   w_mq, w_mk,
                   mlstm_norm_g.reshape(MLSTM_HEADS, 1, MLSTM_HEAD_DIM), mlstm_skip.reshape(1, MLSTM_WIDTH))

    out = _ffn(x2, y_nsa.reshape(n, NSA_WIDTH), y_mem.reshape(n, MLSTM_WIDTH), w_out.astype(bf16),
               norm2_g.reshape(1, d), w_gate.astype(bf16), w_up.astype(bf16), w_down.astype(bf16))
    return out.reshape(b, t, d)


def kernel(x, norm1_g, w_in, q_norm_g, kc_norm_g, ks_norm_g, kw_norm_g, cmp_pos, w_ck1, w_ck2, w_cv1, w_cv2,
           conv_w, conv_b, w_mq, w_mk, b_i, b_f, mlstm_norm_g, mlstm_skip, w_out, norm2_g, w_gate, w_up, w_down):
    depth = norm1_g.shape[0]
    for l in range(depth):
        x = _layer(x, norm1_g[l], w_in[l], q_norm_g[l], kc_norm_g[l], ks_norm_g[l], kw_norm_g[l], cmp_pos[l],
                   w_ck1[l], w_ck2[l], w_cv1[l], w_cv2[l], conv_w[l], conv_b[l], w_mq[l], w_mk[l], b_i[l], b_f[l],
                   mlstm_norm_g[l], mlstm_skip[l], w_out[l], norm2_g[l], w_gate[l], w_up[l], w_down[l])
    return x
```

```python
import functools
import math

import numpy as np
import jax
import jax.numpy as jnp
from jax import lax
from jax.experimental import pallas as pl
from jax.experimental.pallas import tpu as pltpu

f32 = jnp.float32
bf16 = jnp.bfloat16

D_MODEL = 1024
NSA_HEADS = 8
NSA_KV_HEADS = 2
NSA_HEAD_DIM = 64
NSA_GROUP = NSA_HEADS // NSA_KV_HEADS
CMP_BLOCK = 32
CMP_STRIDE = 16
CMP_HIDDEN = 256
SEL_BLOCK = 64
SEL_TOPN = 16
WINDOW = 512
MLSTM_HEADS = 4
MLSTM_HEAD_DIM = 128
CONV_WIDTH = 4
NSA_WIDTH = NSA_HEADS * NSA_HEAD_DIM
MLSTM_WIDTH = MLSTM_HEADS * MLSTM_HEAD_DIM
KV_WIDTH = NSA_KV_HEADS * NSA_HEAD_DIM
D_FF = -(-8 * D_MODEL // (3 * 256)) * 256
NORM_EPS = 1e-6
NEG = -1e30
FORCE = 1e9

LANES = 128
HALF = 64
VMEM_LIMIT = 56 * 1024 * 1024

TM_PROJ = 512
TQ = 256
TK = 512
MLSTM_L = 256
MLSTM_UNROLL = 4
TM_FFN = 512
FF_CHUNK = 256

HI = lax.Precision.HIGHEST
NT = (((1,), (1,)), ((), ()))


def _cparams(sem):
    return pltpu.CompilerParams(dimension_semantics=sem, vmem_limit_bytes=VMEM_LIMIT)


def _split3(x):
    x1 = x.astype(bf16)
    r1 = x - x1.astype(f32)
    x2 = r1.astype(bf16)
    x3 = (r1 - x2.astype(f32)).astype(bf16)
    return x1, x2, x3


COL_Q = 0
COL_KV = COL_Q + NSA_WIDTH
COL_U = COL_KV + 6 * KV_WIDTH
COL_VM = COL_U + MLSTM_WIDTH
COL_OP = COL_VM + MLSTM_WIDTH
COL_SM = COL_OP + MLSTM_WIDTH
PROJ_WIDTH = COL_SM + LANES
GATE_ROWS = 16


def _pair_norm(x, gain, lo):
    sq = x * x
    s_lo = jnp.sum(jnp.where(lo, sq, 0.0), axis=-1, keepdims=True)
    s_hi = jnp.sum(jnp.where(lo, 0.0, sq), axis=-1, keepdims=True)
    inv = jnp.where(lo, lax.rsqrt(s_lo / HALF + NORM_EPS), lax.rsqrt(s_hi / HALF + NORM_EPS))
    return x * inv * gain


def _inproj_kernel(x_ref, g_ref, w_ref, wgt_ref, ksg_ref, kwg_ref, qg_ref,
                   kvc_ref, u_ref, vm_ref, op_ref, sm_ref, gt_ref, ks_ref, vs_ref, kw_ref, vw_ref, q3_ref,
                   *, tiles_per_seq):
    tm = x_ref.shape[0]
    x = x_ref[...]
    h = x * lax.rsqrt(jnp.mean(x * x, axis=-1, keepdims=True) + NORM_EPS) * g_ref[...]
    hb = h.astype(bf16)

    def proj(col, width):
        return jnp.dot(hb, w_ref[:, col:col + width], preferred_element_type=f32)

    lane = lax.broadcasted_iota(jnp.int32, (tm, LANES), 1)
    lo = lane < HALF
    pos = (pl.program_id(0) % tiles_per_seq) * tm + lax.broadcasted_iota(jnp.int32, (tm, LANES), 0)

    q = proj(COL_Q, NSA_WIDTH)
    scale = NSA_HEAD_DIM ** -0.5 * math.log2(math.e)
    for c in range(NSA_HEADS // 2):
        pair = _pair_norm(q[:, c * LANES:(c + 1) * LANES], qg_ref[:, c * LANES:(c + 1) * LANES], lo) * scale
        for par, xq in enumerate((pair, pltpu.roll(pair, HALF, 1))):
            hi = jnp.where(lo, xq, 0.0).astype(bf16).astype(f32)
            res = jnp.where(lo, xq - hi, 0.0)
            head = 2 * c + par
            q3_ref[0, head // NSA_GROUP, head % NSA_GROUP] = jnp.concatenate(
                [hi + pltpu.roll(res, HALF, 1), hi], axis=1).astype(bf16)

    kv = proj(COL_KV, 6 * KV_WIDTH)
    kvc_ref[...] = kv[:, 0:2 * KV_WIDTH]
    code = jnp.where(lax.shift_right_logical(pos, 6) == (lane & (HALF - 1)), 1.0, 0.0)

    def put(xk, o_ref, fill):
        o_ref[0, 0] = jnp.where(lo, xk, fill).astype(bf16)
        o_ref[0, 1] = jnp.where(lo, pltpu.roll(xk, HALF, 1), fill).astype(bf16)

    put(_pair_norm(kv[:, 2 * KV_WIDTH:3 * KV_WIDTH], ksg_ref[...], lo), ks_ref, code)
    put(kv[:, 3 * KV_WIDTH:4 * KV_WIDTH], vs_ref, 1.0)
    put(_pair_norm(kv[:, 4 * KV_WIDTH:5 * KV_WIDTH], kwg_ref[...], lo), kw_ref, 0.0)
    put(kv[:, 5 * KV_WIDTH:6 * KV_WIDTH], vw_ref, 1.0)

    u_ref[...] = proj(COL_U, MLSTM_WIDTH)
    vm_ref[...] = proj(COL_VM, MLSTM_WIDTH)
    op_ref[...] = proj(COL_OP, MLSTM_WIDTH)
    sm_ref[...] = proj(COL_SM, LANES)
    gt_ref[...] = lax.dot_general(wgt_ref[...], hb, NT, preferred_element_type=f32)


def _inproj(x2, g1, w_perm, w_gates_t, ksg2, kwg2, qg2, b, t):
    n = x2.shape[0]
    tm = min(TM_PROJ, t)
    nt = t // tm

    def rows(width):
        return pl.BlockSpec((tm, width), lambda i: (i, 0))

    def const(shape):
        return pl.BlockSpec(shape, lambda i: (0, 0))

    kv_spec = pl.BlockSpec((1, NSA_KV_HEADS, tm, LANES), lambda i: (i // nt, 0, i % nt, 0))
    kv_shape = jax.ShapeDtypeStruct((b, NSA_KV_HEADS, t, LANES), bf16)
    return pl.pallas_call(
        functools.partial(_inproj_kernel, tiles_per_seq=nt),
        grid=(n // tm,),
        in_specs=[rows(D_MODEL), const((1, D_MODEL)), const((D_MODEL, PROJ_WIDTH)), const((GATE_ROWS, D_MODEL)),
                  const((1, LANES)), const((1, LANES)), const((1, NSA_WIDTH))],
        out_specs=[rows(2 * KV_WIDTH), rows(MLSTM_WIDTH), rows(MLSTM_WIDTH), rows(MLSTM_WIDTH), rows(LANES),
                   pl.BlockSpec((GATE_ROWS, tm), lambda i: (0, i)),
                   kv_spec, kv_spec, kv_spec, kv_spec,
                   pl.BlockSpec((1, NSA_KV_HEADS, NSA_GROUP, tm, 2 * LANES),
                                lambda i: (i // nt, 0, 0, i % nt, 0))],
        out_shape=[jax.ShapeDtypeStruct((n, 2 * KV_WIDTH), f32), jax.ShapeDtypeStruct((n, MLSTM_WIDTH), f32),
                   jax.ShapeDtypeStruct((n, MLSTM_WIDTH), f32), jax.ShapeDtypeStruct((n, MLSTM_WIDTH), f32),
                   jax.ShapeDtypeStruct((n, LANES), f32), jax.ShapeDtypeStruct((GATE_ROWS, n), f32),
                   kv_shape, kv_shape, kv_shape, kv_shape,
                   jax.ShapeDtypeStruct((b, NSA_KV_HEADS, NSA_GROUP, t, 2 * LANES), bf16)],
        compiler_params=_cparams(("parallel",)),
        name="inproj",
    )(x2, g1, w_perm, w_gates_t, ksg2, kwg2, qg2)


def _gelu_tanh(x):
    return 0.5 * x * (1.0 + jnp.tanh(math.sqrt(2.0 / math.pi) * (x + 0.044715 * (x * x * x))))


def _hi_lo(x):
    hi = x.astype(bf16)
    return hi, (x - hi.astype(f32)).astype(bf16)


def _compress_kernel(kc_ref, vc_ref, wk1_ref, wk1s_ref, wk2s_ref, wv1_ref, wv1s_ref, wv2s_ref, pos_ref, kcg_ref,
                     kcd_ref, vcl_ref):
    nseg = kc_ref.shape[1] // CMP_STRIDE
    half = NSA_KV_HEADS * CMP_HIDDEN

    def branch(x_ref, w1_ref, w1s_ref, w2s_ref):
        terms = []
        for r in range(CMP_STRIDE):
            hi, lo = _hi_lo(x_ref[0, pl.ds(r, nseg, stride=CMP_STRIDE), :])
            terms += [hi, lo, hi]
        acc = jnp.dot(jnp.concatenate(terms, axis=1), w1s_ref[...], preferred_element_type=f32)
        pos_term = jnp.dot(pos_ref[...], w1_ref[...], precision=HI, preferred_element_type=f32)[0:1]
        hid = (acc[:, 0:half] + pltpu.roll(acc[:, half:2 * half], nseg - 1, 0)
               + jnp.concatenate([pos_term] * NSA_KV_HEADS, axis=1))
        hi, lo = _hi_lo(_gelu_tanh(hid))
        return jnp.dot(jnp.concatenate([hi, lo, hi], axis=1), w2s_ref[...], preferred_element_type=f32)

    kc2 = branch(kc_ref, wk1_ref, wk1s_ref, wk2s_ref)
    vc2 = branch(vc_ref, wv1_ref, wv1s_ref, wv2s_ref)
    lane = lax.broadcasted_iota(jnp.int32, (nseg, LANES), 1)
    for g in range(NSA_KV_HEADS):
        kc = kc2[:, g * LANES:(g + 1) * LANES]
        kc = kc * lax.rsqrt(jnp.mean(kc * kc, axis=-1, keepdims=True) + NORM_EPS) * kcg_ref[...]
        k_hi = kc.astype(bf16)
        k_lo = jnp.where(lane < HALF, kc - k_hi.astype(f32), 0.0).astype(bf16)
        kcd_ref[0, g] = jnp.concatenate([k_hi, k_lo], axis=1)
        vcl_ref[0, g] = jnp.where(lane < HALF, vc2[:, g * LANES:(g + 1) * LANES], 0.0).astype(bf16)


def _compress_weights(w1, w2):
    w1r = w1.reshape(2, CMP_STRIDE, NSA_HEAD_DIM, CMP_HIDDEN)
    z = jnp.zeros_like(w1r[0])

    def two_groups(w):
        return jnp.concatenate([jnp.concatenate([w, z], axis=2), jnp.concatenate([z, w], axis=2)], axis=1)

    wr = jnp.concatenate([two_groups(w1r[0]), two_groups(w1r[1])], axis=2)
    hi, lo = _hi_lo(wr)
    w1s = jnp.concatenate([hi, hi, lo], axis=1).reshape(CMP_STRIDE * 3 * LANES, 2 * NSA_KV_HEADS * CMP_HIDDEN)
    z2 = jnp.zeros_like(w2)
    w2d = jnp.concatenate([jnp.concatenate([w2, w2, z2, z2], axis=1),
                           jnp.concatenate([z2, z2, w2, w2], axis=1)], axis=0)
    hi2, lo2 = _hi_lo(w2d)
    return w1s, jnp.concatenate([hi2, hi2, lo2], axis=0)


def _compress(kv3, wk1, wk1s, wk2s, wv1, wv1s, wv2s, pos8, kcg2):
    b, t, _ = kv3.shape
    nseg = t // CMP_STRIDE

    def const(shape):
        return pl.BlockSpec(shape, lambda bi: (0,) * len(shape), pipeline_mode=pl.Buffered(1))

    return pl.pallas_call(
        _compress_kernel,
        grid=(b,),
        in_specs=[pl.BlockSpec((1, t, KV_WIDTH), lambda bi: (bi, 0, 0)),
                  pl.BlockSpec((1, t, KV_WIDTH), lambda bi: (bi, 0, 1)),
                  const(wk1.shape), const(wk1s.shape), const(wk2s.shape),
                  const(wv1.shape), const(wv1s.shape), const(wv2s.shape),
                  const(pos8.shape), const(kcg2.shape)],
        out_specs=[pl.BlockSpec((1, NSA_KV_HEADS, nseg, 2 * LANES), lambda bi: (bi, 0, 0, 0)),
                   pl.BlockSpec((1, NSA_KV_HEADS, nseg, LANES), lambda bi: (bi, 0, 0, 0))],
        out_shape=[jax.ShapeDtypeStruct((b, NSA_KV_HEADS, nseg, 2 * LANES), bf16),
                   jax.ShapeDtypeStruct((b, NSA_KV_HEADS, nseg, LANES), bf16)],
        compiler_params=_cparams(("parallel",)),
        name="compress",
    )(kv3, kv3, wk1, wk1s, wk2s, wv1, wv1s, wv2s, pos8, kcg2)


def _nsa_kernel(slopes_ref, q3_ref, sm_ref, kcd_ref, vcl_ref, ovt_ref,
                ks_ref, vs_ref, kw_ref, vw_ref, y_ref, qa_scr, m_scr, acc_scr):
    g = pl.program_id(1)
    qi = pl.program_id(2)
    q0 = qi * TQ
    ncp = kcd_ref.shape[2]
    nsel = ncp // (SEL_BLOCK // CMP_STRIDE)
    rows = NSA_GROUP * TQ
    log2e = math.log2(math.e)

    lane = lax.broadcasted_iota(jnp.int32, (TQ, LANES), 1)
    lo = lane < HALF
    slopes = [slopes_ref[g * NSA_GROUP + r] * log2e for r in range(NSA_GROUP)]

    def per_head(x, fn):
        return jnp.concatenate([fn(r, x[r * TQ:(r + 1) * TQ]) for r in range(NSA_GROUP)], axis=0)

    q3 = q3_ref[0, 0].reshape(rows, 2 * LANES)
    q1 = q3[:, 0:LANES]

    cidx = lax.broadcasted_iota(jnp.int32, (TQ, ncp), 1)
    tpos_c = q0 + lax.broadcasted_iota(jnp.int32, (TQ, ncp), 0)
    blk_end = cidx * CMP_STRIDE + (CMP_BLOCK - 1)
    valid_c = jnp.logical_and(tpos_c >= blk_end, cidx < ncp - 1)
    krel_c = (CMP_BLOCK - 1 - q0 + CMP_STRIDE * lax.broadcasted_iota(jnp.int32, (1, ncp), 1)).astype(f32)
    s_c = lax.dot_general(q3, kcd_ref[0, 0], NT, preferred_element_type=f32)
    s_c = per_head(s_c, lambda r, x: jnp.where(valid_c, x + slopes[r] * krel_c, NEG))
    e_c = jnp.exp2(s_c - jnp.max(s_c, axis=-1, keepdims=True))
    p_c = e_c / jnp.sum(e_c, axis=-1, keepdims=True)
    p_c = per_head(p_c, lambda r, x: jnp.where(valid_c, x, 0.0))
    p_sum = p_c[0:TQ]
    for r in range(1, NSA_GROUP):
        p_sum = p_sum + p_c[r * TQ:(r + 1) * TQ]
    o_cmp = jnp.dot(p_c.astype(bf16), vcl_ref[0, 0], preferred_element_type=f32)

    band = WINDOW + TQ
    kb = pl.multiple_of(jnp.maximum(q0 - WINDOW, 0), TQ)
    s_w = lax.dot_general(q1, kw_ref[0, 0, pl.ds(kb, band), :], NT, preferred_element_type=f32)
    krel_w = (kb - q0 + lax.broadcasted_iota(jnp.int32, (1, band), 1)).astype(f32)
    dist_w = (q0 - kb + lax.broadcasted_iota(jnp.int32, (TQ, band), 0)
              - lax.broadcasted_iota(jnp.int32, (TQ, band), 1))
    valid_w = jnp.logical_and(dist_w >= 0, dist_w < WINDOW)
    s_w = per_head(s_w, lambda r, x: jnp.where(valid_w, x + slopes[r] * krel_w, NEG))
    p_w = jnp.exp2(s_w - jnp.max(s_w, axis=-1, keepdims=True))
    acc_w = jnp.dot(p_w.astype(bf16), vw_ref[0, 0, pl.ds(kb, band), :], preferred_element_type=f32)
    o_win = acc_w / pltpu.roll(acc_w, HALF, 1)

    imp = lax.dot_general(ovt_ref[...], jnp.concatenate(_split3(p_sum), axis=1), NT,
                          preferred_element_type=f32)[0:nsel]
    jrow = lax.broadcasted_iota(jnp.int32, (nsel, TQ), 0)
    tcol = q0 + lax.broadcasted_iota(jnp.int32, (nsel, TQ), 1)
    forced = jnp.logical_or(jrow == lax.shift_right_logical(tcol, 6), jrow == 0)
    future = jrow * SEL_BLOCK > tcol
    work = jnp.where(forced, FORCE, jnp.where(future, -FORCE, imp))
    jrow_f = jrow.astype(f32)
    bias_t = jnp.full((nsel, TQ), NEG, f32)
    for _ in range(min(SEL_TOPN, nsel)):
        best = jnp.max(work, axis=0, keepdims=True)
        first = jnp.min(jnp.where(work == best, jrow_f, float(nsel)), axis=0, keepdims=True)
        hit = jrow_f == first
        bias_t = jnp.where(hit, 0.0, bias_t)
        work = jnp.where(hit, -3e38, work)
    if nsel < HALF:
        bias_t = jnp.concatenate([bias_t, jnp.full((HALF - nsel, TQ), NEG, f32)], axis=0)
    bias = jnp.concatenate([bias_t, bias_t], axis=0).T

    bias_b = bias.astype(bf16)
    qa_scr[...] = per_head(q1, lambda r, x: jnp.where(lo, x, bias_b))

    m_scr[...] = jnp.full(m_scr.shape, NEG, f32)
    acc_scr[...] = jnp.zeros(acc_scr.shape, f32)

    def sel_tile(state, kt, causal):
        m_old, acc = state
        k0 = pl.multiple_of(kt * TK, TK)
        s = lax.dot_general(qa_scr[...], ks_ref[0, 0, pl.ds(k0, TK), :], NT, preferred_element_type=f32)
        krel = (k0 - q0 + lax.broadcasted_iota(jnp.int32, (1, TK), 1)).astype(f32)
        if causal:
            ahead = (k0 - q0 + lax.broadcasted_iota(jnp.int32, (TQ, TK), 1)
                     > lax.broadcasted_iota(jnp.int32, (TQ, TK), 0))
            s = per_head(s, lambda r, x: jnp.where(ahead, NEG, x + slopes[r] * krel))
        else:
            s = per_head(s, lambda r, x: x + slopes[r] * krel)
        m_new = jnp.maximum(m_old, jnp.max(s, axis=-1, keepdims=True))
        p = jnp.exp2(s - m_new[:, 0:1])
        acc = jnp.exp2(m_old - m_new) * acc + jnp.dot(p.astype(bf16), vs_ref[0, 0, pl.ds(k0, TK), :],
                                                      preferred_element_type=f32)
        return m_new, acc

    def sel_tiles(tiles):
        state = (m_scr[...], acc_scr[...])
        for kt, causal in tiles:
            state = sel_tile(state, kt, causal)
        m_scr[...] = state[0]
        acc_scr[...] = state[1]

    n_full = q0 // TK

    def sel_body(i, carry):
        sel_tiles([(2 * i, False), (2 * i + 1, False)])
        return carry

    lax.fori_loop(0, n_full // 2, sel_body, 0)

    @pl.when(n_full % 2 == 1)
    def _():
        sel_tiles([(n_full - 1, False), (n_full, True)])

    @pl.when(n_full % 2 == 0)
    def _():
        sel_tiles([(n_full, True)])

    acc_s = acc_scr[...]
    o_slc = acc_s / pltpu.roll(acc_s, HALF, 1)

    sig = jax.nn.sigmoid(sm_ref[0])

    def gate(r, branch):
        col = (g * NSA_GROUP + r) * 3 + branch
        return jnp.sum(jnp.where(lane == col, sig, 0.0), axis=-1, keepdims=True)

    def mix(r):
        sl = slice(r * TQ, (r + 1) * TQ)
        return gate(r, 0) * o_cmp[sl] + gate(r, 1) * o_slc[sl] + gate(r, 2) * o_win[sl]

    for c in range(NSA_GROUP // 2):
        y_ref[0, :, c * LANES:(c + 1) * LANES] = jnp.where(lo, mix(2 * c), pltpu.roll(mix(2 * c + 1), HALF, 1))


def _nsa(slopes, q3, sm3, kcd, vcl, ovt, ks, vs, kw, vw):
    b, t = q3.shape[0], q3.shape[3]
    gw = NSA_GROUP * NSA_HEAD_DIM
    ncp = kcd.shape[2]

    def kv_spec(rows):
        return pl.BlockSpec((1, 1, rows, LANES), lambda bi, gi, qi: (bi, gi, 0, 0))

    return pl.pallas_call(
        _nsa_kernel,
        grid=(b, NSA_KV_HEADS, t // TQ),
        in_specs=[
            pl.BlockSpec(memory_space=pltpu.SMEM),
            pl.BlockSpec((1, 1, NSA_GROUP, TQ, 2 * LANES), lambda bi, gi, qi: (bi, gi, 0, qi, 0)),
            pl.BlockSpec((1, TQ, LANES), lambda bi, gi, qi: (bi, qi, 0)),
            pl.BlockSpec((1, 1, ncp, 2 * LANES), lambda bi, gi, qi: (bi, gi, 0, 0)), kv_spec(ncp),
            pl.BlockSpec(ovt.shape, lambda bi, gi, qi: (0, 0)),
            kv_spec(t), kv_spec(t), kv_spec(t), kv_spec(t),
        ],
        out_specs=pl.BlockSpec((1, TQ, gw), lambda bi, gi, qi: (bi, qi, gi)),
        out_shape=jax.ShapeDtypeStruct((b, t, NSA_WIDTH), f32),
        scratch_shapes=[
            pltpu.VMEM((NSA_GROUP * TQ, LANES), bf16),
            pltpu.VMEM((NSA_GROUP * TQ, LANES), f32),
            pltpu.VMEM((NSA_GROUP * TQ, LANES), f32),
        ],
        compiler_params=_cparams(("parallel", "parallel", "arbitrary")),
        name="nsa",
    )(slopes, q3, sm3, kcd, vcl, ovt, ks, vs, kw, vw)


def _log_sigmoid(x):
    return jnp.minimum(x, 0.0) - jnp.log1p(jnp.exp(-jnp.abs(x)))


def _mlstm_kernel(bi_ref, bf_ref, u_ref, v_ref, op_ref, gi_ref, gf_ref, cw_ref, cb_ref, wq_ref, wk_ref,
                  ng_ref, sk_ref, y_ref, uc_scr, q_scr, kt_scr, ct_scr, m_scr, b_scr, li_scr, xp_scr):
    h = pl.program_id(1)
    t = u_ref.shape[1]
    L = MLSTM_L
    dm = MLSTM_HEAD_DIM

    x = u_ref[0]
    xp_scr[0:8, :] = jnp.zeros((8, dm), f32)
    xp_scr[8:, :] = x
    acc = x * cw_ref[CONV_WIDTH - 1:CONV_WIDTH, :]
    for s in range(1, CONV_WIDTH):
        acc = acc + xp_scr[8 - s:8 - s + t, :] * cw_ref[CONV_WIDTH - 1 - s:CONV_WIDTH - s, :]
    uc = acc + cb_ref[...]
    uc = uc * jax.nn.sigmoid(uc)
    uc_scr[...] = uc
    ucb = uc.astype(bf16)
    q_scr[...] = jnp.dot(ucb, wq_ref[0].astype(bf16), preferred_element_type=f32).astype(bf16)
    k = jnp.dot(ucb, wk_ref[0].astype(bf16), preferred_element_type=f32) * (dm ** -0.5)
    kt_scr[...] = k.T

    ct_scr[...] = jnp.zeros(ct_scr.shape, f32)
    m_scr[...] = jnp.zeros(m_scr.shape, f32)

    li_ = lax.broadcasted_iota(jnp.int32, (L, L), 0)
    si_ = lax.broadcasted_iota(jnp.int32, (L, L), 1)
    causal = si_ <= li_
    diag = si_ == li_
    ones_v = jnp.ones((L, dm), f32)

    log_f = _log_sigmoid(gf_ref[0, 0] + bf_ref[h])
    upper = jnp.where(li_ <= si_, 1.0, 0.0).astype(bf16)
    b_scr[...] = jnp.dot(jnp.concatenate(_split3(log_f), axis=1), jnp.concatenate([upper] * 3, axis=0),
                         preferred_element_type=f32)
    li_scr[...] = gi_ref[0, 0] + bi_ref[h]

    def chunk(c, ct, m_prev):
        r0 = pl.multiple_of(c * L, L)
        qc = q_scr[pl.ds(r0, L), :]
        ktc = kt_scr[:, pl.ds(r0, L)]
        vaug = jnp.concatenate([v_ref[0, pl.ds(r0, L), :], ones_v], axis=1).astype(bf16)
        log_i = li_scr[pl.ds(c, 1), :]
        b_row = b_scr[pl.ds(c, 1), :]
        b_col = jnp.sum(jnp.where(diag, b_row, 0.0), axis=-1, keepdims=True)
        g_sum = b_row[:, L - 1:L]
        dmat = jnp.where(causal, (b_col - b_row) + log_i, NEG)
        m_loc = jnp.max(dmat, axis=-1, keepdims=True)
        p = jnp.dot(qc, ktc.astype(bf16), preferred_element_type=f32) * jnp.exp(dmat - m_loc)
        intra = jnp.dot(p.astype(bf16), vaug, preferred_element_type=f32)
        m_inter = b_col + m_prev
        m_out = jnp.maximum(m_inter, m_loc)
        xo = (jnp.exp(m_inter - m_out) * jnp.dot(qc, ct.astype(bf16), preferred_element_type=f32)
              + jnp.exp(m_loc - m_out) * intra)
        num = xo[:, 0:dm]
        den = xo[:, dm:2 * dm]
        hh = num / jnp.maximum(jnp.abs(den), jnp.exp(-m_out))
        hh = hh * jax.nn.sigmoid(op_ref[0, pl.ds(r0, L), :])
        hh = hh * lax.rsqrt(jnp.mean(hh * hh, axis=-1, keepdims=True) + NORM_EPS) * ng_ref[0]
        y_ref[0, pl.ds(r0, L), :] = hh + sk_ref[...] * uc_scr[pl.ds(r0, L), :]

        w_end = (g_sum - b_row) + log_i
        m_new = jnp.maximum(g_sum + m_prev, jnp.max(w_end, axis=-1, keepdims=True))
        decay = jnp.exp(g_sum + m_prev - m_new)
        w = jnp.exp(w_end - m_new)
        return decay * ct + jnp.dot((ktc * w).astype(bf16), vaug, preferred_element_type=f32), m_new

    def chunk_group(i, carry):
        ct, m_prev = ct_scr[...], m_scr[0:1, 0:1]
        for j in range(MLSTM_UNROLL):
            ct, m_prev = chunk(i * MLSTM_UNROLL + j, ct, m_prev)
        ct_scr[...] = ct
        m_scr[...] = jnp.broadcast_to(m_prev, m_scr.shape)
        return carry

    lax.fori_loop(0, t // (L * MLSTM_UNROLL), chunk_group, 0)


def _mlstm(b_i, b_f, u3, v3, op3, gi4, gf4, cw, cb2, wq, wk, ng3, sk2):
    b, t, _ = u3.shape
    dm = MLSTM_HEAD_DIM
    seq = pl.BlockSpec((1, t, dm), lambda bi, hi: (bi, 0, hi))
    gate = pl.BlockSpec((1, 1, t // MLSTM_L, MLSTM_L), lambda bi, hi: (bi, hi, 0, 0))
    smem = pl.BlockSpec(memory_space=pltpu.SMEM)
    return pl.pallas_call(
        _mlstm_kernel,
        grid=(b, MLSTM_HEADS),
        in_specs=[
            smem, smem, seq, seq, seq, gate, gate,
            pl.BlockSpec((CONV_WIDTH, dm), lambda bi, hi: (0, hi)),
            pl.BlockSpec((1, dm), lambda bi, hi: (0, hi)),
            pl.BlockSpec((1, dm, dm), lambda bi, hi: (hi, 0, 0)),
            pl.BlockSpec((1, dm, dm), lambda bi, hi: (hi, 0, 0)),
            pl.BlockSpec((1, 1, dm), lambda bi, hi: (hi, 0, 0)),
            pl.BlockSpec((1, dm), lambda bi, hi: (0, hi)),
        ],
        out_specs=seq,
        out_shape=jax.ShapeDtypeStruct((b, t, MLSTM_WIDTH), f32),
        scratch_shapes=[
            pltpu.VMEM((t, dm), f32),
            pltpu.VMEM((t, dm), bf16),
            pltpu.VMEM((dm, t), f32),
            pltpu.VMEM((dm, 2 * dm), f32),
            pltpu.VMEM((8, LANES), f32),
            pltpu.VMEM((t // MLSTM_L, MLSTM_L), f32),
            pltpu.VMEM((t // MLSTM_L, MLSTM_L), f32),
            pltpu.VMEM((t + 8, dm), f32),
        ],
        compiler_params=_cparams(("parallel", "parallel")),
        name="mlstm",
    )(b_i, b_f, u3, v3, op3, gi4, gf4, cw, cb2, wq, wk, ng3, sk2)


def _ffn_kernel(x_ref, ya_ref, yb_ref, wo_ref, g2_ref, wg_ref, wu_ref, wd_ref, o_ref, act_scr):
    x1 = (x_ref[...]
          + jnp.dot(ya_ref[...].astype(bf16), wo_ref[0:NSA_WIDTH, :], preferred_element_type=f32)
          + jnp.dot(yb_ref[...].astype(bf16), wo_ref[NSA_WIDTH:NSA_WIDTH + MLSTM_WIDTH, :],
                    preferred_element_type=f32))
    h2 = x1 * lax.rsqrt(jnp.mean(x1 * x1, axis=-1, keepdims=True) + NORM_EPS) * g2_ref[...]
    h2b = h2.astype(bf16)
    for c in range(D_FF // FF_CHUNK):
        cols = slice(c * FF_CHUNK, (c + 1) * FF_CHUNK)
        gt = jnp.dot(h2b, wg_ref[:, cols], preferred_element_type=f32)
        up = jnp.dot(h2b, wu_ref[:, cols], preferred_element_type=f32)
        act_scr[:, cols] = (gt * jax.nn.sigmoid(gt) * up).astype(bf16)
    o_ref[...] = x1 + jnp.dot(act_scr[...], wd_ref[...], preferred_element_type=f32)


def _ffn(x2, ya, yb, wo, g2, wg, wu, wd):
    n = x2.shape[0]
    tm = TM_FFN

    def const(shape):
        return pl.BlockSpec(shape, lambda i: (0, 0), pipeline_mode=pl.Buffered(1))

    return pl.pallas_call(
        _ffn_kernel,
        grid=(n // tm,),
        in_specs=[
            pl.BlockSpec((tm, D_MODEL), lambda i: (i, 0)),
            pl.BlockSpec((tm, NSA_WIDTH), lambda i: (i, 0)),
            pl.BlockSpec((tm, MLSTM_WIDTH), lambda i: (i, 0)),
            const(wo.shape), const(g2.shape), const(wg.shape), const(wu.shape), const(wd.shape),
        ],
        out_specs=pl.BlockSpec((tm, D_MODEL), lambda i: (i, 0)),
        out_shape=jax.ShapeDtypeStruct((n, D_MODEL), f32),
        scratch_shapes=[pltpu.VMEM((tm, D_FF), bf16)],
        compiler_params=_cparams(("parallel",)),
        name="ffn",
    )(x2, ya, yb, wo, g2, wg, wu, wd)


def _overlap_t(ncp):
    nsel = ncp // (SEL_BLOCK // CMP_STRIDE)
    cs = np.arange(ncp) * CMP_STRIDE
    ss = np.arange(nsel) * SEL_BLOCK
    ov = ((cs[None, :] < ss[:, None] + SEL_BLOCK) & (cs[None, :] + CMP_BLOCK > ss[:, None])).astype(np.float32)
    ov[:, ncp - 1] = 0.0
    out = np.zeros((LANES, ncp), np.float32)
    out[:nsel] = ov
    return jnp.asarray(np.tile(out, (1, 3)), dtype=bf16)


def _layer(x, norm1_g, w_in, q_g, kc_g, ks_g, kw_g, cmp_pos, w_ck1, w_ck2, w_cv1, w_cv2, conv_w, conv_b,
           w_mq, w_mk, b_i, b_f, mlstm_norm_g, mlstm_skip, w_out, norm2_g, w_gate, w_up, w_down):
    b, t, d = x.shape
    n = b * t
    x2 = x.reshape(n, d)

    o_gate = NSA_WIDTH + 6 * KV_WIDTH
    o_u = o_gate + 3 * NSA_HEADS
    o_if = o_u + 3 * MLSTM_WIDTH
    w_perm = jnp.concatenate([
        w_in[:, :o_gate], w_in[:, o_u:o_if], w_in[:, o_gate:o_u], w_in[:, o_if:],
        jnp.zeros((d, LANES - 3 * NSA_HEADS - 2 * MLSTM_HEADS), w_in.dtype)], axis=1).astype(bf16)
    w_gates_t = jnp.concatenate([w_in[:, o_if:].T, jnp.zeros((GATE_ROWS - 2 * MLSTM_HEADS, d), w_in.dtype)],
                                axis=0).astype(bf16)
    kvc2, u2, vm2, op2, sm2, gates_t, ks, vs, kw, vw, q3 = _inproj(
        x2, norm1_g.reshape(1, d), w_perm, w_gates_t, jnp.tile(ks_g, 2).reshape(1, LANES),
        jnp.tile(kw_g, 2).reshape(1, LANES), jnp.tile(q_g, NSA_HEADS).reshape(1, NSA_WIDTH), b, t)

    kv3 = kvc2.reshape(b, t, 2 * KV_WIDTH)
    nseg = t // CMP_STRIDE
    pos8 = jnp.broadcast_to(cmp_pos.reshape(1, CMP_BLOCK * NSA_HEAD_DIM), (8, CMP_BLOCK * NSA_HEAD_DIM))
    wk1s, wk2s = _compress_weights(w_ck1, w_ck2)
    wv1s, wv2s = _compress_weights(w_cv1, w_cv2)
    kcd, vcl = _compress(kv3, w_ck1, wk1s, wk2s, w_cv1, wv1s, wv2s, pos8, jnp.tile(kc_g, 2).reshape(1, LANES))
    slopes = jnp.exp2(-8.0 * (jnp.arange(NSA_HEADS, dtype=f32) + 1.0) / NSA_HEADS)
    y_nsa = _nsa(slopes, q3, sm2.reshape(b, t, LANES), kcd, vcl, _overlap_t(nseg), ks, vs, kw, vw)

    gi4 = gates_t[0:MLSTM_HEADS].reshape(MLSTM_HEADS, b, t).transpose(1, 0, 2)
    gf4 = gates_t[MLSTM_HEADS:2 * MLSTM_HEADS].reshape(MLSTM_HEADS, b, t).transpose(1, 0, 2)
    y_mem = _mlstm(b_i, b_f, u2.reshape(b, t, MLSTM_WIDTH), vm2.reshape(b, t, MLSTM_WIDTH),
                   op2.reshape(b, t, MLSTM_WIDTH), gi4.reshape(b, MLSTM_HEADS, t // MLSTM_L, MLSTM_L),
                   gf4.reshape(b, MLSTM_HEADS, t // MLSTM_L, MLSTM_L), conv_w, conv_b.reshape(1, MLSTM_WIDTH),
                   w_mq, w_mk,
                   mlstm_norm_g.reshape(MLSTM_HEADS, 1, MLSTM_HEAD_DIM), mlstm_skip.reshape(1, MLSTM_WIDTH))

    out = _ffn(x2, y_nsa.reshape(n, NSA_WIDTH), y_mem.reshape(n, MLSTM_WIDTH), w_out.astype(bf16),
               norm2_g.reshape(1, d), w_gate.astype(bf16), w_up.astype(bf16), w_down.astype(bf16))
    return out.reshape(b, t, d)


def kernel(x, norm1_g, w_in, q_norm_g, kc_norm_g, ks_norm_g, kw_norm_g, cmp_pos, w_ck1, w_ck2, w_cv1, w_cv2,
           conv_w, conv_b, w_mq, w_mk, b_i, b_f, mlstm_norm_g, mlstm_skip, w_out, norm2_g, w_gate, w_up, w_down):
    depth = norm1_g.shape[0]
    for l in range(depth):
        x = _layer(x, norm1_g[l], w_in[l], q_norm_g[l], kc_norm_g[l], ks_norm_g[l], kw_norm_g[l], cmp_pos[l],
                   w_ck1[l], w_ck2[l], w_cv1[l], w_cv2[l], conv_w[l], conv_b[l], w_mq[l], w_mk[l], b_i[l], b_f[l],
                   mlstm_norm_g[l], mlstm_skip[l], w_out[l], norm2_g[l], w_gate[l], w_up[l], w_down[l])
    return x
```

```python
import functools
import math

import numpy as np
import jax
import jax.numpy as jnp
from jax import lax
from jax.experimental import pallas as pl
from jax.experimental.pallas import tpu as pltpu

f32 = jnp.float32
bf16 = jnp.bfloat16

D_MODEL = 1024
NSA_HEADS = 8
NSA_KV_HEADS = 2
NSA_HEAD_DIM = 64
NSA_GROUP = NSA_HEADS // NSA_KV_HEADS
CMP_BLOCK = 32
CMP_STRIDE = 16
CMP_HIDDEN = 256
SEL_BLOCK = 64
SEL_TOPN = 16
WINDOW = 512
MLSTM_HEADS = 4
MLSTM_HEAD_DIM = 128
CONV_WIDTH = 4
NSA_WIDTH = NSA_HEADS * NSA_HEAD_DIM
MLSTM_WIDTH = MLSTM_HEADS * MLSTM_HEAD_DIM
KV_WIDTH = NSA_KV_HEADS * NSA_HEAD_DIM
D_FF = -(-8 * D_MODEL // (3 * 256)) * 256
NORM_EPS = 1e-6
NEG = -1e30
FORCE = 1e9

LANES = 128
HALF = 64
VMEM_LIMIT = 56 * 1024 * 1024

TM_PROJ = 512
TQ = 256
TK = 512
MLSTM_L = 256
MLSTM_UNROLL = 4
TM_FFN = 512
FF_CHUNK = 256

HI = lax.Precision.HIGHEST
NT = (((1,), (1,)), ((), ()))


def _cparams(sem):
    return pltpu.CompilerParams(dimension_semantics=sem, vmem_limit_bytes=VMEM_LIMIT)


def _split3(x):
    x1 = x.astype(bf16)
    r1 = x - x1.astype(f32)
    x2 = r1.astype(bf16)
    x3 = (r1 - x2.astype(f32)).astype(bf16)
    return x1, x2, x3


COL_Q = 0
COL_KV = COL_Q + NSA_WIDTH
COL_U = COL_KV + 6 * KV_WIDTH
COL_VM = COL_U + MLSTM_WIDTH
COL_OP = COL_VM + MLSTM_WIDTH
COL_SM = COL_OP + MLSTM_WIDTH
PROJ_WIDTH = COL_SM + LANES
GATE_ROWS = 16


def _pair_norm(x, gain, lo):
    sq = x * x
    s_lo = jnp.sum(jnp.where(lo, sq, 0.0), axis=-1, keepdims=True)
    s_hi = jnp.sum(jnp.where(lo, 0.0, sq), axis=-1, keepdims=True)
    inv = jnp.where(lo, lax.rsqrt(s_lo / HALF + NORM_EPS), lax.rsqrt(s_hi / HALF + NORM_EPS))
    return x * inv * gain


def _inproj_kernel(x_ref, g_ref, w_ref, wgt_ref, ksg_ref, kwg_ref, qg_ref,
                   kvc_ref, u_ref, vm_ref, op_ref, sm_ref, gt_ref, ks_ref, vs_ref, kw_ref, vw_ref, q3_ref,
                   *, tiles_per_seq):
    tm = x_ref.shape[0]
    x = x_ref[...]
    h = x * lax.rsqrt(jnp.mean(x * x, axis=-1, keepdims=True) + NORM_EPS) * g_ref[...]
    hb = h.astype(bf16)

    def proj(col, width):
        return jnp.dot(hb, w_ref[:, col:col + width], preferred_element_type=f32)

    lane = lax.broadcasted_iota(jnp.int32, (tm, LANES), 1)
    lo = lane < HALF
    pos = (pl.program_id(0) % tiles_per_seq) * tm + lax.broadcasted_iota(jnp.int32, (tm, LANES), 0)

    q = proj(COL_Q, NSA_WIDTH)
    scale = NSA_HEAD_DIM ** -0.5 * math.log2(math.e)
    for c in range(NSA_HEADS // 2):
        pair = _pair_norm(q[:, c * LANES:(c + 1) * LANES], qg_ref[:, c * LANES:(c + 1) * LANES], lo) * scale
        for par, xq in enumerate((pair, pltpu.roll(pair, HALF, 1))):
            hi = jnp.where(lo, xq, 0.0).astype(bf16).astype(f32)
            res = jnp.where(lo, xq - hi, 0.0)
            head = 2 * c + par
            q3_ref[0, head // NSA_GROUP, head % NSA_GROUP] = jnp.concatenate(
                [hi + pltpu.roll(res, HALF, 1), hi], axis=1).astype(bf16)

    kv = proj(COL_KV, 6 * KV_WIDTH)
    kvc_ref[...] = kv[:, 0:2 * KV_WIDTH]
    code = jnp.where(lax.shift_right_logical(pos, 6) == (lane & (HALF - 1)), 1.0, 0.0)

    def put(xk, o_ref, fill):
        o_ref[0, 0] = jnp.where(lo, xk, fill).astype(bf16)
        o_ref[0, 1] = jnp.where(lo, pltpu.roll(xk, HALF, 1), fill).astype(bf16)

    put(_pair_norm(kv[:, 2 * KV_WIDTH:3 * KV_WIDTH], ksg_ref[...], lo), ks_ref, code)
    put(kv[:, 3 * KV_WIDTH:4 * KV_WIDTH], vs_ref, 1.0)
    put(_pair_norm(kv[:, 4 * KV_WIDTH:5 * KV_WIDTH], kwg_ref[...], lo), kw_ref, 0.0)
    put(kv[:, 5 * KV_WIDTH:6 * KV_WIDTH], vw_ref, 1.0)

    u_ref[...] = proj(COL_U, MLSTM_WIDTH)
    vm_ref[...] = proj(COL_VM, MLSTM_WIDTH)
    op_ref[...] = proj(COL_OP, MLSTM_WIDTH)
    sm_ref[...] = proj(COL_SM, LANES)
    gt_ref[...] = lax.dot_general(wgt_ref[...], hb, NT, preferred_element_type=f32)


def _inproj(x2, g1, w_perm, w_gates_t, ksg2, kwg2, qg2, b, t):
    n = x2.shape[0]
    tm = min(TM_PROJ, t)
    nt = t // tm

    def rows(width):
        return pl.BlockSpec((tm, width), lambda i: (i, 0))

    def const(shape):
        return pl.BlockSpec(shape, lambda i: (0, 0))

    kv_spec = pl.BlockSpec((1, NSA_KV_HEADS, tm, LANES), lambda i: (i // nt, 0, i % nt, 0))
    kv_shape = jax.ShapeDtypeStruct((b, NSA_KV_HEADS, t, LANES), bf16)
    return pl.pallas_call(
        functools.partial(_inproj_kernel, tiles_per_seq=nt),
        grid=(n // tm,),
        in_specs=[rows(D_MODEL), const((1, D_MODEL)), const((D_MODEL, PROJ_WIDTH)), const((GATE_ROWS, D_MODEL)),
                  const((1, LANES)), const((1, LANES)), const((1, NSA_WIDTH))],
        out_specs=[rows(2 * KV_WIDTH), rows(MLSTM_WIDTH), rows(MLSTM_WIDTH), rows(MLSTM_WIDTH), rows(LANES),
                   pl.BlockSpec((GATE_ROWS, tm), lambda i: (0, i)),
                   kv_spec, kv_spec, kv_spec, kv_spec,
                   pl.BlockSpec((1, NSA_KV_HEADS, NSA_GROUP, tm, 2 * LANES),
                                lambda i: (i // nt, 0, 0, i % nt, 0))],
        out_shape=[jax.ShapeDtypeStruct((n, 2 * KV_WIDTH), f32), jax.ShapeDtypeStruct((n, MLSTM_WIDTH), f32),
                   jax.ShapeDtypeStruct((n, MLSTM_WIDTH), f32), jax.ShapeDtypeStruct((n, MLSTM_WIDTH), f32),
                   jax.ShapeDtypeStruct((n, LANES), f32), jax.ShapeDtypeStruct((GATE_ROWS, n), f32),
                   kv_shape, kv_shape, kv_shape, kv_shape,
                   jax.ShapeDtypeStruct((b, NSA_KV_HEADS, NSA_GROUP, t, 2 * LANES), bf16)],
        compiler_params=_cparams(("parallel",)),
        name="inproj",
    )(x2, g1, w_perm, w_gates_t, ksg2, kwg2, qg2)


def _gelu_tanh(x):
    return 0.5 * x * (1.0 + jnp.tanh(math.sqrt(2.0 / math.pi) * (x + 0.044715 * (x * x * x))))


def _hi_lo(x):
    hi = x.astype(bf16)
    return hi, (x - hi.astype(f32)).astype(bf16)


def _compress_kernel(kc_ref, vc_ref, wk1_ref, wk1s_ref, wk2s_ref, wv1_ref, wv1s_ref, wv2s_ref, pos_ref, kcg_ref,
                     kcd_ref, vcl_ref):
    nseg = kc_ref.shape[1] // CMP_STRIDE
    half = NSA_KV_HEADS * CMP_HIDDEN

    def branch(x_ref, w1_ref, w1s_ref, w2s_ref):
        terms = []
        for r in range(CMP_STRIDE):
            hi, lo = _hi_lo(x_ref[0, pl.ds(r, nseg, stride=CMP_STRIDE), :])
            terms += [hi, lo, hi]
        acc = jnp.dot(jnp.concatenate(terms, axis=1), w1s_ref[...], preferred_element_type=f32)
        pos_term = jnp.dot(pos_ref[...], w1_ref[...], precision=HI, preferred_element_type=f32)[0:1]
        hid = (acc[:, 0:half] + pltpu.roll(acc[:, half:2 * half], nseg - 1, 0)
               + jnp.concatenate([pos_term] * NSA_KV_HEADS, axis=1))
        hi, lo = _hi_lo(_gelu_tanh(hid))
        return jnp.dot(jnp.concatenate([hi, lo, hi], axis=1), w2s_ref[...], preferred_element_type=f32)

    kc2 = branch(kc_ref, wk1_ref, wk1s_ref, wk2s_ref)
    vc2 = branch(vc_ref, wv1_ref, wv1s_ref, wv2s_ref)
    lane = lax.broadcasted_iota(jnp.int32, (nseg, LANES), 1)
    for g in range(NSA_KV_HEADS):
        kc = kc2[:, g * LANES:(g + 1) * LANES]
        kc = kc * lax.rsqrt(jnp.mean(kc * kc, axis=-1, keepdims=True) + NORM_EPS) * kcg_ref[...]
        k_hi = kc.astype(bf16)
        k_lo = jnp.where(lane < HALF, kc - k_hi.astype(f32), 0.0).astype(bf16)
        kcd_ref[0, g] = jnp.concatenate([k_hi, k_lo], axis=1)
        vcl_ref[0, g] = jnp.where(lane < HALF, vc2[:, g * LANES:(g + 1) * LANES], 0.0).astype(bf16)


def _compress_weights(w1, w2):
    w1r = w1.reshape(2, CMP_STRIDE, NSA_HEAD_DIM, CMP_HIDDEN)
    z = jnp.zeros_like(w1r[0])

    def two_groups(w):
        return jnp.concatenate([jnp.concatenate([w, z], axis=2), jnp.concatenate([z, w], axis=2)], axis=1)

    wr = jnp.concatenate([two_groups(w1r[0]), two_groups(w1r[1])], axis=2)
    hi, lo = _hi_lo(wr)
    w1s = jnp.concatenate([hi, hi, lo], axis=1).reshape(CMP_STRIDE * 3 * LANES, 2 * NSA_KV_HEADS * CMP_HIDDEN)
    z2 = jnp.zeros_like(w2)
    w2d = jnp.concatenate([jnp.concatenate([w2, w2, z2, z2], axis=1),
                           jnp.concatenate([z2, z2, w2, w2], axis=1)], axis=0)
    hi2, lo2 = _hi_lo(w2d)
    return w1s, jnp.concatenate([hi2, hi2, lo2], axis=0)


def _compress(kv3, wk1, wk1s, wk2s, wv1, wv1s, wv2s, pos8, kcg2):
    b, t, _ = kv3.shape
    nseg = t // CMP_STRIDE

    def const(shape):
        return pl.BlockSpec(shape, lambda bi: (0,) * len(shape), pipeline_mode=pl.Buffered(1))

    return pl.pallas_call(
        _compress_kernel,
        grid=(b,),
        in_specs=[pl.BlockSpec((1, t, KV_WIDTH), lambda bi: (bi, 0, 0)),
                  pl.BlockSpec((1, t, KV_WIDTH), lambda bi: (bi, 0, 1)),
                  const(wk1.shape), const(wk1s.shape), const(wk2s.shape),
                  const(wv1.shape), const(wv1s.shape), const(wv2s.shape),
                  const(pos8.shape), const(kcg2.shape)],
        out_specs=[pl.BlockSpec((1, NSA_KV_HEADS, nseg, 2 * LANES), lambda bi: (bi, 0, 0, 0)),
                   pl.BlockSpec((1, NSA_KV_HEADS, nseg, LANES), lambda bi: (bi, 0, 0, 0))],
        out_shape=[jax.ShapeDtypeStruct((b, NSA_KV_HEADS, nseg, 2 * LANES), bf16),
                   jax.ShapeDtypeStruct((b, NSA_KV_HEADS, nseg, LANES), bf16)],
        compiler_params=_cparams(("parallel",)),
        name="compress",
    )(kv3, kv3, wk1, wk1s, wk2s, wv1, wv1s, wv2s, pos8, kcg2)


def _nsa_kernel(slopes_ref, q3_ref, sm_ref, gsel_ref, kcd_ref, vcl_ref, ovt_ref,
                ks_ref, vs_ref, kw_ref, vw_ref, y_ref, qa_scr, m_scr, acc_scr):
    g = pl.program_id(1)
    qi = pl.program_id(2)
    q0 = qi * TQ
    ncp = kcd_ref.shape[2]
    nsel = ncp // (SEL_BLOCK // CMP_STRIDE)
    rows = NSA_GROUP * TQ
    log2e = math.log2(math.e)

    lane = lax.broadcasted_iota(jnp.int32, (TQ, LANES), 1)
    lo = lane < HALF
    slopes = [slopes_ref[g * NSA_GROUP + r] * log2e for r in range(NSA_GROUP)]

    def per_head(x, fn):
        return jnp.concatenate([fn(r, x[r * TQ:(r + 1) * TQ]) for r in range(NSA_GROUP)], axis=0)

    q3 = q3_ref[0, 0].reshape(rows, 2 * LANES)
    q1 = q3[:, 0:LANES]

    cidx = lax.broadcasted_iota(jnp.int32, (TQ, ncp), 1)
    tpos_c = q0 + lax.broadcasted_iota(jnp.int32, (TQ, ncp), 0)
    blk_end = cidx * CMP_STRIDE + (CMP_BLOCK - 1)
    valid_c = jnp.logical_and(tpos_c >= blk_end, cidx < ncp - 1)
    krel_c = (CMP_BLOCK - 1 - q0 + CMP_STRIDE * lax.broadcasted_iota(jnp.int32, (1, ncp), 1)).astype(f32)
    s_c = lax.dot_general(q3, kcd_ref[0, 0], NT, preferred_element_type=f32)
    s_c = per_head(s_c, lambda r, x: jnp.where(valid_c, x + slopes[r] * krel_c, NEG))
    e_c = jnp.exp2(s_c - jnp.max(s_c, axis=-1, keepdims=True))
    p_c = e_c / jnp.sum(e_c, axis=-1, keepdims=True)
    p_c = per_head(p_c, lambda r, x: jnp.where(valid_c, x, 0.0))
    p_sum = p_c[0:TQ]
    for r in range(1, NSA_GROUP):
        p_sum = p_sum + p_c[r * TQ:(r + 1) * TQ]
    o_cmp = jnp.dot(p_c.astype(bf16), vcl_ref[0, 0], preferred_element_type=f32)

    band = WINDOW + TQ
    kb = pl.multiple_of(jnp.maximum(q0 - WINDOW, 0), TQ)
    s_w = lax.dot_general(q1, kw_ref[0, 0, pl.ds(kb, band), :], NT, preferred_element_type=f32)
    krel_w = (kb - q0 + lax.broadcasted_iota(jnp.int32, (1, band), 1)).astype(f32)
    dist_w = (q0 - kb + lax.broadcasted_iota(jnp.int32, (TQ, band), 0)
              - lax.broadcasted_iota(jnp.int32, (TQ, band), 1))
    valid_w = jnp.logical_and(dist_w >= 0, dist_w < WINDOW)
    s_w = per_head(s_w, lambda r, x: jnp.where(valid_w, x + slopes[r] * krel_w, NEG))
    p_w = jnp.exp2(s_w - jnp.max(s_w, axis=-1, keepdims=True))
    acc_w = jnp.dot(p_w.astype(bf16), vw_ref[0, 0, pl.ds(kb, band), :], preferred_element_type=f32)
    o_win = acc_w / pltpu.roll(acc_w, HALF, 1)

    imp = lax.dot_general(ovt_ref[...], jnp.concatenate(_split3(p_sum), axis=1), NT,
                          preferred_element_type=f32)[0:nsel]
    jrow = lax.broadcasted_iota(jnp.int32, (nsel, TQ), 0)
    tcol = q0 + lax.broadcasted_iota(jnp.int32, (nsel, TQ), 1)
    forced = jnp.logical_or(jrow == lax.shift_right_logical(tcol, 6), jrow == 0)
    future = jrow * SEL_BLOCK > tcol
    work = jnp.where(forced, FORCE, jnp.where(future, -FORCE, imp))
    jrow_f = jrow.astype(f32)
    bias_t = jnp.full((nsel, TQ), NEG, f32)
    for _ in range(min(SEL_TOPN, nsel)):
        best = jnp.max(work, axis=0, keepdims=True)
        first = jnp.min(jnp.where(work == best, jrow_f, float(nsel)), axis=0, keepdims=True)
        hit = jrow_f == first
        bias_t = jnp.where(hit, 0.0, bias_t)
        work = jnp.where(hit, -3e38, work)
    if nsel < HALF:
        bias_t = jnp.concatenate([bias_t, jnp.full((HALF - nsel, TQ), NEG, f32)], axis=0)
    bias = jnp.concatenate([bias_t, bias_t], axis=0).T

    bias_b = bias.astype(bf16)
    qa_scr[...] = per_head(q1, lambda r, x: jnp.where(lo, x, bias_b))

    m_scr[...] = jnp.full(m_scr.shape, NEG, f32)
    acc_scr[...] = jnp.zeros(acc_scr.shape, f32)

    def sel_tile(state, kt, causal):
        m_old, acc = state
        k0 = pl.multiple_of(kt * TK, TK)
        s = lax.dot_general(qa_scr[...], ks_ref[0, 0, pl.ds(k0, TK), :], NT, preferred_element_type=f32)
        krel = (k0 - q0 + lax.broadcasted_iota(jnp.int32, (1, TK), 1)).astype(f32)
        if causal:
            ahead = (k0 - q0 + lax.broadcasted_iota(jnp.int32, (TQ, TK), 1)
                     > lax.broadcasted_iota(jnp.int32, (TQ, TK), 0))
            s = per_head(s, lambda r, x: jnp.where(ahead, NEG, x + slopes[r] * krel))
        else:
            s = per_head(s, lambda r, x: x + slopes[r] * krel)
        m_new = jnp.maximum(m_old, jnp.max(s, axis=-1, keepdims=True))
        p = jnp.exp2(s - m_new[:, 0:1])
        acc = jnp.exp2(m_old - m_new) * acc + jnp.dot(p.astype(bf16), vs_ref[0, 0, pl.ds(k0, TK), :],
                                                      preferred_element_type=f32)
        return m_new, acc

    def sel_tiles(tiles):
        state = (m_scr[...], acc_scr[...])
        for kt, causal in tiles:
            state = sel_tile(state, kt, causal)
        m_scr[...] = state[0]
        acc_scr[...] = state[1]

    n_full = q0 // TK

    def sel_body(i, carry):
        sel_tiles([(2 * i, False), (2 * i + 1, False)])
        return carry

    lax.fori_loop(0, n_full // 2, sel_body, 0)

    @pl.when(n_full % 2 == 1)
    def _():
        sel_tiles([(n_full - 1, False), (n_full, True)])

    @pl.when(n_full % 2 == 0)
    def _():
        sel_tiles([(n_full, True)])

    acc_s = acc_scr[...]
    o_slc = acc_s / pltpu.roll(acc_s, HALF, 1)

    gates = jnp.dot(jnp.concatenate(_hi_lo(jax.nn.sigmoid(sm_ref[0])), axis=1), gsel_ref[0],
                    preferred_element_type=f32)

    def gate(r, branch):
        col = 3 * r + branch
        return gates[:, col * LANES:(col + 1) * LANES]

    def mix(r):
        sl = slice(r * TQ, (r + 1) * TQ)
        return gate(r, 0) * o_cmp[sl] + gate(r, 1) * o_slc[sl] + gate(r, 2) * o_win[sl]

    for c in range(NSA_GROUP // 2):
        y_ref[0, :, c * LANES:(c + 1) * LANES] = jnp.where(lo, mix(2 * c), pltpu.roll(mix(2 * c + 1), HALF, 1))


def _gate_select():
    sel = np.zeros((NSA_KV_HEADS, 2 * LANES, 3 * NSA_GROUP * LANES), np.float32)
    for g in range(NSA_KV_HEADS):
        for c in range(3 * NSA_GROUP):
            src = 3 * NSA_GROUP * g + c
            sel[g, [src, LANES + src], c * LANES:(c + 1) * LANES] = 1.0
    return jnp.asarray(sel, dtype=bf16)


def _nsa(slopes, q3, sm3, kcd, vcl, ovt, ks, vs, kw, vw):
    b, t = q3.shape[0], q3.shape[3]
    gsel = _gate_select()
    gw = NSA_GROUP * NSA_HEAD_DIM
    ncp = kcd.shape[2]

    def kv_spec(rows):
        return pl.BlockSpec((1, 1, rows, LANES), lambda bi, gi, qi: (bi, gi, 0, 0))

    return pl.pallas_call(
        _nsa_kernel,
        grid=(b, NSA_KV_HEADS, t // TQ),
        in_specs=[
            pl.BlockSpec(memory_space=pltpu.SMEM),
            pl.BlockSpec((1, 1, NSA_GROUP, TQ, 2 * LANES), lambda bi, gi, qi: (bi, gi, 0, qi, 0)),
            pl.BlockSpec((1, TQ, LANES), lambda bi, gi, qi: (bi, qi, 0)),
            pl.BlockSpec((1,) + gsel.shape[1:], lambda bi, gi, qi: (gi, 0, 0)),
            pl.BlockSpec((1, 1, ncp, 2 * LANES), lambda bi, gi, qi: (bi, gi, 0, 0)), kv_spec(ncp),
            pl.BlockSpec(ovt.shape, lambda bi, gi, qi: (0, 0)),
            kv_spec(t), kv_spec(t), kv_spec(t), kv_spec(t),
        ],
        out_specs=pl.BlockSpec((1, TQ, gw), lambda bi, gi, qi: (bi, qi, gi)),
        out_shape=jax.ShapeDtypeStruct((b, t, NSA_WIDTH), f32),
        scratch_shapes=[
            pltpu.VMEM((NSA_GROUP * TQ, LANES), bf16),
            pltpu.VMEM((NSA_GROUP * TQ, LANES), f32),
            pltpu.VMEM((NSA_GROUP * TQ, LANES), f32),
        ],
        compiler_params=_cparams(("parallel", "parallel", "arbitrary")),
        name="nsa",
    )(slopes, q3, sm3, gsel, kcd, vcl, ovt, ks, vs, kw, vw)


def _log_sigmoid(x):
    return jnp.minimum(x, 0.0) - jnp.log1p(jnp.exp(-jnp.abs(x)))


def _mlstm_kernel(bi_ref, bf_ref, u_ref, v_ref, op_ref, gi_ref, gf_ref, cw_ref, cb_ref, wq_ref, wk_ref,
                  ng_ref, sk_ref, y_ref, uc_scr, q_scr, kt_scr, ct_scr, m_scr, b_scr, li_scr, xp_scr):
    h = pl.program_id(1)
    t = u_ref.shape[1]
    L = MLSTM_L
    dm = MLSTM_HEAD_DIM

    x = u_ref[0]
    xp_scr[0:8, :] = jnp.zeros((8, dm), f32)
    xp_scr[8:, :] = x
    acc = x * cw_ref[CONV_WIDTH - 1:CONV_WIDTH, :]
    for s in range(1, CONV_WIDTH):
        acc = acc + xp_scr[8 - s:8 - s + t, :] * cw_ref[CONV_WIDTH - 1 - s:CONV_WIDTH - s, :]
    uc = acc + cb_ref[...]
    uc = uc * jax.nn.sigmoid(uc)
    uc_scr[...] = uc
    ucb = uc.astype(bf16)
    q_scr[...] = jnp.dot(ucb, wq_ref[0].astype(bf16), preferred_element_type=f32).astype(bf16)
    k = jnp.dot(ucb, wk_ref[0].astype(bf16), preferred_element_type=f32) * (dm ** -0.5)
    kt_scr[...] = k.T

    ct_scr[...] = jnp.zeros(ct_scr.shape, f32)
    m_scr[...] = jnp.zeros(m_scr.shape, f32)

    li_ = lax.broadcasted_iota(jnp.int32, (L, L), 0)
    si_ = lax.broadcasted_iota(jnp.int32, (L, L), 1)
    causal = si_ <= li_
    diag = si_ == li_
    ones_v = jnp.ones((L, dm), f32)
    ones_sq = jnp.ones((dm, dm), bf16)

    log_f = _log_sigmoid(gf_ref[0, 0] + bf_ref[h])
    upper = jnp.where(li_ <= si_, 1.0, 0.0).astype(bf16)
    b_scr[...] = jnp.dot(jnp.concatenate(_split3(log_f), axis=1), jnp.concatenate([upper] * 3, axis=0),
                         preferred_element_type=f32)
    li_scr[...] = gi_ref[0, 0] + bi_ref[h]

    def chunk(c, ct, m_prev):
        r0 = pl.multiple_of(c * L, L)
        qc = q_scr[pl.ds(r0, L), :]
        ktc = kt_scr[:, pl.ds(r0, L)]
        vaug = jnp.concatenate([v_ref[0, pl.ds(r0, L), :], ones_v], axis=1).astype(bf16)
        log_i = li_scr[pl.ds(c, 1), :]
        b_row = b_scr[pl.ds(c, 1), :]
        b_col = jnp.sum(jnp.where(diag, b_row, 0.0), axis=-1, keepdims=True)
        g_sum = b_row[:, L - 1:L]
        dmat = jnp.where(causal, (b_col - b_row) + log_i, NEG)
        m_loc = jnp.max(dmat, axis=-1, keepdims=True)
        p = jnp.dot(qc, ktc.astype(bf16), preferred_element_type=f32) * jnp.exp(dmat - m_loc)
        intra = jnp.dot(p.astype(bf16), vaug, preferred_element_type=f32)
        m_inter = b_col + m_prev
        m_out = jnp.maximum(m_inter, m_loc)
        xo = (jnp.exp(m_inter - m_out) * jnp.dot(qc, ct.astype(bf16), preferred_element_type=f32)
              + jnp.exp(m_loc - m_out) * intra)
        num = xo[:, 0:dm]
        den = xo[:, dm:2 * dm]
        hh = num / jnp.maximum(jnp.abs(den), jnp.exp(-m_out))
        hh = hh * jax.nn.sigmoid(op_ref[0, pl.ds(r0, L), :])
        ssq = jnp.dot((hh * hh).astype(bf16), ones_sq, preferred_element_type=f32)
        hh = hh * lax.rsqrt(ssq * (1.0 / dm) + NORM_EPS) * ng_ref[0]
        y_ref[0, pl.ds(r0, L), :] = hh + sk_ref[...] * uc_scr[pl.ds(r0, L), :]

        w_end = (g_sum - b_row) + log_i
        m_new = jnp.maximum(g_sum + m_prev, jnp.max(w_end, axis=-1, keepdims=True))
        decay = jnp.exp(g_sum + m_prev - m_new)
        w = jnp.exp(w_end - m_new)
        return decay * ct + jnp.dot((ktc * w).astype(bf16), vaug, preferred_element_type=f32), m_new

    def chunk_group(i, carry):
        ct, m_prev = ct_scr[...], m_scr[0:1, 0:1]
        for j in range(MLSTM_UNROLL):
            ct, m_prev = chunk(i * MLSTM_UNROLL + j, ct, m_prev)
        ct_scr[...] = ct
        m_scr[...] = jnp.broadcast_to(m_prev, m_scr.shape)
        return carry

    lax.fori_loop(0, t // (L * MLSTM_UNROLL), chunk_group, 0)


def _mlstm(b_i, b_f, u3, v3, op3, gi4, gf4, cw, cb2, wq, wk, ng3, sk2):
    b, t, _ = u3.shape
    dm = MLSTM_HEAD_DIM
    seq = pl.BlockSpec((1, t, dm), lambda bi, hi: (bi, 0, hi))
    gate = pl.BlockSpec((1, 1, t // MLSTM_L, MLSTM_L), lambda bi, hi: (bi, hi, 0, 0))
    smem = pl.BlockSpec(memory_space=pltpu.SMEM)
    return pl.pallas_call(
        _mlstm_kernel,
        grid=(b, MLSTM_HEADS),
        in_specs=[
            smem, smem, seq, seq, seq, gate, gate,
            pl.BlockSpec((CONV_WIDTH, dm), lambda bi, hi: (0, hi)),
            pl.BlockSpec((1, dm), lambda bi, hi: (0, hi)),
            pl.BlockSpec((1, dm, dm), lambda bi, hi: (hi, 0, 0)),
            pl.BlockSpec((1, dm, dm), lambda bi, hi: (hi, 0, 0)),
            pl.BlockSpec((1, 1, dm), lambda bi, hi: (hi, 0, 0)),
            pl.BlockSpec((1, dm), lambda bi, hi: (0, hi)),
        ],
        out_specs=seq,
        out_shape=jax.ShapeDtypeStruct((b, t, MLSTM_WIDTH), f32),
        scratch_shapes=[
            pltpu.VMEM((t, dm), f32),
            pltpu.VMEM((t, dm), bf16),
            pltpu.VMEM((dm, t), f32),
            pltpu.VMEM((dm, 2 * dm), f32),
            pltpu.VMEM((8, LANES), f32),
            pltpu.VMEM((t // MLSTM_L, MLSTM_L), f32),
            pltpu.VMEM((t // MLSTM_L, MLSTM_L), f32),
            pltpu.VMEM((t + 8, dm), f32),
        ],
        compiler_params=_cparams(("parallel", "parallel")),
        name="mlstm",
    )(b_i, b_f, u3, v3, op3, gi4, gf4, cw, cb2, wq, wk, ng3, sk2)


def _ffn_kernel(x_ref, ya_ref, yb_ref, wo_ref, g2_ref, wg_ref, wu_ref, wd_ref, o_ref, act_scr):
    x1 = (x_ref[...]
          + jnp.dot(ya_ref[...].astype(bf16), wo_ref[0:NSA_WIDTH, :], preferred_element_type=f32)
          + jnp.dot(yb_ref[...].astype(bf16), wo_ref[NSA_WIDTH:NSA_WIDTH + MLSTM_WIDTH, :],
                    preferred_element_type=f32))
    h2 = x1 * lax.rsqrt(jnp.mean(x1 * x1, axis=-1, keepdims=True) + NORM_EPS) * g2_ref[...]
    h2b = h2.astype(bf16)
    for c in range(D_FF // FF_CHUNK):
        cols = slice(c * FF_CHUNK, (c + 1) * FF_CHUNK)
        gt = jnp.dot(h2b, wg_ref[:, cols], preferred_element_type=f32)
        up = jnp.dot(h2b, wu_ref[:, cols], preferred_element_type=f32)
        act_scr[:, cols] = (gt * jax.nn.sigmoid(gt) * up).astype(bf16)
    o_ref[...] = x1 + jnp.dot(act_scr[...], wd_ref[...], preferred_element_type=f32)


def _ffn(x2, ya, yb, wo, g2, wg, wu, wd):
    n = x2.shape[0]
    tm = TM_FFN

    def const(shape):
        return pl.BlockSpec(shape, lambda i: (0, 0), pipeline_mode=pl.Buffered(1))

    return pl.pallas_call(
        _ffn_kernel,
        grid=(n // tm,),
        in_specs=[
            pl.BlockSpec((tm, D_MODEL), lambda i: (i, 0)),
            pl.BlockSpec((tm, NSA_WIDTH), lambda i: (i, 0)),
            pl.BlockSpec((tm, MLSTM_WIDTH), lambda i: (i, 0)),
            const(wo.shape), const(g2.shape), const(wg.shape), const(wu.shape), const(wd.shape),
        ],
        out_specs=pl.BlockSpec((tm, D_MODEL), lambda i: (i, 0)),
        out_shape=jax.ShapeDtypeStruct((n, D_MODEL), f32),
        scratch_shapes=[pltpu.VMEM((tm, D_FF), bf16)],
        compiler_params=_cparams(("parallel",)),
        name="ffn",
    )(x2, ya, yb, wo, g2, wg, wu, wd)


def _overlap_t(ncp):
    nsel = ncp // (SEL_BLOCK // CMP_STRIDE)
    cs = np.arange(ncp) * CMP_STRIDE
    ss = np.arange(nsel) * SEL_BLOCK
    ov = ((cs[None, :] < ss[:, None] + SEL_BLOCK) & (cs[None, :] + CMP_BLOCK > ss[:, None])).astype(np.float32)
    ov[:, ncp - 1] = 0.0
    out = np.zeros((LANES, ncp), np.float32)
    out[:nsel] = ov
    return jnp.asarray(np.tile(out, (1, 3)), dtype=bf16)


def _layer(x, norm1_g, w_in, q_g, kc_g, ks_g, kw_g, cmp_pos, w_ck1, w_ck2, w_cv1, w_cv2, conv_w, conv_b,
           w_mq, w_mk, b_i, b_f, mlstm_norm_g, mlstm_skip, w_out, norm2_g, w_gate, w_up, w_down):
    b, t, d = x.shape
    n = b * t
    x2 = x.reshape(n, d)

    o_gate = NSA_WIDTH + 6 * KV_WIDTH
    o_u = o_gate + 3 * NSA_HEADS
    o_if = o_u + 3 * MLSTM_WIDTH
    w_perm = jnp.concatenate([
        w_in[:, :o_gate], w_in[:, o_u:o_if], w_in[:, o_gate:o_u], w_in[:, o_if:],
        jnp.zeros((d, LANES - 3 * NSA_HEADS - 2 * MLSTM_HEADS), w_in.dtype)], axis=1).astype(bf16)
    w_gates_t = jnp.concatenate([w_in[:, o_if:].T, jnp.zeros((GATE_ROWS - 2 * MLSTM_HEADS, d), w_in.dtype)],
                                axis=0).astype(bf16)
    kvc2, u2, vm2, op2, sm2, gates_t, ks, vs, kw, vw, q3 = _inproj(
        x2, norm1_g.reshape(1, d), w_perm, w_gates_t, jnp.tile(ks_g, 2).reshape(1, LANES),
        jnp.tile(kw_g, 2).reshape(1, LANES), jnp.tile(q_g, NSA_HEADS).reshape(1, NSA_WIDTH), b, t)

    kv3 = kvc2.reshape(b, t, 2 * KV_WIDTH)
    nseg = t // CMP_STRIDE
    pos8 = jnp.broadcast_to(cmp_pos.reshape(1, CMP_BLOCK * NSA_HEAD_DIM), (8, CMP_BLOCK * NSA_HEAD_DIM))
    wk1s, wk2s = _compress_weights(w_ck1, w_ck2)
    wv1s, wv2s = _compress_weights(w_cv1, w_cv2)
    kcd, vcl = _compress(kv3, w_ck1, wk1s, wk2s, w_cv1, wv1s, wv2s, pos8, jnp.tile(kc_g, 2).reshape(1, LANES))
    slopes = jnp.exp2(-8.0 * (jnp.arange(NSA_HEADS, dtype=f32) + 1.0) / NSA_HEADS)
    y_nsa = _nsa(slopes, q3, sm2.reshape(b, t, LANES), kcd, vcl, _overlap_t(nseg), ks, vs, kw, vw)

    gi4 = gates_t[0:MLSTM_HEADS].reshape(MLSTM_HEADS, b, t).transpose(1, 0, 2)
    gf4 = gates_t[MLSTM_HEADS:2 * MLSTM_HEADS].reshape(MLSTM_HEADS, b, t).transpose(1, 0, 2)
    y_mem = _mlstm(b_i, b_f, u2.reshape(b, t, MLSTM_WIDTH), vm2.reshape(b, t, MLSTM_WIDTH),
                   op2.reshape(b, t, MLSTM_WIDTH), gi4.reshape(b, MLSTM_HEADS, t // MLSTM_L, MLSTM_L),
                   gf4.reshape(b, MLSTM_HEADS, t // MLSTM_L, MLSTM_L), conv_w, conv_b.reshape(1, MLSTM_WIDTH),
                   w_mq, w_mk,
                   mlstm_norm_g.reshape(MLSTM_HEADS, 1, MLSTM_HEAD_DIM), mlstm_skip.reshape(1, MLSTM_WIDTH))

    out = _ffn(x2, y_nsa.reshape(n, NSA_WIDTH), y_mem.reshape(n, MLSTM_WIDTH), w_out.astype(bf16),
               norm2_g.reshape(1, d), w_gate.astype(bf16), w_up.astype(bf16), w_down.astype(bf16))
    return out.reshape(b, t, d)


def kernel(x, norm1_g, w_in, q_norm_g, kc_norm_g, ks_norm_g, kw_norm_g, cmp_pos, w_ck1, w_ck2, w_cv1, w_cv2,
           conv_w, conv_b, w_mq, w_mk, b_i, b_f, mlstm_norm_g, mlstm_skip, w_out, norm2_g, w_gate, w_up, w_down):
    depth = norm1_g.shape[0]
    for l in range(depth):
        x = _layer(x, norm1_g[l], w_in[l], q_norm_g[l], kc_norm_g[l], ks_norm_g[l], kw_norm_g[l], cmp_pos[l],
                   w_ck1[l], w_ck2[l], w_cv1[l], w_cv2[l], conv_w[l], conv_b[l], w_mq[l], w_mk[l], b_i[l], b_f[l],
                   mlstm_norm_g[l], mlstm_skip[l], w_out[l], norm2_g[l], w_gate[l], w_up[l], w_down[l])
    return x
```

```python
import functools
import math

import numpy as np
import jax
import jax.numpy as jnp
from jax import lax
from jax.experimental import pallas as pl
from jax.experimental.pallas import tpu as pltpu

f32 = jnp.float32
bf16 = jnp.bfloat16

D_MODEL = 1024
NSA_HEADS = 8
NSA_KV_HEADS = 2
NSA_HEAD_DIM = 64
NSA_GROUP = NSA_HEADS // NSA_KV_HEADS
CMP_BLOCK = 32
CMP_STRIDE = 16
CMP_HIDDEN = 256
SEL_BLOCK = 64
SEL_TOPN = 16
WINDOW = 512
MLSTM_HEADS = 4
MLSTM_HEAD_DIM = 128
CONV_WIDTH = 4
NSA_WIDTH = NSA_HEADS * NSA_HEAD_DIM
MLSTM_WIDTH = MLSTM_HEADS * MLSTM_HEAD_DIM
KV_WIDTH = NSA_KV_HEADS * NSA_HEAD_DIM
D_FF = -(-8 * D_MODEL // (3 * 256)) * 256
NORM_EPS = 1e-6
NEG = -1e30
FORCE = 1e9
SCORE_BOUND = 100.0

LANES = 128
HALF = 64
VMEM_LIMIT = 56 * 1024 * 1024

TM_PROJ = 512
TQ = 256
TK = 512
MLSTM_L = 256
MLSTM_UNROLL = 4
TM_FFN = 512
FF_CHUNK = 256

HI = lax.Precision.HIGHEST
NT = (((1,), (1,)), ((), ()))


def _cparams(sem):
    return pltpu.CompilerParams(dimension_semantics=sem, vmem_limit_bytes=VMEM_LIMIT)


def _split3(x):
    x1 = x.astype(bf16)
    r1 = x - x1.astype(f32)
    x2 = r1.astype(bf16)
    x3 = (r1 - x2.astype(f32)).astype(bf16)
    return x1, x2, x3


COL_Q = 0
COL_KV = COL_Q + NSA_WIDTH
COL_U = COL_KV + 6 * KV_WIDTH
COL_VM = COL_U + MLSTM_WIDTH
COL_OP = COL_VM + MLSTM_WIDTH
COL_SM = COL_OP + MLSTM_WIDTH
PROJ_WIDTH = COL_SM + LANES
GATE_ROWS = 16


def _pair_norm(x, gain, lo):
    sq = x * x
    s_lo = jnp.sum(jnp.where(lo, sq, 0.0), axis=-1, keepdims=True)
    s_hi = jnp.sum(jnp.where(lo, 0.0, sq), axis=-1, keepdims=True)
    inv = jnp.where(lo, lax.rsqrt(s_lo / HALF + NORM_EPS), lax.rsqrt(s_hi / HALF + NORM_EPS))
    return x * inv * gain


def _inproj_kernel(x_ref, g_ref, w_ref, wgt_ref, ksg_ref, kwg_ref, qg_ref,
                   kvc_ref, u_ref, vm_ref, op_ref, sm_ref, gt_ref, ks_ref, vs_ref, kw_ref, vw_ref, q3_ref,
                   *, tiles_per_seq):
    tm = x_ref.shape[0]
    x = x_ref[...]
    h = x * lax.rsqrt(jnp.mean(x * x, axis=-1, keepdims=True) + NORM_EPS) * g_ref[...]
    hb = h.astype(bf16)

    def proj(col, width):
        return jnp.dot(hb, w_ref[:, col:col + width], preferred_element_type=f32)

    lane = lax.broadcasted_iota(jnp.int32, (tm, LANES), 1)
    lo = lane < HALF
    pos = (pl.program_id(0) % tiles_per_seq) * tm + lax.broadcasted_iota(jnp.int32, (tm, LANES), 0)

    q = proj(COL_Q, NSA_WIDTH)
    scale = NSA_HEAD_DIM ** -0.5 * math.log2(math.e)
    for c in range(NSA_HEADS // 2):
        pair = _pair_norm(q[:, c * LANES:(c + 1) * LANES], qg_ref[:, c * LANES:(c + 1) * LANES], lo) * scale
        for par, xq in enumerate((pair, pltpu.roll(pair, HALF, 1))):
            hi = jnp.where(lo, xq, 0.0).astype(bf16).astype(f32)
            res = jnp.where(lo, xq - hi, 0.0)
            head = 2 * c + par
            q3_ref[0, head // NSA_GROUP, head % NSA_GROUP] = jnp.concatenate(
                [hi + pltpu.roll(res, HALF, 1), hi], axis=1).astype(bf16)

    kv = proj(COL_KV, 6 * KV_WIDTH)
    kvc_ref[...] = kv[:, 0:2 * KV_WIDTH]
    code = jnp.where(lax.shift_right_logical(pos, 6) == (lane & (HALF - 1)), 1.0, 0.0)

    def put(xk, o_ref, fill):
        o_ref[0, 0] = jnp.where(lo, xk, fill).astype(bf16)
        o_ref[0, 1] = jnp.where(lo, pltpu.roll(xk, HALF, 1), fill).astype(bf16)

    put(_pair_norm(kv[:, 2 * KV_WIDTH:3 * KV_WIDTH], ksg_ref[...], lo), ks_ref, code)
    put(kv[:, 3 * KV_WIDTH:4 * KV_WIDTH], vs_ref, 1.0)
    put(_pair_norm(kv[:, 4 * KV_WIDTH:5 * KV_WIDTH], kwg_ref[...], lo), kw_ref, 0.0)
    put(kv[:, 5 * KV_WIDTH:6 * KV_WIDTH], vw_ref, 1.0)

    u_ref[...] = proj(COL_U, MLSTM_WIDTH)
    vm_ref[...] = proj(COL_VM, MLSTM_WIDTH)
    op_ref[...] = proj(COL_OP, MLSTM_WIDTH)
    sm_ref[...] = proj(COL_SM, LANES)
    gt_ref[...] = lax.dot_general(wgt_ref[...], hb, NT, preferred_element_type=f32)


def _inproj(x2, g1, w_perm, w_gates_t, ksg2, kwg2, qg2, b, t):
    n = x2.shape[0]
    tm = min(TM_PROJ, t)
    nt = t // tm

    def rows(width):
        return pl.BlockSpec((tm, width), lambda i: (i, 0))

    def const(shape):
        return pl.BlockSpec(shape, lambda i: (0, 0))

    kv_spec = pl.BlockSpec((1, NSA_KV_HEADS, tm, LANES), lambda i: (i // nt, 0, i % nt, 0))
    kv_shape = jax.ShapeDtypeStruct((b, NSA_KV_HEADS, t, LANES), bf16)
    return pl.pallas_call(
        functools.partial(_inproj_kernel, tiles_per_seq=nt),
        grid=(n // tm,),
        in_specs=[rows(D_MODEL), const((1, D_MODEL)), const((D_MODEL, PROJ_WIDTH)), const((GATE_ROWS, D_MODEL)),
                  const((1, LANES)), const((1, LANES)), const((1, NSA_WIDTH))],
        out_specs=[rows(2 * KV_WIDTH), rows(MLSTM_WIDTH), rows(MLSTM_WIDTH), rows(MLSTM_WIDTH), rows(LANES),
                   pl.BlockSpec((GATE_ROWS, tm), lambda i: (0, i)),
                   kv_spec, kv_spec, kv_spec, kv_spec,
                   pl.BlockSpec((1, NSA_KV_HEADS, NSA_GROUP, tm, 2 * LANES),
                                lambda i: (i // nt, 0, 0, i % nt, 0))],
        out_shape=[jax.ShapeDtypeStruct((n, 2 * KV_WIDTH), f32), jax.ShapeDtypeStruct((n, MLSTM_WIDTH), f32),
                   jax.ShapeDtypeStruct((n, MLSTM_WIDTH), f32), jax.ShapeDtypeStruct((n, MLSTM_WIDTH), f32),
                   jax.ShapeDtypeStruct((n, LANES), f32), jax.ShapeDtypeStruct((GATE_ROWS, n), f32),
                   kv_shape, kv_shape, kv_shape, kv_shape,
                   jax.ShapeDtypeStruct((b, NSA_KV_HEADS, NSA_GROUP, t, 2 * LANES), bf16)],
        compiler_params=_cparams(("parallel",)),
        name="inproj",
    )(x2, g1, w_perm, w_gates_t, ksg2, kwg2, qg2)


def _gelu_tanh(x):
    return 0.5 * x * (1.0 + jnp.tanh(math.sqrt(2.0 / math.pi) * (x + 0.044715 * (x * x * x))))


def _hi_lo(x):
    hi = x.astype(bf16)
    return hi, (x - hi.astype(f32)).astype(bf16)


def _compress_kernel(kc_ref, vc_ref, wk1_ref, wk1s_ref, wk2s_ref, wv1_ref, wv1s_ref, wv2s_ref, pos_ref, kcg_ref,
                     kcd_ref, vcl_ref):
    nseg = kc_ref.shape[1] // CMP_STRIDE
    half = NSA_KV_HEADS * CMP_HIDDEN

    def branch(x_ref, w1_ref, w1s_ref, w2s_ref):
        terms = []
        for r in range(CMP_STRIDE):
            hi, lo = _hi_lo(x_ref[0, pl.ds(r, nseg, stride=CMP_STRIDE), :])
            terms += [hi, lo, hi]
        acc = jnp.dot(jnp.concatenate(terms, axis=1), w1s_ref[...], preferred_element_type=f32)
        pos_term = jnp.dot(pos_ref[...], w1_ref[...], precision=HI, preferred_element_type=f32)[0:1]
        hid = (acc[:, 0:half] + pltpu.roll(acc[:, half:2 * half], nseg - 1, 0)
               + jnp.concatenate([pos_term] * NSA_KV_HEADS, axis=1))
        hi, lo = _hi_lo(_gelu_tanh(hid))
        return jnp.dot(jnp.concatenate([hi, lo, hi], axis=1), w2s_ref[...], preferred_element_type=f32)

    kc2 = branch(kc_ref, wk1_ref, wk1s_ref, wk2s_ref)
    vc2 = branch(vc_ref, wv1_ref, wv1s_ref, wv2s_ref)
    lane = lax.broadcasted_iota(jnp.int32, (nseg, LANES), 1)
    for g in range(NSA_KV_HEADS):
        kc = kc2[:, g * LANES:(g + 1) * LANES]
        kc = kc * lax.rsqrt(jnp.mean(kc * kc, axis=-1, keepdims=True) + NORM_EPS) * kcg_ref[...]
        k_hi = kc.astype(bf16)
        k_lo = jnp.where(lane < HALF, kc - k_hi.astype(f32), 0.0).astype(bf16)
        kcd_ref[0, g] = jnp.concatenate([k_hi, k_lo], axis=1)
        vcl_ref[0, g] = jnp.where(lane < HALF, vc2[:, g * LANES:(g + 1) * LANES], 0.0).astype(bf16)


def _compress_weights(w1, w2):
    w1r = w1.reshape(2, CMP_STRIDE, NSA_HEAD_DIM, CMP_HIDDEN)
    z = jnp.zeros_like(w1r[0])

    def two_groups(w):
        return jnp.concatenate([jnp.concatenate([w, z], axis=2), jnp.concatenate([z, w], axis=2)], axis=1)

    wr = jnp.concatenate([two_groups(w1r[0]), two_groups(w1r[1])], axis=2)
    hi, lo = _hi_lo(wr)
    w1s = jnp.concatenate([hi, hi, lo], axis=1).reshape(CMP_STRIDE * 3 * LANES, 2 * NSA_KV_HEADS * CMP_HIDDEN)
    z2 = jnp.zeros_like(w2)
    w2d = jnp.concatenate([jnp.concatenate([w2, w2, z2, z2], axis=1),
                           jnp.concatenate([z2, z2, w2, w2], axis=1)], axis=0)
    hi2, lo2 = _hi_lo(w2d)
    return w1s, jnp.concatenate([hi2, hi2, lo2], axis=0)


def _compress(kv3, wk1, wk1s, wk2s, wv1, wv1s, wv2s, pos8, kcg2):
    b, t, _ = kv3.shape
    nseg = t // CMP_STRIDE

    def const(shape):
        return pl.BlockSpec(shape, lambda bi: (0,) * len(shape), pipeline_mode=pl.Buffered(1))

    return pl.pallas_call(
        _compress_kernel,
        grid=(b,),
        in_specs=[pl.BlockSpec((1, t, KV_WIDTH), lambda bi: (bi, 0, 0)),
                  pl.BlockSpec((1, t, KV_WIDTH), lambda bi: (bi, 0, 1)),
                  const(wk1.shape), const(wk1s.shape), const(wk2s.shape),
                  const(wv1.shape), const(wv1s.shape), const(wv2s.shape),
                  const(pos8.shape), const(kcg2.shape)],
        out_specs=[pl.BlockSpec((1, NSA_KV_HEADS, nseg, 2 * LANES), lambda bi: (bi, 0, 0, 0)),
                   pl.BlockSpec((1, NSA_KV_HEADS, nseg, LANES), lambda bi: (bi, 0, 0, 0))],
        out_shape=[jax.ShapeDtypeStruct((b, NSA_KV_HEADS, nseg, 2 * LANES), bf16),
                   jax.ShapeDtypeStruct((b, NSA_KV_HEADS, nseg, LANES), bf16)],
        compiler_params=_cparams(("parallel",)),
        name="compress",
    )(kv3, kv3, wk1, wk1s, wk2s, wv1, wv1s, wv2s, pos8, kcg2)


def _nsa_kernel(slopes_ref, bounded_ref, q3_ref, sm_ref, gsel_ref, kcd_ref, vcl_ref, ovt_ref,
                ks_ref, vs_ref, kw_ref, vw_ref, y_ref, qa_scr, m_scr, acc_scr):
    g = pl.program_id(1)
    qi = pl.program_id(2)
    q0 = qi * TQ
    ncp = kcd_ref.shape[2]
    nsel = ncp // (SEL_BLOCK // CMP_STRIDE)
    rows = NSA_GROUP * TQ
    log2e = math.log2(math.e)

    lane = lax.broadcasted_iota(jnp.int32, (TQ, LANES), 1)
    lo = lane < HALF
    slopes = [slopes_ref[g * NSA_GROUP + r] * log2e for r in range(NSA_GROUP)]

    def per_head(x, fn):
        return jnp.concatenate([fn(r, x[r * TQ:(r + 1) * TQ]) for r in range(NSA_GROUP)], axis=0)

    q3 = q3_ref[0, 0].reshape(rows, 2 * LANES)
    q1 = q3[:, 0:LANES]

    cidx = lax.broadcasted_iota(jnp.int32, (TQ, ncp), 1)
    tpos_c = q0 + lax.broadcasted_iota(jnp.int32, (TQ, ncp), 0)
    blk_end = cidx * CMP_STRIDE + (CMP_BLOCK - 1)
    valid_c = jnp.logical_and(tpos_c >= blk_end, cidx < ncp - 1)
    krel_c = (CMP_BLOCK - 1 - q0 + CMP_STRIDE * lax.broadcasted_iota(jnp.int32, (1, ncp), 1)).astype(f32)
    s_c = lax.dot_general(q3, kcd_ref[0, 0], NT, preferred_element_type=f32)
    s_c = per_head(s_c, lambda r, x: jnp.where(valid_c, x + slopes[r] * krel_c, NEG))
    e_c = jnp.exp2(s_c - jnp.max(s_c, axis=-1, keepdims=True))
    p_c = e_c / jnp.sum(e_c, axis=-1, keepdims=True)
    p_c = per_head(p_c, lambda r, x: jnp.where(valid_c, x, 0.0))
    p_sum = p_c[0:TQ]
    for r in range(1, NSA_GROUP):
        p_sum = p_sum + p_c[r * TQ:(r + 1) * TQ]
    o_cmp = jnp.dot(p_c.astype(bf16), vcl_ref[0, 0], preferred_element_type=f32)

    band = WINDOW + TQ
    kb = pl.multiple_of(jnp.maximum(q0 - WINDOW, 0), TQ)
    s_w = lax.dot_general(q1, kw_ref[0, 0, pl.ds(kb, band), :], NT, preferred_element_type=f32)
    krel_w = (kb - q0 + lax.broadcasted_iota(jnp.int32, (1, band), 1)).astype(f32)
    dist_w = (q0 - kb + lax.broadcasted_iota(jnp.int32, (TQ, band), 0)
              - lax.broadcasted_iota(jnp.int32, (TQ, band), 1))
    valid_w = jnp.logical_and(dist_w >= 0, dist_w < WINDOW)
    s_w = per_head(s_w, lambda r, x: jnp.where(valid_w, x + slopes[r] * krel_w, NEG))
    p_w = jnp.exp2(s_w - jnp.max(s_w, axis=-1, keepdims=True))
    acc_w = jnp.dot(p_w.astype(bf16), vw_ref[0, 0, pl.ds(kb, band), :], preferred_element_type=f32)
    o_win = acc_w / pltpu.roll(acc_w, HALF, 1)

    imp = lax.dot_general(ovt_ref[...], jnp.concatenate(_split3(p_sum), axis=1), NT,
                          preferred_element_type=f32)[0:nsel]
    jrow = lax.broadcasted_iota(jnp.int32, (nsel, TQ), 0)
    tcol = q0 + lax.broadcasted_iota(jnp.int32, (nsel, TQ), 1)
    forced = jnp.logical_or(jrow == lax.shift_right_logical(tcol, 6), jrow == 0)
    future = jrow * SEL_BLOCK > tcol
    work = jnp.where(forced, FORCE, jnp.where(future, -FORCE, imp))
    jrow_f = jrow.astype(f32)
    bias_t = jnp.full((nsel, TQ), NEG, f32)
    for _ in range(min(SEL_TOPN, nsel)):
        best = jnp.max(work, axis=0, keepdims=True)
        first = jnp.min(jnp.where(work == best, jrow_f, float(nsel)), axis=0, keepdims=True)
        hit = jrow_f == first
        bias_t = jnp.where(hit, 0.0, bias_t)
        work = jnp.where(hit, -3e38, work)
    if nsel < HALF:
        bias_t = jnp.concatenate([bias_t, jnp.full((HALF - nsel, TQ), NEG, f32)], axis=0)
    bias = jnp.concatenate([bias_t, bias_t], axis=0).T

    bias_b = bias.astype(bf16)
    qa_scr[...] = per_head(q1, lambda r, x: jnp.where(lo, x, bias_b))

    m_scr[...] = jnp.full(m_scr.shape, NEG, f32)
    acc_scr[...] = jnp.zeros(acc_scr.shape, f32)

    def scores(kt, causal, row_shift):
        k0 = pl.multiple_of(kt * TK, TK)
        s = lax.dot_general(qa_scr[...], ks_ref[0, 0, pl.ds(k0, TK), :], NT, preferred_element_type=f32)
        krel = (k0 - q0 + lax.broadcasted_iota(jnp.int32, (1, TK), 1)).astype(f32)

        def alibi(r, x):
            x = x + slopes[r] * krel
            return x if row_shift is None else x - row_shift[r]

        if causal:
            ahead = (k0 - q0 + lax.broadcasted_iota(jnp.int32, (TQ, TK), 1)
                     > lax.broadcasted_iota(jnp.int32, (TQ, TK), 0))
            return per_head(s, lambda r, x: jnp.where(ahead, NEG, alibi(r, x))), k0
        return per_head(s, alibi), k0

    def online_tile(state, kt, causal):
        m_old, acc = state
        s, k0 = scores(kt, causal, None)
        m_new = jnp.maximum(m_old, jnp.max(s, axis=-1, keepdims=True))
        p = jnp.exp2(s - m_new[:, 0:1])
        acc = jnp.exp2(m_old - m_new) * acc + jnp.dot(p.astype(bf16), vs_ref[0, 0, pl.ds(k0, TK), :],
                                                      preferred_element_type=f32)
        return m_new, acc

    def online_tiles(tiles):
        state = (m_scr[...], acc_scr[...])
        for kt, causal in tiles:
            state = online_tile(state, kt, causal)
        m_scr[...] = state[0]
        acc_scr[...] = state[1]

    def bounded_tiles(tiles):
        rowf = lax.broadcasted_iota(jnp.int32, (TQ, LANES), 0).astype(f32)
        row_shift = [jnp.concatenate([slopes[r] * rowf] * (TK // LANES), axis=1) for r in range(NSA_GROUP)]
        acc = acc_scr[...]
        for kt, causal in tiles:
            s, k0 = scores(kt, causal, row_shift)
            acc = acc + jnp.dot(jnp.exp2(s).astype(bf16), vs_ref[0, 0, pl.ds(k0, TK), :],
                                preferred_element_type=f32)
        acc_scr[...] = acc

    n_full = q0 // TK

    def selected_branch(run_tiles):
        def body(i, carry):
            run_tiles([(2 * i, False), (2 * i + 1, False)])
            return carry

        lax.fori_loop(0, n_full // 2, body, 0)

        @pl.when(n_full % 2 == 1)
        def _():
            run_tiles([(n_full - 1, False), (n_full, True)])

        @pl.when(n_full % 2 == 0)
        def _():
            run_tiles([(n_full, True)])

    bounded = bounded_ref[0] == 1

    @pl.when(bounded)
    def _():
        selected_branch(bounded_tiles)

    @pl.when(jnp.logical_not(bounded))
    def _():
        selected_branch(online_tiles)

    acc_s = acc_scr[...]
    o_slc = acc_s / pltpu.roll(acc_s, HALF, 1)

    gates = jnp.dot(jnp.concatenate(_hi_lo(jax.nn.sigmoid(sm_ref[0])), axis=1), gsel_ref[0],
                    preferred_element_type=f32)

    def gate(r, branch):
        col = 3 * r + branch
        return gates[:, col * LANES:(col + 1) * LANES]

    def mix(r):
        sl = slice(r * TQ, (r + 1) * TQ)
        return gate(r, 0) * o_cmp[sl] + gate(r, 1) * o_slc[sl] + gate(r, 2) * o_win[sl]

    for c in range(NSA_GROUP // 2):
        y_ref[0, :, c * LANES:(c + 1) * LANES] = jnp.where(lo, mix(2 * c), pltpu.roll(mix(2 * c + 1), HALF, 1))


def _gate_select():
    sel = np.zeros((NSA_KV_HEADS, 2 * LANES, 3 * NSA_GROUP * LANES), np.float32)
    for g in range(NSA_KV_HEADS):
        for c in range(3 * NSA_GROUP):
            src = 3 * NSA_GROUP * g + c
            sel[g, [src, LANES + src], c * LANES:(c + 1) * LANES] = 1.0
    return jnp.asarray(sel, dtype=bf16)


def _nsa(slopes, bounded, q3, sm3, kcd, vcl, ovt, ks, vs, kw, vw):
    b, t = q3.shape[0], q3.shape[3]
    gsel = _gate_select()
    gw = NSA_GROUP * NSA_HEAD_DIM
    ncp = kcd.shape[2]

    def kv_spec(rows):
        return pl.BlockSpec((1, 1, rows, LANES), lambda bi, gi, qi: (bi, gi, 0, 0))

    return pl.pallas_call(
        _nsa_kernel,
        grid=(b, NSA_KV_HEADS, t // TQ),
        in_specs=[
            pl.BlockSpec(memory_space=pltpu.SMEM),
            pl.BlockSpec(memory_space=pltpu.SMEM),
            pl.BlockSpec((1, 1, NSA_GROUP, TQ, 2 * LANES), lambda bi, gi, qi: (bi, gi, 0, qi, 0)),
            pl.BlockSpec((1, TQ, LANES), lambda bi, gi, qi: (bi, qi, 0)),
            pl.BlockSpec((1,) + gsel.shape[1:], lambda bi, gi, qi: (gi, 0, 0)),
            pl.BlockSpec((1, 1, ncp, 2 * LANES), lambda bi, gi, qi: (bi, gi, 0, 0)), kv_spec(ncp),
            pl.BlockSpec(ovt.shape, lambda bi, gi, qi: (0, 0)),
            kv_spec(t), kv_spec(t), kv_spec(t), kv_spec(t),
        ],
        out_specs=pl.BlockSpec((1, TQ, gw), lambda bi, gi, qi: (bi, qi, gi)),
        out_shape=jax.ShapeDtypeStruct((b, t, NSA_WIDTH), f32),
        scratch_shapes=[
            pltpu.VMEM((NSA_GROUP * TQ, LANES), bf16),
            pltpu.VMEM((NSA_GROUP * TQ, LANES), f32),
            pltpu.VMEM((NSA_GROUP * TQ, LANES), f32),
        ],
        compiler_params=_cparams(("parallel", "parallel", "arbitrary")),
        name="nsa",
    )(slopes, bounded, q3, sm3, gsel, kcd, vcl, ovt, ks, vs, kw, vw)


def _log_sigmoid(x):
    return jnp.minimum(x, 0.0) - jnp.log1p(jnp.exp(-jnp.abs(x)))


def _mlstm_kernel(bi_ref, bf_ref, u_ref, v_ref, op_ref, gi_ref, gf_ref, cw_ref, cb_ref, wq_ref, wk_ref,
                  ng_ref, sk_ref, y_ref, uc_scr, q_scr, kt_scr, ct_scr, m_scr, b_scr, li_scr, xp_scr):
    h = pl.program_id(1)
    t = u_ref.shape[1]
    L = MLSTM_L
    dm = MLSTM_HEAD_DIM

    x = u_ref[0]
    xp_scr[0:8, :] = jnp.zeros((8, dm), f32)
    xp_scr[8:, :] = x
    acc = x * cw_ref[CONV_WIDTH - 1:CONV_WIDTH, :]
    for s in range(1, CONV_WIDTH):
        acc = acc + xp_scr[8 - s:8 - s + t, :] * cw_ref[CONV_WIDTH - 1 - s:CONV_WIDTH - s, :]
    uc = acc + cb_ref[...]
    uc = uc * jax.nn.sigmoid(uc)
    uc_scr[...] = uc
    ucb = uc.astype(bf16)
    q_scr[...] = jnp.dot(ucb, wq_ref[0].astype(bf16), preferred_element_type=f32).astype(bf16)
    k = jnp.dot(ucb, wk_ref[0].astype(bf16), preferred_element_type=f32) * (dm ** -0.5)
    kt_scr[...] = k.T

    ct_scr[...] = jnp.zeros(ct_scr.shape, f32)
    m_scr[...] = jnp.zeros(m_scr.shape, f32)

    li_ = lax.broadcasted_iota(jnp.int32, (L, L), 0)
    si_ = lax.broadcasted_iota(jnp.int32, (L, L), 1)
    causal = si_ <= li_
    diag = si_ == li_
    ones_v = jnp.ones((L, dm), f32)
    ones_sq = jnp.ones((dm, dm), bf16)

    log_f = _log_sigmoid(gf_ref[0, 0] + bf_ref[h])
    upper = jnp.where(li_ <= si_, 1.0, 0.0).astype(bf16)
    b_scr[...] = jnp.dot(jnp.concatenate(_split3(log_f), axis=1), jnp.concatenate([upper] * 3, axis=0),
                         preferred_element_type=f32)
    li_scr[...] = gi_ref[0, 0] + bi_ref[h]

    def chunk(c, ct, m_prev):
        r0 = pl.multiple_of(c * L, L)
        qc = q_scr[pl.ds(r0, L), :]
        ktc = kt_scr[:, pl.ds(r0, L)]
        vaug = jnp.concatenate([v_ref[0, pl.ds(r0, L), :], ones_v], axis=1).astype(bf16)
        log_i = li_scr[pl.ds(c, 1), :]
        b_row = b_scr[pl.ds(c, 1), :]
        b_col = jnp.sum(jnp.where(diag, b_row, 0.0), axis=-1, keepdims=True)
        g_sum = b_row[:, L - 1:L]
        dmat = jnp.where(causal, (b_col - b_row) + log_i, NEG)
        m_loc = jnp.max(dmat, axis=-1, keepdims=True)
        p = jnp.dot(qc, ktc.astype(bf16), preferred_element_type=f32) * jnp.exp(dmat - m_loc)
        intra = jnp.dot(p.astype(bf16), vaug, preferred_element_type=f32)
        m_inter = b_col + m_prev
        m_out = jnp.maximum(m_inter, m_loc)
        xo = (jnp.exp(m_inter - m_out) * jnp.dot(qc, ct.astype(bf16), preferred_element_type=f32)
              + jnp.exp(m_loc - m_out) * intra)
        num = xo[:, 0:dm]
        den = xo[:, dm:2 * dm]
        hh = num / jnp.maximum(jnp.abs(den), jnp.exp(-m_out))
        hh = hh * jax.nn.sigmoid(op_ref[0, pl.ds(r0, L), :])
        ssq = jnp.dot((hh * hh).astype(bf16), ones_sq, preferred_element_type=f32)
        hh = hh * lax.rsqrt(ssq * (1.0 / dm) + NORM_EPS) * ng_ref[0]
        y_ref[0, pl.ds(r0, L), :] = hh + sk_ref[...] * uc_scr[pl.ds(r0, L), :]

        w_end = (g_sum - b_row) + log_i
        m_new = jnp.maximum(g_sum + m_prev, jnp.max(w_end, axis=-1, keepdims=True))
        decay = jnp.exp(g_sum + m_prev - m_new)
        w = jnp.exp(w_end - m_new)
        return decay * ct + jnp.dot((ktc * w).astype(bf16), vaug, preferred_element_type=f32), m_new

    def chunk_group(i, carry):
        ct, m_prev = ct_scr[...], m_scr[0:1, 0:1]
        for j in range(MLSTM_UNROLL):
            ct, m_prev = chunk(i * MLSTM_UNROLL + j, ct, m_prev)
        ct_scr[...] = ct
        m_scr[...] = jnp.broadcast_to(m_prev, m_scr.shape)
        return carry

    lax.fori_loop(0, t // (L * MLSTM_UNROLL), chunk_group, 0)


def _mlstm(b_i, b_f, u3, v3, op3, gi4, gf4, cw, cb2, wq, wk, ng3, sk2):
    b, t, _ = u3.shape
    dm = MLSTM_HEAD_DIM
    seq = pl.BlockSpec((1, t, dm), lambda bi, hi: (bi, 0, hi))
    gate = pl.BlockSpec((1, 1, t // MLSTM_L, MLSTM_L), lambda bi, hi: (bi, hi, 0, 0))
    smem = pl.BlockSpec(memory_space=pltpu.SMEM)
    return pl.pallas_call(
        _mlstm_kernel,
        grid=(b, MLSTM_HEADS),
        in_specs=[
            smem, smem, seq, seq, seq, gate, gate,
            pl.BlockSpec((CONV_WIDTH, dm), lambda bi, hi: (0, hi)),
            pl.BlockSpec((1, dm), lambda bi, hi: (0, hi)),
            pl.BlockSpec((1, dm, dm), lambda bi, hi: (hi, 0, 0)),
            pl.BlockSpec((1, dm, dm), lambda bi, hi: (hi, 0, 0)),
            pl.BlockSpec((1, 1, dm), lambda bi, hi: (hi, 0, 0)),
            pl.BlockSpec((1, dm), lambda bi, hi: (0, hi)),
        ],
        out_specs=seq,
        out_shape=jax.ShapeDtypeStruct((b, t, MLSTM_WIDTH), f32),
        scratch_shapes=[
            pltpu.VMEM((t, dm), f32),
            pltpu.VMEM((t, dm), bf16),
            pltpu.VMEM((dm, t), f32),
            pltpu.VMEM((dm, 2 * dm), f32),
            pltpu.VMEM((8, LANES), f32),
            pltpu.VMEM((t // MLSTM_L, MLSTM_L), f32),
            pltpu.VMEM((t // MLSTM_L, MLSTM_L), f32),
            pltpu.VMEM((t + 8, dm), f32),
        ],
        compiler_params=_cparams(("parallel", "parallel")),
        name="mlstm",
    )(b_i, b_f, u3, v3, op3, gi4, gf4, cw, cb2, wq, wk, ng3, sk2)


def _ffn_kernel(x_ref, ya_ref, yb_ref, wo_ref, g2_ref, wg_ref, wu_ref, wd_ref, o_ref, act_scr):
    x1 = (x_ref[...]
          + jnp.dot(ya_ref[...].astype(bf16), wo_ref[0:NSA_WIDTH, :], preferred_element_type=f32)
          + jnp.dot(yb_ref[...].astype(bf16), wo_ref[NSA_WIDTH:NSA_WIDTH + MLSTM_WIDTH, :],
                    preferred_element_type=f32))
    h2 = x1 * lax.rsqrt(jnp.mean(x1 * x1, axis=-1, keepdims=True) + NORM_EPS) * g2_ref[...]
    h2b = h2.astype(bf16)
    for c in range(D_FF // FF_CHUNK):
        cols = slice(c * FF_CHUNK, (c + 1) * FF_CHUNK)
        gt = jnp.dot(h2b, wg_ref[:, cols], preferred_element_type=f32)
        up = jnp.dot(h2b, wu_ref[:, cols], preferred_element_type=f32)
        act_scr[:, cols] = (gt * jax.nn.sigmoid(gt) * up).astype(bf16)
    o_ref[...] = x1 + jnp.dot(act_scr[...], wd_ref[...], preferred_element_type=f32)


def _ffn(x2, ya, yb, wo, g2, wg, wu, wd):
    n = x2.shape[0]
    tm = TM_FFN

    def const(shape):
        return pl.BlockSpec(shape, lambda i: (0, 0), pipeline_mode=pl.Buffered(1))

    return pl.pallas_call(
        _ffn_kernel,
        grid=(n // tm,),
        in_specs=[
            pl.BlockSpec((tm, D_MODEL), lambda i: (i, 0)),
            pl.BlockSpec((tm, NSA_WIDTH), lambda i: (i, 0)),
            pl.BlockSpec((tm, MLSTM_WIDTH), lambda i: (i, 0)),
            const(wo.shape), const(g2.shape), const(wg.shape), const(wu.shape), const(wd.shape),
        ],
        out_specs=pl.BlockSpec((tm, D_MODEL), lambda i: (i, 0)),
        out_shape=jax.ShapeDtypeStruct((n, D_MODEL), f32),
        scratch_shapes=[pltpu.VMEM((tm, D_FF), bf16)],
        compiler_params=_cparams(("parallel",)),
        name="ffn",
    )(x2, ya, yb, wo, g2, wg, wu, wd)


def _overlap_t(ncp):
    nsel = ncp // (SEL_BLOCK // CMP_STRIDE)
    cs = np.arange(ncp) * CMP_STRIDE
    ss = np.arange(nsel) * SEL_BLOCK
    ov = ((cs[None, :] < ss[:, None] + SEL_BLOCK) & (cs[None, :] + CMP_BLOCK > ss[:, None])).astype(np.float32)
    ov[:, ncp - 1] = 0.0
    out = np.zeros((LANES, ncp), np.float32)
    out[:nsel] = ov
    return jnp.asarray(np.tile(out, (1, 3)), dtype=bf16)


def _layer(x, norm1_g, w_in, q_g, kc_g, ks_g, kw_g, cmp_pos, w_ck1, w_ck2, w_cv1, w_cv2, conv_w, conv_b,
           w_mq, w_mk, b_i, b_f, mlstm_norm_g, mlstm_skip, w_out, norm2_g, w_gate, w_up, w_down):
    b, t, d = x.shape
    n = b * t
    x2 = x.reshape(n, d)

    o_gate = NSA_WIDTH + 6 * KV_WIDTH
    o_u = o_gate + 3 * NSA_HEADS
    o_if = o_u + 3 * MLSTM_WIDTH
    w_perm = jnp.concatenate([
        w_in[:, :o_gate], w_in[:, o_u:o_if], w_in[:, o_gate:o_u], w_in[:, o_if:],
        jnp.zeros((d, LANES - 3 * NSA_HEADS - 2 * MLSTM_HEADS), w_in.dtype)], axis=1).astype(bf16)
    w_gates_t = jnp.concatenate([w_in[:, o_if:].T, jnp.zeros((GATE_ROWS - 2 * MLSTM_HEADS, d), w_in.dtype)],
                                axis=0).astype(bf16)
    kvc2, u2, vm2, op2, sm2, gates_t, ks, vs, kw, vw, q3 = _inproj(
        x2, norm1_g.reshape(1, d), w_perm, w_gates_t, jnp.tile(ks_g, 2).reshape(1, LANES),
        jnp.tile(kw_g, 2).reshape(1, LANES), jnp.tile(q_g, NSA_HEADS).reshape(1, NSA_WIDTH), b, t)

    kv3 = kvc2.reshape(b, t, 2 * KV_WIDTH)
    nseg = t // CMP_STRIDE
    pos8 = jnp.broadcast_to(cmp_pos.reshape(1, CMP_BLOCK * NSA_HEAD_DIM), (8, CMP_BLOCK * NSA_HEAD_DIM))
    wk1s, wk2s = _compress_weights(w_ck1, w_ck2)
    wv1s, wv2s = _compress_weights(w_cv1, w_cv2)
    kcd, vcl = _compress(kv3, w_ck1, wk1s, wk2s, w_cv1, wv1s, wv2s, pos8, jnp.tile(kc_g, 2).reshape(1, LANES))
    slopes = jnp.exp2(-8.0 * (jnp.arange(NSA_HEADS, dtype=f32) + 1.0) / NSA_HEADS)
    score_cap = (NSA_HEAD_DIM ** 0.5 * math.log2(math.e)) * jnp.max(jnp.abs(q_g)) * jnp.max(jnp.abs(ks_g))
    bounded = (score_cap <= SCORE_BOUND).astype(jnp.int32).reshape(1)
    y_nsa = _nsa(slopes, bounded, q3, sm2.reshape(b, t, LANES), kcd, vcl, _overlap_t(nseg), ks, vs, kw, vw)

    gi4 = gates_t[0:MLSTM_HEADS].reshape(MLSTM_HEADS, b, t).transpose(1, 0, 2)
    gf4 = gates_t[MLSTM_HEADS:2 * MLSTM_HEADS].reshape(MLSTM_HEADS, b, t).transpose(1, 0, 2)
    y_mem = _mlstm(b_i, b_f, u2.reshape(b, t, MLSTM_WIDTH), vm2.reshape(b, t, MLSTM_WIDTH),
                   op2.reshape(b, t, MLSTM_WIDTH), gi4.reshape(b, MLSTM_HEADS, t // MLSTM_L, MLSTM_L),
                   gf4.reshape(b, MLSTM_HEADS, t // MLSTM_L, MLSTM_L), conv_w, conv_b.reshape(1, MLSTM_WIDTH),
                   w_mq, w_mk,
                   mlstm_norm_g.reshape(MLSTM_HEADS, 1, MLSTM_HEAD_DIM), mlstm_skip.reshape(1, MLSTM_WIDTH))

    out = _ffn(x2, y_nsa.reshape(n, NSA_WIDTH), y_mem.reshape(n, MLSTM_WIDTH), w_out.astype(bf16),
               norm2_g.reshape(1, d), w_gate.astype(bf16), w_up.astype(bf16), w_down.astype(bf16))
    return out.reshape(b, t, d)


def kernel(x, norm1_g, w_in, q_norm_g, kc_norm_g, ks_norm_g, kw_norm_g, cmp_pos, w_ck1, w_ck2, w_cv1, w_cv2,
           conv_w, conv_b, w_mq, w_mk, b_i, b_f, mlstm_norm_g, mlstm_skip, w_out, norm2_g, w_gate, w_up, w_down):
    depth = norm1_g.shape[0]
    for l in range(depth):
        x = _layer(x, norm1_g[l], w_in[l], q_norm_g[l], kc_norm_g[l], ks_norm_g[l], kw_norm_g[l], cmp_pos[l],
                   w_ck1[l], w_ck2[l], w_cv1[l], w_cv2[l], conv_w[l], conv_b[l], w_mq[l], w_mk[l], b_i[l], b_f[l],
                   mlstm_norm_g[l], mlstm_skip[l], w_out[l], norm2_g[l], w_gate[l], w_up[l], w_down[l])
    return x
```

```python
import functools
import math

import numpy as np
import jax
import jax.numpy as jnp
from jax import lax
from jax.experimental import pallas as pl
from jax.experimental.pallas import tpu as pltpu

f32 = jnp.float32
bf16 = jnp.bfloat16

D_MODEL = 1024
NSA_HEADS = 8
NSA_KV_HEADS = 2
NSA_HEAD_DIM = 64
NSA_GROUP = NSA_HEADS // NSA_KV_HEADS
CMP_BLOCK = 32
CMP_STRIDE = 16
CMP_HIDDEN = 256
SEL_BLOCK = 64
SEL_TOPN = 16
WINDOW = 512
MLSTM_HEADS = 4
MLSTM_HEAD_DIM = 128
CONV_WIDTH = 4
NSA_WIDTH = NSA_HEADS * NSA_HEAD_DIM
MLSTM_WIDTH = MLSTM_HEADS * MLSTM_HEAD_DIM
KV_WIDTH = NSA_KV_HEADS * NSA_HEAD_DIM
D_FF = -(-8 * D_MODEL // (3 * 256)) * 256
NORM_EPS = 1e-6
NEG = -1e30
FORCE = 1e9
SCORE_BOUND = 100.0

LANES = 128
HALF = 64
VMEM_LIMIT = 56 * 1024 * 1024

TM_PROJ = 512
TQ = 256
TK = 512
MLSTM_L = 256
MLSTM_UNROLL = 4
TM_FFN = 512
FF_CHUNK = 256

HI = lax.Precision.HIGHEST
NT = (((1,), (1,)), ((), ()))


def _cparams(sem):
    return pltpu.CompilerParams(dimension_semantics=sem, vmem_limit_bytes=VMEM_LIMIT)


def _split3(x):
    x1 = x.astype(bf16)
    r1 = x - x1.astype(f32)
    x2 = r1.astype(bf16)
    x3 = (r1 - x2.astype(f32)).astype(bf16)
    return x1, x2, x3


COL_Q = 0
COL_KV = COL_Q + NSA_WIDTH
COL_U = COL_KV + 6 * KV_WIDTH
COL_VM = COL_U + MLSTM_WIDTH
COL_OP = COL_VM + MLSTM_WIDTH
COL_SM = COL_OP + MLSTM_WIDTH
PROJ_WIDTH = COL_SM + LANES
GATE_ROWS = 16


def _pair_norm(x, gain, lo):
    sq = x * x
    s_lo = jnp.sum(jnp.where(lo, sq, 0.0), axis=-1, keepdims=True)
    s_hi = jnp.sum(jnp.where(lo, 0.0, sq), axis=-1, keepdims=True)
    inv = jnp.where(lo, lax.rsqrt(s_lo / HALF + NORM_EPS), lax.rsqrt(s_hi / HALF + NORM_EPS))
    return x * inv * gain


def _inproj_kernel(x_ref, g_ref, w_ref, wgt_ref, ksg_ref, kwg_ref, qg_ref,
                   kvc_ref, u_ref, vm_ref, op_ref, sm_ref, gt_ref, ks_ref, vs_ref, kw_ref, vw_ref, q3_ref,
                   *, tiles_per_seq):
    tm = x_ref.shape[0]
    x = x_ref[...]
    h = x * lax.rsqrt(jnp.mean(x * x, axis=-1, keepdims=True) + NORM_EPS) * g_ref[...]
    hb = h.astype(bf16)

    def proj(col, width):
        return jnp.dot(hb, w_ref[:, col:col + width], preferred_element_type=f32)

    lane = lax.broadcasted_iota(jnp.int32, (tm, LANES), 1)
    lo = lane < HALF
    pos = (pl.program_id(0) % tiles_per_seq) * tm + lax.broadcasted_iota(jnp.int32, (tm, LANES), 0)

    q = proj(COL_Q, NSA_WIDTH)
    scale = NSA_HEAD_DIM ** -0.5 * math.log2(math.e)
    for c in range(NSA_HEADS // 2):
        pair = _pair_norm(q[:, c * LANES:(c + 1) * LANES], qg_ref[:, c * LANES:(c + 1) * LANES], lo) * scale
        for par, xq in enumerate((pair, pltpu.roll(pair, HALF, 1))):
            hi = jnp.where(lo, xq, 0.0).astype(bf16).astype(f32)
            res = jnp.where(lo, xq - hi, 0.0)
            head = 2 * c + par
            q3_ref[0, head // NSA_GROUP, head % NSA_GROUP] = jnp.concatenate(
                [hi + pltpu.roll(res, HALF, 1), hi], axis=1).astype(bf16)

    kv = proj(COL_KV, 6 * KV_WIDTH)
    kvc_ref[...] = kv[:, 0:2 * KV_WIDTH]
    code = jnp.where(lax.shift_right_logical(pos, 6) == (lane & (HALF - 1)), 1.0, 0.0)

    def put(xk, o_ref, fill):
        o_ref[0, 0] = jnp.where(lo, xk, fill).astype(bf16)
        o_ref[0, 1] = jnp.where(lo, pltpu.roll(xk, HALF, 1), fill).astype(bf16)

    put(_pair_norm(kv[:, 2 * KV_WIDTH:3 * KV_WIDTH], ksg_ref[...], lo), ks_ref, code)
    put(kv[:, 3 * KV_WIDTH:4 * KV_WIDTH], vs_ref, 1.0)
    put(_pair_norm(kv[:, 4 * KV_WIDTH:5 * KV_WIDTH], kwg_ref[...], lo), kw_ref, 0.0)
    put(kv[:, 5 * KV_WIDTH:6 * KV_WIDTH], vw_ref, 1.0)

    u_ref[...] = proj(COL_U, MLSTM_WIDTH)
    vm_ref[...] = proj(COL_VM, MLSTM_WIDTH)
    op_ref[...] = proj(COL_OP, MLSTM_WIDTH)
    sm_ref[...] = proj(COL_SM, LANES)
    gt_ref[...] = lax.dot_general(wgt_ref[...], hb, NT, preferred_element_type=f32)


def _inproj(x2, g1, w_perm, w_gates_t, ksg2, kwg2, qg2, b, t):
    n = x2.shape[0]
    tm = min(TM_PROJ, t)
    nt = t // tm

    def rows(width):
        return pl.BlockSpec((tm, width), lambda i: (i, 0))

    def const(shape):
        return pl.BlockSpec(shape, lambda i: (0, 0))

    kv_spec = pl.BlockSpec((1, NSA_KV_HEADS, tm, LANES), lambda i: (i // nt, 0, i % nt, 0))
    kv_shape = jax.ShapeDtypeStruct((b, NSA_KV_HEADS, t, LANES), bf16)
    return pl.pallas_call(
        functools.partial(_inproj_kernel, tiles_per_seq=nt),
        grid=(n // tm,),
        in_specs=[rows(D_MODEL), const((1, D_MODEL)), const((D_MODEL, PROJ_WIDTH)), const((GATE_ROWS, D_MODEL)),
                  const((1, LANES)), const((1, LANES)), const((1, NSA_WIDTH))],
        out_specs=[rows(2 * KV_WIDTH), rows(MLSTM_WIDTH), rows(MLSTM_WIDTH), rows(MLSTM_WIDTH), rows(LANES),
                   pl.BlockSpec((GATE_ROWS, tm), lambda i: (0, i)),
                   kv_spec, kv_spec, kv_spec, kv_spec,
                   pl.BlockSpec((1, NSA_KV_HEADS, NSA_GROUP, tm, 2 * LANES),
                                lambda i: (i // nt, 0, 0, i % nt, 0))],
        out_shape=[jax.ShapeDtypeStruct((n, 2 * KV_WIDTH), f32), jax.ShapeDtypeStruct((n, MLSTM_WIDTH), f32),
                   jax.ShapeDtypeStruct((n, MLSTM_WIDTH), f32), jax.ShapeDtypeStruct((n, MLSTM_WIDTH), f32),
                   jax.ShapeDtypeStruct((n, LANES), f32), jax.ShapeDtypeStruct((GATE_ROWS, n), f32),
                   kv_shape, kv_shape, kv_shape, kv_shape,
                   jax.ShapeDtypeStruct((b, NSA_KV_HEADS, NSA_GROUP, t, 2 * LANES), bf16)],
        compiler_params=_cparams(("parallel",)),
        name="inproj",
    )(x2, g1, w_perm, w_gates_t, ksg2, kwg2, qg2)


def _gelu_tanh(x):
    return 0.5 * x * (1.0 + jnp.tanh(math.sqrt(2.0 / math.pi) * (x + 0.044715 * (x * x * x))))


def _hi_lo(x):
    hi = x.astype(bf16)
    return hi, (x - hi.astype(f32)).astype(bf16)


def _compress_kernel(kc_ref, vc_ref, wk1_ref, wk1s_ref, wk2s_ref, wv1_ref, wv1s_ref, wv2s_ref, pos_ref, kcg_ref,
                     kcd_ref, vcl_ref):
    nseg = kc_ref.shape[1] // CMP_STRIDE
    half = NSA_KV_HEADS * CMP_HIDDEN

    def branch(x_ref, w1_ref, w1s_ref, w2s_ref):
        terms = []
        for r in range(CMP_STRIDE):
            hi, lo = _hi_lo(x_ref[0, pl.ds(r, nseg, stride=CMP_STRIDE), :])
            terms += [hi, lo, hi]
        acc = jnp.dot(jnp.concatenate(terms, axis=1), w1s_ref[...], preferred_element_type=f32)
        pos_term = jnp.dot(pos_ref[...], w1_ref[...], precision=HI, preferred_element_type=f32)[0:1]
        hid = (acc[:, 0:half] + pltpu.roll(acc[:, half:2 * half], nseg - 1, 0)
               + jnp.concatenate([pos_term] * NSA_KV_HEADS, axis=1))
        hi, lo = _hi_lo(_gelu_tanh(hid))
        return jnp.dot(jnp.concatenate([hi, lo, hi], axis=1), w2s_ref[...], preferred_element_type=f32)

    kc2 = branch(kc_ref, wk1_ref, wk1s_ref, wk2s_ref)
    vc2 = branch(vc_ref, wv1_ref, wv1s_ref, wv2s_ref)
    lane = lax.broadcasted_iota(jnp.int32, (nseg, LANES), 1)
    for g in range(NSA_KV_HEADS):
        kc = kc2[:, g * LANES:(g + 1) * LANES]
        kc = kc * lax.rsqrt(jnp.mean(kc * kc, axis=-1, keepdims=True) + NORM_EPS) * kcg_ref[...]
        k_hi = kc.astype(bf16)
        k_lo = jnp.where(lane < HALF, kc - k_hi.astype(f32), 0.0).astype(bf16)
        kcd_ref[0, g] = jnp.concatenate([k_hi, k_lo], axis=1)
        vcl_ref[0, g] = jnp.where(lane < HALF, vc2[:, g * LANES:(g + 1) * LANES], 0.0).astype(bf16)


def _compress_weights(w1, w2):
    w1r = w1.reshape(2, CMP_STRIDE, NSA_HEAD_DIM, CMP_HIDDEN)
    z = jnp.zeros_like(w1r[0])

    def two_groups(w):
        return jnp.concatenate([jnp.concatenate([w, z], axis=2), jnp.concatenate([z, w], axis=2)], axis=1)

    wr = jnp.concatenate([two_groups(w1r[0]), two_groups(w1r[1])], axis=2)
    hi, lo = _hi_lo(wr)
    w1s = jnp.concatenate([hi, hi, lo], axis=1).reshape(CMP_STRIDE * 3 * LANES, 2 * NSA_KV_HEADS * CMP_HIDDEN)
    z2 = jnp.zeros_like(w2)
    w2d = jnp.concatenate([jnp.concatenate([w2, w2, z2, z2], axis=1),
                           jnp.concatenate([z2, z2, w2, w2], axis=1)], axis=0)
    hi2, lo2 = _hi_lo(w2d)
    return w1s, jnp.concatenate([hi2, hi2, lo2], axis=0)


def _compress(kv3, wk1, wk1s, wk2s, wv1, wv1s, wv2s, pos8, kcg2):
    b, t, _ = kv3.shape
    nseg = t // CMP_STRIDE

    def const(shape):
        return pl.BlockSpec(shape, lambda bi: (0,) * len(shape), pipeline_mode=pl.Buffered(1))

    return pl.pallas_call(
        _compress_kernel,
        grid=(b,),
        in_specs=[pl.BlockSpec((1, t, KV_WIDTH), lambda bi: (bi, 0, 0)),
                  pl.BlockSpec((1, t, KV_WIDTH), lambda bi: (bi, 0, 1)),
                  const(wk1.shape), const(wk1s.shape), const(wk2s.shape),
                  const(wv1.shape), const(wv1s.shape), const(wv2s.shape),
                  const(pos8.shape), const(kcg2.shape)],
        out_specs=[pl.BlockSpec((1, NSA_KV_HEADS, nseg, 2 * LANES), lambda bi: (bi, 0, 0, 0)),
                   pl.BlockSpec((1, NSA_KV_HEADS, nseg, LANES), lambda bi: (bi, 0, 0, 0))],
        out_shape=[jax.ShapeDtypeStruct((b, NSA_KV_HEADS, nseg, 2 * LANES), bf16),
                   jax.ShapeDtypeStruct((b, NSA_KV_HEADS, nseg, LANES), bf16)],
        compiler_params=_cparams(("parallel",)),
        name="compress",
    )(kv3, kv3, wk1, wk1s, wk2s, wv1, wv1s, wv2s, pos8, kcg2)


def _nsa_kernel(slopes_ref, q3_ref, sm_ref, gsel_ref, kcd_ref, vcl_ref, ovt_ref,
                ks_ref, vs_ref, kw_ref, vw_ref, y_ref, qa_scr, m_scr, acc_scr, *, bounded):
    g = pl.program_id(1)
    qi = pl.program_id(2)
    q0 = qi * TQ
    ncp = kcd_ref.shape[2]
    nsel = ncp // (SEL_BLOCK // CMP_STRIDE)
    rows = NSA_GROUP * TQ
    log2e = math.log2(math.e)

    lane = lax.broadcasted_iota(jnp.int32, (TQ, LANES), 1)
    lo = lane < HALF
    slopes = [slopes_ref[g * NSA_GROUP + r] * log2e for r in range(NSA_GROUP)]

    def per_head(x, fn):
        return jnp.concatenate([fn(r, x[r * TQ:(r + 1) * TQ]) for r in range(NSA_GROUP)], axis=0)

    q3 = q3_ref[0, 0].reshape(rows, 2 * LANES)
    q1 = q3[:, 0:LANES]

    cidx = lax.broadcasted_iota(jnp.int32, (TQ, ncp), 1)
    tpos_c = q0 + lax.broadcasted_iota(jnp.int32, (TQ, ncp), 0)
    blk_end = cidx * CMP_STRIDE + (CMP_BLOCK - 1)
    valid_c = jnp.logical_and(tpos_c >= blk_end, cidx < ncp - 1)
    krel_c = (CMP_BLOCK - 1 - q0 + CMP_STRIDE * lax.broadcasted_iota(jnp.int32, (1, ncp), 1)).astype(f32)
    s_c = lax.dot_general(q3, kcd_ref[0, 0], NT, preferred_element_type=f32)
    if bounded:
        rowf = lax.broadcasted_iota(jnp.int32, (TQ, LANES), 0).astype(f32)

        def row_term(r, width):
            return jnp.concatenate([slopes[r] * rowf] * (width // LANES), axis=1)

        s_c = per_head(s_c, lambda r, x: jnp.where(valid_c, x + slopes[r] * krel_c - row_term(r, ncp), NEG))
        e_c = jnp.exp2(s_c)
    else:
        s_c = per_head(s_c, lambda r, x: jnp.where(valid_c, x + slopes[r] * krel_c, NEG))
        e_c = jnp.exp2(s_c - jnp.max(s_c, axis=-1, keepdims=True))
    p_c = e_c / jnp.sum(e_c, axis=-1, keepdims=True)
    p_c = per_head(p_c, lambda r, x: jnp.where(valid_c, x, 0.0))
    p_sum = p_c[0:TQ]
    for r in range(1, NSA_GROUP):
        p_sum = p_sum + p_c[r * TQ:(r + 1) * TQ]
    o_cmp = jnp.dot(p_c.astype(bf16), vcl_ref[0, 0], preferred_element_type=f32)

    band = WINDOW + TQ
    kb = pl.multiple_of(jnp.maximum(q0 - WINDOW, 0), TQ)
    s_w = lax.dot_general(q1, kw_ref[0, 0, pl.ds(kb, band), :], NT, preferred_element_type=f32)
    krel_w = (kb - q0 + lax.broadcasted_iota(jnp.int32, (1, band), 1)).astype(f32)
    dist_w = (q0 - kb + lax.broadcasted_iota(jnp.int32, (TQ, band), 0)
              - lax.broadcasted_iota(jnp.int32, (TQ, band), 1))
    valid_w = jnp.logical_and(dist_w >= 0, dist_w < WINDOW)
    if bounded:
        p_w = jnp.exp2(per_head(s_w, lambda r, x: jnp.where(
            valid_w, x + slopes[r] * krel_w - row_term(r, band), NEG)))
    else:
        s_w = per_head(s_w, lambda r, x: jnp.where(valid_w, x + slopes[r] * krel_w, NEG))
        p_w = jnp.exp2(s_w - jnp.max(s_w, axis=-1, keepdims=True))
    acc_w = jnp.dot(p_w.astype(bf16), vw_ref[0, 0, pl.ds(kb, band), :], preferred_element_type=f32)
    o_win = acc_w / pltpu.roll(acc_w, HALF, 1)

    imp = lax.dot_general(ovt_ref[...], jnp.concatenate(_split3(p_sum), axis=1), NT,
                          preferred_element_type=f32)[0:nsel]
    jrow = lax.broadcasted_iota(jnp.int32, (nsel, TQ), 0)
    tcol = q0 + lax.broadcasted_iota(jnp.int32, (nsel, TQ), 1)
    forced = jnp.logical_or(jrow == lax.shift_right_logical(tcol, 6), jrow == 0)
    future = jrow * SEL_BLOCK > tcol
    work = jnp.where(forced, FORCE, jnp.where(future, -FORCE, imp))
    jrow_f = jrow.astype(f32)
    bias_t = jnp.full((nsel, TQ), NEG, f32)
    for _ in range(min(SEL_TOPN, nsel)):
        best = jnp.max(work, axis=0, keepdims=True)
        first = jnp.min(jnp.where(work == best, jrow_f, float(nsel)), axis=0, keepdims=True)
        hit = jrow_f == first
        bias_t = jnp.where(hit, 0.0, bias_t)
        work = jnp.where(hit, -3e38, work)
    if nsel < HALF:
        bias_t = jnp.concatenate([bias_t, jnp.full((HALF - nsel, TQ), NEG, f32)], axis=0)
    bias = jnp.concatenate([bias_t, bias_t], axis=0).T

    bias_b = bias.astype(bf16)
    qa_scr[...] = per_head(q1, lambda r, x: jnp.where(lo, x, bias_b))

    m_scr[...] = jnp.full(m_scr.shape, NEG, f32)
    acc_scr[...] = jnp.zeros(acc_scr.shape, f32)

    def scores(kt, causal, row_shift):
        k0 = pl.multiple_of(kt * TK, TK)
        s = lax.dot_general(qa_scr[...], ks_ref[0, 0, pl.ds(k0, TK), :], NT, preferred_element_type=f32)
        krel = (k0 - q0 + lax.broadcasted_iota(jnp.int32, (1, TK), 1)).astype(f32)

        def alibi(r, x):
            x = x + slopes[r] * krel
            return x if row_shift is None else x - row_shift[r]

        if causal:
            ahead = (k0 - q0 + lax.broadcasted_iota(jnp.int32, (TQ, TK), 1)
                     > lax.broadcasted_iota(jnp.int32, (TQ, TK), 0))
            return per_head(s, lambda r, x: jnp.where(ahead, NEG, alibi(r, x))), k0
        return per_head(s, alibi), k0

    def online_tile(state, kt, causal):
        m_old, acc = state
        s, k0 = scores(kt, causal, None)
        m_new = jnp.maximum(m_old, jnp.max(s, axis=-1, keepdims=True))
        p = jnp.exp2(s - m_new[:, 0:1])
        acc = jnp.exp2(m_old - m_new) * acc + jnp.dot(p.astype(bf16), vs_ref[0, 0, pl.ds(k0, TK), :],
                                                      preferred_element_type=f32)
        return m_new, acc

    def online_tiles(tiles):
        state = (m_scr[...], acc_scr[...])
        for kt, causal in tiles:
            state = online_tile(state, kt, causal)
        m_scr[...] = state[0]
        acc_scr[...] = state[1]

    def bounded_tiles(tiles):
        row_shift = [row_term(r, TK) for r in range(NSA_GROUP)]
        acc = acc_scr[...]
        for kt, causal in tiles:
            s, k0 = scores(kt, causal, row_shift)
            acc = acc + jnp.dot(jnp.exp2(s).astype(bf16), vs_ref[0, 0, pl.ds(k0, TK), :],
                                preferred_element_type=f32)
        acc_scr[...] = acc

    n_full = q0 // TK

    def selected_branch(run_tiles):
        def body(i, carry):
            run_tiles([(2 * i, False), (2 * i + 1, False)])
            return carry

        lax.fori_loop(0, n_full // 2, body, 0)

        @pl.when(n_full % 2 == 1)
        def _():
            run_tiles([(n_full - 1, False), (n_full, True)])

        @pl.when(n_full % 2 == 0)
        def _():
            run_tiles([(n_full, True)])

    selected_branch(bounded_tiles if bounded else online_tiles)

    acc_s = acc_scr[...]
    o_slc = acc_s / pltpu.roll(acc_s, HALF, 1)

    gates = jnp.dot(jnp.concatenate(_hi_lo(jax.nn.sigmoid(sm_ref[0])), axis=1), gsel_ref[0],
                    preferred_element_type=f32)

    def gate(r, branch):
        col = 3 * r + branch
        return gates[:, col * LANES:(col + 1) * LANES]

    def mix(r):
        sl = slice(r * TQ, (r + 1) * TQ)
        return gate(r, 0) * o_cmp[sl] + gate(r, 1) * o_slc[sl] + gate(r, 2) * o_win[sl]

    for c in range(NSA_GROUP // 2):
        y_ref[0, :, c * LANES:(c + 1) * LANES] = jnp.where(lo, mix(2 * c), pltpu.roll(mix(2 * c + 1), HALF, 1))


def _gate_select():
    sel = np.zeros((NSA_KV_HEADS, 2 * LANES, 3 * NSA_GROUP * LANES), np.float32)
    for g in range(NSA_KV_HEADS):
        for c in range(3 * NSA_GROUP):
            src = 3 * NSA_GROUP * g + c
            sel[g, [src, LANES + src], c * LANES:(c + 1) * LANES] = 1.0
    return jnp.asarray(sel, dtype=bf16)


def _nsa(slopes, q3, sm3, kcd, vcl, ovt, ks, vs, kw, vw, *, bounded):
    b, t = q3.shape[0], q3.shape[3]
    gsel = _gate_select()
    gw = NSA_GROUP * NSA_HEAD_DIM
    ncp = kcd.shape[2]

    def kv_spec(rows):
        return pl.BlockSpec((1, 1, rows, LANES), lambda bi, gi, qi: (bi, gi, 0, 0))

    return pl.pallas_call(
        functools.partial(_nsa_kernel, bounded=bounded),
        grid=(b, NSA_KV_HEADS, t // TQ),
        in_specs=[
            pl.BlockSpec(memory_space=pltpu.SMEM),
            pl.BlockSpec((1, 1, NSA_GROUP, TQ, 2 * LANES), lambda bi, gi, qi: (bi, gi, 0, qi, 0)),
            pl.BlockSpec((1, TQ, LANES), lambda bi, gi, qi: (bi, qi, 0)),
            pl.BlockSpec((1,) + gsel.shape[1:], lambda bi, gi, qi: (gi, 0, 0)),
            pl.BlockSpec((1, 1, ncp, 2 * LANES), lambda bi, gi, qi: (bi, gi, 0, 0)), kv_spec(ncp),
            pl.BlockSpec(ovt.shape, lambda bi, gi, qi: (0, 0)),
            kv_spec(t), kv_spec(t), kv_spec(t), kv_spec(t),
        ],
        out_specs=pl.BlockSpec((1, TQ, gw), lambda bi, gi, qi: (bi, qi, gi)),
        out_shape=jax.ShapeDtypeStruct((b, t, NSA_WIDTH), f32),
        scratch_shapes=[
            pltpu.VMEM((NSA_GROUP * TQ, LANES), bf16),
            pltpu.VMEM((NSA_GROUP * TQ, LANES), f32),
            pltpu.VMEM((NSA_GROUP * TQ, LANES), f32),
        ],
        compiler_params=_cparams(("parallel", "parallel", "arbitrary")),
        name="nsa",
    )(slopes, q3, sm3, gsel, kcd, vcl, ovt, ks, vs, kw, vw)


def _log_sigmoid(x):
    return jnp.minimum(x, 0.0) - jnp.log1p(jnp.exp(-jnp.abs(x)))


def _mlstm_kernel(bi_ref, bf_ref, u_ref, v_ref, op_ref, gi_ref, gf_ref, cw_ref, cb_ref, wq_ref, wk_ref,
                  ng_ref, sk_ref, y_ref, uc_scr, q_scr, kt_scr, ct_scr, m_scr, b_scr, li_scr, xp_scr):
    h = pl.program_id(1)
    t = u_ref.shape[1]
    L = MLSTM_L
    dm = MLSTM_HEAD_DIM

    x = u_ref[0]
    xp_scr[0:8, :] = jnp.zeros((8, dm), f32)
    xp_scr[8:, :] = x
    acc = x * cw_ref[CONV_WIDTH - 1:CONV_WIDTH, :]
    for s in range(1, CONV_WIDTH):
        acc = acc + xp_scr[8 - s:8 - s + t, :] * cw_ref[CONV_WIDTH - 1 - s:CONV_WIDTH - s, :]
    uc = acc + cb_ref[...]
    uc = uc * jax.nn.sigmoid(uc)
    uc_scr[...] = uc
    ucb = uc.astype(bf16)
    q_scr[...] = jnp.dot(ucb, wq_ref[0].astype(bf16), preferred_element_type=f32).astype(bf16)
    k = jnp.dot(ucb, wk_ref[0].astype(bf16), preferred_element_type=f32) * (dm ** -0.5)
    kt_scr[...] = k.T

    ct_scr[...] = jnp.zeros(ct_scr.shape, f32)
    m_scr[...] = jnp.zeros(m_scr.shape, f32)

    li_ = lax.broadcasted_iota(jnp.int32, (L, L), 0)
    si_ = lax.broadcasted_iota(jnp.int32, (L, L), 1)
    causal = si_ <= li_
    diag = si_ == li_
    ones_v = jnp.ones((L, dm), f32)
    ones_sq = jnp.ones((dm, dm), bf16)

    log_f = _log_sigmoid(gf_ref[0, 0] + bf_ref[h])
    upper = jnp.where(li_ <= si_, 1.0, 0.0).astype(bf16)
    b_scr[...] = jnp.dot(jnp.concatenate(_split3(log_f), axis=1), jnp.concatenate([upper] * 3, axis=0),
                         preferred_element_type=f32)
    li_scr[...] = gi_ref[0, 0] + bi_ref[h]

    def chunk(c, ct, m_prev):
        r0 = pl.multiple_of(c * L, L)
        qc = q_scr[pl.ds(r0, L), :]
        ktc = kt_scr[:, pl.ds(r0, L)]
        vaug = jnp.concatenate([v_ref[0, pl.ds(r0, L), :], ones_v], axis=1).astype(bf16)
        log_i = li_scr[pl.ds(c, 1), :]
        b_row = b_scr[pl.ds(c, 1), :]
        b_col = jnp.sum(jnp.where(diag, b_row, 0.0), axis=-1, keepdims=True)
        g_sum = b_row[:, L - 1:L]
        dmat = jnp.where(causal, (b_col - b_row) + log_i, NEG)
        m_loc = jnp.max(dmat, axis=-1, keepdims=True)
        p = jnp.dot(qc, ktc.astype(bf16), preferred_element_type=f32) * jnp.exp(dmat - m_loc)
        intra = jnp.dot(p.astype(bf16), vaug, preferred_element_type=f32)
        m_inter = b_col + m_prev
        m_out = jnp.maximum(m_inter, m_loc)
        xo = (jnp.exp(m_inter - m_out) * jnp.dot(qc, ct.astype(bf16), preferred_element_type=f32)
              + jnp.exp(m_loc - m_out) * intra)
        num = xo[:, 0:dm]
        den = xo[:, dm:2 * dm]
        hh = num / jnp.maximum(jnp.abs(den), jnp.exp(-m_out))
        hh = hh * jax.nn.sigmoid(op_ref[0, pl.ds(r0, L), :])
        ssq = jnp.dot((hh * hh).astype(bf16), ones_sq, preferred_element_type=f32)
        hh = hh * lax.rsqrt(ssq * (1.0 / dm) + NORM_EPS) * ng_ref[0]
        y_ref[0, pl.ds(r0, L), :] = hh + sk_ref[...] * uc_scr[pl.ds(r0, L), :]

        w_end = (g_sum - b_row) + log_i
        m_new = jnp.maximum(g_sum + m_prev, jnp.max(w_end, axis=-1, keepdims=True))
        decay = jnp.exp(g_sum + m_prev - m_new)
        w = jnp.exp(w_end - m_new)
        return decay * ct + jnp.dot((ktc * w).astype(bf16), vaug, preferred_element_type=f32), m_new

    def chunk_group(i, carry):
        ct, m_prev = ct_scr[...], m_scr[0:1, 0:1]
        for j in range(MLSTM_UNROLL):
            ct, m_prev = chunk(i * MLSTM_UNROLL + j, ct, m_prev)
        ct_scr[...] = ct
        m_scr[...] = jnp.broadcast_to(m_prev, m_scr.shape)
        return carry

    lax.fori_loop(0, t // (L * MLSTM_UNROLL), chunk_group, 0)


def _mlstm(b_i, b_f, u3, v3, op3, gi4, gf4, cw, cb2, wq, wk, ng3, sk2):
    b, t, _ = u3.shape
    dm = MLSTM_HEAD_DIM
    seq = pl.BlockSpec((1, t, dm), lambda bi, hi: (bi, 0, hi))
    gate = pl.BlockSpec((1, 1, t // MLSTM_L, MLSTM_L), lambda bi, hi: (bi, hi, 0, 0))
    smem = pl.BlockSpec(memory_space=pltpu.SMEM)
    return pl.pallas_call(
        _mlstm_kernel,
        grid=(b, MLSTM_HEADS),
        in_specs=[
            smem, smem, seq, seq, seq, gate, gate,
            pl.BlockSpec((CONV_WIDTH, dm), lambda bi, hi: (0, hi)),
            pl.BlockSpec((1, dm), lambda bi, hi: (0, hi)),
            pl.BlockSpec((1, dm, dm), lambda bi, hi: (hi, 0, 0)),
            pl.BlockSpec((1, dm, dm), lambda bi, hi: (hi, 0, 0)),
            pl.BlockSpec((1, 1, dm), lambda bi, hi: (hi, 0, 0)),
            pl.BlockSpec((1, dm), lambda bi, hi: (0, hi)),
        ],
        out_specs=seq,
        out_shape=jax.ShapeDtypeStruct((b, t, MLSTM_WIDTH), f32),
        scratch_shapes=[
            pltpu.VMEM((t, dm), f32),
            pltpu.VMEM((t, dm), bf16),
            pltpu.VMEM((dm, t), f32),
            pltpu.VMEM((dm, 2 * dm), f32),
            pltpu.VMEM((8, LANES), f32),
            pltpu.VMEM((t // MLSTM_L, MLSTM_L), f32),
            pltpu.VMEM((t // MLSTM_L, MLSTM_L), f32),
            pltpu.VMEM((t + 8, dm), f32),
        ],
        compiler_params=_cparams(("parallel", "parallel")),
        name="mlstm",
    )(b_i, b_f, u3, v3, op3, gi4, gf4, cw, cb2, wq, wk, ng3, sk2)


def _ffn_kernel(x_ref, ya_ref, yb_ref, wo_ref, g2_ref, wg_ref, wu_ref, wd_ref, o_ref, act_scr):
    x1 = (x_ref[...]
          + jnp.dot(ya_ref[...].astype(bf16), wo_ref[0:NSA_WIDTH, :], preferred_element_type=f32)
          + jnp.dot(yb_ref[...].astype(bf16), wo_ref[NSA_WIDTH:NSA_WIDTH + MLSTM_WIDTH, :],
                    preferred_element_type=f32))
    h2 = x1 * lax.rsqrt(jnp.mean(x1 * x1, axis=-1, keepdims=True) + NORM_EPS) * g2_ref[...]
    h2b = h2.astype(bf16)
    for c in range(D_FF // FF_CHUNK):
        cols = slice(c * FF_CHUNK, (c + 1) * FF_CHUNK)
        gt = jnp.dot(h2b, wg_ref[:, cols], preferred_element_type=f32)
        up = jnp.dot(h2b, wu_ref[:, cols], preferred_element_type=f32)
        act_scr[:, cols] = (gt * jax.nn.sigmoid(gt) * up).astype(bf16)
    o_ref[...] = x1 + jnp.dot(act_scr[...], wd_ref[...], preferred_element_type=f32)


def _ffn(x2, ya, yb, wo, g2, wg, wu, wd):
    n = x2.shape[0]
    tm = TM_FFN

    def const(shape):
        return pl.BlockSpec(shape, lambda i: (0, 0), pipeline_mode=pl.Buffered(1))

    return pl.pallas_call(
        _ffn_kernel,
        grid=(n // tm,),
        in_specs=[
            pl.BlockSpec((tm, D_MODEL), lambda i: (i, 0)),
            pl.BlockSpec((tm, NSA_WIDTH), lambda i: (i, 0)),
            pl.BlockSpec((tm, MLSTM_WIDTH), lambda i: (i, 0)),
            const(wo.shape), const(g2.shape), const(wg.shape), const(wu.shape), const(wd.shape),
        ],
        out_specs=pl.BlockSpec((tm, D_MODEL), lambda i: (i, 0)),
        out_shape=jax.ShapeDtypeStruct((n, D_MODEL), f32),
        scratch_shapes=[pltpu.VMEM((tm, D_FF), bf16)],
        compiler_params=_cparams(("parallel",)),
        name="ffn",
    )(x2, ya, yb, wo, g2, wg, wu, wd)


def _overlap_t(ncp):
    nsel = ncp // (SEL_BLOCK // CMP_STRIDE)
    cs = np.arange(ncp) * CMP_STRIDE
    ss = np.arange(nsel) * SEL_BLOCK
    ov = ((cs[None, :] < ss[:, None] + SEL_BLOCK) & (cs[None, :] + CMP_BLOCK > ss[:, None])).astype(np.float32)
    ov[:, ncp - 1] = 0.0
    out = np.zeros((LANES, ncp), np.float32)
    out[:nsel] = ov
    return jnp.asarray(np.tile(out, (1, 3)), dtype=bf16)


def _layer(x, norm1_g, w_in, q_g, kc_g, ks_g, kw_g, cmp_pos, w_ck1, w_ck2, w_cv1, w_cv2, conv_w, conv_b,
           w_mq, w_mk, b_i, b_f, mlstm_norm_g, mlstm_skip, w_out, norm2_g, w_gate, w_up, w_down):
    b, t, d = x.shape
    n = b * t
    x2 = x.reshape(n, d)

    o_gate = NSA_WIDTH + 6 * KV_WIDTH
    o_u = o_gate + 3 * NSA_HEADS
    o_if = o_u + 3 * MLSTM_WIDTH
    w_perm = jnp.concatenate([
        w_in[:, :o_gate], w_in[:, o_u:o_if], w_in[:, o_gate:o_u], w_in[:, o_if:],
        jnp.zeros((d, LANES - 3 * NSA_HEADS - 2 * MLSTM_HEADS), w_in.dtype)], axis=1).astype(bf16)
    w_gates_t = jnp.concatenate([w_in[:, o_if:].T, jnp.zeros((GATE_ROWS - 2 * MLSTM_HEADS, d), w_in.dtype)],
                                axis=0).astype(bf16)
    kvc2, u2, vm2, op2, sm2, gates_t, ks, vs, kw, vw, q3 = _inproj(
        x2, norm1_g.reshape(1, d), w_perm, w_gates_t, jnp.tile(ks_g, 2).reshape(1, LANES),
        jnp.tile(kw_g, 2).reshape(1, LANES), jnp.tile(q_g, NSA_HEADS).reshape(1, NSA_WIDTH), b, t)

    kv3 = kvc2.reshape(b, t, 2 * KV_WIDTH)
    nseg = t // CMP_STRIDE
    pos8 = jnp.broadcast_to(cmp_pos.reshape(1, CMP_BLOCK * NSA_HEAD_DIM), (8, CMP_BLOCK * NSA_HEAD_DIM))
    wk1s, wk2s = _compress_weights(w_ck1, w_ck2)
    wv1s, wv2s = _compress_weights(w_cv1, w_cv2)
    kcd, vcl = _compress(kv3, w_ck1, wk1s, wk2s, w_cv1, wv1s, wv2s, pos8, jnp.tile(kc_g, 2).reshape(1, LANES))
    slopes = jnp.exp2(-8.0 * (jnp.arange(NSA_HEADS, dtype=f32) + 1.0) / NSA_HEADS)
    k_gain = jnp.maximum(jnp.max(jnp.abs(kc_g)), jnp.maximum(jnp.max(jnp.abs(ks_g)), jnp.max(jnp.abs(kw_g))))
    score_cap = (NSA_HEAD_DIM ** 0.5 * math.log2(math.e)) * jnp.max(jnp.abs(q_g)) * k_gain
    nsa_args = (slopes, q3, sm2.reshape(b, t, LANES), kcd, vcl, _overlap_t(nseg), ks, vs, kw, vw)
    y_nsa = lax.cond(score_cap <= SCORE_BOUND, functools.partial(_nsa, bounded=True),
                     functools.partial(_nsa, bounded=False), *nsa_args)

    gi4 = gates_t[0:MLSTM_HEADS].reshape(MLSTM_HEADS, b, t).transpose(1, 0, 2)
    gf4 = gates_t[MLSTM_HEADS:2 * MLSTM_HEADS].reshape(MLSTM_HEADS, b, t).transpose(1, 0, 2)
    y_mem = _mlstm(b_i, b_f, u2.reshape(b, t, MLSTM_WIDTH), vm2.reshape(b, t, MLSTM_WIDTH),
                   op2.reshape(b, t, MLSTM_WIDTH), gi4.reshape(b, MLSTM_HEADS, t // MLSTM_L, MLSTM_L),
                   gf4.reshape(b, MLSTM_HEADS, t // MLSTM_L, MLSTM_L), conv_w, conv_b.reshape(1, MLSTM_WIDTH),
                   w_mq, w_mk,
                   mlstm_norm_g.reshape(MLSTM_HEADS, 1, MLSTM_HEAD_DIM), mlstm_skip.reshape(1, MLSTM_WIDTH))

    out = _ffn(x2, y_nsa.reshape(n, NSA_WIDTH), y_mem.reshape(n, MLSTM_WIDTH), w_out.astype(bf16),
               norm2_g.reshape(1, d), w_gate.astype(bf16), w_up.astype(bf16), w_down.astype(bf16))
    return out.reshape(b, t, d)


def kernel(x, norm1_g, w_in, q_norm_g, kc_norm_g, ks_norm_g, kw_norm_g, cmp_pos, w_ck1, w_ck2, w_cv1, w_cv2,
           conv_w, conv_b, w_mq, w_mk, b_i, b_f, mlstm_norm_g, mlstm_skip, w_out, norm2_g, w_gate, w_up, w_down):
    depth = norm1_g.shape[0]
    for l in range(depth):
        x = _layer(x, norm1_g[l], w_in[l], q_norm_g[l], kc_norm_g[l], ks_norm_g[l], kw_norm_g[l], cmp_pos[l],
                   w_ck1[l], w_ck2[l], w_cv1[l], w_cv2[l], conv_w[l], conv_b[l], w_mq[l], w_mk[l], b_i[l], b_f[l],
                   mlstm_norm_g[l], mlstm_skip[l], w_out[l], norm2_g[l], w_gate[l], w_up[l], w_down[l])
    return x
```

```python
import functools
import math

import numpy as np
import jax
import jax.numpy as jnp
from jax import lax
from jax.experimental import pallas as pl
from jax.experimental.pallas import tpu as pltpu

f32 = jnp.float32
bf16 = jnp.bfloat16

D_MODEL = 1024
NSA_HEADS = 8
NSA_KV_HEADS = 2
NSA_HEAD_DIM = 64
NSA_GROUP = NSA_HEADS // NSA_KV_HEADS
CMP_BLOCK = 32
CMP_STRIDE = 16
CMP_HIDDEN = 256
SEL_BLOCK = 64
SEL_TOPN = 16
WINDOW = 512
MLSTM_HEADS = 4
MLSTM_HEAD_DIM = 128
CONV_WIDTH = 4
NSA_WIDTH = NSA_HEADS * NSA_HEAD_DIM
MLSTM_WIDTH = MLSTM_HEADS * MLSTM_HEAD_DIM
KV_WIDTH = NSA_KV_HEADS * NSA_HEAD_DIM
D_FF = -(-8 * D_MODEL // (3 * 256)) * 256
NORM_EPS = 1e-6
NEG = -1e30
FORCE = 1e9
SCORE_BOUND = 100.0

LANES = 128
HALF = 64
VMEM_LIMIT = 56 * 1024 * 1024

TM_PROJ = 512
TQ = 256
TK = 512
MLSTM_L = 256
MLSTM_UNROLL = 4
TM_FFN = 512
FF_CHUNK = 256

HI = lax.Precision.HIGHEST
NT = (((1,), (1,)), ((), ()))


def _cparams(sem):
    return pltpu.CompilerParams(dimension_semantics=sem, vmem_limit_bytes=VMEM_LIMIT)


def _split3(x):
    x1 = x.astype(bf16)
    r1 = x - x1.astype(f32)
    x2 = r1.astype(bf16)
    x3 = (r1 - x2.astype(f32)).astype(bf16)
    return x1, x2, x3


COL_Q = 0
COL_KV = COL_Q + NSA_WIDTH
COL_U = COL_KV + 6 * KV_WIDTH
COL_VM = COL_U + MLSTM_WIDTH
COL_OP = COL_VM + MLSTM_WIDTH
COL_SM = COL_OP + MLSTM_WIDTH
PROJ_WIDTH = COL_SM + LANES
GATE_ROWS = 16


def _pair_norm(x, gain, lo):
    sq = x * x
    s_lo = jnp.sum(jnp.where(lo, sq, 0.0), axis=-1, keepdims=True)
    s_hi = jnp.sum(jnp.where(lo, 0.0, sq), axis=-1, keepdims=True)
    inv = jnp.where(lo, lax.rsqrt(s_lo / HALF + NORM_EPS), lax.rsqrt(s_hi / HALF + NORM_EPS))
    return x * inv * gain


def _inproj_kernel(x_ref, g_ref, w_ref, wgt_ref, ksg_ref, kwg_ref, qg_ref,
                   kvc_ref, u_ref, vm_ref, op_ref, sm_ref, gt_ref, ks_ref, vs_ref, kw_ref, vw_ref, q3_ref,
                   *, tiles_per_seq):
    tm = x_ref.shape[0]
    x = x_ref[...]
    h = x * lax.rsqrt(jnp.mean(x * x, axis=-1, keepdims=True) + NORM_EPS) * g_ref[...]
    hb = h.astype(bf16)

    def proj(col, width):
        return jnp.dot(hb, w_ref[:, col:col + width], preferred_element_type=f32)

    lane = lax.broadcasted_iota(jnp.int32, (tm, LANES), 1)
    lo = lane < HALF
    pos = (pl.program_id(0) % tiles_per_seq) * tm + lax.broadcasted_iota(jnp.int32, (tm, LANES), 0)

    q = proj(COL_Q, NSA_WIDTH)
    scale = NSA_HEAD_DIM ** -0.5 * math.log2(math.e)
    for c in range(NSA_HEADS // 2):
        pair = _pair_norm(q[:, c * LANES:(c + 1) * LANES], qg_ref[:, c * LANES:(c + 1) * LANES], lo) * scale
        for par, xq in enumerate((pair, pltpu.roll(pair, HALF, 1))):
            hi = jnp.where(lo, xq, 0.0).astype(bf16).astype(f32)
            res = jnp.where(lo, xq - hi, 0.0)
            head = 2 * c + par
            q3_ref[0, head // NSA_GROUP, head % NSA_GROUP] = jnp.concatenate(
                [hi + pltpu.roll(res, HALF, 1), hi], axis=1).astype(bf16)

    kv = proj(COL_KV, 6 * KV_WIDTH)
    kvc_ref[...] = kv[:, 0:2 * KV_WIDTH]
    code = jnp.where(lax.shift_right_logical(pos, 6) == (lane & (HALF - 1)), 1.0, 0.0)

    def put(xk, o_ref, fill):
        o_ref[0, 0] = jnp.where(lo, xk, fill).astype(bf16)
        o_ref[0, 1] = jnp.where(lo, pltpu.roll(xk, HALF, 1), fill).astype(bf16)

    put(_pair_norm(kv[:, 2 * KV_WIDTH:3 * KV_WIDTH], ksg_ref[...], lo), ks_ref, code)
    put(kv[:, 3 * KV_WIDTH:4 * KV_WIDTH], vs_ref, 1.0)
    put(_pair_norm(kv[:, 4 * KV_WIDTH:5 * KV_WIDTH], kwg_ref[...], lo), kw_ref, 0.0)
    put(kv[:, 5 * KV_WIDTH:6 * KV_WIDTH], vw_ref, 1.0)

    u_ref[...] = proj(COL_U, MLSTM_WIDTH)
    vm_ref[...] = proj(COL_VM, MLSTM_WIDTH)
    op_ref[...] = proj(COL_OP, MLSTM_WIDTH)
    sm_ref[...] = proj(COL_SM, LANES)
    gt_ref[...] = lax.dot_general(wgt_ref[...], hb, NT, preferred_element_type=f32)


def _inproj(x2, g1, w_perm, w_gates_t, ksg2, kwg2, qg2, b, t):
    n = x2.shape[0]
    tm = min(TM_PROJ, t)
    nt = t // tm

    def rows(width):
        return pl.BlockSpec((tm, width), lambda i: (i, 0))

    def const(shape):
        return pl.BlockSpec(shape, lambda i: (0, 0))

    kv_spec = pl.BlockSpec((1, NSA_KV_HEADS, tm, LANES), lambda i: (i // nt, 0, i % nt, 0))
    kv_shape = jax.ShapeDtypeStruct((b, NSA_KV_HEADS, t, LANES), bf16)
    return pl.pallas_call(
        functools.partial(_inproj_kernel, tiles_per_seq=nt),
        grid=(n // tm,),
        in_specs=[rows(D_MODEL), const((1, D_MODEL)), const((D_MODEL, PROJ_WIDTH)), const((GATE_ROWS, D_MODEL)),
                  const((1, LANES)), const((1, LANES)), const((1, NSA_WIDTH))],
        out_specs=[rows(2 * KV_WIDTH), rows(MLSTM_WIDTH), rows(MLSTM_WIDTH), rows(MLSTM_WIDTH), rows(LANES),
                   pl.BlockSpec((GATE_ROWS, tm), lambda i: (0, i)),
                   kv_spec, kv_spec, kv_spec, kv_spec,
                   pl.BlockSpec((1, NSA_KV_HEADS, NSA_GROUP, tm, 2 * LANES),
                                lambda i: (i // nt, 0, 0, i % nt, 0))],
        out_shape=[jax.ShapeDtypeStruct((n, 2 * KV_WIDTH), f32), jax.ShapeDtypeStruct((n, MLSTM_WIDTH), f32),
                   jax.ShapeDtypeStruct((n, MLSTM_WIDTH), f32), jax.ShapeDtypeStruct((n, MLSTM_WIDTH), f32),
                   jax.ShapeDtypeStruct((n, LANES), f32), jax.ShapeDtypeStruct((GATE_ROWS, n), f32),
                   kv_shape, kv_shape, kv_shape, kv_shape,
                   jax.ShapeDtypeStruct((b, NSA_KV_HEADS, NSA_GROUP, t, 2 * LANES), bf16)],
        compiler_params=_cparams(("parallel",)),
        name="inproj",
    )(x2, g1, w_perm, w_gates_t, ksg2, kwg2, qg2)


def _gelu_tanh(x):
    return 0.5 * x * (1.0 + jnp.tanh(math.sqrt(2.0 / math.pi) * (x + 0.044715 * (x * x * x))))


def _hi_lo(x):
    hi = x.astype(bf16)
    return hi, (x - hi.astype(f32)).astype(bf16)


def _compress_kernel(kc_ref, vc_ref, wk1_ref, wk1s_ref, wk2s_ref, wv1_ref, wv1s_ref, wv2s_ref, pos_ref, kcg_ref,
                     kcd_ref, vcl_ref):
    nseg = kc_ref.shape[1] // CMP_STRIDE
    half = NSA_KV_HEADS * CMP_HIDDEN

    def branch(x_ref, w1_ref, w1s_ref, w2s_ref):
        terms = []
        for r in range(CMP_STRIDE):
            hi, lo = _hi_lo(x_ref[0, pl.ds(r, nseg, stride=CMP_STRIDE), :])
            terms += [hi, lo, hi]
        acc = jnp.dot(jnp.concatenate(terms, axis=1), w1s_ref[...], preferred_element_type=f32)
        pos_term = jnp.dot(pos_ref[...], w1_ref[...], precision=HI, preferred_element_type=f32)[0:1]
        hid = (acc[:, 0:half] + pltpu.roll(acc[:, half:2 * half], nseg - 1, 0)
               + jnp.concatenate([pos_term] * NSA_KV_HEADS, axis=1))
        hi, lo = _hi_lo(_gelu_tanh(hid))
        return jnp.dot(jnp.concatenate([hi, lo, hi], axis=1), w2s_ref[...], preferred_element_type=f32)

    kc2 = branch(kc_ref, wk1_ref, wk1s_ref, wk2s_ref)
    vc2 = branch(vc_ref, wv1_ref, wv1s_ref, wv2s_ref)
    lane = lax.broadcasted_iota(jnp.int32, (nseg, LANES), 1)
    for g in range(NSA_KV_HEADS):
        kc = kc2[:, g * LANES:(g + 1) * LANES]
        kc = kc * lax.rsqrt(jnp.mean(kc * kc, axis=-1, keepdims=True) + NORM_EPS) * kcg_ref[...]
        k_hi = kc.astype(bf16)
        k_lo = jnp.where(lane < HALF, kc - k_hi.astype(f32), 0.0).astype(bf16)
        kcd_ref[0, g] = jnp.concatenate([k_hi, k_lo], axis=1)
        vcl_ref[0, g] = jnp.where(lane < HALF, vc2[:, g * LANES:(g + 1) * LANES], 0.0).astype(bf16)


def _compress_weights(w1, w2):
    w1r = w1.reshape(2, CMP_STRIDE, NSA_HEAD_DIM, CMP_HIDDEN)
    z = jnp.zeros_like(w1r[0])

    def two_groups(w):
        return jnp.concatenate([jnp.concatenate([w, z], axis=2), jnp.concatenate([z, w], axis=2)], axis=1)

    wr = jnp.concatenate([two_groups(w1r[0]), two_groups(w1r[1])], axis=2)
    hi, lo = _hi_lo(wr)
    w1s = jnp.concatenate([hi, hi, lo], axis=1).reshape(CMP_STRIDE * 3 * LANES, 2 * NSA_KV_HEADS * CMP_HIDDEN)
    z2 = jnp.zeros_like(w2)
    w2d = jnp.concatenate([jnp.concatenate([w2, w2, z2, z2], axis=1),
                           jnp.concatenate([z2, z2, w2, w2], axis=1)], axis=0)
    hi2, lo2 = _hi_lo(w2d)
    return w1s, jnp.concatenate([hi2, hi2, lo2], axis=0)


def _compress(kv3, wk1, wk1s, wk2s, wv1, wv1s, wv2s, pos8, kcg2):
    b, t, _ = kv3.shape
    nseg = t // CMP_STRIDE

    def const(shape):
        return pl.BlockSpec(shape, lambda bi: (0,) * len(shape), pipeline_mode=pl.Buffered(1))

    return pl.pallas_call(
        _compress_kernel,
        grid=(b,),
        in_specs=[pl.BlockSpec((1, t, KV_WIDTH), lambda bi: (bi, 0, 0)),
                  pl.BlockSpec((1, t, KV_WIDTH), lambda bi: (bi, 0, 1)),
                  const(wk1.shape), const(wk1s.shape), const(wk2s.shape),
                  const(wv1.shape), const(wv1s.shape), const(wv2s.shape),
                  const(pos8.shape), const(kcg2.shape)],
        out_specs=[pl.BlockSpec((1, NSA_KV_HEADS, nseg, 2 * LANES), lambda bi: (bi, 0, 0, 0)),
                   pl.BlockSpec((1, NSA_KV_HEADS, nseg, LANES), lambda bi: (bi, 0, 0, 0))],
        out_shape=[jax.ShapeDtypeStruct((b, NSA_KV_HEADS, nseg, 2 * LANES), bf16),
                   jax.ShapeDtypeStruct((b, NSA_KV_HEADS, nseg, LANES), bf16)],
        compiler_params=_cparams(("parallel",)),
        name="compress",
    )(kv3, kv3, wk1, wk1s, wk2s, wv1, wv1s, wv2s, pos8, kcg2)


def _nsa_kernel(slopes_ref, q3_ref, sm_ref, gsel_ref, kcd_ref, vcl_ref, ovt_ref,
                ks_ref, vs_ref, kw_ref, vw_ref, y_ref, qa_scr, m_scr, acc_scr, *, bounded):
    g = pl.program_id(1)
    qi = pl.program_id(2)
    q0 = qi * TQ
    ncp = kcd_ref.shape[2]
    nsel = ncp // (SEL_BLOCK // CMP_STRIDE)
    rows = NSA_GROUP * TQ
    log2e = math.log2(math.e)

    lane = lax.broadcasted_iota(jnp.int32, (TQ, LANES), 1)
    lo = lane < HALF
    slopes = [slopes_ref[g * NSA_GROUP + r] * log2e for r in range(NSA_GROUP)]

    def per_head(x, fn):
        return jnp.concatenate([fn(r, x[r * TQ:(r + 1) * TQ]) for r in range(NSA_GROUP)], axis=0)

    q3 = q3_ref[0, 0].reshape(rows, 2 * LANES)
    q1 = q3[:, 0:LANES]

    cidx = lax.broadcasted_iota(jnp.int32, (TQ, ncp), 1)
    tpos_c = q0 + lax.broadcasted_iota(jnp.int32, (TQ, ncp), 0)
    blk_end = cidx * CMP_STRIDE + (CMP_BLOCK - 1)
    valid_c = jnp.logical_and(tpos_c >= blk_end, cidx < ncp - 1)
    krel_c = (CMP_BLOCK - 1 - q0 + CMP_STRIDE * lax.broadcasted_iota(jnp.int32, (1, ncp), 1)).astype(f32)
    s_c = lax.dot_general(q3, kcd_ref[0, 0], NT, preferred_element_type=f32)
    if bounded:
        rowf = lax.broadcasted_iota(jnp.int32, (TQ, LANES), 0).astype(f32)

        def row_term(r, width):
            return jnp.concatenate([slopes[r] * rowf] * (width // LANES), axis=1)

        s_c = per_head(s_c, lambda r, x: jnp.where(valid_c, x + slopes[r] * krel_c - row_term(r, ncp), NEG))
        e_c = jnp.exp2(s_c)
    else:
        s_c = per_head(s_c, lambda r, x: jnp.where(valid_c, x + slopes[r] * krel_c, NEG))
        e_c = jnp.exp2(s_c - jnp.max(s_c, axis=-1, keepdims=True))
    p_c = e_c / jnp.sum(e_c, axis=-1, keepdims=True)
    p_c = per_head(p_c, lambda r, x: jnp.where(valid_c, x, 0.0))
    p_sum = p_c[0:TQ]
    for r in range(1, NSA_GROUP):
        p_sum = p_sum + p_c[r * TQ:(r + 1) * TQ]
    o_cmp = jnp.dot(p_c.astype(bf16), vcl_ref[0, 0], preferred_element_type=f32)

    band = WINDOW + TQ
    kb = pl.multiple_of(jnp.maximum(q0 - WINDOW, 0), TQ)
    s_w = lax.dot_general(q1, kw_ref[0, 0, pl.ds(kb, band), :], NT, preferred_element_type=f32)
    krel_w = (kb - q0 + lax.broadcasted_iota(jnp.int32, (1, band), 1)).astype(f32)
    dist_w = (q0 - kb + lax.broadcasted_iota(jnp.int32, (TQ, band), 0)
              - lax.broadcasted_iota(jnp.int32, (TQ, band), 1))
    valid_w = jnp.logical_and(dist_w >= 0, dist_w < WINDOW)
    if bounded:
        p_w = jnp.exp2(per_head(s_w, lambda r, x: jnp.where(
            valid_w, x + slopes[r] * krel_w - row_term(r, band), NEG)))
    else:
        s_w = per_head(s_w, lambda r, x: jnp.where(valid_w, x + slopes[r] * krel_w, NEG))
        p_w = jnp.exp2(s_w - jnp.max(s_w, axis=-1, keepdims=True))
    acc_w = jnp.dot(p_w.astype(bf16), vw_ref[0, 0, pl.ds(kb, band), :], preferred_element_type=f32)
    o_win = acc_w / pltpu.roll(acc_w, HALF, 1)

    imp = lax.dot_general(ovt_ref[...], jnp.concatenate(_split3(p_sum), axis=1), NT,
                          preferred_element_type=f32)[0:nsel]
    jrow = lax.broadcasted_iota(jnp.int32, (nsel, TQ), 0)
    tcol = q0 + lax.broadcasted_iota(jnp.int32, (nsel, TQ), 1)
    forced = jnp.logical_or(jrow == lax.shift_right_logical(tcol, 6), jrow == 0)
    future = jrow * SEL_BLOCK > tcol
    work = jnp.where(forced, FORCE, jnp.where(future, -FORCE, imp))
    jrow_f = jrow.astype(f32)
    bias_t = jnp.full((nsel, TQ), NEG, f32)
    for _ in range(min(SEL_TOPN, nsel)):
        best = jnp.max(work, axis=0, keepdims=True)
        first = jnp.min(jnp.where(work == best, jrow_f, float(nsel)), axis=0, keepdims=True)
        hit = jrow_f == first
        bias_t = jnp.where(hit, 0.0, bias_t)
        work = jnp.where(hit, -3e38, work)
    if nsel < HALF:
        bias_t = jnp.concatenate([bias_t, jnp.full((HALF - nsel, TQ), NEG, f32)], axis=0)
    bias = jnp.concatenate([bias_t, bias_t], axis=0).T

    bias_b = bias.astype(bf16)
    qa_scr[...] = per_head(q1, lambda r, x: jnp.where(lo, x, bias_b))

    m_scr[...] = jnp.full(m_scr.shape, NEG, f32)
    acc_scr[...] = jnp.zeros(acc_scr.shape, f32)

    def scores(kt, causal, row_shift):
        k0 = pl.multiple_of(kt * TK, TK)
        s = lax.dot_general(qa_scr[...], ks_ref[0, 0, pl.ds(k0, TK), :], NT, preferred_element_type=f32)
        krel = (k0 - q0 + lax.broadcasted_iota(jnp.int32, (1, TK), 1)).astype(f32)

        def alibi(r, x):
            x = x + slopes[r] * krel
            return x if row_shift is None else x - row_shift[r]

        if causal:
            ahead = (k0 - q0 + lax.broadcasted_iota(jnp.int32, (TQ, TK), 1)
                     > lax.broadcasted_iota(jnp.int32, (TQ, TK), 0))
            return per_head(s, lambda r, x: jnp.where(ahead, NEG, alibi(r, x))), k0
        return per_head(s, alibi), k0

    def online_tile(state, kt, causal):
        m_old, acc = state
        s, k0 = scores(kt, causal, None)
        m_new = jnp.maximum(m_old, jnp.max(s, axis=-1, keepdims=True))
        p = jnp.exp2(s - m_new[:, 0:1])
        acc = jnp.exp2(m_old - m_new) * acc + jnp.dot(p.astype(bf16), vs_ref[0, 0, pl.ds(k0, TK), :],
                                                      preferred_element_type=f32)
        return m_new, acc

    def online_tiles(tiles):
        state = (m_scr[...], acc_scr[...])
        for kt, causal in tiles:
            state = online_tile(state, kt, causal)
        m_scr[...] = state[0]
        acc_scr[...] = state[1]

    def bounded_tiles(tiles):
        row_shift = [row_term(r, TK) for r in range(NSA_GROUP)]
        acc = acc_scr[...]
        for kt, causal in tiles:
            s, k0 = scores(kt, causal, row_shift)
            acc = acc + jnp.dot(jnp.exp2(s).astype(bf16), vs_ref[0, 0, pl.ds(k0, TK), :],
                                preferred_element_type=f32)
        acc_scr[...] = acc

    n_full = q0 // TK

    def selected_branch(run_tiles):
        def body(i, carry):
            run_tiles([(2 * i, False), (2 * i + 1, False)])
            return carry

        lax.fori_loop(0, n_full // 2, body, 0)

        @pl.when(n_full % 2 == 1)
        def _():
            run_tiles([(n_full - 1, False), (n_full, True)])

        @pl.when(n_full % 2 == 0)
        def _():
            run_tiles([(n_full, True)])

    selected_branch(bounded_tiles if bounded else online_tiles)

    acc_s = acc_scr[...]
    o_slc = acc_s / pltpu.roll(acc_s, HALF, 1)

    gates = jnp.dot(jnp.concatenate(_hi_lo(jax.nn.sigmoid(sm_ref[0])), axis=1), gsel_ref[0],
                    preferred_element_type=f32)

    def gate(r, branch):
        col = 3 * r + branch
        return gates[:, col * LANES:(col + 1) * LANES]

    def mix(r):
        sl = slice(r * TQ, (r + 1) * TQ)
        return gate(r, 0) * o_cmp[sl] + gate(r, 1) * o_slc[sl] + gate(r, 2) * o_win[sl]

    for c in range(NSA_GROUP // 2):
        y_ref[0, :, c * LANES:(c + 1) * LANES] = jnp.where(lo, mix(2 * c), pltpu.roll(mix(2 * c + 1), HALF, 1))


def _gate_select():
    sel = np.zeros((NSA_KV_HEADS, 2 * LANES, 3 * NSA_GROUP * LANES), np.float32)
    for g in range(NSA_KV_HEADS):
        for c in range(3 * NSA_GROUP):
            src = 3 * NSA_GROUP * g + c
            sel[g, [src, LANES + src], c * LANES:(c + 1) * LANES] = 1.0
    return jnp.asarray(sel, dtype=bf16)


def _nsa(slopes, q3, sm3, kcd, vcl, ovt, ks, vs, kw, vw, *, bounded):
    b, t = q3.shape[0], q3.shape[3]
    gsel = _gate_select()
    gw = NSA_GROUP * NSA_HEAD_DIM
    ncp = kcd.shape[2]

    def kv_spec(rows):
        return pl.BlockSpec((1, 1, rows, LANES), lambda bi, gi, qi: (bi, gi, 0, 0))

    return pl.pallas_call(
        functools.partial(_nsa_kernel, bounded=bounded),
        grid=(b, NSA_KV_HEADS, t // TQ),
        in_specs=[
            pl.BlockSpec(memory_space=pltpu.SMEM),
            pl.BlockSpec((1, 1, NSA_GROUP, TQ, 2 * LANES), lambda bi, gi, qi: (bi, gi, 0, qi, 0)),
            pl.BlockSpec((1, TQ, LANES), lambda bi, gi, qi: (bi, qi, 0)),
            pl.BlockSpec((1,) + gsel.shape[1:], lambda bi, gi, qi: (gi, 0, 0)),
            pl.BlockSpec((1, 1, ncp, 2 * LANES), lambda bi, gi, qi: (bi, gi, 0, 0)), kv_spec(ncp),
            pl.BlockSpec(ovt.shape, lambda bi, gi, qi: (0, 0)),
            kv_spec(t), kv_spec(t), kv_spec(t), kv_spec(t),
        ],
        out_specs=pl.BlockSpec((1, TQ, gw), lambda bi, gi, qi: (bi, qi, gi)),
        out_shape=jax.ShapeDtypeStruct((b, t, NSA_WIDTH), f32),
        scratch_shapes=[
            pltpu.VMEM((NSA_GROUP * TQ, LANES), bf16),
            pltpu.VMEM((NSA_GROUP * TQ, LANES), f32),
            pltpu.VMEM((NSA_GROUP * TQ, LANES), f32),
        ],
        compiler_params=_cparams(("parallel", "parallel", "arbitrary")),
        name="nsa",
    )(slopes, q3, sm3, gsel, kcd, vcl, ovt, ks, vs, kw, vw)


def _log_sigmoid(x):
    return jnp.minimum(x, 0.0) - jnp.log1p(jnp.exp(-jnp.abs(x)))


def _mlstm_kernel(bi_ref, bf_ref, u_ref, v_ref, op_ref, gi_ref, gf_ref, cw_ref, cb_ref, wq_ref, wk_ref,
                  ng_ref, sk_ref, y_ref, uc_scr, q_scr, kt_scr, ct_scr, m_scr, b_scr, li_scr, xp_scr):
    h = pl.program_id(1)
    t = u_ref.shape[1]
    L = MLSTM_L
    dm = MLSTM_HEAD_DIM

    x = u_ref[0]
    xp_scr[0:8, :] = jnp.zeros((8, dm), f32)
    xp_scr[8:, :] = x
    acc = x * cw_ref[CONV_WIDTH - 1:CONV_WIDTH, :]
    for s in range(1, CONV_WIDTH):
        acc = acc + xp_scr[8 - s:8 - s + t, :] * cw_ref[CONV_WIDTH - 1 - s:CONV_WIDTH - s, :]
    uc = acc + cb_ref[...]
    uc = uc * jax.nn.sigmoid(uc)
    uc_scr[...] = uc
    ucb = uc.astype(bf16)
    q_scr[...] = jnp.dot(ucb, wq_ref[0].astype(bf16), preferred_element_type=f32).astype(bf16)
    k = jnp.dot(ucb, wk_ref[0].astype(bf16), preferred_element_type=f32) * (dm ** -0.5)
    kt_scr[...] = k.T

    ct_scr[...] = jnp.zeros(ct_scr.shape, f32)
    m_scr[...] = jnp.zeros(m_scr.shape, f32)

    li_ = lax.broadcasted_iota(jnp.int32, (L, L), 0)
    si_ = lax.broadcasted_iota(jnp.int32, (L, L), 1)
    causal = si_ <= li_
    diag = si_ == li_
    ones_v = jnp.ones((L, dm), f32)
    ones_sq = jnp.ones((dm, dm), bf16)

    def wide(x):
        return jnp.concatenate([x] * (L // dm), axis=1)

    log_f = _log_sigmoid(gf_ref[0, 0] + bf_ref[h])
    upper = jnp.where(li_ <= si_, 1.0, 0.0).astype(bf16)
    b_scr[...] = jnp.dot(jnp.concatenate(_split3(log_f), axis=1), jnp.concatenate([upper] * 3, axis=0),
                         preferred_element_type=f32)
    li_scr[...] = gi_ref[0, 0] + bi_ref[h]

    def chunk(c, ct, m_prev):
        r0 = pl.multiple_of(c * L, L)
        qc = q_scr[pl.ds(r0, L), :]
        ktc = kt_scr[:, pl.ds(r0, L)]
        vaug = jnp.concatenate([v_ref[0, pl.ds(r0, L), :], ones_v], axis=1).astype(bf16)
        log_i = li_scr[pl.ds(c, 1), :]
        b_row = b_scr[pl.ds(c, 1), :]
        b_col = jnp.broadcast_to(jnp.sum(jnp.where(diag, b_row, 0.0), axis=-1, keepdims=True), (L, dm))
        g_sum = b_row[:, L - 1:L]
        dmat = jnp.where(causal, (wide(b_col) - b_row) + log_i, NEG)
        m_loc = jnp.broadcast_to(jnp.max(dmat, axis=-1, keepdims=True), (L, dm))
        p = jnp.dot(qc, ktc.astype(bf16), preferred_element_type=f32) * jnp.exp(dmat - wide(m_loc))
        intra = jnp.dot(p.astype(bf16), vaug, preferred_element_type=f32)
        m_inter = b_col + m_prev
        m_out = jnp.maximum(m_inter, m_loc)
        xo = (wide(jnp.exp(m_inter - m_out)) * jnp.dot(qc, ct.astype(bf16), preferred_element_type=f32)
              + wide(jnp.exp(m_loc - m_out)) * intra)
        num = xo[:, 0:dm]
        den = xo[:, dm:2 * dm]
        hh = num / jnp.maximum(jnp.abs(den), jnp.exp(-m_out))
        hh = hh * jax.nn.sigmoid(op_ref[0, pl.ds(r0, L), :])
        ssq = jnp.dot((hh * hh).astype(bf16), ones_sq, preferred_element_type=f32)
        hh = hh * lax.rsqrt(ssq * (1.0 / dm) + NORM_EPS) * ng_ref[0]
        y_ref[0, pl.ds(r0, L), :] = hh + sk_ref[...] * uc_scr[pl.ds(r0, L), :]

        w_end = (g_sum - b_row) + log_i
        m_new = jnp.maximum(g_sum + m_prev, jnp.max(w_end, axis=-1, keepdims=True))
        decay = jnp.exp(g_sum + m_prev - m_new)
        w = jnp.exp(w_end - m_new)
        return decay * ct + jnp.dot((ktc * w).astype(bf16), vaug, preferred_element_type=f32), m_new

    def chunk_group(i, carry):
        ct, m_prev = ct_scr[...], m_scr[0:1, 0:1]
        for j in range(MLSTM_UNROLL):
            ct, m_prev = chunk(i * MLSTM_UNROLL + j, ct, m_prev)
        ct_scr[...] = ct
        m_scr[...] = jnp.broadcast_to(m_prev, m_scr.shape)
        return carry

    lax.fori_loop(0, t // (L * MLSTM_UNROLL), chunk_group, 0)


def _mlstm(b_i, b_f, u3, v3, op3, gi4, gf4, cw, cb2, wq, wk, ng3, sk2):
    b, t, _ = u3.shape
    dm = MLSTM_HEAD_DIM
    seq = pl.BlockSpec((1, t, dm), lambda bi, hi: (bi, 0, hi))
    gate = pl.BlockSpec((1, 1, t // MLSTM_L, MLSTM_L), lambda bi, hi: (bi, hi, 0, 0))
    smem = pl.BlockSpec(memory_space=pltpu.SMEM)
    return pl.pallas_call(
        _mlstm_kernel,
        grid=(b, MLSTM_HEADS),
        in_specs=[
            smem, smem, seq, seq, seq, gate, gate,
            pl.BlockSpec((CONV_WIDTH, dm), lambda bi, hi: (0, hi)),
            pl.BlockSpec((1, dm), lambda bi, hi: (0, hi)),
            pl.BlockSpec((1, dm, dm), lambda bi, hi: (hi, 0, 0)),
            pl.BlockSpec((1, dm, dm), lambda bi, hi: (hi, 0, 0)),
            pl.BlockSpec((1, 1, dm), lambda bi, hi: (hi, 0, 0)),
            pl.BlockSpec((1, dm), lambda bi, hi: (0, hi)),
        ],
        out_specs=seq,
        out_shape=jax.ShapeDtypeStruct((b, t, MLSTM_WIDTH), f32),
        scratch_shapes=[
            pltpu.VMEM((t, dm), f32),
            pltpu.VMEM((t, dm), bf16),
            pltpu.VMEM((dm, t), f32),
            pltpu.VMEM((dm, 2 * dm), f32),
            pltpu.VMEM((8, LANES), f32),
            pltpu.VMEM((t // MLSTM_L, MLSTM_L), f32),
            pltpu.VMEM((t // MLSTM_L, MLSTM_L), f32),
            pltpu.VMEM((t + 8, dm), f32),
        ],
        compiler_params=_cparams(("parallel", "parallel")),
        name="mlstm",
    )(b_i, b_f, u3, v3, op3, gi4, gf4, cw, cb2, wq, wk, ng3, sk2)


def _ffn_kernel(x_ref, ya_ref, yb_ref, wo_ref, g2_ref, wg_ref, wu_ref, wd_ref, o_ref, act_scr):
    x1 = (x_ref[...]
          + jnp.dot(ya_ref[...].astype(bf16), wo_ref[0:NSA_WIDTH, :], preferred_element_type=f32)
          + jnp.dot(yb_ref[...].astype(bf16), wo_ref[NSA_WIDTH:NSA_WIDTH + MLSTM_WIDTH, :],
                    preferred_element_type=f32))
    h2 = x1 * lax.rsqrt(jnp.mean(x1 * x1, axis=-1, keepdims=True) + NORM_EPS) * g2_ref[...]
    h2b = h2.astype(bf16)
    for c in range(D_FF // FF_CHUNK):
        cols = slice(c * FF_CHUNK, (c + 1) * FF_CHUNK)
        gt = jnp.dot(h2b, wg_ref[:, cols], preferred_element_type=f32)
        up = jnp.dot(h2b, wu_ref[:, cols], preferred_element_type=f32)
        act_scr[:, cols] = (gt * jax.nn.sigmoid(gt) * up).astype(bf16)
    o_ref[...] = x1 + jnp.dot(act_scr[...], wd_ref[...], preferred_element_type=f32)


def _ffn(x2, ya, yb, wo, g2, wg, wu, wd):
    n = x2.shape[0]
    tm = TM_FFN

    def const(shape):
        return pl.BlockSpec(shape, lambda i: (0, 0), pipeline_mode=pl.Buffered(1))

    return pl.pallas_call(
        _ffn_kernel,
        grid=(n // tm,),
        in_specs=[
            pl.BlockSpec((tm, D_MODEL), lambda i: (i, 0)),
            pl.BlockSpec((tm, NSA_WIDTH), lambda i: (i, 0)),
            pl.BlockSpec((tm, MLSTM_WIDTH), lambda i: (i, 0)),
            const(wo.shape), const(g2.shape), const(wg.shape), const(wu.shape), const(wd.shape),
        ],
        out_specs=pl.BlockSpec((tm, D_MODEL), lambda i: (i, 0)),
        out_shape=jax.ShapeDtypeStruct((n, D_MODEL), f32),
        scratch_shapes=[pltpu.VMEM((tm, D_FF), bf16)],
        compiler_params=_cparams(("parallel",)),
        name="ffn",
    )(x2, ya, yb, wo, g2, wg, wu, wd)


def _overlap_t(ncp):
    nsel = ncp // (SEL_BLOCK // CMP_STRIDE)
    cs = np.arange(ncp) * CMP_STRIDE
    ss = np.arange(nsel) * SEL_BLOCK
    ov = ((cs[None, :] < ss[:, None] + SEL_BLOCK) & (cs[None, :] + CMP_BLOCK > ss[:, None])).astype(np.float32)
    ov[:, ncp - 1] = 0.0
    out = np.zeros((LANES, ncp), np.float32)
    out[:nsel] = ov
    return jnp.asarray(np.tile(out, (1, 3)), dtype=bf16)


def _layer(x, norm1_g, w_in, q_g, kc_g, ks_g, kw_g, cmp_pos, w_ck1, w_ck2, w_cv1, w_cv2, conv_w, conv_b,
           w_mq, w_mk, b_i, b_f, mlstm_norm_g, mlstm_skip, w_out, norm2_g, w_gate, w_up, w_down):
    b, t, d = x.shape
    n = b * t
    x2 = x.reshape(n, d)

    o_gate = NSA_WIDTH + 6 * KV_WIDTH
    o_u = o_gate + 3 * NSA_HEADS
    o_if = o_u + 3 * MLSTM_WIDTH
    w_perm = jnp.concatenate([
        w_in[:, :o_gate], w_in[:, o_u:o_if], w_in[:, o_gate:o_u], w_in[:, o_if:],
        jnp.zeros((d, LANES - 3 * NSA_HEADS - 2 * MLSTM_HEADS), w_in.dtype)], axis=1).astype(bf16)
    w_gates_t = jnp.concatenate([w_in[:, o_if:].T, jnp.zeros((GATE_ROWS - 2 * MLSTM_HEADS, d), w_in.dtype)],
                                axis=0).astype(bf16)
    kvc2, u2, vm2, op2, sm2, gates_t, ks, vs, kw, vw, q3 = _inproj(
        x2, norm1_g.reshape(1, d), w_perm, w_gates_t, jnp.tile(ks_g, 2).reshape(1, LANES),
        jnp.tile(kw_g, 2).reshape(1, LANES), jnp.tile(q_g, NSA_HEADS).reshape(1, NSA_WIDTH), b, t)

    kv3 = kvc2.reshape(b, t, 2 * KV_WIDTH)
    nseg = t // CMP_STRIDE
    pos8 = jnp.broadcast_to(cmp_pos.reshape(1, CMP_BLOCK * NSA_HEAD_DIM), (8, CMP_BLOCK * NSA_HEAD_DIM))
    wk1s, wk2s = _compress_weights(w_ck1, w_ck2)
    wv1s, wv2s = _compress_weights(w_cv1, w_cv2)
    kcd, vcl = _compress(kv3, w_ck1, wk1s, wk2s, w_cv1, wv1s, wv2s, pos8, jnp.tile(kc_g, 2).reshape(1, LANES))
    slopes = jnp.exp2(-8.0 * (jnp.arange(NSA_HEADS, dtype=f32) + 1.0) / NSA_HEADS)
    k_gain = jnp.maximum(jnp.max(jnp.abs(kc_g)), jnp.maximum(jnp.max(jnp.abs(ks_g)), jnp.max(jnp.abs(kw_g))))
    score_cap = (NSA_HEAD_DIM ** 0.5 * math.log2(math.e)) * jnp.max(jnp.abs(q_g)) * k_gain
    nsa_args = (slopes, q3, sm2.reshape(b, t, LANES), kcd, vcl, _overlap_t(nseg), ks, vs, kw, vw)
    y_nsa = lax.cond(score_cap <= SCORE_BOUND, functools.partial(_nsa, bounded=True),
                     functools.partial(_nsa, bounded=False), *nsa_args)

    gi4 = gates_t[0:MLSTM_HEADS].reshape(MLSTM_HEADS, b, t).transpose(1, 0, 2)
    gf4 = gates_t[MLSTM_HEADS:2 * MLSTM_HEADS].reshape(MLSTM_HEADS, b, t).transpose(1, 0, 2)
    y_mem = _mlstm(b_i, b_f, u2.reshape(b, t, MLSTM_WIDTH), vm2.reshape(b, t, MLSTM_WIDTH),
                   op2.reshape(b, t, MLSTM_WIDTH), gi4.reshape(b, MLSTM_HEADS, t // MLSTM_L, MLSTM_L),
                   gf4.reshape(b, MLSTM_HEADS, t // MLSTM_L, MLSTM_L), conv_w, conv_b.reshape(1, MLSTM_WIDTH),
                   w_mq, w_mk,
                   mlstm_norm_g.reshape(MLSTM_HEADS, 1, MLSTM_HEAD_DIM), mlstm_skip.reshape(1, MLSTM_WIDTH))

    out = _ffn(x2, y_nsa.reshape(n, NSA_WIDTH), y_mem.reshape(n, MLSTM_WIDTH), w_out.astype(bf16),
               norm2_g.reshape(1, d), w_gate.astype(bf16), w_up.astype(bf16), w_down.astype(bf16))
    return out.reshape(b, t, d)


def kernel(x, norm1_g, w_in, q_norm_g, kc_norm_g, ks_norm_g, kw_norm_g, cmp_pos, w_ck1, w_ck2, w_cv1, w_cv2,
           conv_w, conv_b, w_mq, w_mk, b_i, b_f, mlstm_norm_g, mlstm_skip, w_out, norm2_g, w_gate, w_up, w_down):
    depth = norm1_g.shape[0]
    for l in range(depth):
        x = _layer(x, norm1_g[l], w_in[l], q_norm_g[l], kc_norm_g[l], ks_norm_g[l], kw_norm_g[l], cmp_pos[l],
                   w_ck1[l], w_ck2[l], w_cv1[l], w_cv2[l], conv_w[l], conv_b[l], w_mq[l], w_mk[l], b_i[l], b_f[l],
                   mlstm_norm_g[l], mlstm_skip[l], w_out[l], norm2_g[l], w_gate[l], w_up[l], w_down[l])
    return x
```

```python
import functools
import math

import numpy as np
import jax
import jax.numpy as jnp
from jax import lax
from jax.experimental import pallas as pl
from jax.experimental.pallas import tpu as pltpu

f32 = jnp.float32
bf16 = jnp.bfloat16

D_MODEL = 1024
NSA_HEADS = 8
NSA_KV_HEADS = 2
NSA_HEAD_DIM = 64
NSA_GROUP = NSA_HEADS // NSA_KV_HEADS
CMP_BLOCK = 32
CMP_STRIDE = 16
CMP_HIDDEN = 256
SEL_BLOCK = 64
SEL_TOPN = 16
WINDOW = 512
MLSTM_HEADS = 4
MLSTM_HEAD_DIM = 128
CONV_WIDTH = 4
NSA_WIDTH = NSA_HEADS * NSA_HEAD_DIM
MLSTM_WIDTH = MLSTM_HEADS * MLSTM_HEAD_DIM
KV_WIDTH = NSA_KV_HEADS * NSA_HEAD_DIM
D_FF = -(-8 * D_MODEL // (3 * 256)) * 256
NORM_EPS = 1e-6
NEG = -1e30
FORCE = 1e9
SCORE_BOUND = 100.0

LANES = 128
HALF = 64
VMEM_LIMIT = 56 * 1024 * 1024

TM_PROJ = 512
TQ = 512
TK = 512
MLSTM_L = 256
MLSTM_UNROLL = 4
TM_FFN = 512
FF_CHUNK = 256

HI = lax.Precision.HIGHEST
NT = (((1,), (1,)), ((), ()))


def _cparams(sem):
    return pltpu.CompilerParams(dimension_semantics=sem, vmem_limit_bytes=VMEM_LIMIT)


def _split3(x):
    x1 = x.astype(bf16)
    r1 = x - x1.astype(f32)
    x2 = r1.astype(bf16)
    x3 = (r1 - x2.astype(f32)).astype(bf16)
    return x1, x2, x3


COL_Q = 0
COL_KV = COL_Q + NSA_WIDTH
COL_U = COL_KV + 6 * KV_WIDTH
COL_VM = COL_U + MLSTM_WIDTH
COL_OP = COL_VM + MLSTM_WIDTH
COL_SM = COL_OP + MLSTM_WIDTH
PROJ_WIDTH = COL_SM + LANES
GATE_ROWS = 16


def _pair_norm(x, gain, lo):
    sq = x * x
    s_lo = jnp.sum(jnp.where(lo, sq, 0.0), axis=-1, keepdims=True)
    s_hi = jnp.sum(jnp.where(lo, 0.0, sq), axis=-1, keepdims=True)
    inv = jnp.where(lo, lax.rsqrt(s_lo / HALF + NORM_EPS), lax.rsqrt(s_hi / HALF + NORM_EPS))
    return x * inv * gain


def _inproj_kernel(x_ref, g_ref, w_ref, wgt_ref, ksg_ref, kwg_ref, qg_ref,
                   kvc_ref, u_ref, vm_ref, op_ref, sm_ref, gt_ref, ks_ref, vs_ref, kw_ref, vw_ref, q3_ref,
                   *, tiles_per_seq):
    tm = x_ref.shape[0]
    x = x_ref[...]
    h = x * lax.rsqrt(jnp.mean(x * x, axis=-1, keepdims=True) + NORM_EPS) * g_ref[...]
    hb = h.astype(bf16)

    def proj(col, width):
        return jnp.dot(hb, w_ref[:, col:col + width], preferred_element_type=f32)

    lane = lax.broadcasted_iota(jnp.int32, (tm, LANES), 1)
    lo = lane < HALF
    pos = (pl.program_id(0) % tiles_per_seq) * tm + lax.broadcasted_iota(jnp.int32, (tm, LANES), 0)

    q = proj(COL_Q, NSA_WIDTH)
    scale = NSA_HEAD_DIM ** -0.5 * math.log2(math.e)
    for c in range(NSA_HEADS // 2):
        pair = _pair_norm(q[:, c * LANES:(c + 1) * LANES], qg_ref[:, c * LANES:(c + 1) * LANES], lo) * scale
        for par, xq in enumerate((pair, pltpu.roll(pair, HALF, 1))):
            hi = jnp.where(lo, xq, 0.0).astype(bf16).astype(f32)
            res = jnp.where(lo, xq - hi, 0.0)
            head = 2 * c + par
            q3_ref[0, head // NSA_GROUP, head % NSA_GROUP] = jnp.concatenate(
                [hi + pltpu.roll(res, HALF, 1), hi], axis=1).astype(bf16)

    kv = proj(COL_KV, 6 * KV_WIDTH)
    kvc_ref[...] = kv[:, 0:2 * KV_WIDTH]
    code = jnp.where(lax.shift_right_logical(pos, 6) == (lane & (HALF - 1)), 1.0, 0.0)

    def put(xk, o_ref, fill):
        o_ref[0, 0] = jnp.where(lo, xk, fill).astype(bf16)
        o_ref[0, 1] = jnp.where(lo, pltpu.roll(xk, HALF, 1), fill).astype(bf16)

    put(_pair_norm(kv[:, 2 * KV_WIDTH:3 * KV_WIDTH], ksg_ref[...], lo), ks_ref, code)
    put(kv[:, 3 * KV_WIDTH:4 * KV_WIDTH], vs_ref, 1.0)
    put(_pair_norm(kv[:, 4 * KV_WIDTH:5 * KV_WIDTH], kwg_ref[...], lo), kw_ref, 0.0)
    put(kv[:, 5 * KV_WIDTH:6 * KV_WIDTH], vw_ref, 1.0)

    u_ref[...] = proj(COL_U, MLSTM_WIDTH)
    vm_ref[...] = proj(COL_VM, MLSTM_WIDTH)
    op_ref[...] = proj(COL_OP, MLSTM_WIDTH)
    sm_ref[...] = proj(COL_SM, LANES)
    gt_ref[...] = lax.dot_general(wgt_ref[...], hb, NT, preferred_element_type=f32)


def _inproj(x2, g1, w_perm, w_gates_t, ksg2, kwg2, qg2, b, t):
    n = x2.shape[0]
    tm = min(TM_PROJ, t)
    nt = t // tm

    def rows(width):
        return pl.BlockSpec((tm, width), lambda i: (i, 0))

    def const(shape):
        return pl.BlockSpec(shape, lambda i: (0, 0))

    kv_spec = pl.BlockSpec((1, NSA_KV_HEADS, tm, LANES), lambda i: (i // nt, 0, i % nt, 0))
    kv_shape = jax.ShapeDtypeStruct((b, NSA_KV_HEADS, t, LANES), bf16)
    return pl.pallas_call(
        functools.partial(_inproj_kernel, tiles_per_seq=nt),
        grid=(n // tm,),
        in_specs=[rows(D_MODEL), const((1, D_MODEL)), const((D_MODEL, PROJ_WIDTH)), const((GATE_ROWS, D_MODEL)),
                  const((1, LANES)), const((1, LANES)), const((1, NSA_WIDTH))],
        out_specs=[rows(2 * KV_WIDTH), rows(MLSTM_WIDTH), rows(MLSTM_WIDTH), rows(MLSTM_WIDTH), rows(LANES),
                   pl.BlockSpec((GATE_ROWS, tm), lambda i: (0, i)),
                   kv_spec, kv_spec, kv_spec, kv_spec,
                   pl.BlockSpec((1, NSA_KV_HEADS, NSA_GROUP, tm, 2 * LANES),
                                lambda i: (i // nt, 0, 0, i % nt, 0))],
        out_shape=[jax.ShapeDtypeStruct((n, 2 * KV_WIDTH), f32), jax.ShapeDtypeStruct((n, MLSTM_WIDTH), f32),
                   jax.ShapeDtypeStruct((n, MLSTM_WIDTH), f32), jax.ShapeDtypeStruct((n, MLSTM_WIDTH), f32),
                   jax.ShapeDtypeStruct((n, LANES), f32), jax.ShapeDtypeStruct((GATE_ROWS, n), f32),
                   kv_shape, kv_shape, kv_shape, kv_shape,
                   jax.ShapeDtypeStruct((b, NSA_KV_HEADS, NSA_GROUP, t, 2 * LANES), bf16)],
        compiler_params=_cparams(("parallel",)),
        name="inproj",
    )(x2, g1, w_perm, w_gates_t, ksg2, kwg2, qg2)


def _gelu_tanh(x):
    return 0.5 * x * (1.0 + jnp.tanh(math.sqrt(2.0 / math.pi) * (x + 0.044715 * (x * x * x))))


def _hi_lo(x):
    hi = x.astype(bf16)
    return hi, (x - hi.astype(f32)).astype(bf16)


def _compress_kernel(kc_ref, vc_ref, wk1_ref, wk1s_ref, wk2s_ref, wv1_ref, wv1s_ref, wv2s_ref, pos_ref, kcg_ref,
                     kcd_ref, vcl_ref):
    nseg = kc_ref.shape[1] // CMP_STRIDE
    half = NSA_KV_HEADS * CMP_HIDDEN

    def branch(x_ref, w1_ref, w1s_ref, w2s_ref):
        terms = []
        for r in range(CMP_STRIDE):
            hi, lo = _hi_lo(x_ref[0, pl.ds(r, nseg, stride=CMP_STRIDE), :])
            terms += [hi, lo, hi]
        acc = jnp.dot(jnp.concatenate(terms, axis=1), w1s_ref[...], preferred_element_type=f32)
        pos_term = jnp.dot(pos_ref[...], w1_ref[...], precision=HI, preferred_element_type=f32)[0:1]
        hid = (acc[:, 0:half] + pltpu.roll(acc[:, half:2 * half], nseg - 1, 0)
               + jnp.concatenate([pos_term] * NSA_KV_HEADS, axis=1))
        hi, lo = _hi_lo(_gelu_tanh(hid))
        return jnp.dot(jnp.concatenate([hi, lo, hi], axis=1), w2s_ref[...], preferred_element_type=f32)

    kc2 = branch(kc_ref, wk1_ref, wk1s_ref, wk2s_ref)
    vc2 = branch(vc_ref, wv1_ref, wv1s_ref, wv2s_ref)
    lane = lax.broadcasted_iota(jnp.int32, (nseg, LANES), 1)
    for g in range(NSA_KV_HEADS):
        kc = kc2[:, g * LANES:(g + 1) * LANES]
        kc = kc * lax.rsqrt(jnp.mean(kc * kc, axis=-1, keepdims=True) + NORM_EPS) * kcg_ref[...]
        k_hi = kc.astype(bf16)
        k_lo = jnp.where(lane < HALF, kc - k_hi.astype(f32), 0.0).astype(bf16)
        kcd_ref[0, g] = jnp.concatenate([k_hi, k_lo], axis=1)
        vcl_ref[0, g] = jnp.where(lane < HALF, vc2[:, g * LANES:(g + 1) * LANES], 0.0).astype(bf16)


def _compress_weights(w1, w2):
    w1r = w1.reshape(2, CMP_STRIDE, NSA_HEAD_DIM, CMP_HIDDEN)
    z = jnp.zeros_like(w1r[0])

    def two_groups(w):
        return jnp.concatenate([jnp.concatenate([w, z], axis=2), jnp.concatenate([z, w], axis=2)], axis=1)

    wr = jnp.concatenate([two_groups(w1r[0]), two_groups(w1r[1])], axis=2)
    hi, lo = _hi_lo(wr)
    w1s = jnp.concatenate([hi, hi, lo], axis=1).reshape(CMP_STRIDE * 3 * LANES, 2 * NSA_KV_HEADS * CMP_HIDDEN)
    z2 = jnp.zeros_like(w2)
    w2d = jnp.concatenate([jnp.concatenate([w2, w2, z2, z2], axis=1),
                           jnp.concatenate([z2, z2, w2, w2], axis=1)], axis=0)
    hi2, lo2 = _hi_lo(w2d)
    return w1s, jnp.concatenate([hi2, hi2, lo2], axis=0)


def _compress(kv3, wk1, wk1s, wk2s, wv1, wv1s, wv2s, pos8, kcg2):
    b, t, _ = kv3.shape
    nseg = t // CMP_STRIDE

    def const(shape):
        return pl.BlockSpec(shape, lambda bi: (0,) * len(shape), pipeline_mode=pl.Buffered(1))

    return pl.pallas_call(
        _compress_kernel,
        grid=(b,),
        in_specs=[pl.BlockSpec((1, t, KV_WIDTH), lambda bi: (bi, 0, 0)),
                  pl.BlockSpec((1, t, KV_WIDTH), lambda bi: (bi, 0, 1)),
                  const(wk1.shape), const(wk1s.shape), const(wk2s.shape),
                  const(wv1.shape), const(wv1s.shape), const(wv2s.shape),
                  const(pos8.shape), const(kcg2.shape)],
        out_specs=[pl.BlockSpec((1, NSA_KV_HEADS, nseg, 2 * LANES), lambda bi: (bi, 0, 0, 0)),
                   pl.BlockSpec((1, NSA_KV_HEADS, nseg, LANES), lambda bi: (bi, 0, 0, 0))],
        out_shape=[jax.ShapeDtypeStruct((b, NSA_KV_HEADS, nseg, 2 * LANES), bf16),
                   jax.ShapeDtypeStruct((b, NSA_KV_HEADS, nseg, LANES), bf16)],
        compiler_params=_cparams(("parallel",)),
        name="compress",
    )(kv3, kv3, wk1, wk1s, wk2s, wv1, wv1s, wv2s, pos8, kcg2)


def _nsa_kernel(slopes_ref, q3_ref, sm_ref, gsel_ref, wdist_ref, kcd_ref, vcl_ref, ovt_ref,
                ks_ref, vs_ref, kw_ref, vw_ref, y_ref, qa_scr, m_scr, acc_scr, *, bounded):
    g = pl.program_id(1)
    qi = pl.program_id(2)
    q0 = qi * TQ
    ncp = kcd_ref.shape[2]
    nsel = ncp // (SEL_BLOCK // CMP_STRIDE)
    rows = NSA_GROUP * TQ
    log2e = math.log2(math.e)

    lane = lax.broadcasted_iota(jnp.int32, (TQ, LANES), 1)
    lo = lane < HALF
    slopes = [slopes_ref[g * NSA_GROUP + r] * log2e for r in range(NSA_GROUP)]

    def per_head(x, fn):
        return jnp.concatenate([fn(r, x[r * TQ:(r + 1) * TQ]) for r in range(NSA_GROUP)], axis=0)

    q3 = q3_ref[0, 0].reshape(rows, 2 * LANES)
    q1 = q3[:, 0:LANES]

    cidx = lax.broadcasted_iota(jnp.int32, (TQ, ncp), 1)
    tpos_c = q0 + lax.broadcasted_iota(jnp.int32, (TQ, ncp), 0)
    blk_end = cidx * CMP_STRIDE + (CMP_BLOCK - 1)
    valid_c = jnp.logical_and(tpos_c >= blk_end, cidx < ncp - 1)
    krel_c = (CMP_BLOCK - 1 - q0 + CMP_STRIDE * lax.broadcasted_iota(jnp.int32, (1, ncp), 1)).astype(f32)
    s_c = lax.dot_general(q3, kcd_ref[0, 0], NT, preferred_element_type=f32)
    if bounded:
        rowf = lax.broadcasted_iota(jnp.int32, (TQ, LANES), 0).astype(f32)

        def row_term(r, width):
            return jnp.concatenate([slopes[r] * rowf] * (width // LANES), axis=1)

        s_c = per_head(s_c, lambda r, x: jnp.where(valid_c, x + slopes[r] * krel_c - row_term(r, ncp), NEG))
        e_c = jnp.exp2(s_c)
    else:
        s_c = per_head(s_c, lambda r, x: jnp.where(valid_c, x + slopes[r] * krel_c, NEG))
        e_c = jnp.exp2(s_c - jnp.max(s_c, axis=-1, keepdims=True))
    p_c = e_c / jnp.sum(e_c, axis=-1, keepdims=True)
    p_c = per_head(p_c, lambda r, x: jnp.where(valid_c, x, 0.0))
    p_sum = p_c[0:TQ]
    for r in range(1, NSA_GROUP):
        p_sum = p_sum + p_c[r * TQ:(r + 1) * TQ]
    o_cmp = jnp.dot(p_c.astype(bf16), vcl_ref[0, 0], preferred_element_type=f32)

    band = WINDOW + TQ
    kb = pl.multiple_of(jnp.maximum(q0 - WINDOW, 0), TQ)
    s_w = lax.dot_general(q1, kw_ref[0, 0, pl.ds(kb, band), :], NT, preferred_element_type=f32)
    dist_w = wdist_ref[jnp.minimum(qi, WINDOW // TQ)]
    s_w = per_head(s_w, lambda r, x: x + slopes[r] * dist_w)
    if bounded:
        p_w = jnp.exp2(s_w)
    else:
        p_w = jnp.exp2(s_w - jnp.max(s_w, axis=-1, keepdims=True))
    acc_w = jnp.dot(p_w.astype(bf16), vw_ref[0, 0, pl.ds(kb, band), :], preferred_element_type=f32)
    o_win = acc_w / pltpu.roll(acc_w, HALF, 1)

    imp = lax.dot_general(ovt_ref[...], jnp.concatenate(_split3(p_sum), axis=1), NT,
                          preferred_element_type=f32)[0:nsel]
    jrow = lax.broadcasted_iota(jnp.int32, (nsel, TQ), 0)
    tcol = q0 + lax.broadcasted_iota(jnp.int32, (nsel, TQ), 1)
    forced = jnp.logical_or(jrow == lax.shift_right_logical(tcol, 6), jrow == 0)
    future = jrow * SEL_BLOCK > tcol
    taken = -3e38
    work = jnp.where(forced, taken, jnp.where(future, -FORCE, imp))
    jrow_f = jrow.astype(f32)
    bias_t = jnp.where(forced, 0.0, NEG)
    for _ in range(min(SEL_TOPN, nsel) - 2):
        best = jnp.max(work, axis=0, keepdims=True)
        first = jnp.min(jnp.where(work == best, jrow_f, float(nsel)), axis=0, keepdims=True)
        hit = jrow_f == first
        bias_t = jnp.where(hit, 0.0, bias_t)
        work = jnp.where(hit, taken, work)
    if nsel < HALF:
        bias_t = jnp.concatenate([bias_t, jnp.full((HALF - nsel, TQ), NEG, f32)], axis=0)
    bias = jnp.concatenate([bias_t, bias_t], axis=0).T

    bias_b = bias.astype(bf16)
    qa_scr[...] = per_head(q1, lambda r, x: jnp.where(lo, x, bias_b))

    m_scr[...] = jnp.full(m_scr.shape, NEG, f32)
    acc_scr[...] = jnp.zeros(acc_scr.shape, f32)

    def scores(kt, causal, row_shift):
        k0 = pl.multiple_of(kt * TK, TK)
        s = lax.dot_general(qa_scr[...], ks_ref[0, 0, pl.ds(k0, TK), :], NT, preferred_element_type=f32)
        krel = (k0 - q0 + lax.broadcasted_iota(jnp.int32, (1, TK), 1)).astype(f32)

        def alibi(r, x):
            x = x + slopes[r] * krel
            return x if row_shift is None else x - row_shift[r]

        if causal:
            ahead = (k0 - q0 + lax.broadcasted_iota(jnp.int32, (TQ, TK), 1)
                     > lax.broadcasted_iota(jnp.int32, (TQ, TK), 0))
            return per_head(s, lambda r, x: jnp.where(ahead, NEG, alibi(r, x))), k0
        return per_head(s, alibi), k0

    def online_tile(state, kt, causal):
        m_old, acc = state
        s, k0 = scores(kt, causal, None)
        m_new = jnp.maximum(m_old, jnp.max(s, axis=-1, keepdims=True))
        p = jnp.exp2(s - m_new[:, 0:1])
        acc = jnp.exp2(m_old - m_new) * acc + jnp.dot(p.astype(bf16), vs_ref[0, 0, pl.ds(k0, TK), :],
                                                      preferred_element_type=f32)
        return m_new, acc

    def online_tiles(tiles):
        state = (m_scr[...], acc_scr[...])
        for kt, causal in tiles:
            state = online_tile(state, kt, causal)
        m_scr[...] = state[0]
        acc_scr[...] = state[1]

    def bounded_tiles(tiles):
        row_shift = [row_term(r, TK) for r in range(NSA_GROUP)]
        acc = acc_scr[...]
        for kt, causal in tiles:
            s, k0 = scores(kt, causal, row_shift)
            acc = acc + jnp.dot(jnp.exp2(s).astype(bf16), vs_ref[0, 0, pl.ds(k0, TK), :],
                                preferred_element_type=f32)
        acc_scr[...] = acc

    n_full = q0 // TK

    def selected_branch(run_tiles):
        def body(i, carry):
            run_tiles([(2 * i, False), (2 * i + 1, False)])
            return carry

        lax.fori_loop(0, n_full // 2, body, 0)

        @pl.when(n_full % 2 == 1)
        def _():
            run_tiles([(n_full - 1, False), (n_full, True)])

        @pl.when(n_full % 2 == 0)
        def _():
            run_tiles([(n_full, True)])

    selected_branch(bounded_tiles if bounded else online_tiles)

    acc_s = acc_scr[...]
    o_slc = acc_s / pltpu.roll(acc_s, HALF, 1)

    gates = jnp.dot(jnp.concatenate(_hi_lo(jax.nn.sigmoid(sm_ref[0])), axis=1), gsel_ref[0],
                    preferred_element_type=f32)

    def gate(r, branch):
        col = 3 * r + branch
        return gates[:, col * LANES:(col + 1) * LANES]

    def mix(r):
        sl = slice(r * TQ, (r + 1) * TQ)
        return gate(r, 0) * o_cmp[sl] + gate(r, 1) * o_slc[sl] + gate(r, 2) * o_win[sl]

    for c in range(NSA_GROUP // 2):
        y_ref[0, :, c * LANES:(c + 1) * LANES] = jnp.where(lo, mix(2 * c), pltpu.roll(mix(2 * c + 1), HALF, 1))


def _gate_select():
    sel = np.zeros((NSA_KV_HEADS, 2 * LANES, 3 * NSA_GROUP * LANES), np.float32)
    for g in range(NSA_KV_HEADS):
        for c in range(3 * NSA_GROUP):
            src = 3 * NSA_GROUP * g + c
            sel[g, [src, LANES + src], c * LANES:(c + 1) * LANES] = 1.0
    return jnp.asarray(sel, dtype=bf16)


def _window_distance():
    i = np.arange(TQ)[None, :, None]
    j = np.arange(WINDOW + TQ)[None, None, :]
    off = -TQ * np.arange(WINDOW // TQ + 1)[:, None, None]
    rel = off + j - i
    return jnp.asarray(np.where((rel <= 0) & (rel > -WINDOW), rel, NEG), dtype=f32)


def _nsa(slopes, q3, sm3, kcd, vcl, ovt, ks, vs, kw, vw, *, bounded):
    b, t = q3.shape[0], q3.shape[3]
    gsel = _gate_select()
    wdist = _window_distance()
    gw = NSA_GROUP * NSA_HEAD_DIM
    ncp = kcd.shape[2]

    def kv_spec(rows):
        return pl.BlockSpec((1, 1, rows, LANES), lambda bi, gi, qi: (bi, gi, 0, 0))

    return pl.pallas_call(
        functools.partial(_nsa_kernel, bounded=bounded),
        grid=(b, NSA_KV_HEADS, t // TQ),
        in_specs=[
            pl.BlockSpec(memory_space=pltpu.SMEM),
            pl.BlockSpec((1, 1, NSA_GROUP, TQ, 2 * LANES), lambda bi, gi, qi: (bi, gi, 0, qi, 0)),
            pl.BlockSpec((1, TQ, LANES), lambda bi, gi, qi: (bi, qi, 0)),
            pl.BlockSpec((1,) + gsel.shape[1:], lambda bi, gi, qi: (gi, 0, 0)),
            pl.BlockSpec(wdist.shape, lambda bi, gi, qi: (0, 0, 0)),
            pl.BlockSpec((1, 1, ncp, 2 * LANES), lambda bi, gi, qi: (bi, gi, 0, 0)), kv_spec(ncp),
            pl.BlockSpec(ovt.shape, lambda bi, gi, qi: (0, 0)),
            kv_spec(t), kv_spec(t), kv_spec(t), kv_spec(t),
        ],
        out_specs=pl.BlockSpec((1, TQ, gw), lambda bi, gi, qi: (bi, qi, gi)),
        out_shape=jax.ShapeDtypeStruct((b, t, NSA_WIDTH), f32),
        scratch_shapes=[
            pltpu.VMEM((NSA_GROUP * TQ, LANES), bf16),
            pltpu.VMEM((NSA_GROUP * TQ, LANES), f32),
            pltpu.VMEM((NSA_GROUP * TQ, LANES), f32),
        ],
        compiler_params=_cparams(("parallel", "parallel", "arbitrary")),
        name="nsa",
    )(slopes, q3, sm3, gsel, wdist, kcd, vcl, ovt, ks, vs, kw, vw)


def _log_sigmoid(x):
    return jnp.minimum(x, 0.0) - jnp.log1p(jnp.exp(-jnp.abs(x)))


def _mlstm_kernel(bi_ref, bf_ref, u_ref, v_ref, op_ref, gi_ref, gf_ref, cw_ref, cb_ref, wq_ref, wk_ref,
                  ng_ref, sk_ref, y_ref, uc_scr, q_scr, kt_scr, ct_scr, m_scr, b_scr, li_scr, xp_scr):
    h = pl.program_id(1)
    t = u_ref.shape[1]
    L = MLSTM_L
    dm = MLSTM_HEAD_DIM

    x = u_ref[0]
    xp_scr[0:8, :] = jnp.zeros((8, dm), f32)
    xp_scr[8:, :] = x
    acc = x * cw_ref[CONV_WIDTH - 1:CONV_WIDTH, :]
    for s in range(1, CONV_WIDTH):
        acc = acc + xp_scr[8 - s:8 - s + t, :] * cw_ref[CONV_WIDTH - 1 - s:CONV_WIDTH - s, :]
    uc = acc + cb_ref[...]
    uc = uc * jax.nn.sigmoid(uc)
    uc_scr[...] = uc
    ucb = uc.astype(bf16)
    q_scr[...] = jnp.dot(ucb, wq_ref[0].astype(bf16), preferred_element_type=f32).astype(bf16)
    k = jnp.dot(ucb, wk_ref[0].astype(bf16), preferred_element_type=f32) * (dm ** -0.5)
    kt_scr[...] = k.T

    ct_scr[...] = jnp.zeros(ct_scr.shape, f32)
    m_scr[...] = jnp.zeros(m_scr.shape, f32)

    li_ = lax.broadcasted_iota(jnp.int32, (L, L), 0)
    si_ = lax.broadcasted_iota(jnp.int32, (L, L), 1)
    causal = si_ <= li_
    diag = si_ == li_
    ones_v = jnp.ones((L, dm), f32)
    ones_sq = jnp.ones((dm, dm), bf16)

    def wide(x):
        return jnp.concatenate([x] * (L // dm), axis=1)

    log_f = _log_sigmoid(gf_ref[0, 0] + bf_ref[h])
    upper = jnp.where(li_ <= si_, 1.0, 0.0).astype(bf16)
    b_scr[...] = jnp.dot(jnp.concatenate(_split3(log_f), axis=1), jnp.concatenate([upper] * 3, axis=0),
                         preferred_element_type=f32)
    li_scr[...] = gi_ref[0, 0] + bi_ref[h]

    def chunk(c, ct, m_prev):
        r0 = pl.multiple_of(c * L, L)
        qc = q_scr[pl.ds(r0, L), :]
        ktc = kt_scr[:, pl.ds(r0, L)]
        vaug = jnp.concatenate([v_ref[0, pl.ds(r0, L), :], ones_v], axis=1).astype(bf16)
        log_i = li_scr[pl.ds(c, 1), :]
        b_row = b_scr[pl.ds(c, 1), :]
        b_col = jnp.broadcast_to(jnp.sum(jnp.where(diag, b_row, 0.0), axis=-1, keepdims=True), (L, dm))
        g_sum = b_row[:, L - 1:L]
        dmat = jnp.where(causal, (wide(b_col) - b_row) + log_i, NEG)
        m_loc = jnp.broadcast_to(jnp.max(dmat, axis=-1, keepdims=True), (L, dm))
        p = jnp.dot(qc, ktc.astype(bf16), preferred_element_type=f32) * jnp.exp(dmat - wide(m_loc))
        intra = jnp.dot(p.astype(bf16), vaug, preferred_element_type=f32)
        m_inter = b_col + m_prev
        m_out = jnp.maximum(m_inter, m_loc)
        xo = (wide(jnp.exp(m_inter - m_out)) * jnp.dot(qc, ct.astype(bf16), preferred_element_type=f32)
              + wide(jnp.exp(m_loc - m_out)) * intra)
        num = xo[:, 0:dm]
        den = xo[:, dm:2 * dm]
        hh = num / jnp.maximum(jnp.abs(den), jnp.exp(-m_out))
        hh = hh * jax.nn.sigmoid(op_ref[0, pl.ds(r0, L), :])
        ssq = jnp.dot((hh * hh).astype(bf16), ones_sq, preferred_element_type=f32)
        hh = hh * lax.rsqrt(ssq * (1.0 / dm) + NORM_EPS) * ng_ref[0]
        y_ref[0, pl.ds(r0, L), :] = hh + sk_ref[...] * uc_scr[pl.ds(r0, L), :]

        w_end = (g_sum - b_row) + log_i
        m_new = jnp.maximum(g_sum + m_prev, jnp.max(w_end, axis=-1, keepdims=True))
        decay = jnp.exp(g_sum + m_prev - m_new)
        w = jnp.exp(w_end - m_new)
        return decay * ct + jnp.dot((ktc * w).astype(bf16), vaug, preferred_element_type=f32), m_new

    def chunk_group(i, carry):
        ct, m_prev = ct_scr[...], m_scr[0:1, 0:1]
        for j in range(MLSTM_UNROLL):
            ct, m_prev = chunk(i * MLSTM_UNROLL + j, ct, m_prev)
        ct_scr[...] = ct
        m_scr[...] = jnp.broadcast_to(m_prev, m_scr.shape)
        return carry

    lax.fori_loop(0, t // (L * MLSTM_UNROLL), chunk_group, 0)


def _mlstm(b_i, b_f, u3, v3, op3, gi4, gf4, cw, cb2, wq, wk, ng3, sk2):
    b, t, _ = u3.shape
    dm = MLSTM_HEAD_DIM
    seq = pl.BlockSpec((1, t, dm), lambda bi, hi: (bi, 0, hi))
    gate = pl.BlockSpec((1, 1, t // MLSTM_L, MLSTM_L), lambda bi, hi: (bi, hi, 0, 0))
    smem = pl.BlockSpec(memory_space=pltpu.SMEM)
    return pl.pallas_call(
        _mlstm_kernel,
        grid=(b, MLSTM_HEADS),
        in_specs=[
            smem, smem, seq, seq, seq, gate, gate,
            pl.BlockSpec((CONV_WIDTH, dm), lambda bi, hi: (0, hi)),
            pl.BlockSpec((1, dm), lambda bi, hi: (0, hi)),
            pl.BlockSpec((1, dm, dm), lambda bi, hi: (hi, 0, 0)),
            pl.BlockSpec((1, dm, dm), lambda bi, hi: (hi, 0, 0)),
            pl.BlockSpec((1, 1, dm), lambda bi, hi: (hi, 0, 0)),
            pl.BlockSpec((1, dm), lambda bi, hi: (0, hi)),
        ],
        out_specs=seq,
        out_shape=jax.ShapeDtypeStruct((b, t, MLSTM_WIDTH), f32),
        scratch_shapes=[
            pltpu.VMEM((t, dm), f32),
            pltpu.VMEM((t, dm), bf16),
            pltpu.VMEM((dm, t), f32),
            pltpu.VMEM((dm, 2 * dm), f32),
            pltpu.VMEM((8, LANES), f32),
            pltpu.VMEM((t // MLSTM_L, MLSTM_L), f32),
            pltpu.VMEM((t // MLSTM_L, MLSTM_L), f32),
            pltpu.VMEM((t + 8, dm), f32),
        ],
        compiler_params=_cparams(("parallel", "parallel")),
        name="mlstm",
    )(b_i, b_f, u3, v3, op3, gi4, gf4, cw, cb2, wq, wk, ng3, sk2)


def _ffn_kernel(x_ref, ya_ref, yb_ref, wo_ref, g2_ref, wg_ref, wu_ref, wd_ref, o_ref, act_scr):
    x1 = (x_ref[...]
          + jnp.dot(ya_ref[...].astype(bf16), wo_ref[0:NSA_WIDTH, :], preferred_element_type=f32)
          + jnp.dot(yb_ref[...].astype(bf16), wo_ref[NSA_WIDTH:NSA_WIDTH + MLSTM_WIDTH, :],
                    preferred_element_type=f32))
    h2 = x1 * lax.rsqrt(jnp.mean(x1 * x1, axis=-1, keepdims=True) + NORM_EPS) * g2_ref[...]
    h2b = h2.astype(bf16)
    for c in range(D_FF // FF_CHUNK):
        cols = slice(c * FF_CHUNK, (c + 1) * FF_CHUNK)
        gt = jnp.dot(h2b, wg_ref[:, cols], preferred_element_type=f32)
        up = jnp.dot(h2b, wu_ref[:, cols], preferred_element_type=f32)
        act_scr[:, cols] = (gt * jax.nn.sigmoid(gt) * up).astype(bf16)
    o_ref[...] = x1 + jnp.dot(act_scr[...], wd_ref[...], preferred_element_type=f32)


def _ffn(x2, ya, yb, wo, g2, wg, wu, wd):
    n = x2.shape[0]
    tm = TM_FFN

    def const(shape):
        return pl.BlockSpec(shape, lambda i: (0, 0), pipeline_mode=pl.Buffered(1))

    return pl.pallas_call(
        _ffn_kernel,
        grid=(n // tm,),
        in_specs=[
            pl.BlockSpec((tm, D_MODEL), lambda i: (i, 0)),
            pl.BlockSpec((tm, NSA_WIDTH), lambda i: (i, 0)),
            pl.BlockSpec((tm, MLSTM_WIDTH), lambda i: (i, 0)),
            const(wo.shape), const(g2.shape), const(wg.shape), const(wu.shape), const(wd.shape),
        ],
        out_specs=pl.BlockSpec((tm, D_MODEL), lambda i: (i, 0)),
        out_shape=jax.ShapeDtypeStruct((n, D_MODEL), f32),
        scratch_shapes=[pltpu.VMEM((tm, D_FF), bf16)],
        compiler_params=_cparams(("parallel",)),
        name="ffn",
    )(x2, ya, yb, wo, g2, wg, wu, wd)


def _overlap_t(ncp):
    nsel = ncp // (SEL_BLOCK // CMP_STRIDE)
    cs = np.arange(ncp) * CMP_STRIDE
    ss = np.arange(nsel) * SEL_BLOCK
    ov = ((cs[None, :] < ss[:, None] + SEL_BLOCK) & (cs[None, :] + CMP_BLOCK > ss[:, None])).astype(np.float32)
    ov[:, ncp - 1] = 0.0
    out = np.zeros((LANES, ncp), np.float32)
    out[:nsel] = ov
    return jnp.asarray(np.tile(out, (1, 3)), dtype=bf16)


def _layer(x, norm1_g, w_in, q_g, kc_g, ks_g, kw_g, cmp_pos, w_ck1, w_ck2, w_cv1, w_cv2, conv_w, conv_b,
           w_mq, w_mk, b_i, b_f, mlstm_norm_g, mlstm_skip, w_out, norm2_g, w_gate, w_up, w_down):
    b, t, d = x.shape
    n = b * t
    x2 = x.reshape(n, d)

    o_gate = NSA_WIDTH + 6 * KV_WIDTH
    o_u = o_gate + 3 * NSA_HEADS
    o_if = o_u + 3 * MLSTM_WIDTH
    w_perm = jnp.concatenate([
        w_in[:, :o_gate], w_in[:, o_u:o_if], w_in[:, o_gate:o_u], w_in[:, o_if:],
        jnp.zeros((d, LANES - 3 * NSA_HEADS - 2 * MLSTM_HEADS), w_in.dtype)], axis=1).astype(bf16)
    w_gates_t = jnp.concatenate([w_in[:, o_if:].T, jnp.zeros((GATE_ROWS - 2 * MLSTM_HEADS, d), w_in.dtype)],
                                axis=0).astype(bf16)
    kvc2, u2, vm2, op2, sm2, gates_t, ks, vs, kw, vw, q3 = _inproj(
        x2, norm1_g.reshape(1, d), w_perm, w_gates_t, jnp.tile(ks_g, 2).reshape(1, LANES),
        jnp.tile(kw_g, 2).reshape(1, LANES), jnp.tile(q_g, NSA_HEADS).reshape(1, NSA_WIDTH), b, t)

    kv3 = kvc2.reshape(b, t, 2 * KV_WIDTH)
    nseg = t // CMP_STRIDE
    pos8 = jnp.broadcast_to(cmp_pos.reshape(1, CMP_BLOCK * NSA_HEAD_DIM), (8, CMP_BLOCK * NSA_HEAD_DIM))
    wk1s, wk2s = _compress_weights(w_ck1, w_ck2)
    wv1s, wv2s = _compress_weights(w_cv1, w_cv2)
    kcd, vcl = _compress(kv3, w_ck1, wk1s, wk2s, w_cv1, wv1s, wv2s, pos8, jnp.tile(kc_g, 2).reshape(1, LANES))
    slopes = jnp.exp2(-8.0 * (jnp.arange(NSA_HEADS, dtype=f32) + 1.0) / NSA_HEADS)
    k_gain = jnp.maximum(jnp.max(jnp.abs(kc_g)), jnp.maximum(jnp.max(jnp.abs(ks_g)), jnp.max(jnp.abs(kw_g))))
    score_cap = (NSA_HEAD_DIM ** 0.5 * math.log2(math.e)) * jnp.max(jnp.abs(q_g)) * k_gain
    nsa_args = (slopes, q3, sm2.reshape(b, t, LANES), kcd, vcl, _overlap_t(nseg), ks, vs, kw, vw)
    y_nsa = lax.cond(score_cap <= SCORE_BOUND, functools.partial(_nsa, bounded=True),
                     functools.partial(_nsa, bounded=False), *nsa_args)

    gi4 = gates_t[0:MLSTM_HEADS].reshape(MLSTM_HEADS, b, t).transpose(1, 0, 2)
    gf4 = gates_t[MLSTM_HEADS:2 * MLSTM_HEADS].reshape(MLSTM_HEADS, b, t).transpose(1, 0, 2)
    y_mem = _mlstm(b_i, b_f, u2.reshape(b, t, MLSTM_WIDTH), vm2.reshape(b, t, MLSTM_WIDTH),
                   op2.reshape(b, t, MLSTM_WIDTH), gi4.reshape(b, MLSTM_HEADS, t // MLSTM_L, MLSTM_L),
                   gf4.reshape(b, MLSTM_HEADS, t // MLSTM_L, MLSTM_L), conv_w, conv_b.reshape(1, MLSTM_WIDTH),
                   w_mq, w_mk,
                   mlstm_norm_g.reshape(MLSTM_HEADS, 1, MLSTM_HEAD_DIM), mlstm_skip.reshape(1, MLSTM_WIDTH))

    out = _ffn(x2, y_nsa.reshape(n, NSA_WIDTH), y_mem.reshape(n, MLSTM_WIDTH), w_out.astype(bf16),
               norm2_g.reshape(1, d), w_gate.astype(bf16), w_up.astype(bf16), w_down.astype(bf16))
    return out.reshape(b, t, d)


def kernel(x, norm1_g, w_in, q_norm_g, kc_norm_g, ks_norm_g, kw_norm_g, cmp_pos, w_ck1, w_ck2, w_cv1, w_cv2,
           conv_w, conv_b, w_mq, w_mk, b_i, b_f, mlstm_norm_g, mlstm_skip, w_out, norm2_g, w_gate, w_up, w_down):
    depth = norm1_g.shape[0]
    for l in range(depth):
        x = _layer(x, norm1_g[l], w_in[l], q_norm_g[l], kc_norm_g[l], ks_norm_g[l], kw_norm_g[l], cmp_pos[l],
                   w_ck1[l], w_ck2[l], w_cv1[l], w_cv2[l], conv_w[l], conv_b[l], w_mq[l], w_mk[l], b_i[l], b_f[l],
                   mlstm_norm_g[l], mlstm_skip[l], w_out[l], norm2_g[l], w_gate[l], w_up[l], w_down[l])
    return x
```

```python
import functools
import math

import numpy as np
import jax
import jax.numpy as jnp
from jax import lax
from jax.experimental import pallas as pl
from jax.experimental.pallas import tpu as pltpu

f32 = jnp.float32
bf16 = jnp.bfloat16

D_MODEL = 1024
NSA_HEADS = 8
NSA_KV_HEADS = 2
NSA_HEAD_DIM = 64
NSA_GROUP = NSA_HEADS // NSA_KV_HEADS
CMP_BLOCK = 32
CMP_STRIDE = 16
CMP_HIDDEN = 256
SEL_BLOCK = 64
SEL_TOPN = 16
WINDOW = 512
MLSTM_HEADS = 4
MLSTM_HEAD_DIM = 128
CONV_WIDTH = 4
NSA_WIDTH = NSA_HEADS * NSA_HEAD_DIM
MLSTM_WIDTH = MLSTM_HEADS * MLSTM_HEAD_DIM
KV_WIDTH = NSA_KV_HEADS * NSA_HEAD_DIM
D_FF = -(-8 * D_MODEL // (3 * 256)) * 256
NORM_EPS = 1e-6
NEG = -1e30
FORCE = 1e9
SCORE_BOUND = 100.0

LANES = 128
SUBLANES = 8
HALF = 64
SEL_SHIFT = SEL_BLOCK.bit_length() - 1
assert 1 << SEL_SHIFT == SEL_BLOCK and SEL_BLOCK == HALF
VMEM_LIMIT = 56 * 1024 * 1024

TM_PROJ = 1024
TQ = 512
TK = 512
MLSTM_L = 256
MLSTM_UNROLL = 4
TM_FFN = 512
FF_CHUNK = 256

HI = lax.Precision.HIGHEST
NT = (((1,), (1,)), ((), ()))


def _cparams(sem):
    return pltpu.CompilerParams(dimension_semantics=sem, vmem_limit_bytes=VMEM_LIMIT)


def _split3(x):
    x1 = x.astype(bf16)
    r1 = x - x1.astype(f32)
    x2 = r1.astype(bf16)
    x3 = (r1 - x2.astype(f32)).astype(bf16)
    return x1, x2, x3


COL_Q = 0
COL_KV = COL_Q + NSA_WIDTH
COL_U = COL_KV + 6 * KV_WIDTH
COL_VM = COL_U + MLSTM_WIDTH
COL_OP = COL_VM + MLSTM_WIDTH
COL_SM = COL_OP + MLSTM_WIDTH
PROJ_WIDTH = COL_SM + LANES
GATE_ROWS = 16


def _pair_norm(x, gain, lo):
    sq = x * x
    s_lo = jnp.sum(jnp.where(lo, sq, 0.0), axis=-1, keepdims=True)
    s_hi = jnp.sum(jnp.where(lo, 0.0, sq), axis=-1, keepdims=True)
    inv = jnp.where(lo, lax.rsqrt(s_lo / HALF + NORM_EPS), lax.rsqrt(s_hi / HALF + NORM_EPS))
    return x * inv * gain


def _inproj_kernel(x_ref, g_ref, w_ref, wgt_ref, ksg_ref, kwg_ref, qg_ref,
                   kvc_ref, u_ref, vm_ref, op_ref, sm_ref, gt_ref, ks_ref, vs_ref, kw_ref, vw_ref, q3_ref,
                   *, tiles_per_seq):
    tm = x_ref.shape[0]
    x = x_ref[...]
    h = x * lax.rsqrt(jnp.mean(x * x, axis=-1, keepdims=True) + NORM_EPS) * g_ref[...]
    hb = h.astype(bf16)

    def proj(col, width):
        return jnp.dot(hb, w_ref[:, col:col + width], preferred_element_type=f32)

    lane = lax.broadcasted_iota(jnp.int32, (tm, LANES), 1)
    lo = lane < HALF
    pos = (pl.program_id(0) % tiles_per_seq) * tm + lax.broadcasted_iota(jnp.int32, (tm, LANES), 0)

    q = proj(COL_Q, NSA_WIDTH)
    scale = NSA_HEAD_DIM ** -0.5 * math.log2(math.e)
    for c in range(NSA_HEADS // 2):
        pair = _pair_norm(q[:, c * LANES:(c + 1) * LANES], qg_ref[:, c * LANES:(c + 1) * LANES], lo) * scale
        for par, xq in enumerate((pair, pltpu.roll(pair, HALF, 1))):
            hi = jnp.where(lo, xq, 0.0).astype(bf16).astype(f32)
            res = jnp.where(lo, xq - hi, 0.0)
            head = 2 * c + par
            q3_ref[0, head // NSA_GROUP, head % NSA_GROUP] = jnp.concatenate(
                [hi + pltpu.roll(res, HALF, 1), hi], axis=1).astype(bf16)

    kv = proj(COL_KV, 6 * KV_WIDTH)
    kvc_ref[...] = kv[:, 0:2 * KV_WIDTH]
    code = jnp.where(lax.shift_right_logical(pos, SEL_SHIFT) == (lane & (HALF - 1)), 1.0, 0.0)

    def put(xk, o_ref, fill):
        o_ref[0, 0] = jnp.where(lo, xk, fill).astype(bf16)
        o_ref[0, 1] = jnp.where(lo, pltpu.roll(xk, HALF, 1), fill).astype(bf16)

    put(_pair_norm(kv[:, 2 * KV_WIDTH:3 * KV_WIDTH], ksg_ref[...], lo), ks_ref, code)
    put(kv[:, 3 * KV_WIDTH:4 * KV_WIDTH], vs_ref, 1.0)
    put(_pair_norm(kv[:, 4 * KV_WIDTH:5 * KV_WIDTH], kwg_ref[...], lo), kw_ref, 0.0)
    put(kv[:, 5 * KV_WIDTH:6 * KV_WIDTH], vw_ref, 1.0)

    u_ref[...] = proj(COL_U, MLSTM_WIDTH)
    vm_ref[...] = proj(COL_VM, MLSTM_WIDTH)
    op_ref[...] = proj(COL_OP, MLSTM_WIDTH)
    sm_ref[...] = proj(COL_SM, LANES)
    gt_ref[...] = lax.dot_general(wgt_ref[...], hb, NT, preferred_element_type=f32)


def _inproj(x2, g1, w_perm, w_gates_t, ksg2, kwg2, qg2, b, t):
    n = x2.shape[0]
    tm = min(TM_PROJ, t)
    nt = t // tm

    def rows(width):
        return pl.BlockSpec((tm, width), lambda i: (i, 0))

    def const(shape):
        return pl.BlockSpec(shape, lambda i: (0, 0))

    kv_spec = pl.BlockSpec((1, NSA_KV_HEADS, tm, LANES), lambda i: (i // nt, 0, i % nt, 0))
    kv_shape = jax.ShapeDtypeStruct((b, NSA_KV_HEADS, t, LANES), bf16)
    return pl.pallas_call(
        functools.partial(_inproj_kernel, tiles_per_seq=nt),
        grid=(n // tm,),
        in_specs=[rows(D_MODEL), const((1, D_MODEL)), const((D_MODEL, PROJ_WIDTH)), const((GATE_ROWS, D_MODEL)),
                  const((1, LANES)), const((1, LANES)), const((1, NSA_WIDTH))],
        out_specs=[rows(2 * KV_WIDTH), rows(MLSTM_WIDTH), rows(MLSTM_WIDTH), rows(MLSTM_WIDTH), rows(LANES),
                   pl.BlockSpec((GATE_ROWS, tm), lambda i: (0, i)),
                   kv_spec, kv_spec, kv_spec, kv_spec,
                   pl.BlockSpec((1, NSA_KV_HEADS, NSA_GROUP, tm, 2 * LANES),
                                lambda i: (i // nt, 0, 0, i % nt, 0))],
        out_shape=[jax.ShapeDtypeStruct((n, 2 * KV_WIDTH), f32), jax.ShapeDtypeStruct((n, MLSTM_WIDTH), f32),
                   jax.ShapeDtypeStruct((n, MLSTM_WIDTH), f32), jax.ShapeDtypeStruct((n, MLSTM_WIDTH), f32),
                   jax.ShapeDtypeStruct((n, LANES), f32), jax.ShapeDtypeStruct((GATE_ROWS, n), f32),
                   kv_shape, kv_shape, kv_shape, kv_shape,
                   jax.ShapeDtypeStruct((b, NSA_KV_HEADS, NSA_GROUP, t, 2 * LANES), bf16)],
        compiler_params=_cparams(("parallel",)),
        name="inproj",
    )(x2, g1, w_perm, w_gates_t, ksg2, kwg2, qg2)


def _gelu_tanh(x):
    return 0.5 * x * (1.0 + jnp.tanh(math.sqrt(2.0 / math.pi) * (x + 0.044715 * (x * x * x))))


def _hi_lo(x):
    hi = x.astype(bf16)
    return hi, (x - hi.astype(f32)).astype(bf16)


def _compress_kernel(kc_ref, vc_ref, wk1_ref, wk1s_ref, wk2s_ref, wv1_ref, wv1s_ref, wv2s_ref, pos_ref, kcg_ref,
                     kcd_ref, vcl_ref):
    nseg = kc_ref.shape[1] // CMP_STRIDE
    half = NSA_KV_HEADS * CMP_HIDDEN

    def branch(x_ref, w1_ref, w1s_ref, w2s_ref):
        terms = []
        for r in range(CMP_STRIDE):
            hi, lo = _hi_lo(x_ref[0, pl.ds(r, nseg, stride=CMP_STRIDE), :])
            terms += [hi, lo, hi]
        acc = jnp.dot(jnp.concatenate(terms, axis=1), w1s_ref[...], preferred_element_type=f32)
        pos_term = jnp.dot(pos_ref[...], w1_ref[...], precision=HI, preferred_element_type=f32)[0:1]
        hid = (acc[:, 0:half] + pltpu.roll(acc[:, half:2 * half], nseg - 1, 0)
               + jnp.concatenate([pos_term] * NSA_KV_HEADS, axis=1))
        hi, lo = _hi_lo(_gelu_tanh(hid))
        return jnp.dot(jnp.concatenate([hi, lo, hi], axis=1), w2s_ref[...], preferred_element_type=f32)

    kc2 = branch(kc_ref, wk1_ref, wk1s_ref, wk2s_ref)
    vc2 = branch(vc_ref, wv1_ref, wv1s_ref, wv2s_ref)
    lane = lax.broadcasted_iota(jnp.int32, (nseg, LANES), 1)
    for g in range(NSA_KV_HEADS):
        kc = kc2[:, g * LANES:(g + 1) * LANES]
        kc = kc * lax.rsqrt(jnp.mean(kc * kc, axis=-1, keepdims=True) + NORM_EPS) * kcg_ref[...]
        k_hi = kc.astype(bf16)
        k_lo = jnp.where(lane < HALF, kc - k_hi.astype(f32), 0.0).astype(bf16)
        kcd_ref[0, g] = jnp.concatenate([k_hi, k_lo], axis=1)
        vcl_ref[0, g] = jnp.where(lane < HALF, vc2[:, g * LANES:(g + 1) * LANES], 0.0).astype(bf16)


def _compress_weights(w1, w2):
    w1r = w1.reshape(2, CMP_STRIDE, NSA_HEAD_DIM, CMP_HIDDEN)
    z = jnp.zeros_like(w1r[0])

    def two_groups(w):
        return jnp.concatenate([jnp.concatenate([w, z], axis=2), jnp.concatenate([z, w], axis=2)], axis=1)

    wr = jnp.concatenate([two_groups(w1r[0]), two_groups(w1r[1])], axis=2)
    hi, lo = _hi_lo(wr)
    w1s = jnp.concatenate([hi, hi, lo], axis=1).reshape(CMP_STRIDE * 3 * LANES, 2 * NSA_KV_HEADS * CMP_HIDDEN)
    z2 = jnp.zeros_like(w2)
    w2d = jnp.concatenate([jnp.concatenate([w2, w2, z2, z2], axis=1),
                           jnp.concatenate([z2, z2, w2, w2], axis=1)], axis=0)
    hi2, lo2 = _hi_lo(w2d)
    return w1s, jnp.concatenate([hi2, hi2, lo2], axis=0)


def _compress(kv3, wk1, wk1s, wk2s, wv1, wv1s, wv2s, pos8, kcg2):
    b, t, _ = kv3.shape
    nseg = t // CMP_STRIDE

    def const(shape):
        return pl.BlockSpec(shape, lambda bi: (0,) * len(shape), pipeline_mode=pl.Buffered(1))

    return pl.pallas_call(
        _compress_kernel,
        grid=(b,),
        in_specs=[pl.BlockSpec((1, t, KV_WIDTH), lambda bi: (bi, 0, 0)),
                  pl.BlockSpec((1, t, KV_WIDTH), lambda bi: (bi, 0, 1)),
                  const(wk1.shape), const(wk1s.shape), const(wk2s.shape),
                  const(wv1.shape), const(wv1s.shape), const(wv2s.shape),
                  const(pos8.shape), const(kcg2.shape)],
        out_specs=[pl.BlockSpec((1, NSA_KV_HEADS, nseg, 2 * LANES), lambda bi: (bi, 0, 0, 0)),
                   pl.BlockSpec((1, NSA_KV_HEADS, nseg, LANES), lambda bi: (bi, 0, 0, 0))],
        out_shape=[jax.ShapeDtypeStruct((b, NSA_KV_HEADS, nseg, 2 * LANES), bf16),
                   jax.ShapeDtypeStruct((b, NSA_KV_HEADS, nseg, LANES), bf16)],
        compiler_params=_cparams(("parallel",)),
        name="compress",
    )(kv3, kv3, wk1, wk1s, wk2s, wv1, wv1s, wv2s, pos8, kcg2)


def _nsa_kernel(slopes_ref, q3_ref, sm_ref, gsel_ref, wdist_ref, kcd_ref, vcl_ref, ovt_ref,
                ks_ref, vs_ref, kw_ref, vw_ref, y_ref, qa_scr, m_scr, acc_scr, *, bounded):
    g = pl.program_id(1)
    qi = pl.program_id(2)
    q0 = qi * TQ
    ncp = kcd_ref.shape[2]
    nsel = ncp // (SEL_BLOCK // CMP_STRIDE)
    rows = NSA_GROUP * TQ
    log2e = math.log2(math.e)

    lane = lax.broadcasted_iota(jnp.int32, (TQ, LANES), 1)
    lo = lane < HALF
    slopes = [slopes_ref[g * NSA_GROUP + r] * log2e for r in range(NSA_GROUP)]

    def per_head(x, fn):
        return jnp.concatenate([fn(r, x[r * TQ:(r + 1) * TQ]) for r in range(NSA_GROUP)], axis=0)

    q3 = q3_ref[0, 0].reshape(rows, 2 * LANES)
    q1 = q3[:, 0:LANES]

    cidx = lax.broadcasted_iota(jnp.int32, (TQ, ncp), 1)
    tpos_c = q0 + lax.broadcasted_iota(jnp.int32, (TQ, ncp), 0)
    blk_end = cidx * CMP_STRIDE + (CMP_BLOCK - 1)
    valid_c = jnp.logical_and(tpos_c >= blk_end, cidx < ncp - 1)
    krel_c = (CMP_BLOCK - 1 - q0 + CMP_STRIDE * lax.broadcasted_iota(jnp.int32, (1, ncp), 1)).astype(f32)
    s_c = lax.dot_general(q3, kcd_ref[0, 0], NT, preferred_element_type=f32)
    if bounded:
        rowf = lax.broadcasted_iota(jnp.int32, (TQ, LANES), 0).astype(f32)

        def row_term(r, width):
            return jnp.concatenate([slopes[r] * rowf] * (width // LANES), axis=1)

        s_c = per_head(s_c, lambda r, x: jnp.where(valid_c, x + slopes[r] * krel_c - row_term(r, ncp), NEG))
        e_c = jnp.exp2(s_c)
    else:
        s_c = per_head(s_c, lambda r, x: jnp.where(valid_c, x + slopes[r] * krel_c, NEG))
        e_c = jnp.exp2(s_c - jnp.max(s_c, axis=-1, keepdims=True))
    p_c = e_c / jnp.sum(e_c, axis=-1, keepdims=True)
    p_c = per_head(p_c, lambda r, x: jnp.where(valid_c, x, 0.0))
    p_sum = p_c[0:TQ]
    for r in range(1, NSA_GROUP):
        p_sum = p_sum + p_c[r * TQ:(r + 1) * TQ]
    o_cmp = jnp.dot(p_c.astype(bf16), vcl_ref[0, 0], preferred_element_type=f32)

    band = WINDOW + TQ
    kb = pl.multiple_of(jnp.maximum(q0 - WINDOW, 0), TQ)
    s_w = lax.dot_general(q1, kw_ref[0, 0, pl.ds(kb, band), :], NT, preferred_element_type=f32)
    dist_w = wdist_ref[jnp.minimum(qi, WINDOW // TQ)]
    s_w = per_head(s_w, lambda r, x: x + slopes[r] * dist_w)
    if bounded:
        p_w = jnp.exp2(s_w)
    else:
        p_w = jnp.exp2(s_w - jnp.max(s_w, axis=-1, keepdims=True))
    acc_w = jnp.dot(p_w.astype(bf16), vw_ref[0, 0, pl.ds(kb, band), :], preferred_element_type=f32)
    o_win = acc_w / pltpu.roll(acc_w, HALF, 1)

    imp = lax.dot_general(ovt_ref[...], jnp.concatenate(_split3(p_sum), axis=1), NT,
                          preferred_element_type=f32)[0:nsel]
    jrow = lax.broadcasted_iota(jnp.int32, (nsel, TQ), 0)
    tcol = q0 + lax.broadcasted_iota(jnp.int32, (nsel, TQ), 1)
    forced = jnp.logical_or(jrow == lax.shift_right_logical(tcol, SEL_SHIFT), jrow == 0)
    future = jrow * SEL_BLOCK > tcol
    taken = -3e38
    work = jnp.where(forced, taken, jnp.where(future, -FORCE, imp))
    jrow_f = jrow.astype(f32)
    bias_t = jnp.where(forced, 0.0, NEG)
    for _ in range(min(SEL_TOPN, nsel) - 2):
        best = jnp.max(work, axis=0, keepdims=True)
        first = jnp.min(jnp.where(work == best, jrow_f, float(nsel)), axis=0, keepdims=True)
        hit = jrow_f == first
        bias_t = jnp.where(hit, 0.0, bias_t)
        work = jnp.where(hit, taken, work)
    if nsel < HALF:
        bias_t = jnp.concatenate([bias_t, jnp.full((HALF - nsel, TQ), NEG, f32)], axis=0)
    bias = jnp.concatenate([bias_t, bias_t], axis=0).T

    bias_b = bias.astype(bf16)
    qa_scr[...] = per_head(q1, lambda r, x: jnp.where(lo, x, bias_b))

    m_scr[...] = jnp.full(m_scr.shape, NEG, f32)
    acc_scr[...] = jnp.zeros(acc_scr.shape, f32)

    def scores(kt, causal, row_shift):
        k0 = pl.multiple_of(kt * TK, TK)
        s = lax.dot_general(qa_scr[...], ks_ref[0, 0, pl.ds(k0, TK), :], NT, preferred_element_type=f32)
        krel = (k0 - q0 + lax.broadcasted_iota(jnp.int32, (1, TK), 1)).astype(f32)

        def alibi(r, x):
            x = x + slopes[r] * krel
            return x if row_shift is None else x - row_shift[r]

        if causal:
            ahead = (k0 - q0 + lax.broadcasted_iota(jnp.int32, (TQ, TK), 1)
                     > lax.broadcasted_iota(jnp.int32, (TQ, TK), 0))
            return per_head(s, lambda r, x: jnp.where(ahead, NEG, alibi(r, x))), k0
        return per_head(s, alibi), k0

    def online_tile(state, kt, causal):
        m_old, acc = state
        s, k0 = scores(kt, causal, None)
        m_new = jnp.maximum(m_old, jnp.max(s, axis=-1, keepdims=True))
        p = jnp.exp2(s - m_new[:, 0:1])
        acc = jnp.exp2(m_old - m_new) * acc + jnp.dot(p.astype(bf16), vs_ref[0, 0, pl.ds(k0, TK), :],
                                                      preferred_element_type=f32)
        return m_new, acc

    def online_tiles(tiles):
        state = (m_scr[...], acc_scr[...])
        for kt, causal in tiles:
            state = online_tile(state, kt, causal)
        m_scr[...] = state[0]
        acc_scr[...] = state[1]

    def bounded_tiles(tiles):
        row_shift = [row_term(r, TK) for r in range(NSA_GROUP)]
        acc = acc_scr[...]
        for kt, causal in tiles:
            s, k0 = scores(kt, causal, row_shift)
            acc = acc + jnp.dot(jnp.exp2(s).astype(bf16), vs_ref[0, 0, pl.ds(k0, TK), :],
                                preferred_element_type=f32)
        acc_scr[...] = acc

    n_full = q0 // TK

    def selected_branch(run_tiles):
        def body(i, carry):
            run_tiles([(2 * i, False), (2 * i + 1, False)])
            return carry

        lax.fori_loop(0, n_full // 2, body, 0)

        @pl.when(n_full % 2 == 1)
        def _():
            run_tiles([(n_full - 1, False), (n_full, True)])

        @pl.when(n_full % 2 == 0)
        def _():
            run_tiles([(n_full, True)])

    selected_branch(bounded_tiles if bounded else online_tiles)

    acc_s = acc_scr[...]
    o_slc = acc_s / pltpu.roll(acc_s, HALF, 1)

    gates = jnp.dot(jnp.concatenate(_hi_lo(jax.nn.sigmoid(sm_ref[0])), axis=1), gsel_ref[0],
                    preferred_element_type=f32)

    def gate(r, branch):
        col = 3 * r + branch
        return gates[:, col * LANES:(col + 1) * LANES]

    def mix(r):
        sl = slice(r * TQ, (r + 1) * TQ)
        return gate(r, 0) * o_cmp[sl] + gate(r, 1) * o_slc[sl] + gate(r, 2) * o_win[sl]

    for c in range(NSA_GROUP // 2):
        y_ref[0, :, c * LANES:(c + 1) * LANES] = jnp.where(lo, mix(2 * c), pltpu.roll(mix(2 * c + 1), HALF, 1))


def _gate_select():
    sel = np.zeros((NSA_KV_HEADS, 2 * LANES, 3 * NSA_GROUP * LANES), np.float32)
    for g in range(NSA_KV_HEADS):
        for c in range(3 * NSA_GROUP):
            src = 3 * NSA_GROUP * g + c
            sel[g, [src, LANES + src], c * LANES:(c + 1) * LANES] = 1.0
    return jnp.asarray(sel, dtype=bf16)


def _window_distance():
    i = np.arange(TQ)[None, :, None]
    j = np.arange(WINDOW + TQ)[None, None, :]
    off = -TQ * np.arange(WINDOW // TQ + 1)[:, None, None]
    rel = off + j - i
    return jnp.asarray(np.where((rel <= 0) & (rel > -WINDOW), rel, NEG), dtype=f32)


def _nsa(slopes, q3, sm3, kcd, vcl, ovt, ks, vs, kw, vw, *, bounded):
    b, t = q3.shape[0], q3.shape[3]
    gsel = _gate_select()
    wdist = _window_distance()
    gw = NSA_GROUP * NSA_HEAD_DIM
    ncp = kcd.shape[2]

    def kv_spec(rows):
        return pl.BlockSpec((1, 1, rows, LANES), lambda bi, gi, qi: (bi, gi, 0, 0))

    return pl.pallas_call(
        functools.partial(_nsa_kernel, bounded=bounded),
        grid=(b, NSA_KV_HEADS, t // TQ),
        in_specs=[
            pl.BlockSpec(memory_space=pltpu.SMEM),
            pl.BlockSpec((1, 1, NSA_GROUP, TQ, 2 * LANES), lambda bi, gi, qi: (bi, gi, 0, qi, 0)),
            pl.BlockSpec((1, TQ, LANES), lambda bi, gi, qi: (bi, qi, 0)),
            pl.BlockSpec((1,) + gsel.shape[1:], lambda bi, gi, qi: (gi, 0, 0)),
            pl.BlockSpec(wdist.shape, lambda bi, gi, qi: (0, 0, 0)),
            pl.BlockSpec((1, 1, ncp, 2 * LANES), lambda bi, gi, qi: (bi, gi, 0, 0)), kv_spec(ncp),
            pl.BlockSpec(ovt.shape, lambda bi, gi, qi: (0, 0)),
            kv_spec(t), kv_spec(t), kv_spec(t), kv_spec(t),
        ],
        out_specs=pl.BlockSpec((1, TQ, gw), lambda bi, gi, qi: (bi, qi, gi)),
        out_shape=jax.ShapeDtypeStruct((b, t, NSA_WIDTH), f32),
        scratch_shapes=[
            pltpu.VMEM((NSA_GROUP * TQ, LANES), bf16),
            pltpu.VMEM((NSA_GROUP * TQ, LANES), f32),
            pltpu.VMEM((NSA_GROUP * TQ, LANES), f32),
        ],
        compiler_params=_cparams(("parallel", "parallel", "arbitrary")),
        name="nsa",
    )(slopes, q3, sm3, gsel, wdist, kcd, vcl, ovt, ks, vs, kw, vw)


def _log_sigmoid(x):
    return jnp.minimum(x, 0.0) - jnp.log1p(jnp.exp(-jnp.abs(x)))


def _mlstm_kernel(bi_ref, bf_ref, u_ref, v_ref, op_ref, gi_ref, gf_ref, cw_ref, cb_ref, wq_ref, wk_ref,
                  ng_ref, sk_ref, y_ref, uc_scr, q_scr, kt_scr, ct_scr, m_scr, b_scr, li_scr, xp_scr):
    h = pl.program_id(1)
    t = u_ref.shape[1]
    L = MLSTM_L
    dm = MLSTM_HEAD_DIM

    x = u_ref[0]
    xp_scr[0:SUBLANES, :] = jnp.zeros((SUBLANES, dm), f32)
    xp_scr[SUBLANES:, :] = x
    acc = x * cw_ref[CONV_WIDTH - 1:CONV_WIDTH, :]
    for s in range(1, CONV_WIDTH):
        acc = acc + xp_scr[SUBLANES - s:SUBLANES - s + t, :] * cw_ref[CONV_WIDTH - 1 - s:CONV_WIDTH - s, :]
    uc = acc + cb_ref[...]
    uc = uc * jax.nn.sigmoid(uc)
    uc_scr[...] = uc
    ucb = uc.astype(bf16)
    q_scr[...] = jnp.dot(ucb, wq_ref[0].astype(bf16), preferred_element_type=f32).astype(bf16)
    k = jnp.dot(ucb, wk_ref[0].astype(bf16), preferred_element_type=f32) * (dm ** -0.5)
    kt_scr[...] = k.T

    ct_scr[...] = jnp.zeros(ct_scr.shape, f32)
    m_scr[...] = jnp.zeros(m_scr.shape, f32)

    li_ = lax.broadcasted_iota(jnp.int32, (L, L), 0)
    si_ = lax.broadcasted_iota(jnp.int32, (L, L), 1)
    causal = si_ <= li_
    diag = si_ == li_
    ones_v = jnp.ones((L, dm), f32)
    ones_sq = jnp.ones((dm, dm), bf16)

    def wide(x):
        return jnp.concatenate([x] * (L // dm), axis=1)

    log_f = _log_sigmoid(gf_ref[0, 0] + bf_ref[h])
    upper = jnp.where(li_ <= si_, 1.0, 0.0).astype(bf16)
    b_scr[...] = jnp.dot(jnp.concatenate(_split3(log_f), axis=1), jnp.concatenate([upper] * 3, axis=0),
                         preferred_element_type=f32)
    li_scr[...] = gi_ref[0, 0] + bi_ref[h]

    def chunk(c, ct, m_prev):
        r0 = pl.multiple_of(c * L, L)
        qc = q_scr[pl.ds(r0, L), :]
        ktc = kt_scr[:, pl.ds(r0, L)]
        vaug = jnp.concatenate([v_ref[0, pl.ds(r0, L), :], ones_v], axis=1).astype(bf16)
        log_i = li_scr[pl.ds(c, 1), :]
        b_row = b_scr[pl.ds(c, 1), :]
        b_col = jnp.broadcast_to(jnp.sum(jnp.where(diag, b_row, 0.0), axis=-1, keepdims=True), (L, dm))
        g_sum = b_row[:, L - 1:L]
        dmat = jnp.where(causal, (wide(b_col) - b_row) + log_i, NEG)
        m_loc = jnp.broadcast_to(jnp.max(dmat, axis=-1, keepdims=True), (L, dm))
        p = jnp.dot(qc, ktc.astype(bf16), preferred_element_type=f32) * jnp.exp(dmat - wide(m_loc))
        intra = jnp.dot(p.astype(bf16), vaug, preferred_element_type=f32)
        m_inter = b_col + m_prev
        m_out = jnp.maximum(m_inter, m_loc)
        xo = (wide(jnp.exp(m_inter - m_out)) * jnp.dot(qc, ct.astype(bf16), preferred_element_type=f32)
              + wide(jnp.exp(m_loc - m_out)) * intra)
        num = xo[:, 0:dm]
        den = xo[:, dm:2 * dm]
        hh = num / jnp.maximum(jnp.abs(den), jnp.exp(-m_out))
        hh = hh * jax.nn.sigmoid(op_ref[0, pl.ds(r0, L), :])
        ssq = jnp.dot((hh * hh).astype(bf16), ones_sq, preferred_element_type=f32)
        hh = hh * lax.rsqrt(ssq * (1.0 / dm) + NORM_EPS) * ng_ref[0]
        y_ref[0, pl.ds(r0, L), :] = hh + sk_ref[...] * uc_scr[pl.ds(r0, L), :]

        w_end = (g_sum - b_row) + log_i
        m_new = jnp.maximum(g_sum + m_prev, jnp.max(w_end, axis=-1, keepdims=True))
        decay = jnp.exp(g_sum + m_prev - m_new)
        w = jnp.exp(w_end - m_new)
        return decay * ct + jnp.dot((ktc * w).astype(bf16), vaug, preferred_element_type=f32), m_new

    def chunk_group(i, carry):
        ct, m_prev = ct_scr[...], m_scr[0:1, 0:1]
        for j in range(MLSTM_UNROLL):
            ct, m_prev = chunk(i * MLSTM_UNROLL + j, ct, m_prev)
        ct_scr[...] = ct
        m_scr[...] = jnp.broadcast_to(m_prev, m_scr.shape)
        return carry

    lax.fori_loop(0, t // (L * MLSTM_UNROLL), chunk_group, 0)


def _mlstm(b_i, b_f, u3, v3, op3, gi4, gf4, cw, cb2, wq, wk, ng3, sk2):
    b, t, _ = u3.shape
    dm = MLSTM_HEAD_DIM
    seq = pl.BlockSpec((1, t, dm), lambda bi, hi: (bi, 0, hi))
    gate = pl.BlockSpec((1, 1, t // MLSTM_L, MLSTM_L), lambda bi, hi: (bi, hi, 0, 0))
    smem = pl.BlockSpec(memory_space=pltpu.SMEM)
    return pl.pallas_call(
        _mlstm_kernel,
        grid=(b, MLSTM_HEADS),
        in_specs=[
            smem, smem, seq, seq, seq, gate, gate,
            pl.BlockSpec((CONV_WIDTH, dm), lambda bi, hi: (0, hi)),
            pl.BlockSpec((1, dm), lambda bi, hi: (0, hi)),
            pl.BlockSpec((1, dm, dm), lambda bi, hi: (hi, 0, 0)),
            pl.BlockSpec((1, dm, dm), lambda bi, hi: (hi, 0, 0)),
            pl.BlockSpec((1, 1, dm), lambda bi, hi: (hi, 0, 0)),
            pl.BlockSpec((1, dm), lambda bi, hi: (0, hi)),
        ],
        out_specs=seq,
        out_shape=jax.ShapeDtypeStruct((b, t, MLSTM_WIDTH), f32),
        scratch_shapes=[
            pltpu.VMEM((t, dm), f32),
            pltpu.VMEM((t, dm), bf16),
            pltpu.VMEM((dm, t), f32),
            pltpu.VMEM((dm, 2 * dm), f32),
            pltpu.VMEM((SUBLANES, LANES), f32),
            pltpu.VMEM((t // MLSTM_L, MLSTM_L), f32),
            pltpu.VMEM((t // MLSTM_L, MLSTM_L), f32),
            pltpu.VMEM((t + SUBLANES, dm), f32),
        ],
        compiler_params=_cparams(("parallel", "parallel")),
        name="mlstm",
    )(b_i, b_f, u3, v3, op3, gi4, gf4, cw, cb2, wq, wk, ng3, sk2)


def _ffn_kernel(x_ref, ya_ref, yb_ref, wo_ref, g2_ref, wg_ref, wu_ref, wd_ref, o_ref, act_scr):
    x1 = (x_ref[...]
          + jnp.dot(ya_ref[...].astype(bf16), wo_ref[0:NSA_WIDTH, :], preferred_element_type=f32)
          + jnp.dot(yb_ref[...].astype(bf16), wo_ref[NSA_WIDTH:NSA_WIDTH + MLSTM_WIDTH, :],
                    preferred_element_type=f32))
    h2 = x1 * lax.rsqrt(jnp.mean(x1 * x1, axis=-1, keepdims=True) + NORM_EPS) * g2_ref[...]
    h2b = h2.astype(bf16)
    for c in range(D_FF // FF_CHUNK):
        cols = slice(c * FF_CHUNK, (c + 1) * FF_CHUNK)
        gt = jnp.dot(h2b, wg_ref[:, cols], preferred_element_type=f32)
        up = jnp.dot(h2b, wu_ref[:, cols], preferred_element_type=f32)
        act_scr[:, cols] = (gt * jax.nn.sigmoid(gt) * up).astype(bf16)
    o_ref[...] = x1 + jnp.dot(act_scr[...], wd_ref[...], preferred_element_type=f32)


def _ffn(x2, ya, yb, wo, g2, wg, wu, wd):
    n = x2.shape[0]
    tm = TM_FFN

    def const(shape):
        return pl.BlockSpec(shape, lambda i: (0, 0), pipeline_mode=pl.Buffered(1))

    return pl.pallas_call(
        _ffn_kernel,
        grid=(n // tm,),
        in_specs=[
            pl.BlockSpec((tm, D_MODEL), lambda i: (i, 0)),
            pl.BlockSpec((tm, NSA_WIDTH), lambda i: (i, 0)),
            pl.BlockSpec((tm, MLSTM_WIDTH), lambda i: (i, 0)),
            const(wo.shape), const(g2.shape), const(wg.shape), const(wu.shape), const(wd.shape),
        ],
        out_specs=pl.BlockSpec((tm, D_MODEL), lambda i: (i, 0)),
        out_shape=jax.ShapeDtypeStruct((n, D_MODEL), f32),
        scratch_shapes=[pltpu.VMEM((tm, D_FF), bf16)],
        compiler_params=_cparams(("parallel",)),
        name="ffn",
    )(x2, ya, yb, wo, g2, wg, wu, wd)


def _overlap_t(ncp):
    nsel = ncp // (SEL_BLOCK // CMP_STRIDE)
    cs = np.arange(ncp) * CMP_STRIDE
    ss = np.arange(nsel) * SEL_BLOCK
    ov = ((cs[None, :] < ss[:, None] + SEL_BLOCK) & (cs[None, :] + CMP_BLOCK > ss[:, None])).astype(np.float32)
    ov[:, ncp - 1] = 0.0
    out = np.zeros((LANES, ncp), np.float32)
    out[:nsel] = ov
    return jnp.asarray(np.tile(out, (1, 3)), dtype=bf16)


def _layer(x, norm1_g, w_in, q_g, kc_g, ks_g, kw_g, cmp_pos, w_ck1, w_ck2, w_cv1, w_cv2, conv_w, conv_b,
           w_mq, w_mk, b_i, b_f, mlstm_norm_g, mlstm_skip, w_out, norm2_g, w_gate, w_up, w_down):
    b, t, d = x.shape
    n = b * t
    x2 = x.reshape(n, d)

    o_gate = NSA_WIDTH + 6 * KV_WIDTH
    o_u = o_gate + 3 * NSA_HEADS
    o_if = o_u + 3 * MLSTM_WIDTH
    w_perm = jnp.concatenate([
        w_in[:, :o_gate], w_in[:, o_u:o_if], w_in[:, o_gate:o_u], w_in[:, o_if:],
        jnp.zeros((d, LANES - 3 * NSA_HEADS - 2 * MLSTM_HEADS), w_in.dtype)], axis=1).astype(bf16)
    w_gates_t = jnp.concatenate([w_in[:, o_if:].T, jnp.zeros((GATE_ROWS - 2 * MLSTM_HEADS, d), w_in.dtype)],
                                axis=0).astype(bf16)
    kvc2, u2, vm2, op2, sm2, gates_t, ks, vs, kw, vw, q3 = _inproj(
        x2, norm1_g.reshape(1, d), w_perm, w_gates_t, jnp.tile(ks_g, 2).reshape(1, LANES),
        jnp.tile(kw_g, 2).reshape(1, LANES), jnp.tile(q_g, NSA_HEADS).reshape(1, NSA_WIDTH), b, t)

    kv3 = kvc2.reshape(b, t, 2 * KV_WIDTH)
    nseg = t // CMP_STRIDE
    pos8 = jnp.broadcast_to(cmp_pos.reshape(1, CMP_BLOCK * NSA_HEAD_DIM), (SUBLANES, CMP_BLOCK * NSA_HEAD_DIM))
    wk1s, wk2s = _compress_weights(w_ck1, w_ck2)
    wv1s, wv2s = _compress_weights(w_cv1, w_cv2)
    kcd, vcl = _compress(kv3, w_ck1, wk1s, wk2s, w_cv1, wv1s, wv2s, pos8, jnp.tile(kc_g, 2).reshape(1, LANES))
    slopes = jnp.exp2(-8.0 * (jnp.arange(NSA_HEADS, dtype=f32) + 1.0) / NSA_HEADS)
    k_gain = jnp.maximum(jnp.max(jnp.abs(kc_g)), jnp.maximum(jnp.max(jnp.abs(ks_g)), jnp.max(jnp.abs(kw_g))))
    score_cap = (NSA_HEAD_DIM ** 0.5 * math.log2(math.e)) * jnp.max(jnp.abs(q_g)) * k_gain
    nsa_args = (slopes, q3, sm2.reshape(b, t, LANES), kcd, vcl, _overlap_t(nseg), ks, vs, kw, vw)
    y_nsa = lax.cond(score_cap <= SCORE_BOUND, functools.partial(_nsa, bounded=True),
                     functools.partial(_nsa, bounded=False), *nsa_args)

    gi4 = gates_t[0:MLSTM_HEADS].reshape(MLSTM_HEADS, b, t).transpose(1, 0, 2)
    gf4 = gates_t[MLSTM_HEADS:2 * MLSTM_HEADS].reshape(MLSTM_HEADS, b, t).transpose(1, 0, 2)
    y_mem = _mlstm(b_i, b_f, u2.reshape(b, t, MLSTM_WIDTH), vm2.reshape(b, t, MLSTM_WIDTH),
                   op2.reshape(b, t, MLSTM_WIDTH), gi4.reshape(b, MLSTM_HEADS, t // MLSTM_L, MLSTM_L),
                   gf4.reshape(b, MLSTM_HEADS, t // MLSTM_L, MLSTM_L), conv_w, conv_b.reshape(1, MLSTM_WIDTH),
                   w_mq, w_mk,
                   mlstm_norm_g.reshape(MLSTM_HEADS, 1, MLSTM_HEAD_DIM), mlstm_skip.reshape(1, MLSTM_WIDTH))

    out = _ffn(x2, y_nsa.reshape(n, NSA_WIDTH), y_mem.reshape(n, MLSTM_WIDTH), w_out.astype(bf16),
               norm2_g.reshape(1, d), w_gate.astype(bf16), w_up.astype(bf16), w_down.astype(bf16))
    return out.reshape(b, t, d)


def kernel(x, norm1_g, w_in, q_norm_g, kc_norm_g, ks_norm_g, kw_norm_g, cmp_pos, w_ck1, w_ck2, w_cv1, w_cv2,
           conv_w, conv_b, w_mq, w_mk, b_i, b_f, mlstm_norm_g, mlstm_skip, w_out, norm2_g, w_gate, w_up, w_down):
    depth = norm1_g.shape[0]
    for l in range(depth):
        x = _layer(x, norm1_g[l], w_in[l], q_norm_g[l], kc_norm_g[l], ks_norm_g[l], kw_norm_g[l], cmp_pos[l],
                   w_ck1[l], w_ck2[l], w_cv1[l], w_cv2[l], conv_w[l], conv_b[l], w_mq[l], w_mk[l], b_i[l], b_f[l],
                   mlstm_norm_g[l], mlstm_skip[l], w_out[l], norm2_g[l], w_gate[l], w_up[l], w_down[l])
    return x
```

```python
import functools
import math

import numpy as np
import jax
import jax.numpy as jnp
from jax import lax
from jax.experimental import pallas as pl
from jax.experimental.pallas import tpu as pltpu

f32 = jnp.float32
bf16 = jnp.bfloat16

D_MODEL = 1024
NSA_HEADS = 8
NSA_KV_HEADS = 2
NSA_HEAD_DIM = 64
NSA_GROUP = NSA_HEADS // NSA_KV_HEADS
CMP_BLOCK = 32
CMP_STRIDE = 16
CMP_HIDDEN = 256
SEL_BLOCK = 64
SEL_TOPN = 16
WINDOW = 512
MLSTM_HEADS = 4
MLSTM_HEAD_DIM = 128
CONV_WIDTH = 4
NSA_WIDTH = NSA_HEADS * NSA_HEAD_DIM
MLSTM_WIDTH = MLSTM_HEADS * MLSTM_HEAD_DIM
KV_WIDTH = NSA_KV_HEADS * NSA_HEAD_DIM
D_FF = -(-8 * D_MODEL // (3 * 256)) * 256
NORM_EPS = 1e-6
NEG = -1e30
FORCE = 1e9
SCORE_BOUND = 100.0

LANES = 128
SUBLANES = 8
HALF = 64
SEL_SHIFT = SEL_BLOCK.bit_length() - 1
assert 1 << SEL_SHIFT == SEL_BLOCK and SEL_BLOCK == HALF
VMEM_LIMIT = 56 * 1024 * 1024

TM_PROJ = 1024
TQ = 512
TK = 512
MLSTM_L = 256
MLSTM_UNROLL = 4
TM_FFN = 1024
FF_CHUNK = 256

HI = lax.Precision.HIGHEST
NT = (((1,), (1,)), ((), ()))


def _cparams(sem):
    return pltpu.CompilerParams(dimension_semantics=sem, vmem_limit_bytes=VMEM_LIMIT)


def _split3(x):
    x1 = x.astype(bf16)
    r1 = x - x1.astype(f32)
    x2 = r1.astype(bf16)
    x3 = (r1 - x2.astype(f32)).astype(bf16)
    return x1, x2, x3


COL_Q = 0
COL_KV = COL_Q + NSA_WIDTH
COL_U = COL_KV + 6 * KV_WIDTH
COL_VM = COL_U + MLSTM_WIDTH
COL_OP = COL_VM + MLSTM_WIDTH
COL_SM = COL_OP + MLSTM_WIDTH
PROJ_WIDTH = COL_SM + LANES
GATE_ROWS = 16


def _pair_norm(x, gain, lo):
    sq = x * x
    s_lo = jnp.sum(jnp.where(lo, sq, 0.0), axis=-1, keepdims=True)
    s_hi = jnp.sum(jnp.where(lo, 0.0, sq), axis=-1, keepdims=True)
    inv = jnp.where(lo, lax.rsqrt(s_lo / HALF + NORM_EPS), lax.rsqrt(s_hi / HALF + NORM_EPS))
    return x * inv * gain


def _inproj_kernel(x_ref, g_ref, w_ref, wgt_ref, ksg_ref, kwg_ref, qg_ref,
                   kvc_ref, u_ref, vm_ref, op_ref, sm_ref, gt_ref, ks_ref, vs_ref, kw_ref, vw_ref, q3_ref,
                   *, tiles_per_seq):
    tm = x_ref.shape[0]
    x = x_ref[...]
    h = x * lax.rsqrt(jnp.mean(x * x, axis=-1, keepdims=True) + NORM_EPS) * g_ref[...]
    hb = h.astype(bf16)

    def proj(col, width):
        return jnp.dot(hb, w_ref[:, col:col + width], preferred_element_type=f32)

    lane = lax.broadcasted_iota(jnp.int32, (tm, LANES), 1)
    lo = lane < HALF
    pos = (pl.program_id(0) % tiles_per_seq) * tm + lax.broadcasted_iota(jnp.int32, (tm, LANES), 0)

    q = proj(COL_Q, NSA_WIDTH)
    scale = NSA_HEAD_DIM ** -0.5 * math.log2(math.e)
    for c in range(NSA_HEADS // 2):
        pair = _pair_norm(q[:, c * LANES:(c + 1) * LANES], qg_ref[:, c * LANES:(c + 1) * LANES], lo) * scale
        for par, xq in enumerate((pair, pltpu.roll(pair, HALF, 1))):
            hi = jnp.where(lo, xq, 0.0).astype(bf16).astype(f32)
            res = jnp.where(lo, xq - hi, 0.0)
            head = 2 * c + par
            q3_ref[0, head // NSA_GROUP, head % NSA_GROUP] = jnp.concatenate(
                [hi + pltpu.roll(res, HALF, 1), hi], axis=1).astype(bf16)

    kv = proj(COL_KV, 6 * KV_WIDTH)
    kvc_ref[...] = kv[:, 0:2 * KV_WIDTH]
    code = jnp.where(lax.shift_right_logical(pos, SEL_SHIFT) == (lane & (HALF - 1)), 1.0, 0.0)

    def put(xk, o_ref, fill):
        o_ref[0, 0] = jnp.where(lo, xk, fill).astype(bf16)
        o_ref[0, 1] = jnp.where(lo, pltpu.roll(xk, HALF, 1), fill).astype(bf16)

    put(_pair_norm(kv[:, 2 * KV_WIDTH:3 * KV_WIDTH], ksg_ref[...], lo), ks_ref, code)
    put(kv[:, 3 * KV_WIDTH:4 * KV_WIDTH], vs_ref, 1.0)
    put(_pair_norm(kv[:, 4 * KV_WIDTH:5 * KV_WIDTH], kwg_ref[...], lo), kw_ref, 0.0)
    put(kv[:, 5 * KV_WIDTH:6 * KV_WIDTH], vw_ref, 1.0)

    u_ref[...] = proj(COL_U, MLSTM_WIDTH)
    vm_ref[...] = proj(COL_VM, MLSTM_WIDTH)
    op_ref[...] = proj(COL_OP, MLSTM_WIDTH)
    sm_ref[...] = proj(COL_SM, LANES)
    gt_ref[...] = lax.dot_general(wgt_ref[...], hb, NT, preferred_element_type=f32)


def _inproj(x2, g1, w_perm, w_gates_t, ksg2, kwg2, qg2, b, t):
    n = x2.shape[0]
    tm = min(TM_PROJ, t)
    nt = t // tm

    def rows(width):
        return pl.BlockSpec((tm, width), lambda i: (i, 0))

    def const(shape):
        return pl.BlockSpec(shape, lambda i: (0, 0))

    kv_spec = pl.BlockSpec((1, NSA_KV_HEADS, tm, LANES), lambda i: (i // nt, 0, i % nt, 0))
    kv_shape = jax.ShapeDtypeStruct((b, NSA_KV_HEADS, t, LANES), bf16)
    return pl.pallas_call(
        functools.partial(_inproj_kernel, tiles_per_seq=nt),
        grid=(n // tm,),
        in_specs=[rows(D_MODEL), const((1, D_MODEL)), const((D_MODEL, PROJ_WIDTH)), const((GATE_ROWS, D_MODEL)),
                  const((1, LANES)), const((1, LANES)), const((1, NSA_WIDTH))],
        out_specs=[rows(2 * KV_WIDTH), rows(MLSTM_WIDTH), rows(MLSTM_WIDTH), rows(MLSTM_WIDTH), rows(LANES),
                   pl.BlockSpec((GATE_ROWS, tm), lambda i: (0, i)),
                   kv_spec, kv_spec, kv_spec, kv_spec,
                   pl.BlockSpec((1, NSA_KV_HEADS, NSA_GROUP, tm, 2 * LANES),
                                lambda i: (i // nt, 0, 0, i % nt, 0))],
        out_shape=[jax.ShapeDtypeStruct((n, 2 * KV_WIDTH), f32), jax.ShapeDtypeStruct((n, MLSTM_WIDTH), f32),
                   jax.ShapeDtypeStruct((n, MLSTM_WIDTH), f32), jax.ShapeDtypeStruct((n, MLSTM_WIDTH), f32),
                   jax.ShapeDtypeStruct((n, LANES), f32), jax.ShapeDtypeStruct((GATE_ROWS, n), f32),
                   kv_shape, kv_shape, kv_shape, kv_shape,
                   jax.ShapeDtypeStruct((b, NSA_KV_HEADS, NSA_GROUP, t, 2 * LANES), bf16)],
        compiler_params=_cparams(("parallel",)),
        name="inproj",
    )(x2, g1, w_perm, w_gates_t, ksg2, kwg2, qg2)


def _gelu_tanh(x):
    return 0.5 * x * (1.0 + jnp.tanh(math.sqrt(2.0 / math.pi) * (x + 0.044715 * (x * x * x))))


def _hi_lo(x):
    hi = x.astype(bf16)
    return hi, (x - hi.astype(f32)).astype(bf16)


def _compress_kernel(kc_ref, vc_ref, wk1_ref, wk1s_ref, wk2s_ref, wv1_ref, wv1s_ref, wv2s_ref, pos_ref, kcg_ref,
                     kcd_ref, vcl_ref):
    nseg = kc_ref.shape[1] // CMP_STRIDE
    half = NSA_KV_HEADS * CMP_HIDDEN

    def branch(x_ref, w1_ref, w1s_ref, w2s_ref):
        terms = []
        for r in range(CMP_STRIDE):
            hi, lo = _hi_lo(x_ref[0, pl.ds(r, nseg, stride=CMP_STRIDE), :])
            terms += [hi, lo, hi]
        acc = jnp.dot(jnp.concatenate(terms, axis=1), w1s_ref[...], preferred_element_type=f32)
        pos_term = jnp.dot(pos_ref[...], w1_ref[...], precision=HI, preferred_element_type=f32)[0:1]
        hid = (acc[:, 0:half] + pltpu.roll(acc[:, half:2 * half], nseg - 1, 0)
               + jnp.concatenate([pos_term] * NSA_KV_HEADS, axis=1))
        hi, lo = _hi_lo(_gelu_tanh(hid))
        return jnp.dot(jnp.concatenate([hi, lo, hi], axis=1), w2s_ref[...], preferred_element_type=f32)

    kc2 = branch(kc_ref, wk1_ref, wk1s_ref, wk2s_ref)
    vc2 = branch(vc_ref, wv1_ref, wv1s_ref, wv2s_ref)
    lane = lax.broadcasted_iota(jnp.int32, (nseg, LANES), 1)
    for g in range(NSA_KV_HEADS):
        kc = kc2[:, g * LANES:(g + 1) * LANES]
        kc = kc * lax.rsqrt(jnp.mean(kc * kc, axis=-1, keepdims=True) + NORM_EPS) * kcg_ref[...]
        k_hi = kc.astype(bf16)
        k_lo = jnp.where(lane < HALF, kc - k_hi.astype(f32), 0.0).astype(bf16)
        kcd_ref[0, g] = jnp.concatenate([k_hi, k_lo], axis=1)
        vcl_ref[0, g] = jnp.where(lane < HALF, vc2[:, g * LANES:(g + 1) * LANES], 0.0).astype(bf16)


def _compress_weights(w1, w2):
    w1r = w1.reshape(2, CMP_STRIDE, NSA_HEAD_DIM, CMP_HIDDEN)
    z = jnp.zeros_like(w1r[0])

    def two_groups(w):
        return jnp.concatenate([jnp.concatenate([w, z], axis=2), jnp.concatenate([z, w], axis=2)], axis=1)

    wr = jnp.concatenate([two_groups(w1r[0]), two_groups(w1r[1])], axis=2)
    hi, lo = _hi_lo(wr)
    w1s = jnp.concatenate([hi, hi, lo], axis=1).reshape(CMP_STRIDE * 3 * LANES, 2 * NSA_KV_HEADS * CMP_HIDDEN)
    z2 = jnp.zeros_like(w2)
    w2d = jnp.concatenate([jnp.concatenate([w2, w2, z2, z2], axis=1),
                           jnp.concatenate([z2, z2, w2, w2], axis=1)], axis=0)
    hi2, lo2 = _hi_lo(w2d)
    return w1s, jnp.concatenate([hi2, hi2, lo2], axis=0)


def _compress(kv3, wk1, wk1s, wk2s, wv1, wv1s, wv2s, pos8, kcg2):
    b, t, _ = kv3.shape
    nseg = t // CMP_STRIDE

    def const(shape):
        return pl.BlockSpec(shape, lambda bi: (0,) * len(shape), pipeline_mode=pl.Buffered(1))

    return pl.pallas_call(
        _compress_kernel,
        grid=(b,),
        in_specs=[pl.BlockSpec((1, t, KV_WIDTH), lambda bi: (bi, 0, 0)),
                  pl.BlockSpec((1, t, KV_WIDTH), lambda bi: (bi, 0, 1)),
                  const(wk1.shape), const(wk1s.shape), const(wk2s.shape),
                  const(wv1.shape), const(wv1s.shape), const(wv2s.shape),
                  const(pos8.shape), const(kcg2.shape)],
        out_specs=[pl.BlockSpec((1, NSA_KV_HEADS, nseg, 2 * LANES), lambda bi: (bi, 0, 0, 0)),
                   pl.BlockSpec((1, NSA_KV_HEADS, nseg, LANES), lambda bi: (bi, 0, 0, 0))],
        out_shape=[jax.ShapeDtypeStruct((b, NSA_KV_HEADS, nseg, 2 * LANES), bf16),
                   jax.ShapeDtypeStruct((b, NSA_KV_HEADS, nseg, LANES), bf16)],
        compiler_params=_cparams(("parallel",)),
        name="compress",
    )(kv3, kv3, wk1, wk1s, wk2s, wv1, wv1s, wv2s, pos8, kcg2)


def _nsa_kernel(slopes_ref, q3_ref, sm_ref, gsel_ref, wdist_ref, kcd_ref, vcl_ref, ovt_ref,
                ks_ref, vs_ref, kw_ref, vw_ref, y_ref, qa_scr, m_scr, acc_scr, *, bounded):
    g = pl.program_id(1)
    qi = pl.program_id(2)
    q0 = qi * TQ
    ncp = kcd_ref.shape[2]
    nsel = ncp // (SEL_BLOCK // CMP_STRIDE)
    rows = NSA_GROUP * TQ
    log2e = math.log2(math.e)

    lane = lax.broadcasted_iota(jnp.int32, (TQ, LANES), 1)
    lo = lane < HALF
    slopes = [slopes_ref[g * NSA_GROUP + r] * log2e for r in range(NSA_GROUP)]

    def per_head(x, fn):
        return jnp.concatenate([fn(r, x[r * TQ:(r + 1) * TQ]) for r in range(NSA_GROUP)], axis=0)

    q3 = q3_ref[0, 0].reshape(rows, 2 * LANES)
    q1 = q3[:, 0:LANES]

    cidx = lax.broadcasted_iota(jnp.int32, (TQ, ncp), 1)
    tpos_c = q0 + lax.broadcasted_iota(jnp.int32, (TQ, ncp), 0)
    blk_end = cidx * CMP_STRIDE + (CMP_BLOCK - 1)
    valid_c = jnp.logical_and(tpos_c >= blk_end, cidx < ncp - 1)
    krel_c = (CMP_BLOCK - 1 - q0 + CMP_STRIDE * lax.broadcasted_iota(jnp.int32, (1, ncp), 1)).astype(f32)
    s_c = lax.dot_general(q3, kcd_ref[0, 0], NT, preferred_element_type=f32)
    if bounded:
        rowf = lax.broadcasted_iota(jnp.int32, (TQ, LANES), 0).astype(f32)

        def row_term(r, width):
            return jnp.concatenate([slopes[r] * rowf] * (width // LANES), axis=1)

        s_c = per_head(s_c, lambda r, x: jnp.where(valid_c, x + slopes[r] * krel_c - row_term(r, ncp), NEG))
        e_c = jnp.exp2(s_c)
    else:
        s_c = per_head(s_c, lambda r, x: jnp.where(valid_c, x + slopes[r] * krel_c, NEG))
        e_c = jnp.exp2(s_c - jnp.max(s_c, axis=-1, keepdims=True))
    p_c = e_c / jnp.sum(e_c, axis=-1, keepdims=True)
    p_c = per_head(p_c, lambda r, x: jnp.where(valid_c, x, 0.0))
    p_sum = p_c[0:TQ]
    for r in range(1, NSA_GROUP):
        p_sum = p_sum + p_c[r * TQ:(r + 1) * TQ]
    o_cmp = jnp.dot(p_c.astype(bf16), vcl_ref[0, 0], preferred_element_type=f32)

    band = WINDOW + TQ
    kb = pl.multiple_of(jnp.maximum(q0 - WINDOW, 0), TQ)
    s_w = lax.dot_general(q1, kw_ref[0, 0, pl.ds(kb, band), :], NT, preferred_element_type=f32)
    dist_w = wdist_ref[jnp.minimum(qi, WINDOW // TQ)]
    s_w = per_head(s_w, lambda r, x: x + slopes[r] * dist_w)
    if bounded:
        p_w = jnp.exp2(s_w)
    else:
        p_w = jnp.exp2(s_w - jnp.max(s_w, axis=-1, keepdims=True))
    acc_w = jnp.dot(p_w.astype(bf16), vw_ref[0, 0, pl.ds(kb, band), :], preferred_element_type=f32)
    o_win = acc_w / pltpu.roll(acc_w, HALF, 1)

    imp = lax.dot_general(ovt_ref[...], jnp.concatenate(_split3(p_sum), axis=1), NT,
                          preferred_element_type=f32)[0:nsel]
    jrow = lax.broadcasted_iota(jnp.int32, (nsel, TQ), 0)
    tcol = q0 + lax.broadcasted_iota(jnp.int32, (nsel, TQ), 1)
    forced = jnp.logical_or(jrow == lax.shift_right_logical(tcol, SEL_SHIFT), jrow == 0)
    future = jrow * SEL_BLOCK > tcol
    taken = -3e38
    work = jnp.where(forced, taken, jnp.where(future, -FORCE, imp))
    jrow_f = jrow.astype(f32)
    bias_t = jnp.where(forced, 0.0, NEG)
    for _ in range(min(SEL_TOPN, nsel) - 2):
        best = jnp.max(work, axis=0, keepdims=True)
        first = jnp.min(jnp.where(work == best, jrow_f, float(nsel)), axis=0, keepdims=True)
        hit = jrow_f == first
        bias_t = jnp.where(hit, 0.0, bias_t)
        work = jnp.where(hit, taken, work)
    if nsel < HALF:
        bias_t = jnp.concatenate([bias_t, jnp.full((HALF - nsel, TQ), NEG, f32)], axis=0)
    bias = jnp.concatenate([bias_t, bias_t], axis=0).T

    bias_b = bias.astype(bf16)
    qa_scr[...] = per_head(q1, lambda r, x: jnp.where(lo, x, bias_b))

    m_scr[...] = jnp.full(m_scr.shape, NEG, f32)
    acc_scr[...] = jnp.zeros(acc_scr.shape, f32)

    def scores(kt, causal, row_shift):
        k0 = pl.multiple_of(kt * TK, TK)
        s = lax.dot_general(qa_scr[...], ks_ref[0, 0, pl.ds(k0, TK), :], NT, preferred_element_type=f32)
        krel = (k0 - q0 + lax.broadcasted_iota(jnp.int32, (1, TK), 1)).astype(f32)

        def alibi(r, x):
            x = x + slopes[r] * krel
            return x if row_shift is None else x - row_shift[r]

        if causal:
            ahead = (k0 - q0 + lax.broadcasted_iota(jnp.int32, (TQ, TK), 1)
                     > lax.broadcasted_iota(jnp.int32, (TQ, TK), 0))
            return per_head(s, lambda r, x: jnp.where(ahead, NEG, alibi(r, x))), k0
        return per_head(s, alibi), k0

    def online_tile(state, kt, causal):
        m_old, acc = state
        s, k0 = scores(kt, causal, None)
        m_new = jnp.maximum(m_old, jnp.max(s, axis=-1, keepdims=True))
        p = jnp.exp2(s - m_new[:, 0:1])
        acc = jnp.exp2(m_old - m_new) * acc + jnp.dot(p.astype(bf16), vs_ref[0, 0, pl.ds(k0, TK), :],
                                                      preferred_element_type=f32)
        return m_new, acc

    def online_tiles(tiles):
        state = (m_scr[...], acc_scr[...])
        for kt, causal in tiles:
            state = online_tile(state, kt, causal)
        m_scr[...] = state[0]
        acc_scr[...] = state[1]

    def bounded_tiles(tiles):
        row_shift = [row_term(r, TK) for r in range(NSA_GROUP)]
        acc = acc_scr[...]
        for kt, causal in tiles:
            s, k0 = scores(kt, causal, row_shift)
            acc = acc + jnp.dot(jnp.exp2(s).astype(bf16), vs_ref[0, 0, pl.ds(k0, TK), :],
                                preferred_element_type=f32)
        acc_scr[...] = acc

    n_full = q0 // TK

    def selected_branch(run_tiles):
        def body(i, carry):
            run_tiles([(2 * i, False), (2 * i + 1, False)])
            return carry

        lax.fori_loop(0, n_full // 2, body, 0)

        @pl.when(n_full % 2 == 1)
        def _():
            run_tiles([(n_full - 1, False), (n_full, True)])

        @pl.when(n_full % 2 == 0)
        def _():
            run_tiles([(n_full, True)])

    selected_branch(bounded_tiles if bounded else online_tiles)

    acc_s = acc_scr[...]
    o_slc = acc_s / pltpu.roll(acc_s, HALF, 1)

    gates = jnp.dot(jnp.concatenate(_hi_lo(jax.nn.sigmoid(sm_ref[0])), axis=1), gsel_ref[0],
                    preferred_element_type=f32)

    def gate(r, branch):
        col = 3 * r + branch
        return gates[:, col * LANES:(col + 1) * LANES]

    def mix(r):
        sl = slice(r * TQ, (r + 1) * TQ)
        return gate(r, 0) * o_cmp[sl] + gate(r, 1) * o_slc[sl] + gate(r, 2) * o_win[sl]

    for c in range(NSA_GROUP // 2):
        y_ref[0, :, c * LANES:(c + 1) * LANES] = jnp.where(lo, mix(2 * c), pltpu.roll(mix(2 * c + 1), HALF, 1))


def _gate_select():
    sel = np.zeros((NSA_KV_HEADS, 2 * LANES, 3 * NSA_GROUP * LANES), np.float32)
    for g in range(NSA_KV_HEADS):
        for c in range(3 * NSA_GROUP):
            src = 3 * NSA_GROUP * g + c
            sel[g, [src, LANES + src], c * LANES:(c + 1) * LANES] = 1.0
    return jnp.asarray(sel, dtype=bf16)


def _window_distance():
    i = np.arange(TQ)[None, :, None]
    j = np.arange(WINDOW + TQ)[None, None, :]
    off = -TQ * np.arange(WINDOW // TQ + 1)[:, None, None]
    rel = off + j - i
    return jnp.asarray(np.where((rel <= 0) & (rel > -WINDOW), rel, NEG), dtype=f32)


def _nsa(slopes, q3, sm3, kcd, vcl, ovt, ks, vs, kw, vw, *, bounded):
    b, t = q3.shape[0], q3.shape[3]
    gsel = _gate_select()
    wdist = _window_distance()
    gw = NSA_GROUP * NSA_HEAD_DIM
    ncp = kcd.shape[2]

    def kv_spec(rows):
        return pl.BlockSpec((1, 1, rows, LANES), lambda bi, gi, qi: (bi, gi, 0, 0))

    return pl.pallas_call(
        functools.partial(_nsa_kernel, bounded=bounded),
        grid=(b, NSA_KV_HEADS, t // TQ),
        in_specs=[
            pl.BlockSpec(memory_space=pltpu.SMEM),
            pl.BlockSpec((1, 1, NSA_GROUP, TQ, 2 * LANES), lambda bi, gi, qi: (bi, gi, 0, qi, 0)),
            pl.BlockSpec((1, TQ, LANES), lambda bi, gi, qi: (bi, qi, 0)),
            pl.BlockSpec((1,) + gsel.shape[1:], lambda bi, gi, qi: (gi, 0, 0)),
            pl.BlockSpec(wdist.shape, lambda bi, gi, qi: (0, 0, 0)),
            pl.BlockSpec((1, 1, ncp, 2 * LANES), lambda bi, gi, qi: (bi, gi, 0, 0)), kv_spec(ncp),
            pl.BlockSpec(ovt.shape, lambda bi, gi, qi: (0, 0)),
            kv_spec(t), kv_spec(t), kv_spec(t), kv_spec(t),
        ],
        out_specs=pl.BlockSpec((1, TQ, gw), lambda bi, gi, qi: (bi, qi, gi)),
        out_shape=jax.ShapeDtypeStruct((b, t, NSA_WIDTH), f32),
        scratch_shapes=[
            pltpu.VMEM((NSA_GROUP * TQ, LANES), bf16),
            pltpu.VMEM((NSA_GROUP * TQ, LANES), f32),
            pltpu.VMEM((NSA_GROUP * TQ, LANES), f32),
        ],
        compiler_params=_cparams(("parallel", "parallel", "arbitrary")),
        name="nsa",
    )(slopes, q3, sm3, gsel, wdist, kcd, vcl, ovt, ks, vs, kw, vw)


def _log_sigmoid(x):
    return jnp.minimum(x, 0.0) - jnp.log1p(jnp.exp(-jnp.abs(x)))


def _mlstm_kernel(bi_ref, bf_ref, u_ref, v_ref, op_ref, gi_ref, gf_ref, cw_ref, cb_ref, wq_ref, wk_ref,
                  ng_ref, sk_ref, y_ref, uc_scr, q_scr, kt_scr, ct_scr, m_scr, b_scr, li_scr, xp_scr):
    h = pl.program_id(1)
    t = u_ref.shape[1]
    L = MLSTM_L
    dm = MLSTM_HEAD_DIM

    x = u_ref[0]
    xp_scr[0:SUBLANES, :] = jnp.zeros((SUBLANES, dm), f32)
    xp_scr[SUBLANES:, :] = x
    acc = x * cw_ref[CONV_WIDTH - 1:CONV_WIDTH, :]
    for s in range(1, CONV_WIDTH):
        acc = acc + xp_scr[SUBLANES - s:SUBLANES - s + t, :] * cw_ref[CONV_WIDTH - 1 - s:CONV_WIDTH - s, :]
    uc = acc + cb_ref[...]
    uc = uc * jax.nn.sigmoid(uc)
    uc_scr[...] = uc
    ucb = uc.astype(bf16)
    q_scr[...] = jnp.dot(ucb, wq_ref[0].astype(bf16), preferred_element_type=f32).astype(bf16)
    k = jnp.dot(ucb, wk_ref[0].astype(bf16), preferred_element_type=f32) * (dm ** -0.5)
    kt_scr[...] = k.T

    ct_scr[...] = jnp.zeros(ct_scr.shape, f32)
    m_scr[...] = jnp.zeros(m_scr.shape, f32)

    li_ = lax.broadcasted_iota(jnp.int32, (L, L), 0)
    si_ = lax.broadcasted_iota(jnp.int32, (L, L), 1)
    causal = si_ <= li_
    diag = si_ == li_
    ones_v = jnp.ones((L, dm), f32)
    ones_sq = jnp.ones((dm, dm), bf16)

    def wide(x):
        return jnp.concatenate([x] * (L // dm), axis=1)

    log_f = _log_sigmoid(gf_ref[0, 0] + bf_ref[h])
    upper = jnp.where(li_ <= si_, 1.0, 0.0).astype(bf16)
    b_scr[...] = jnp.dot(jnp.concatenate(_split3(log_f), axis=1), jnp.concatenate([upper] * 3, axis=0),
                         preferred_element_type=f32)
    li_scr[...] = gi_ref[0, 0] + bi_ref[h]

    def chunk(c, ct, m_prev):
        r0 = pl.multiple_of(c * L, L)
        qc = q_scr[pl.ds(r0, L), :]
        ktc = kt_scr[:, pl.ds(r0, L)]
        vaug = jnp.concatenate([v_ref[0, pl.ds(r0, L), :], ones_v], axis=1).astype(bf16)
        log_i = li_scr[pl.ds(c, 1), :]
        b_row = b_scr[pl.ds(c, 1), :]
        b_col = jnp.broadcast_to(jnp.sum(jnp.where(diag, b_row, 0.0), axis=-1, keepdims=True), (L, dm))
        g_sum = b_row[:, L - 1:L]
        dmat = jnp.where(causal, (wide(b_col) - b_row) + log_i, NEG)
        m_loc = jnp.broadcast_to(jnp.max(dmat, axis=-1, keepdims=True), (L, dm))
        p = jnp.dot(qc, ktc.astype(bf16), preferred_element_type=f32) * jnp.exp(dmat - wide(m_loc))
        intra = jnp.dot(p.astype(bf16), vaug, preferred_element_type=f32)
        m_inter = b_col + m_prev
        m_out = jnp.maximum(m_inter, m_loc)
        xo = (wide(jnp.exp(m_inter - m_out)) * jnp.dot(qc, ct.astype(bf16), preferred_element_type=f32)
              + wide(jnp.exp(m_loc - m_out)) * intra)
        num = xo[:, 0:dm]
        den = xo[:, dm:2 * dm]
        hh = num / jnp.maximum(jnp.abs(den), jnp.exp(-m_out))
        hh = hh * jax.nn.sigmoid(op_ref[0, pl.ds(r0, L), :])
        ssq = jnp.dot((hh * hh).astype(bf16), ones_sq, preferred_element_type=f32)
        hh = hh * lax.rsqrt(ssq * (1.0 / dm) + NORM_EPS) * ng_ref[0]
        y_ref[0, pl.ds(r0, L), :] = hh + sk_ref[...] * uc_scr[pl.ds(r0, L), :]

        w_end = (g_sum - b_row) + log_i
        m_new = jnp.maximum(g_sum + m_prev, jnp.max(w_end, axis=-1, keepdims=True))
        decay = jnp.exp(g_sum + m_prev - m_new)
        w = jnp.exp(w_end - m_new)
        return decay * ct + jnp.dot((ktc * w).astype(bf16), vaug, preferred_element_type=f32), m_new

    def chunk_group(i, carry):
        ct, m_prev = ct_scr[...], m_scr[0:1, 0:1]
        for j in range(MLSTM_UNROLL):
            ct, m_prev = chunk(i * MLSTM_UNROLL + j, ct, m_prev)
        ct_scr[...] = ct
        m_scr[...] = jnp.broadcast_to(m_prev, m_scr.shape)
        return carry

    lax.fori_loop(0, t // (L * MLSTM_UNROLL), chunk_group, 0)


def _mlstm(b_i, b_f, u3, v3, op3, gi4, gf4, cw, cb2, wq, wk, ng3, sk2):
    b, t, _ = u3.shape
    dm = MLSTM_HEAD_DIM
    seq = pl.BlockSpec((1, t, dm), lambda bi, hi: (bi, 0, hi))
    gate = pl.BlockSpec((1, 1, t // MLSTM_L, MLSTM_L), lambda bi, hi: (bi, hi, 0, 0))
    smem = pl.BlockSpec(memory_space=pltpu.SMEM)
    return pl.pallas_call(
        _mlstm_kernel,
        grid=(b, MLSTM_HEADS),
        in_specs=[
            smem, smem, seq, seq, seq, gate, gate,
            pl.BlockSpec((CONV_WIDTH, dm), lambda bi, hi: (0, hi)),
            pl.BlockSpec((1, dm), lambda bi, hi: (0, hi)),
            pl.BlockSpec((1, dm, dm), lambda bi, hi: (hi, 0, 0)),
            pl.BlockSpec((1, dm, dm), lambda bi, hi: (hi, 0, 0)),
            pl.BlockSpec((1, 1, dm), lambda bi, hi: (hi, 0, 0)),
            pl.BlockSpec((1, dm), lambda bi, hi: (0, hi)),
        ],
        out_specs=seq,
        out_shape=jax.ShapeDtypeStruct((b, t, MLSTM_WIDTH), f32),
        scratch_shapes=[
            pltpu.VMEM((t, dm), f32),
            pltpu.VMEM((t, dm), bf16),
            pltpu.VMEM((dm, t), f32),
            pltpu.VMEM((dm, 2 * dm), f32),
            pltpu.VMEM((SUBLANES, LANES), f32),
            pltpu.VMEM((t // MLSTM_L, MLSTM_L), f32),
            pltpu.VMEM((t // MLSTM_L, MLSTM_L), f32),
            pltpu.VMEM((t + SUBLANES, dm), f32),
        ],
        compiler_params=_cparams(("parallel", "parallel")),
        name="mlstm",
    )(b_i, b_f, u3, v3, op3, gi4, gf4, cw, cb2, wq, wk, ng3, sk2)


def _ffn_kernel(x_ref, ya_ref, yb_ref, wo_ref, g2_ref, wg_ref, wu_ref, wd_ref, o_ref, act_scr):
    x1 = (x_ref[...]
          + jnp.dot(ya_ref[...].astype(bf16), wo_ref[0:NSA_WIDTH, :], preferred_element_type=f32)
          + jnp.dot(yb_ref[...].astype(bf16), wo_ref[NSA_WIDTH:NSA_WIDTH + MLSTM_WIDTH, :],
                    preferred_element_type=f32))
    h2 = x1 * lax.rsqrt(jnp.mean(x1 * x1, axis=-1, keepdims=True) + NORM_EPS) * g2_ref[...]
    h2b = h2.astype(bf16)
    for c in range(D_FF // FF_CHUNK):
        cols = slice(c * FF_CHUNK, (c + 1) * FF_CHUNK)
        gt = jnp.dot(h2b, wg_ref[:, cols], preferred_element_type=f32)
        up = jnp.dot(h2b, wu_ref[:, cols], preferred_element_type=f32)
        act_scr[:, cols] = (gt * jax.nn.sigmoid(gt) * up).astype(bf16)
    o_ref[...] = x1 + jnp.dot(act_scr[...], wd_ref[...], preferred_element_type=f32)


def _ffn(x2, ya, yb, wo, g2, wg, wu, wd):
    n = x2.shape[0]
    tm = TM_FFN

    def const(shape):
        return pl.BlockSpec(shape, lambda i: (0, 0), pipeline_mode=pl.Buffered(1))

    return pl.pallas_call(
        _ffn_kernel,
        grid=(n // tm,),
        in_specs=[
            pl.BlockSpec((tm, D_MODEL), lambda i: (i, 0)),
            pl.BlockSpec((tm, NSA_WIDTH), lambda i: (i, 0)),
            pl.BlockSpec((tm, MLSTM_WIDTH), lambda i: (i, 0)),
            const(wo.shape), const(g2.shape), const(wg.shape), const(wu.shape), const(wd.shape),
        ],
        out_specs=pl.BlockSpec((tm, D_MODEL), lambda i: (i, 0)),
        out_shape=jax.ShapeDtypeStruct((n, D_MODEL), f32),
        scratch_shapes=[pltpu.VMEM((tm, D_FF), bf16)],
        compiler_params=_cparams(("parallel",)),
        name="ffn",
    )(x2, ya, yb, wo, g2, wg, wu, wd)


def _overlap_t(ncp):
    nsel = ncp // (SEL_BLOCK // CMP_STRIDE)
    cs = np.arange(ncp) * CMP_STRIDE
    ss = np.arange(nsel) * SEL_BLOCK
    ov = ((cs[None, :] < ss[:, None] + SEL_BLOCK) & (cs[None, :] + CMP_BLOCK > ss[:, None])).astype(np.float32)
    ov[:, ncp - 1] = 0.0
    out = np.zeros((LANES, ncp), np.float32)
    out[:nsel] = ov
    return jnp.asarray(np.tile(out, (1, 3)), dtype=bf16)


def _layer(x, norm1_g, w_in, q_g, kc_g, ks_g, kw_g, cmp_pos, w_ck1, w_ck2, w_cv1, w_cv2, conv_w, conv_b,
           w_mq, w_mk, b_i, b_f, mlstm_norm_g, mlstm_skip, w_out, norm2_g, w_gate, w_up, w_down):
    b, t, d = x.shape
    n = b * t
    x2 = x.reshape(n, d)

    o_gate = NSA_WIDTH + 6 * KV_WIDTH
    o_u = o_gate + 3 * NSA_HEADS
    o_if = o_u + 3 * MLSTM_WIDTH
    w_perm = jnp.concatenate([
        w_in[:, :o_gate], w_in[:, o_u:o_if], w_in[:, o_gate:o_u], w_in[:, o_if:],
        jnp.zeros((d, LANES - 3 * NSA_HEADS - 2 * MLSTM_HEADS), w_in.dtype)], axis=1).astype(bf16)
    w_gates_t = jnp.concatenate([w_in[:, o_if:].T, jnp.zeros((GATE_ROWS - 2 * MLSTM_HEADS, d), w_in.dtype)],
                                axis=0).astype(bf16)
    kvc2, u2, vm2, op2, sm2, gates_t, ks, vs, kw, vw, q3 = _inproj(
        x2, norm1_g.reshape(1, d), w_perm, w_gates_t, jnp.tile(ks_g, 2).reshape(1, LANES),
        jnp.tile(kw_g, 2).reshape(1, LANES), jnp.tile(q_g, NSA_HEADS).reshape(1, NSA_WIDTH), b, t)

    kv3 = kvc2.reshape(b, t, 2 * KV_WIDTH)
    nseg = t // CMP_STRIDE
    pos8 = jnp.broadcast_to(cmp_pos.reshape(1, CMP_BLOCK * NSA_HEAD_DIM), (SUBLANES, CMP_BLOCK * NSA_HEAD_DIM))
    wk1s, wk2s = _compress_weights(w_ck1, w_ck2)
    wv1s, wv2s = _compress_weights(w_cv1, w_cv2)
    kcd, vcl = _compress(kv3, w_ck1, wk1s, wk2s, w_cv1, wv1s, wv2s, pos8, jnp.tile(kc_g, 2).reshape(1, LANES))
    slopes = jnp.exp2(-8.0 * (jnp.arange(NSA_HEADS, dtype=f32) + 1.0) / NSA_HEADS)
    k_gain = jnp.maximum(jnp.max(jnp.abs(kc_g)), jnp.maximum(jnp.max(jnp.abs(ks_g)), jnp.max(jnp.abs(kw_g))))
    score_cap = (NSA_HEAD_DIM ** 0.5 * math.log2(math.e)) * jnp.max(jnp.abs(q_g)) * k_gain
    nsa_args = (slopes, q3, sm2.reshape(b, t, LANES), kcd, vcl, _overlap_t(nseg), ks, vs, kw, vw)
    y_nsa = lax.cond(score_cap <= SCORE_BOUND, functools.partial(_nsa, bounded=True),
                     functools.partial(_nsa, bounded=False), *nsa_args)

    gi4 = gates_t[0:MLSTM_HEADS].reshape(MLSTM_HEADS, b, t).transpose(1, 0, 2)
    gf4 = gates_t[MLSTM_HEADS:2 * MLSTM_HEADS].reshape(MLSTM_HEADS, b, t).transpose(1, 0, 2)
    y_mem = _mlstm(b_i, b_f, u2.reshape(b, t, MLSTM_WIDTH), vm2.reshape(b, t, MLSTM_WIDTH),
                   op2.reshape(b, t, MLSTM_WIDTH), gi4.reshape(b, MLSTM_HEADS, t // MLSTM_L, MLSTM_L),
                   gf4.reshape(b, MLSTM_HEADS, t // MLSTM_L, MLSTM_L), conv_w, conv_b.reshape(1, MLSTM_WIDTH),
                   w_mq, w_mk,
                   mlstm_norm_g.reshape(MLSTM_HEADS, 1, MLSTM_HEAD_DIM), mlstm_skip.reshape(1, MLSTM_WIDTH))

    out = _ffn(x2, y_nsa.reshape(n, NSA_WIDTH), y_mem.reshape(n, MLSTM_WIDTH), w_out.astype(bf16),
               norm2_g.reshape(1, d), w_gate.astype(bf16), w_up.astype(bf16), w_down.astype(bf16))
    return out.reshape(b, t, d)


def kernel(x, norm1_g, w_in, q_norm_g, kc_norm_g, ks_norm_g, kw_norm_g, cmp_pos, w_ck1, w_ck2, w_cv1, w_cv2,
           conv_w, conv_b, w_mq, w_mk, b_i, b_f, mlstm_norm_g, mlstm_skip, w_out, norm2_g, w_gate, w_up, w_down):
    depth = norm1_g.shape[0]
    for l in range(depth):
        x = _layer(x, norm1_g[l], w_in[l], q_norm_g[l], kc_norm_g[l], ks_norm_g[l], kw_norm_g[l], cmp_pos[l],
                   w_ck1[l], w_ck2[l], w_cv1[l], w_cv2[l], conv_w[l], conv_b[l], w_mq[l], w_mk[l], b_i[l], b_f[l],
                   mlstm_norm_g[l], mlstm_skip[l], w_out[l], norm2_g[l], w_gate[l], w_up[l], w_down[l])
    return x
```

```python
import functools
import math

import numpy as np
import jax
import jax.numpy as jnp
from jax import lax
from jax.experimental import pallas as pl
from jax.experimental.pallas import tpu as pltpu

f32 = jnp.float32
bf16 = jnp.bfloat16

D_MODEL = 1024
NSA_HEADS = 8
NSA_KV_HEADS = 2
NSA_HEAD_DIM = 64
NSA_GROUP = NSA_HEADS // NSA_KV_HEADS
CMP_BLOCK = 32
CMP_STRIDE = 16
CMP_HIDDEN = 256
SEL_BLOCK = 64
SEL_TOPN = 16
WINDOW = 512
MLSTM_HEADS = 4
MLSTM_HEAD_DIM = 128
CONV_WIDTH = 4
NSA_WIDTH = NSA_HEADS * NSA_HEAD_DIM
MLSTM_WIDTH = MLSTM_HEADS * MLSTM_HEAD_DIM
KV_WIDTH = NSA_KV_HEADS * NSA_HEAD_DIM
D_FF = -(-8 * D_MODEL // (3 * 256)) * 256
NORM_EPS = 1e-6
NEG = -1e30
FORCE = 1e9
SCORE_BOUND = 100.0

LANES = 128
SUBLANES = 8
HALF = 64
SEL_SHIFT = SEL_BLOCK.bit_length() - 1
assert 1 << SEL_SHIFT == SEL_BLOCK and SEL_BLOCK == HALF
VMEM_LIMIT = 56 * 1024 * 1024

TM_PROJ = 1024
TQ = 512
TK = 512
MLSTM_L = 256
MLSTM_UNROLL = 8
TM_FFN = 1024
FF_CHUNK = 256

HI = lax.Precision.HIGHEST
NT = (((1,), (1,)), ((), ()))


def _cparams(sem):
    return pltpu.CompilerParams(dimension_semantics=sem, vmem_limit_bytes=VMEM_LIMIT)


def _split3(x):
    x1 = x.astype(bf16)
    r1 = x - x1.astype(f32)
    x2 = r1.astype(bf16)
    x3 = (r1 - x2.astype(f32)).astype(bf16)
    return x1, x2, x3


COL_Q = 0
COL_KV = COL_Q + NSA_WIDTH
COL_U = COL_KV + 6 * KV_WIDTH
COL_VM = COL_U + MLSTM_WIDTH
COL_OP = COL_VM + MLSTM_WIDTH
COL_SM = COL_OP + MLSTM_WIDTH
PROJ_WIDTH = COL_SM + LANES
GATE_ROWS = 16


def _pair_norm(x, gain, lo):
    sq = x * x
    s_lo = jnp.sum(jnp.where(lo, sq, 0.0), axis=-1, keepdims=True)
    s_hi = jnp.sum(jnp.where(lo, 0.0, sq), axis=-1, keepdims=True)
    inv = jnp.where(lo, lax.rsqrt(s_lo / HALF + NORM_EPS), lax.rsqrt(s_hi / HALF + NORM_EPS))
    return x * inv * gain


def _inproj_kernel(x_ref, g_ref, w_ref, wgt_ref, ksg_ref, kwg_ref, qg_ref,
                   kvc_ref, u_ref, vm_ref, op_ref, sm_ref, gt_ref, ks_ref, vs_ref, kw_ref, vw_ref, q3_ref,
                   *, tiles_per_seq):
    tm = x_ref.shape[0]
    x = x_ref[...]
    h = x * lax.rsqrt(jnp.mean(x * x, axis=-1, keepdims=True) + NORM_EPS) * g_ref[...]
    hb = h.astype(bf16)

    def proj(col, width):
        return jnp.dot(hb, w_ref[:, col:col + width], preferred_element_type=f32)

    lane = lax.broadcasted_iota(jnp.int32, (tm, LANES), 1)
    lo = lane < HALF
    pos = (pl.program_id(0) % tiles_per_seq) * tm + lax.broadcasted_iota(jnp.int32, (tm, LANES), 0)

    q = proj(COL_Q, NSA_WIDTH)
    scale = NSA_HEAD_DIM ** -0.5 * math.log2(math.e)
    for c in range(NSA_HEADS // 2):
        pair = _pair_norm(q[:, c * LANES:(c + 1) * LANES], qg_ref[:, c * LANES:(c + 1) * LANES], lo) * scale
        for par, xq in enumerate((pair, pltpu.roll(pair, HALF, 1))):
            hi = jnp.where(lo, xq, 0.0).astype(bf16).astype(f32)
            res = jnp.where(lo, xq - hi, 0.0)
            head = 2 * c + par
            q3_ref[0, head // NSA_GROUP, head % NSA_GROUP] = jnp.concatenate(
                [hi + pltpu.roll(res, HALF, 1), hi], axis=1).astype(bf16)

    kv = proj(COL_KV, 6 * KV_WIDTH)
    kvc_ref[...] = kv[:, 0:2 * KV_WIDTH]
    code = jnp.where(lax.shift_right_logical(pos, SEL_SHIFT) == (lane & (HALF - 1)), 1.0, 0.0)

    def put(xk, o_ref, fill):
        o_ref[0, 0] = jnp.where(lo, xk, fill).astype(bf16)
        o_ref[0, 1] = jnp.where(lo, pltpu.roll(xk, HALF, 1), fill).astype(bf16)

    put(_pair_norm(kv[:, 2 * KV_WIDTH:3 * KV_WIDTH], ksg_ref[...], lo), ks_ref, code)
    put(kv[:, 3 * KV_WIDTH:4 * KV_WIDTH], vs_ref, 1.0)
    put(_pair_norm(kv[:, 4 * KV_WIDTH:5 * KV_WIDTH], kwg_ref[...], lo), kw_ref, 0.0)
    put(kv[:, 5 * KV_WIDTH:6 * KV_WIDTH], vw_ref, 1.0)

    u_ref[...] = proj(COL_U, MLSTM_WIDTH)
    vm_ref[...] = proj(COL_VM, MLSTM_WIDTH)
    op_ref[...] = proj(COL_OP, MLSTM_WIDTH)
    sm_ref[...] = proj(COL_SM, LANES)
    gt_ref[...] = lax.dot_general(wgt_ref[...], hb, NT, preferred_element_type=f32)


def _inproj(x2, g1, w_perm, w_gates_t, ksg2, kwg2, qg2, b, t):
    n = x2.shape[0]
    tm = min(TM_PROJ, t)
    nt = t // tm

    def rows(width):
        return pl.BlockSpec((tm, width), lambda i: (i, 0))

    def const(shape):
        return pl.BlockSpec(shape, lambda i: (0, 0))

    kv_spec = pl.BlockSpec((1, NSA_KV_HEADS, tm, LANES), lambda i: (i // nt, 0, i % nt, 0))
    kv_shape = jax.ShapeDtypeStruct((b, NSA_KV_HEADS, t, LANES), bf16)
    return pl.pallas_call(
        functools.partial(_inproj_kernel, tiles_per_seq=nt),
        grid=(n // tm,),
        in_specs=[rows(D_MODEL), const((1, D_MODEL)), const((D_MODEL, PROJ_WIDTH)), const((GATE_ROWS, D_MODEL)),
                  const((1, LANES)), const((1, LANES)), const((1, NSA_WIDTH))],
        out_specs=[rows(2 * KV_WIDTH), rows(MLSTM_WIDTH), rows(MLSTM_WIDTH), rows(MLSTM_WIDTH), rows(LANES),
                   pl.BlockSpec((GATE_ROWS, tm), lambda i: (0, i)),
                   kv_spec, kv_spec, kv_spec, kv_spec,
                   pl.BlockSpec((1, NSA_KV_HEADS, NSA_GROUP, tm, 2 * LANES),
                                lambda i: (i // nt, 0, 0, i % nt, 0))],
        out_shape=[jax.ShapeDtypeStruct((n, 2 * KV_WIDTH), f32), jax.ShapeDtypeStruct((n, MLSTM_WIDTH), f32),
                   jax.ShapeDtypeStruct((n, MLSTM_WIDTH), f32), jax.ShapeDtypeStruct((n, MLSTM_WIDTH), f32),
                   jax.ShapeDtypeStruct((n, LANES), f32), jax.ShapeDtypeStruct((GATE_ROWS, n), f32),
                   kv_shape, kv_shape, kv_shape, kv_shape,
                   jax.ShapeDtypeStruct((b, NSA_KV_HEADS, NSA_GROUP, t, 2 * LANES), bf16)],
        compiler_params=_cparams(("parallel",)),
        name="inproj",
    )(x2, g1, w_perm, w_gates_t, ksg2, kwg2, qg2)


def _gelu_tanh(x):
    return 0.5 * x * (1.0 + jnp.tanh(math.sqrt(2.0 / math.pi) * (x + 0.044715 * (x * x * x))))


def _hi_lo(x):
    hi = x.astype(bf16)
    return hi, (x - hi.astype(f32)).astype(bf16)


def _compress_kernel(kc_ref, vc_ref, wk1_ref, wk1s_ref, wk2s_ref, wv1_ref, wv1s_ref, wv2s_ref, pos_ref, kcg_ref,
                     kcd_ref, vcl_ref):
    nseg = kc_ref.shape[1] // CMP_STRIDE
    half = NSA_KV_HEADS * CMP_HIDDEN

    def branch(x_ref, w1_ref, w1s_ref, w2s_ref):
        terms = []
        for r in range(CMP_STRIDE):
            hi, lo = _hi_lo(x_ref[0, pl.ds(r, nseg, stride=CMP_STRIDE), :])
            terms += [hi, lo, hi]
        acc = jnp.dot(jnp.concatenate(terms, axis=1), w1s_ref[...], preferred_element_type=f32)
        pos_term = jnp.dot(pos_ref[...], w1_ref[...], precision=HI, preferred_element_type=f32)[0:1]
        hid = (acc[:, 0:half] + pltpu.roll(acc[:, half:2 * half], nseg - 1, 0)
               + jnp.concatenate([pos_term] * NSA_KV_HEADS, axis=1))
        hi, lo = _hi_lo(_gelu_tanh(hid))
        return jnp.dot(jnp.concatenate([hi, lo, hi], axis=1), w2s_ref[...], preferred_element_type=f32)

    kc2 = branch(kc_ref, wk1_ref, wk1s_ref, wk2s_ref)
    vc2 = branch(vc_ref, wv1_ref, wv1s_ref, wv2s_ref)
    lane = lax.broadcasted_iota(jnp.int32, (nseg, LANES), 1)
    for g in range(NSA_KV_HEADS):
        kc = kc2[:, g * LANES:(g + 1) * LANES]
        kc = kc * lax.rsqrt(jnp.mean(kc * kc, axis=-1, keepdims=True) + NORM_EPS) * kcg_ref[...]
        k_hi = kc.astype(bf16)
        k_lo = jnp.where(lane < HALF, kc - k_hi.astype(f32), 0.0).astype(bf16)
        kcd_ref[0, g] = jnp.concatenate([k_hi, k_lo], axis=1)
        vcl_ref[0, g] = jnp.where(lane < HALF, vc2[:, g * LANES:(g + 1) * LANES], 0.0).astype(bf16)


def _compress_weights(w1, w2):
    w1r = w1.reshape(2, CMP_STRIDE, NSA_HEAD_DIM, CMP_HIDDEN)
    z = jnp.zeros_like(w1r[0])

    def two_groups(w):
        return jnp.concatenate([jnp.concatenate([w, z], axis=2), jnp.concatenate([z, w], axis=2)], axis=1)

    wr = jnp.concatenate([two_groups(w1r[0]), two_groups(w1r[1])], axis=2)
    hi, lo = _hi_lo(wr)
    w1s = jnp.concatenate([hi, hi, lo], axis=1).reshape(CMP_STRIDE * 3 * LANES, 2 * NSA_KV_HEADS * CMP_HIDDEN)
    z2 = jnp.zeros_like(w2)
    w2d = jnp.concatenate([jnp.concatenate([w2, w2, z2, z2], axis=1),
                           jnp.concatenate([z2, z2, w2, w2], axis=1)], axis=0)
    hi2, lo2 = _hi_lo(w2d)
    return w1s, jnp.concatenate([hi2, hi2, lo2], axis=0)


def _compress(kv3, wk1, wk1s, wk2s, wv1, wv1s, wv2s, pos8, kcg2):
    b, t, _ = kv3.shape
    nseg = t // CMP_STRIDE

    def const(shape):
        return pl.BlockSpec(shape, lambda bi: (0,) * len(shape), pipeline_mode=pl.Buffered(1))

    return pl.pallas_call(
        _compress_kernel,
        grid=(b,),
        in_specs=[pl.BlockSpec((1, t, KV_WIDTH), lambda bi: (bi, 0, 0)),
                  pl.BlockSpec((1, t, KV_WIDTH), lambda bi: (bi, 0, 1)),
                  const(wk1.shape), const(wk1s.shape), const(wk2s.shape),
                  const(wv1.shape), const(wv1s.shape), const(wv2s.shape),
                  const(pos8.shape), const(kcg2.shape)],
        out_specs=[pl.BlockSpec((1, NSA_KV_HEADS, nseg, 2 * LANES), lambda bi: (bi, 0, 0, 0)),
                   pl.BlockSpec((1, NSA_KV_HEADS, nseg, LANES), lambda bi: (bi, 0, 0, 0))],
        out_shape=[jax.ShapeDtypeStruct((b, NSA_KV_HEADS, nseg, 2 * LANES), bf16),
                   jax.ShapeDtypeStruct((b, NSA_KV_HEADS, nseg, LANES), bf16)],
        compiler_params=_cparams(("parallel",)),
        name="compress",
    )(kv3, kv3, wk1, wk1s, wk2s, wv1, wv1s, wv2s, pos8, kcg2)


def _nsa_kernel(slopes_ref, q3_ref, sm_ref, gsel_ref, wdist_ref, kcd_ref, vcl_ref, ovt_ref,
                ks_ref, vs_ref, kw_ref, vw_ref, y_ref, qa_scr, m_scr, acc_scr, *, bounded):
    g = pl.program_id(1)
    qi = pl.program_id(2)
    q0 = qi * TQ
    ncp = kcd_ref.shape[2]
    nsel = ncp // (SEL_BLOCK // CMP_STRIDE)
    rows = NSA_GROUP * TQ
    log2e = math.log2(math.e)

    lane = lax.broadcasted_iota(jnp.int32, (TQ, LANES), 1)
    lo = lane < HALF
    slopes = [slopes_ref[g * NSA_GROUP + r] * log2e for r in range(NSA_GROUP)]

    def per_head(x, fn):
        return jnp.concatenate([fn(r, x[r * TQ:(r + 1) * TQ]) for r in range(NSA_GROUP)], axis=0)

    q3 = q3_ref[0, 0].reshape(rows, 2 * LANES)
    q1 = q3[:, 0:LANES]

    cidx = lax.broadcasted_iota(jnp.int32, (TQ, ncp), 1)
    tpos_c = q0 + lax.broadcasted_iota(jnp.int32, (TQ, ncp), 0)
    blk_end = cidx * CMP_STRIDE + (CMP_BLOCK - 1)
    valid_c = jnp.logical_and(tpos_c >= blk_end, cidx < ncp - 1)
    krel_c = (CMP_BLOCK - 1 - q0 + CMP_STRIDE * lax.broadcasted_iota(jnp.int32, (1, ncp), 1)).astype(f32)
    s_c = lax.dot_general(q3, kcd_ref[0, 0], NT, preferred_element_type=f32)
    if bounded:
        rowf = lax.broadcasted_iota(jnp.int32, (TQ, LANES), 0).astype(f32)

        def row_term(r, width):
            return jnp.concatenate([slopes[r] * rowf] * (width // LANES), axis=1)

        s_c = per_head(s_c, lambda r, x: jnp.where(valid_c, x + slopes[r] * krel_c - row_term(r, ncp), NEG))
        e_c = jnp.exp2(s_c)
    else:
        s_c = per_head(s_c, lambda r, x: jnp.where(valid_c, x + slopes[r] * krel_c, NEG))
        e_c = jnp.exp2(s_c - jnp.max(s_c, axis=-1, keepdims=True))
    p_c = e_c / jnp.sum(e_c, axis=-1, keepdims=True)
    p_c = per_head(p_c, lambda r, x: jnp.where(valid_c, x, 0.0))
    p_sum = p_c[0:TQ]
    for r in range(1, NSA_GROUP):
        p_sum = p_sum + p_c[r * TQ:(r + 1) * TQ]
    o_cmp = jnp.dot(p_c.astype(bf16), vcl_ref[0, 0], preferred_element_type=f32)

    band = WINDOW + TQ
    kb = pl.multiple_of(jnp.maximum(q0 - WINDOW, 0), TQ)
    s_w = lax.dot_general(q1, kw_ref[0, 0, pl.ds(kb, band), :], NT, preferred_element_type=f32)
    dist_w = wdist_ref[jnp.minimum(qi, WINDOW // TQ)]
    s_w = per_head(s_w, lambda r, x: x + slopes[r] * dist_w)
    if bounded:
        p_w = jnp.exp2(s_w)
    else:
        p_w = jnp.exp2(s_w - jnp.max(s_w, axis=-1, keepdims=True))
    acc_w = jnp.dot(p_w.astype(bf16), vw_ref[0, 0, pl.ds(kb, band), :], preferred_element_type=f32)
    o_win = acc_w / pltpu.roll(acc_w, HALF, 1)

    imp = lax.dot_general(ovt_ref[...], jnp.concatenate(_split3(p_sum), axis=1), NT,
                          preferred_element_type=f32)[0:nsel]
    jrow = lax.broadcasted_iota(jnp.int32, (nsel, TQ), 0)
    tcol = q0 + lax.broadcasted_iota(jnp.int32, (nsel, TQ), 1)
    forced = jnp.logical_or(jrow == lax.shift_right_logical(tcol, SEL_SHIFT), jrow == 0)
    future = jrow * SEL_BLOCK > tcol
    taken = -3e38
    work = jnp.where(forced, taken, jnp.where(future, -FORCE, imp))
    jrow_f = jrow.astype(f32)
    bias_t = jnp.where(forced, 0.0, NEG)
    for _ in range(min(SEL_TOPN, nsel) - 2):
        best = jnp.max(work, axis=0, keepdims=True)
        first = jnp.min(jnp.where(work == best, jrow_f, float(nsel)), axis=0, keepdims=True)
        hit = jrow_f == first
        bias_t = jnp.where(hit, 0.0, bias_t)
        work = jnp.where(hit, taken, work)
    if nsel < HALF:
        bias_t = jnp.concatenate([bias_t, jnp.full((HALF - nsel, TQ), NEG, f32)], axis=0)
    bias = jnp.concatenate([bias_t, bias_t], axis=0).T

    bias_b = bias.astype(bf16)
    qa_scr[...] = per_head(q1, lambda r, x: jnp.where(lo, x, bias_b))

    m_scr[...] = jnp.full(m_scr.shape, NEG, f32)
    acc_scr[...] = jnp.zeros(acc_scr.shape, f32)

    def scores(kt, causal, row_shift):
        k0 = pl.multiple_of(kt * TK, TK)
        s = lax.dot_general(qa_scr[...], ks_ref[0, 0, pl.ds(k0, TK), :], NT, preferred_element_type=f32)
        krel = (k0 - q0 + lax.broadcasted_iota(jnp.int32, (1, TK), 1)).astype(f32)

        def alibi(r, x):
            x = x + slopes[r] * krel
            return x if row_shift is None else x - row_shift[r]

        if causal:
            ahead = (k0 - q0 + lax.broadcasted_iota(jnp.int32, (TQ, TK), 1)
                     > lax.broadcasted_iota(jnp.int32, (TQ, TK), 0))
            return per_head(s, lambda r, x: jnp.where(ahead, NEG, alibi(r, x))), k0
        return per_head(s, alibi), k0

    def online_tile(state, kt, causal):
        m_old, acc = state
        s, k0 = scores(kt, causal, None)
        m_new = jnp.maximum(m_old, jnp.max(s, axis=-1, keepdims=True))
        p = jnp.exp2(s - m_new[:, 0:1])
        acc = jnp.exp2(m_old - m_new) * acc + jnp.dot(p.astype(bf16), vs_ref[0, 0, pl.ds(k0, TK), :],
                                                      preferred_element_type=f32)
        return m_new, acc

    def online_tiles(tiles):
        state = (m_scr[...], acc_scr[...])
        for kt, causal in tiles:
            state = online_tile(state, kt, causal)
        m_scr[...] = state[0]
        acc_scr[...] = state[1]

    def bounded_tiles(tiles):
        row_shift = [row_term(r, TK) for r in range(NSA_GROUP)]
        acc = acc_scr[...]
        for kt, causal in tiles:
            s, k0 = scores(kt, causal, row_shift)
            acc = acc + jnp.dot(jnp.exp2(s).astype(bf16), vs_ref[0, 0, pl.ds(k0, TK), :],
                                preferred_element_type=f32)
        acc_scr[...] = acc

    n_full = q0 // TK

    def selected_branch(run_tiles):
        def body(i, carry):
            run_tiles([(2 * i, False), (2 * i + 1, False)])
            return carry

        lax.fori_loop(0, n_full // 2, body, 0)

        @pl.when(n_full % 2 == 1)
        def _():
            run_tiles([(n_full - 1, False), (n_full, True)])

        @pl.when(n_full % 2 == 0)
        def _():
            run_tiles([(n_full, True)])

    selected_branch(bounded_tiles if bounded else online_tiles)

    acc_s = acc_scr[...]
    o_slc = acc_s / pltpu.roll(acc_s, HALF, 1)

    gates = jnp.dot(jnp.concatenate(_hi_lo(jax.nn.sigmoid(sm_ref[0])), axis=1), gsel_ref[0],
                    preferred_element_type=f32)

    def gate(r, branch):
        col = 3 * r + branch
        return gates[:, col * LANES:(col + 1) * LANES]

    def mix(r):
        sl = slice(r * TQ, (r + 1) * TQ)
        return gate(r, 0) * o_cmp[sl] + gate(r, 1) * o_slc[sl] + gate(r, 2) * o_win[sl]

    for c in range(NSA_GROUP // 2):
        y_ref[0, :, c * LANES:(c + 1) * LANES] = jnp.where(lo, mix(2 * c), pltpu.roll(mix(2 * c + 1), HALF, 1))


def _gate_select():
    sel = np.zeros((NSA_KV_HEADS, 2 * LANES, 3 * NSA_GROUP * LANES), np.float32)
    for g in range(NSA_KV_HEADS):
        for c in range(3 * NSA_GROUP):
            src = 3 * NSA_GROUP * g + c
            sel[g, [src, LANES + src], c * LANES:(c + 1) * LANES] = 1.0
    return jnp.asarray(sel, dtype=bf16)


def _window_distance():
    i = np.arange(TQ)[None, :, None]
    j = np.arange(WINDOW + TQ)[None, None, :]
    off = -TQ * np.arange(WINDOW // TQ + 1)[:, None, None]
    rel = off + j - i
    return jnp.asarray(np.where((rel <= 0) & (rel > -WINDOW), rel, NEG), dtype=f32)


def _nsa(slopes, q3, sm3, kcd, vcl, ovt, ks, vs, kw, vw, *, bounded):
    b, t = q3.shape[0], q3.shape[3]
    gsel = _gate_select()
    wdist = _window_distance()
    gw = NSA_GROUP * NSA_HEAD_DIM
    ncp = kcd.shape[2]

    def kv_spec(rows):
        return pl.BlockSpec((1, 1, rows, LANES), lambda bi, gi, qi: (bi, gi, 0, 0))

    return pl.pallas_call(
        functools.partial(_nsa_kernel, bounded=bounded),
        grid=(b, NSA_KV_HEADS, t // TQ),
        in_specs=[
            pl.BlockSpec(memory_space=pltpu.SMEM),
            pl.BlockSpec((1, 1, NSA_GROUP, TQ, 2 * LANES), lambda bi, gi, qi: (bi, gi, 0, qi, 0)),
            pl.BlockSpec((1, TQ, LANES), lambda bi, gi, qi: (bi, qi, 0)),
            pl.BlockSpec((1,) + gsel.shape[1:], lambda bi, gi, qi: (gi, 0, 0)),
            pl.BlockSpec(wdist.shape, lambda bi, gi, qi: (0, 0, 0)),
            pl.BlockSpec((1, 1, ncp, 2 * LANES), lambda bi, gi, qi: (bi, gi, 0, 0)), kv_spec(ncp),
            pl.BlockSpec(ovt.shape, lambda bi, gi, qi: (0, 0)),
            kv_spec(t), kv_spec(t), kv_spec(t), kv_spec(t),
        ],
        out_specs=pl.BlockSpec((1, TQ, gw), lambda bi, gi, qi: (bi, qi, gi)),
        out_shape=jax.ShapeDtypeStruct((b, t, NSA_WIDTH), f32),
        scratch_shapes=[
            pltpu.VMEM((NSA_GROUP * TQ, LANES), bf16),
            pltpu.VMEM((NSA_GROUP * TQ, LANES), f32),
            pltpu.VMEM((NSA_GROUP * TQ, LANES), f32),
        ],
        compiler_params=_cparams(("parallel", "parallel", "arbitrary")),
        name="nsa",
    )(slopes, q3, sm3, gsel, wdist, kcd, vcl, ovt, ks, vs, kw, vw)


def _log_sigmoid(x):
    return jnp.minimum(x, 0.0) - jnp.log1p(jnp.exp(-jnp.abs(x)))


def _mlstm_kernel(bi_ref, bf_ref, u_ref, v_ref, op_ref, gi_ref, gf_ref, cw_ref, cb_ref, wq_ref, wk_ref,
                  ng_ref, sk_ref, y_ref, uc_scr, q_scr, kt_scr, ct_scr, m_scr, b_scr, li_scr, xp_scr):
    h = pl.program_id(1)
    t = u_ref.shape[1]
    L = MLSTM_L
    dm = MLSTM_HEAD_DIM

    x = u_ref[0]
    xp_scr[0:SUBLANES, :] = jnp.zeros((SUBLANES, dm), f32)
    xp_scr[SUBLANES:, :] = x
    acc = x * cw_ref[CONV_WIDTH - 1:CONV_WIDTH, :]
    for s in range(1, CONV_WIDTH):
        acc = acc + xp_scr[SUBLANES - s:SUBLANES - s + t, :] * cw_ref[CONV_WIDTH - 1 - s:CONV_WIDTH - s, :]
    uc = acc + cb_ref[...]
    uc = uc * jax.nn.sigmoid(uc)
    uc_scr[...] = uc
    ucb = uc.astype(bf16)
    q_scr[...] = jnp.dot(ucb, wq_ref[0].astype(bf16), preferred_element_type=f32).astype(bf16)
    k = jnp.dot(ucb, wk_ref[0].astype(bf16), preferred_element_type=f32) * (dm ** -0.5)
    kt_scr[...] = k.T

    ct_scr[...] = jnp.zeros(ct_scr.shape, f32)
    m_scr[...] = jnp.zeros(m_scr.shape, f32)

    li_ = lax.broadcasted_iota(jnp.int32, (L, L), 0)
    si_ = lax.broadcasted_iota(jnp.int32, (L, L), 1)
    causal = si_ <= li_
    diag = si_ == li_
    ones_v = jnp.ones((L, dm), f32)
    ones_sq = jnp.ones((dm, dm), bf16)

    def wide(x):
        return jnp.concatenate([x] * (L // dm), axis=1)

    log_f = _log_sigmoid(gf_ref[0, 0] + bf_ref[h])
    upper = jnp.where(li_ <= si_, 1.0, 0.0).astype(bf16)
    b_scr[...] = jnp.dot(jnp.concatenate(_split3(log_f), axis=1), jnp.concatenate([upper] * 3, axis=0),
                         preferred_element_type=f32)
    li_scr[...] = gi_ref[0, 0] + bi_ref[h]

    def chunk(c, ct, m_prev):
        r0 = pl.multiple_of(c * L, L)
        qc = q_scr[pl.ds(r0, L), :]
        ktc = kt_scr[:, pl.ds(r0, L)]
        vaug = jnp.concatenate([v_ref[0, pl.ds(r0, L), :], ones_v], axis=1).astype(bf16)
        log_i = li_scr[pl.ds(c, 1), :]
        b_row = b_scr[pl.ds(c, 1), :]
        b_col = jnp.broadcast_to(jnp.sum(jnp.where(diag, b_row, 0.0), axis=-1, keepdims=True), (L, dm))
        g_sum = b_row[:, L - 1:L]
        dmat = jnp.where(causal, (wide(b_col) - b_row) + log_i, NEG)
        m_loc = jnp.broadcast_to(jnp.max(dmat, axis=-1, keepdims=True), (L, dm))
        p = jnp.dot(qc, ktc.astype(bf16), preferred_element_type=f32) * jnp.exp(dmat - wide(m_loc))
        intra = jnp.dot(p.astype(bf16), vaug, preferred_element_type=f32)
        m_inter = b_col + m_prev
        m_out = jnp.maximum(m_inter, m_loc)
        xo = (wide(jnp.exp(m_inter - m_out)) * jnp.dot(qc, ct.astype(bf16), preferred_element_type=f32)
              + wide(jnp.exp(m_loc - m_out)) * intra)
        num = xo[:, 0:dm]
        den = xo[:, dm:2 * dm]
        hh = num / jnp.maximum(jnp.abs(den), jnp.exp(-m_out))
        hh = hh * jax.nn.sigmoid(op_ref[0, pl.ds(r0, L), :])
        ssq = jnp.dot((hh * hh).astype(bf16), ones_sq, preferred_element_type=f32)
        hh = hh * lax.rsqrt(ssq * (1.0 / dm) + NORM_EPS) * ng_ref[0]
        y_ref[0, pl.ds(r0, L), :] = hh + sk_ref[...] * uc_scr[pl.ds(r0, L), :]

        w_end = (g_sum - b_row) + log_i
        m_new = jnp.maximum(g_sum + m_prev, jnp.max(w_end, axis=-1, keepdims=True))
        decay = jnp.exp(g_sum + m_prev - m_new)
        w = jnp.exp(w_end - m_new)
        return decay * ct + jnp.dot((ktc * w).astype(bf16), vaug, preferred_element_type=f32), m_new

    def chunk_group(i, carry):
        ct, m_prev = ct_scr[...], m_scr[0:1, 0:1]
        for j in range(MLSTM_UNROLL):
            ct, m_prev = chunk(i * MLSTM_UNROLL + j, ct, m_prev)
        ct_scr[...] = ct
        m_scr[...] = jnp.broadcast_to(m_prev, m_scr.shape)
        return carry

    lax.fori_loop(0, t // (L * MLSTM_UNROLL), chunk_group, 0)


def _mlstm(b_i, b_f, u3, v3, op3, gi4, gf4, cw, cb2, wq, wk, ng3, sk2):
    b, t, _ = u3.shape
    dm = MLSTM_HEAD_DIM
    seq = pl.BlockSpec((1, t, dm), lambda bi, hi: (bi, 0, hi))
    gate = pl.BlockSpec((1, 1, t // MLSTM_L, MLSTM_L), lambda bi, hi: (bi, hi, 0, 0))
    smem = pl.BlockSpec(memory_space=pltpu.SMEM)
    return pl.pallas_call(
        _mlstm_kernel,
        grid=(b, MLSTM_HEADS),
        in_specs=[
            smem, smem, seq, seq, seq, gate, gate,
            pl.BlockSpec((CONV_WIDTH, dm), lambda bi, hi: (0, hi)),
            pl.BlockSpec((1, dm), lambda bi, hi: (0, hi)),
            pl.BlockSpec((1, dm, dm), lambda bi, hi: (hi, 0, 0)),
            pl.BlockSpec((1, dm, dm), lambda bi, hi: (hi, 0, 0)),
            pl.BlockSpec((1, 1, dm), lambda bi, hi: (hi, 0, 0)),
            pl.BlockSpec((1, dm), lambda bi, hi: (0, hi)),
        ],
        out_specs=seq,
        out_shape=jax.ShapeDtypeStruct((b, t, MLSTM_WIDTH), f32),
        scratch_shapes=[
            pltpu.VMEM((t, dm), f32),
            pltpu.VMEM((t, dm), bf16),
            pltpu.VMEM((dm, t), f32),
            pltpu.VMEM((dm, 2 * dm), f32),
            pltpu.VMEM((SUBLANES, LANES), f32),
            pltpu.VMEM((t // MLSTM_L, MLSTM_L), f32),
            pltpu.VMEM((t // MLSTM_L, MLSTM_L), f32),
            pltpu.VMEM((t + SUBLANES, dm), f32),
        ],
        compiler_params=_cparams(("parallel", "parallel")),
        name="mlstm",
    )(b_i, b_f, u3, v3, op3, gi4, gf4, cw, cb2, wq, wk, ng3, sk2)


def _ffn_kernel(x_ref, ya_ref, yb_ref, wo_ref, g2_ref, wg_ref, wu_ref, wd_ref, o_ref, act_scr):
    x1 = (x_ref[...]
          + jnp.dot(ya_ref[...].astype(bf16), wo_ref[0:NSA_WIDTH, :], preferred_element_type=f32)
          + jnp.dot(yb_ref[...].astype(bf16), wo_ref[NSA_WIDTH:NSA_WIDTH + MLSTM_WIDTH, :],
                    preferred_element_type=f32))
    h2 = x1 * lax.rsqrt(jnp.mean(x1 * x1, axis=-1, keepdims=True) + NORM_EPS) * g2_ref[...]
    h2b = h2.astype(bf16)
    for c in range(D_FF // FF_CHUNK):
        cols = slice(c * FF_CHUNK, (c + 1) * FF_CHUNK)
        gt = jnp.dot(h2b, wg_ref[:, cols], preferred_element_type=f32)
        up = jnp.dot(h2b, wu_ref[:, cols], preferred_element_type=f32)
        act_scr[:, cols] = (gt * jax.nn.sigmoid(gt) * up).astype(bf16)
    o_ref[...] = x1 + jnp.dot(act_scr[...], wd_ref[...], preferred_element_type=f32)


def _ffn(x2, ya, yb, wo, g2, wg, wu, wd):
    n = x2.shape[0]
    tm = TM_FFN

    def const(shape):
        return pl.BlockSpec(shape, lambda i: (0, 0), pipeline_mode=pl.Buffered(1))

    return pl.pallas_call(
        _ffn_kernel,
        grid=(n // tm,),
        in_specs=[
            pl.BlockSpec((tm, D_MODEL), lambda i: (i, 0)),
            pl.BlockSpec((tm, NSA_WIDTH), lambda i: (i, 0)),
            pl.BlockSpec((tm, MLSTM_WIDTH), lambda i: (i, 0)),
            const(wo.shape), const(g2.shape), const(wg.shape), const(wu.shape), const(wd.shape),
        ],
        out_specs=pl.BlockSpec((tm, D_MODEL), lambda i: (i, 0)),
        out_shape=jax.ShapeDtypeStruct((n, D_MODEL), f32),
        scratch_shapes=[pltpu.VMEM((tm, D_FF), bf16)],
        compiler_params=_cparams(("parallel",)),
        name="ffn",
    )(x2, ya, yb, wo, g2, wg, wu, wd)


def _overlap_t(ncp):
    nsel = ncp // (SEL_BLOCK // CMP_STRIDE)
    cs = np.arange(ncp) * CMP_STRIDE
    ss = np.arange(nsel) * SEL_BLOCK
    ov = ((cs[None, :] < ss[:, None] + SEL_BLOCK) & (cs[None, :] + CMP_BLOCK > ss[:, None])).astype(np.float32)
    ov[:, ncp - 1] = 0.0
    out = np.zeros((LANES, ncp), np.float32)
    out[:nsel] = ov
    return jnp.asarray(np.tile(out, (1, 3)), dtype=bf16)


def _layer(x, norm1_g, w_in, q_g, kc_g, ks_g, kw_g, cmp_pos, w_ck1, w_ck2, w_cv1, w_cv2, conv_w, conv_b,
           w_mq, w_mk, b_i, b_f, mlstm_norm_g, mlstm_skip, w_out, norm2_g, w_gate, w_up, w_down):
    b, t, d = x.shape
    n = b * t
    x2 = x.reshape(n, d)

    o_gate = NSA_WIDTH + 6 * KV_WIDTH
    o_u = o_gate + 3 * NSA_HEADS
    o_if = o_u + 3 * MLSTM_WIDTH
    w_perm = jnp.concatenate([
        w_in[:, :o_gate], w_in[:, o_u:o_if], w_in[:, o_gate:o_u], w_in[:, o_if:],
        jnp.zeros((d, LANES - 3 * NSA_HEADS - 2 * MLSTM_HEADS), w_in.dtype)], axis=1).astype(bf16)
    w_gates_t = jnp.concatenate([w_in[:, o_if:].T, jnp.zeros((GATE_ROWS - 2 * MLSTM_HEADS, d), w_in.dtype)],
                                axis=0).astype(bf16)
    kvc2, u2, vm2, op2, sm2, gates_t, ks, vs, kw, vw, q3 = _inproj(
        x2, norm1_g.reshape(1, d), w_perm, w_gates_t, jnp.tile(ks_g, 2).reshape(1, LANES),
        jnp.tile(kw_g, 2).reshape(1, LANES), jnp.tile(q_g, NSA_HEADS).reshape(1, NSA_WIDTH), b, t)

    kv3 = kvc2.reshape(b, t, 2 * KV_WIDTH)
    nseg = t // CMP_STRIDE
    pos8 = jnp.broadcast_to(cmp_pos.reshape(1, CMP_BLOCK * NSA_HEAD_DIM), (SUBLANES, CMP_BLOCK * NSA_HEAD_DIM))
    wk1s, wk2s = _compress_weights(w_ck1, w_ck2)
    wv1s, wv2s = _compress_weights(w_cv1, w_cv2)
    kcd, vcl = _compress(kv3, w_ck1, wk1s, wk2s, w_cv1, wv1s, wv2s, pos8, jnp.tile(kc_g, 2).reshape(1, LANES))
    slopes = jnp.exp2(-8.0 * (jnp.arange(NSA_HEADS, dtype=f32) + 1.0) / NSA_HEADS)
    k_gain = jnp.maximum(jnp.max(jnp.abs(kc_g)), jnp.maximum(jnp.max(jnp.abs(ks_g)), jnp.max(jnp.abs(kw_g))))
    score_cap = (NSA_HEAD_DIM ** 0.5 * math.log2(math.e)) * jnp.max(jnp.abs(q_g)) * k_gain
    nsa_args = (slopes, q3, sm2.reshape(b, t, LANES), kcd, vcl, _overlap_t(nseg), ks, vs, kw, vw)
    y_nsa = lax.cond(score_cap <= SCORE_BOUND, functools.partial(_nsa, bounded=True),
                     functools.partial(_nsa, bounded=False), *nsa_args)

    gi4 = gates_t[0:MLSTM_HEADS].reshape(MLSTM_HEADS, b, t).transpose(1, 0, 2)
    gf4 = gates_t[MLSTM_HEADS:2 * MLSTM_HEADS].reshape(MLSTM_HEADS, b, t).transpose(1, 0, 2)
    y_mem = _mlstm(b_i, b_f, u2.reshape(b, t, MLSTM_WIDTH), vm2.reshape(b, t, MLSTM_WIDTH),
                   op2.reshape(b, t, MLSTM_WIDTH), gi4.reshape(b, MLSTM_HEADS, t // MLSTM_L, MLSTM_L),
                   gf4.reshape(b, MLSTM_HEADS, t // MLSTM_L, MLSTM_L), conv_w, conv_b.reshape(1, MLSTM_WIDTH),
                   w_mq, w_mk,
                   mlstm_norm_g.reshape(MLSTM_HEADS, 1, MLSTM_HEAD_DIM), mlstm_skip.reshape(1, MLSTM_WIDTH))

    out = _ffn(x2, y_nsa.reshape(n, NSA_WIDTH), y_mem.reshape(n, MLSTM_WIDTH), w_out.astype(bf16),
               norm2_g.reshape(1, d), w_gate.astype(bf16), w_up.astype(bf16), w_down.astype(bf16))
    return out.reshape(b, t, d)


def kernel(x, norm1_g, w_in, q_norm_g, kc_norm_g, ks_norm_g, kw_norm_g, cmp_pos, w_ck1, w_ck2, w_cv1, w_cv2,
           conv_w, conv_b, w_mq, w_mk, b_i, b_f, mlstm_norm_g, mlstm_skip, w_out, norm2_g, w_gate, w_up, w_down):
    depth = norm1_g.shape[0]
    for l in range(depth):
        x = _layer(x, norm1_g[l], w_in[l], q_norm_g[l], kc_norm_g[l], ks_norm_g[l], kw_norm_g[l], cmp_pos[l],
                   w_ck1[l], w_ck2[l], w_cv1[l], w_cv2[l], conv_w[l], conv_b[l], w_mq[l], w_mk[l], b_i[l], b_f[l],
                   mlstm_norm_g[l], mlstm_skip[l], w_out[l], norm2_g[l], w_gate[l], w_up[l], w_down[l])
    return x
```

```python
import functools
import math

import numpy as np
import jax
import jax.numpy as jnp
from jax import lax
from jax.experimental import pallas as pl
from jax.experimental.pallas import tpu as pltpu

f32 = jnp.float32
bf16 = jnp.bfloat16

D_MODEL = 1024
NSA_HEADS = 8
NSA_KV_HEADS = 2
NSA_HEAD_DIM = 64
NSA_GROUP = NSA_HEADS // NSA_KV_HEADS
CMP_BLOCK = 32
CMP_STRIDE = 16
CMP_HIDDEN = 256
SEL_BLOCK = 64
SEL_TOPN = 16
WINDOW = 512
MLSTM_HEADS = 4
MLSTM_HEAD_DIM = 128
CONV_WIDTH = 4
NSA_WIDTH = NSA_HEADS * NSA_HEAD_DIM
MLSTM_WIDTH = MLSTM_HEADS * MLSTM_HEAD_DIM
KV_WIDTH = NSA_KV_HEADS * NSA_HEAD_DIM
D_FF = -(-8 * D_MODEL // (3 * 256)) * 256
NORM_EPS = 1e-6
NEG = -1e30
FORCE = 1e9
SCORE_BOUND = 100.0

LANES = 128
SUBLANES = 8
HALF = 64
SEL_SHIFT = SEL_BLOCK.bit_length() - 1
assert 1 << SEL_SHIFT == SEL_BLOCK and SEL_BLOCK == HALF
VMEM_LIMIT = 56 * 1024 * 1024

TM_PROJ = 1024
TQ = 512
TK = 512
MLSTM_L = 256
MLSTM_UNROLL = 8
TM_FFN = 1024
FF_CHUNK = 256

HI = lax.Precision.HIGHEST
NT = (((1,), (1,)), ((), ()))


def _cparams(sem):
    return pltpu.CompilerParams(dimension_semantics=sem, vmem_limit_bytes=VMEM_LIMIT)


def _split3(x):
    x1 = x.astype(bf16)
    r1 = x - x1.astype(f32)
    x2 = r1.astype(bf16)
    x3 = (r1 - x2.astype(f32)).astype(bf16)
    return x1, x2, x3


COL_Q = 0
COL_KV = COL_Q + NSA_WIDTH
COL_U = COL_KV + 6 * KV_WIDTH
COL_VM = COL_U + MLSTM_WIDTH
COL_OP = COL_VM + MLSTM_WIDTH
COL_SM = COL_OP + MLSTM_WIDTH
PROJ_WIDTH = COL_SM + LANES
GATE_ROWS = 16


def _pair_norm(x, gain, lo):
    sq = x * x
    s_lo = jnp.sum(jnp.where(lo, sq, 0.0), axis=-1, keepdims=True)
    s_hi = jnp.sum(jnp.where(lo, 0.0, sq), axis=-1, keepdims=True)
    inv = jnp.where(lo, lax.rsqrt(s_lo / HALF + NORM_EPS), lax.rsqrt(s_hi / HALF + NORM_EPS))
    return x * inv * gain


def _inproj_kernel(x_ref, g_ref, w_ref, wgt_ref, ksg_ref, kwg_ref, qg_ref,
                   kvc_ref, u_ref, vm_ref, op_ref, sm_ref, gt_ref, ks_ref, vs_ref, kw_ref, vw_ref, q3_ref,
                   *, tiles_per_seq):
    tm = x_ref.shape[0]
    x = x_ref[...]
    h = x * lax.rsqrt(jnp.mean(x * x, axis=-1, keepdims=True) + NORM_EPS) * g_ref[...]
    hb = h.astype(bf16)

    def proj(col, width):
        return jnp.dot(hb, w_ref[:, col:col + width], preferred_element_type=f32)

    lane = lax.broadcasted_iota(jnp.int32, (tm, LANES), 1)
    lo = lane < HALF
    pos = (pl.program_id(0) % tiles_per_seq) * tm + lax.broadcasted_iota(jnp.int32, (tm, LANES), 0)

    q = proj(COL_Q, NSA_WIDTH)
    scale = NSA_HEAD_DIM ** -0.5 * math.log2(math.e)
    for c in range(NSA_HEADS // 2):
        pair = _pair_norm(q[:, c * LANES:(c + 1) * LANES], qg_ref[:, c * LANES:(c + 1) * LANES], lo) * scale
        for par, xq in enumerate((pair, pltpu.roll(pair, HALF, 1))):
            hi = jnp.where(lo, xq, 0.0).astype(bf16).astype(f32)
            res = jnp.where(lo, xq - hi, 0.0)
            head = 2 * c + par
            q3_ref[0, head // NSA_GROUP, head % NSA_GROUP] = jnp.concatenate(
                [hi + pltpu.roll(res, HALF, 1), hi], axis=1).astype(bf16)

    kv = proj(COL_KV, 6 * KV_WIDTH)
    kvc_ref[...] = kv[:, 0:2 * KV_WIDTH]
    code = jnp.where(lax.shift_right_logical(pos, SEL_SHIFT) == (lane & (HALF - 1)), 1.0, 0.0)

    def put(xk, o_ref, fill):
        o_ref[0, 0] = jnp.where(lo, xk, fill).astype(bf16)
        o_ref[0, 1] = jnp.where(lo, pltpu.roll(xk, HALF, 1), fill).astype(bf16)

    put(_pair_norm(kv[:, 2 * KV_WIDTH:3 * KV_WIDTH], ksg_ref[...], lo), ks_ref, code)
    put(kv[:, 3 * KV_WIDTH:4 * KV_WIDTH], vs_ref, 1.0)
    put(_pair_norm(kv[:, 4 * KV_WIDTH:5 * KV_WIDTH], kwg_ref[...], lo), kw_ref, 0.0)
    put(kv[:, 5 * KV_WIDTH:6 * KV_WIDTH], vw_ref, 1.0)

    u_ref[...] = proj(COL_U, MLSTM_WIDTH)
    vm_ref[...] = proj(COL_VM, MLSTM_WIDTH)
    op_ref[...] = proj(COL_OP, MLSTM_WIDTH)
    sm_ref[...] = proj(COL_SM, LANES)
    gt_ref[...] = lax.dot_general(wgt_ref[...], hb, NT, preferred_element_type=f32)


def _inproj(x2, g1, w_perm, w_gates_t, ksg2, kwg2, qg2, b, t):
    n = x2.shape[0]
    tm = min(TM_PROJ, t)
    nt = t // tm

    def rows(width):
        return pl.BlockSpec((tm, width), lambda i: (i, 0))

    def const(shape):
        return pl.BlockSpec(shape, lambda i: (0, 0))

    kv_spec = pl.BlockSpec((1, NSA_KV_HEADS, tm, LANES), lambda i: (i // nt, 0, i % nt, 0))
    kv_shape = jax.ShapeDtypeStruct((b, NSA_KV_HEADS, t, LANES), bf16)
    return pl.pallas_call(
        functools.partial(_inproj_kernel, tiles_per_seq=nt),
        grid=(n // tm,),
        in_specs=[rows(D_MODEL), const((1, D_MODEL)), const((D_MODEL, PROJ_WIDTH)), const((GATE_ROWS, D_MODEL)),
                  const((1, LANES)), const((1, LANES)), const((1, NSA_WIDTH))],
        out_specs=[rows(2 * KV_WIDTH), rows(MLSTM_WIDTH), rows(MLSTM_WIDTH), rows(MLSTM_WIDTH), rows(LANES),
                   pl.BlockSpec((GATE_ROWS, tm), lambda i: (0, i)),
                   kv_spec, kv_spec, kv_spec, kv_spec,
                   pl.BlockSpec((1, NSA_KV_HEADS, NSA_GROUP, tm, 2 * LANES),
                                lambda i: (i // nt, 0, 0, i % nt, 0))],
        out_shape=[jax.ShapeDtypeStruct((n, 2 * KV_WIDTH), f32), jax.ShapeDtypeStruct((n, MLSTM_WIDTH), f32),
                   jax.ShapeDtypeStruct((n, MLSTM_WIDTH), f32), jax.ShapeDtypeStruct((n, MLSTM_WIDTH), f32),
                   jax.ShapeDtypeStruct((n, LANES), f32), jax.ShapeDtypeStruct((GATE_ROWS, n), f32),
                   kv_shape, kv_shape, kv_shape, kv_shape,
                   jax.ShapeDtypeStruct((b, NSA_KV_HEADS, NSA_GROUP, t, 2 * LANES), bf16)],
        compiler_params=_cparams(("parallel",)),
        name="inproj",
    )(x2, g1, w_perm, w_gates_t, ksg2, kwg2, qg2)


def _gelu_tanh(x):
    return 0.5 * x * (1.0 + jnp.tanh(math.sqrt(2.0 / math.pi) * (x + 0.044715 * (x * x * x))))


def _hi_lo(x):
    hi = x.astype(bf16)
    return hi, (x - hi.astype(f32)).astype(bf16)


def _compress_kernel(kc_ref, vc_ref, wk1_ref, wk1s_ref, wk2s_ref, wv1_ref, wv1s_ref, wv2s_ref, pos_ref, kcg_ref,
                     kcd_ref, vcl_ref):
    nseg = kc_ref.shape[1] // CMP_STRIDE
    half = NSA_KV_HEADS * CMP_HIDDEN

    def branch(x_ref, w1_ref, w1s_ref, w2s_ref):
        terms = []
        for r in range(CMP_STRIDE):
            hi, lo = _hi_lo(x_ref[0, pl.ds(r, nseg, stride=CMP_STRIDE), :])
            terms += [hi, lo, hi]
        acc = jnp.dot(jnp.concatenate(terms, axis=1), w1s_ref[...], preferred_element_type=f32)
        pos_term = jnp.dot(pos_ref[...], w1_ref[...], precision=HI, preferred_element_type=f32)[0:1]
        hid = (acc[:, 0:half] + pltpu.roll(acc[:, half:2 * half], nseg - 1, 0)
               + jnp.concatenate([pos_term] * NSA_KV_HEADS, axis=1))
        hi, lo = _hi_lo(_gelu_tanh(hid))
        return jnp.dot(jnp.concatenate([hi, lo, hi], axis=1), w2s_ref[...], preferred_element_type=f32)

    kc2 = branch(kc_ref, wk1_ref, wk1s_ref, wk2s_ref)
    vc2 = branch(vc_ref, wv1_ref, wv1s_ref, wv2s_ref)
    lane = lax.broadcasted_iota(jnp.int32, (nseg, LANES), 1)
    for g in range(NSA_KV_HEADS):
        kc = kc2[:, g * LANES:(g + 1) * LANES]
        kc = kc * lax.rsqrt(jnp.mean(kc * kc, axis=-1, keepdims=True) + NORM_EPS) * kcg_ref[...]
        k_hi = kc.astype(bf16)
        k_lo = jnp.where(lane < HALF, kc - k_hi.astype(f32), 0.0).astype(bf16)
        kcd_ref[0, g] = jnp.concatenate([k_hi, k_lo], axis=1)
        vcl_ref[0, g] = jnp.where(lane < HALF, vc2[:, g * LANES:(g + 1) * LANES], 0.0).astype(bf16)


def _compress_weights(w1, w2):
    w1r = w1.reshape(2, CMP_STRIDE, NSA_HEAD_DIM, CMP_HIDDEN)
    z = jnp.zeros_like(w1r[0])

    def two_groups(w):
        return jnp.concatenate([jnp.concatenate([w, z], axis=2), jnp.concatenate([z, w], axis=2)], axis=1)

    wr = jnp.concatenate([two_groups(w1r[0]), two_groups(w1r[1])], axis=2)
    hi, lo = _hi_lo(wr)
    w1s = jnp.concatenate([hi, hi, lo], axis=1).reshape(CMP_STRIDE * 3 * LANES, 2 * NSA_KV_HEADS * CMP_HIDDEN)
    z2 = jnp.zeros_like(w2)
    w2d = jnp.concatenate([jnp.concatenate([w2, w2, z2, z2], axis=1),
                           jnp.concatenate([z2, z2, w2, w2], axis=1)], axis=0)
    hi2, lo2 = _hi_lo(w2d)
    return w1s, jnp.concatenate([hi2, hi2, lo2], axis=0)


def _compress(kv3, wk1, wk1s, wk2s, wv1, wv1s, wv2s, pos8, kcg2):
    b, t, _ = kv3.shape
    nseg = t // CMP_STRIDE

    def const(shape):
        return pl.BlockSpec(shape, lambda bi: (0,) * len(shape), pipeline_mode=pl.Buffered(1))

    return pl.pallas_call(
        _compress_kernel,
        grid=(b,),
        in_specs=[pl.BlockSpec((1, t, KV_WIDTH), lambda bi: (bi, 0, 0)),
                  pl.BlockSpec((1, t, KV_WIDTH), lambda bi: (bi, 0, 1)),
                  const(wk1.shape), const(wk1s.shape), const(wk2s.shape),
                  const(wv1.shape), const(wv1s.shape), const(wv2s.shape),
                  const(pos8.shape), const(kcg2.shape)],
        out_specs=[pl.BlockSpec((1, NSA_KV_HEADS, nseg, 2 * LANES), lambda bi: (bi, 0, 0, 0)),
                   pl.BlockSpec((1, NSA_KV_HEADS, nseg, LANES), lambda bi: (bi, 0, 0, 0))],
        out_shape=[jax.ShapeDtypeStruct((b, NSA_KV_HEADS, nseg, 2 * LANES), bf16),
                   jax.ShapeDtypeStruct((b, NSA_KV_HEADS, nseg, LANES), bf16)],
        compiler_params=_cparams(("parallel",)),
        name="compress",
    )(kv3, kv3, wk1, wk1s, wk2s, wv1, wv1s, wv2s, pos8, kcg2)


def _nsa_kernel(slopes_ref, q3_ref, sm_ref, gsel_ref, wdist_ref, kcd_ref, vcl_ref, ovt_ref,
                ks_ref, vs_ref, kw_ref, vw_ref, y_ref, qa_scr, m_scr, acc_scr, *, bounded):
    g = pl.program_id(1)
    qi = pl.program_id(2)
    q0 = qi * TQ
    ncp = kcd_ref.shape[2]
    nsel = ncp // (SEL_BLOCK // CMP_STRIDE)
    rows = NSA_GROUP * TQ
    log2e = math.log2(math.e)

    lane = lax.broadcasted_iota(jnp.int32, (TQ, LANES), 1)
    lo = lane < HALF
    slopes = [slopes_ref[g * NSA_GROUP + r] * log2e for r in range(NSA_GROUP)]

    def per_head(x, fn):
        return jnp.concatenate([fn(r, x[r * TQ:(r + 1) * TQ]) for r in range(NSA_GROUP)], axis=0)

    q3 = q3_ref[0, 0].reshape(rows, 2 * LANES)
    q1 = q3[:, 0:LANES]

    cidx = lax.broadcasted_iota(jnp.int32, (TQ, ncp), 1)
    tpos_c = q0 + lax.broadcasted_iota(jnp.int32, (TQ, ncp), 0)
    blk_end = cidx * CMP_STRIDE + (CMP_BLOCK - 1)
    valid_c = jnp.logical_and(tpos_c >= blk_end, cidx < ncp - 1)
    krel_c = (CMP_BLOCK - 1 - q0 + CMP_STRIDE * lax.broadcasted_iota(jnp.int32, (1, ncp), 1)).astype(f32)
    s_c = lax.dot_general(q3, kcd_ref[0, 0], NT, preferred_element_type=f32)
    if bounded:
        rowf = lax.broadcasted_iota(jnp.int32, (TQ, LANES), 0).astype(f32)

        def row_term(r, width):
            return jnp.concatenate([slopes[r] * rowf] * (width // LANES), axis=1)

        s_c = per_head(s_c, lambda r, x: jnp.where(valid_c, x + slopes[r] * krel_c - row_term(r, ncp), NEG))
        e_c = jnp.exp2(s_c)
    else:
        s_c = per_head(s_c, lambda r, x: jnp.where(valid_c, x + slopes[r] * krel_c, NEG))
        e_c = jnp.exp2(s_c - jnp.max(s_c, axis=-1, keepdims=True))
    p_c = e_c / jnp.sum(e_c, axis=-1, keepdims=True)
    p_c = per_head(p_c, lambda r, x: jnp.where(valid_c, x, 0.0))
    p_sum = p_c[0:TQ]
    for r in range(1, NSA_GROUP):
        p_sum = p_sum + p_c[r * TQ:(r + 1) * TQ]
    o_cmp = jnp.dot(p_c.astype(bf16), vcl_ref[0, 0], preferred_element_type=f32)

    band = WINDOW + TQ
    kb = pl.multiple_of(jnp.maximum(q0 - WINDOW, 0), TQ)
    s_w = lax.dot_general(q1, kw_ref[0, 0, pl.ds(kb, band), :], NT, preferred_element_type=f32)
    dist_w = wdist_ref[jnp.minimum(qi, WINDOW // TQ)]
    s_w = per_head(s_w, lambda r, x: x + slopes[r] * dist_w)
    if bounded:
        p_w = jnp.exp2(s_w)
    else:
        p_w = jnp.exp2(s_w - jnp.max(s_w, axis=-1, keepdims=True))
    acc_w = jnp.dot(p_w.astype(bf16), vw_ref[0, 0, pl.ds(kb, band), :], preferred_element_type=f32)
    o_win = acc_w / pltpu.roll(acc_w, HALF, 1)

    imp = lax.dot_general(ovt_ref[...], jnp.concatenate(_split3(p_sum), axis=1), NT,
                          preferred_element_type=f32)[0:nsel]
    jrow = lax.broadcasted_iota(jnp.int32, (nsel, TQ), 0)
    tcol = q0 + lax.broadcasted_iota(jnp.int32, (nsel, TQ), 1)
    forced = jnp.logical_or(jrow == lax.shift_right_logical(tcol, SEL_SHIFT), jrow == 0)
    future = jrow * SEL_BLOCK > tcol
    taken = -3e38
    work = jnp.where(forced, taken, jnp.where(future, -FORCE, imp))
    jrow_f = jrow.astype(f32)
    bias_t = jnp.where(forced, 0.0, NEG)
    for _ in range(min(SEL_TOPN, nsel) - 2):
        best = jnp.max(work, axis=0, keepdims=True)
        first = jnp.min(jnp.where(work == best, jrow_f, float(nsel)), axis=0, keepdims=True)
        hit = jrow_f == first
        bias_t = jnp.where(hit, 0.0, bias_t)
        work = jnp.where(hit, taken, work)
    if nsel < HALF:
        bias_t = jnp.concatenate([bias_t, jnp.full((HALF - nsel, TQ), NEG, f32)], axis=0)
    bias = jnp.concatenate([bias_t, bias_t], axis=0).T

    bias_b = bias.astype(bf16)
    qa_scr[...] = per_head(q1, lambda r, x: jnp.where(lo, x, bias_b))

    m_scr[...] = jnp.full(m_scr.shape, NEG, f32)
    acc_scr[...] = jnp.zeros(acc_scr.shape, f32)

    def scores(kt, causal, row_shift):
        k0 = pl.multiple_of(kt * TK, TK)
        s = lax.dot_general(qa_scr[...], ks_ref[0, 0, pl.ds(k0, TK), :], NT, preferred_element_type=f32)
        krel = (k0 - q0 + lax.broadcasted_iota(jnp.int32, (1, TK), 1)).astype(f32)

        def alibi(r, x):
            x = x + slopes[r] * krel
            return x if row_shift is None else x - row_shift[r]

        if causal:
            ahead = (k0 - q0 + lax.broadcasted_iota(jnp.int32, (TQ, TK), 1)
                     > lax.broadcasted_iota(jnp.int32, (TQ, TK), 0))
            return per_head(s, lambda r, x: jnp.where(ahead, NEG, alibi(r, x))), k0
        return per_head(s, alibi), k0

    def online_tile(state, kt, causal):
        m_old, acc = state
        s, k0 = scores(kt, causal, None)
        m_new = jnp.maximum(m_old, jnp.max(s, axis=-1, keepdims=True))
        p = jnp.exp2(s - m_new[:, 0:1])
        acc = jnp.exp2(m_old - m_new) * acc + jnp.dot(p.astype(bf16), vs_ref[0, 0, pl.ds(k0, TK), :],
                                                      preferred_element_type=f32)
        return m_new, acc

    def online_tiles(tiles):
        state = (m_scr[...], acc_scr[...])
        for kt, causal in tiles:
            state = online_tile(state, kt, causal)
        m_scr[...] = state[0]
        acc_scr[...] = state[1]

    def bounded_tiles(tiles):
        row_shift = [row_term(r, TK) for r in range(NSA_GROUP)]
        acc = acc_scr[...]
        for kt, causal in tiles:
            s, k0 = scores(kt, causal, row_shift)
            acc = acc + jnp.dot(jnp.exp2(s).astype(bf16), vs_ref[0, 0, pl.ds(k0, TK), :],
                                preferred_element_type=f32)
        acc_scr[...] = acc

    n_full = q0 // TK

    def selected_branch(run_tiles):
        def body(i, carry):
            run_tiles([(2 * i, False), (2 * i + 1, False)])
            return carry

        lax.fori_loop(0, n_full // 2, body, 0)

        @pl.when(n_full % 2 == 1)
        def _():
            run_tiles([(n_full - 1, False), (n_full, True)])

        @pl.when(n_full % 2 == 0)
        def _():
            run_tiles([(n_full, True)])

    selected_branch(bounded_tiles if bounded else online_tiles)

    acc_s = acc_scr[...]
    o_slc = acc_s / pltpu.roll(acc_s, HALF, 1)

    gates = jnp.dot(jnp.concatenate(_hi_lo(jax.nn.sigmoid(sm_ref[0])), axis=1), gsel_ref[0],
                    preferred_element_type=f32)

    def gate(r, branch):
        col = 3 * r + branch
        return gates[:, col * LANES:(col + 1) * LANES]

    def mix(r):
        sl = slice(r * TQ, (r + 1) * TQ)
        return gate(r, 0) * o_cmp[sl] + gate(r, 1) * o_slc[sl] + gate(r, 2) * o_win[sl]

    for c in range(NSA_GROUP // 2):
        y_ref[0, :, c * LANES:(c + 1) * LANES] = jnp.where(
            lo, mix(2 * c), pltpu.roll(mix(2 * c + 1), HALF, 1)).astype(y_ref.dtype)


def _gate_select():
    sel = np.zeros((NSA_KV_HEADS, 2 * LANES, 3 * NSA_GROUP * LANES), np.float32)
    for g in range(NSA_KV_HEADS):
        for c in range(3 * NSA_GROUP):
            src = 3 * NSA_GROUP * g + c
            sel[g, [src, LANES + src], c * LANES:(c + 1) * LANES] = 1.0
    return jnp.asarray(sel, dtype=bf16)


def _window_distance():
    i = np.arange(TQ)[None, :, None]
    j = np.arange(WINDOW + TQ)[None, None, :]
    off = -TQ * np.arange(WINDOW // TQ + 1)[:, None, None]
    rel = off + j - i
    return jnp.asarray(np.where((rel <= 0) & (rel > -WINDOW), rel, NEG), dtype=f32)


def _nsa(slopes, q3, sm3, kcd, vcl, ovt, ks, vs, kw, vw, *, bounded):
    b, t = q3.shape[0], q3.shape[3]
    gsel = _gate_select()
    wdist = _window_distance()
    gw = NSA_GROUP * NSA_HEAD_DIM
    ncp = kcd.shape[2]

    def kv_spec(rows):
        return pl.BlockSpec((1, 1, rows, LANES), lambda bi, gi, qi: (bi, gi, 0, 0))

    return pl.pallas_call(
        functools.partial(_nsa_kernel, bounded=bounded),
        grid=(b, NSA_KV_HEADS, t // TQ),
        in_specs=[
            pl.BlockSpec(memory_space=pltpu.SMEM),
            pl.BlockSpec((1, 1, NSA_GROUP, TQ, 2 * LANES), lambda bi, gi, qi: (bi, gi, 0, qi, 0)),
            pl.BlockSpec((1, TQ, LANES), lambda bi, gi, qi: (bi, qi, 0)),
            pl.BlockSpec((1,) + gsel.shape[1:], lambda bi, gi, qi: (gi, 0, 0)),
            pl.BlockSpec(wdist.shape, lambda bi, gi, qi: (0, 0, 0)),
            pl.BlockSpec((1, 1, ncp, 2 * LANES), lambda bi, gi, qi: (bi, gi, 0, 0)), kv_spec(ncp),
            pl.BlockSpec(ovt.shape, lambda bi, gi, qi: (0, 0)),
            kv_spec(t), kv_spec(t), kv_spec(t), kv_spec(t),
        ],
        out_specs=pl.BlockSpec((1, TQ, gw), lambda bi, gi, qi: (bi, qi, gi)),
        out_shape=jax.ShapeDtypeStruct((b, t, NSA_WIDTH), bf16),
        scratch_shapes=[
            pltpu.VMEM((NSA_GROUP * TQ, LANES), bf16),
            pltpu.VMEM((NSA_GROUP * TQ, LANES), f32),
            pltpu.VMEM((NSA_GROUP * TQ, LANES), f32),
        ],
        compiler_params=_cparams(("parallel", "parallel", "arbitrary")),
        name="nsa",
    )(slopes, q3, sm3, gsel, wdist, kcd, vcl, ovt, ks, vs, kw, vw)


def _log_sigmoid(x):
    return jnp.minimum(x, 0.0) - jnp.log1p(jnp.exp(-jnp.abs(x)))


def _mlstm_kernel(bi_ref, bf_ref, u_ref, v_ref, op_ref, gi_ref, gf_ref, cw_ref, cb_ref, wq_ref, wk_ref,
                  ng_ref, sk_ref, y_ref, uc_scr, q_scr, kt_scr, ct_scr, m_scr, b_scr, li_scr, xp_scr):
    h = pl.program_id(1)
    t = u_ref.shape[1]
    L = MLSTM_L
    dm = MLSTM_HEAD_DIM

    x = u_ref[0]
    xp_scr[0:SUBLANES, :] = jnp.zeros((SUBLANES, dm), f32)
    xp_scr[SUBLANES:, :] = x
    acc = x * cw_ref[CONV_WIDTH - 1:CONV_WIDTH, :]
    for s in range(1, CONV_WIDTH):
        acc = acc + xp_scr[SUBLANES - s:SUBLANES - s + t, :] * cw_ref[CONV_WIDTH - 1 - s:CONV_WIDTH - s, :]
    uc = acc + cb_ref[...]
    uc = uc * jax.nn.sigmoid(uc)
    uc_scr[...] = uc
    ucb = uc.astype(bf16)
    q_scr[...] = jnp.dot(ucb, wq_ref[0].astype(bf16), preferred_element_type=f32).astype(bf16)
    k = jnp.dot(ucb, wk_ref[0].astype(bf16), preferred_element_type=f32) * (dm ** -0.5)
    kt_scr[...] = k.T

    ct_scr[...] = jnp.zeros(ct_scr.shape, f32)
    m_scr[...] = jnp.zeros(m_scr.shape, f32)

    li_ = lax.broadcasted_iota(jnp.int32, (L, L), 0)
    si_ = lax.broadcasted_iota(jnp.int32, (L, L), 1)
    causal = si_ <= li_
    diag = si_ == li_
    ones_v = jnp.ones((L, dm), f32)
    ones_sq = jnp.ones((dm, dm), bf16)

    def wide(x):
        return jnp.concatenate([x] * (L // dm), axis=1)

    log_f = _log_sigmoid(gf_ref[0, 0] + bf_ref[h])
    upper = jnp.where(li_ <= si_, 1.0, 0.0).astype(bf16)
    b_scr[...] = jnp.dot(jnp.concatenate(_split3(log_f), axis=1), jnp.concatenate([upper] * 3, axis=0),
                         preferred_element_type=f32)
    li_scr[...] = gi_ref[0, 0] + bi_ref[h]

    def chunk(c, ct, m_prev):
        r0 = pl.multiple_of(c * L, L)
        qc = q_scr[pl.ds(r0, L), :]
        ktc = kt_scr[:, pl.ds(r0, L)]
        vaug = jnp.concatenate([v_ref[0, pl.ds(r0, L), :], ones_v], axis=1).astype(bf16)
        log_i = li_scr[pl.ds(c, 1), :]
        b_row = b_scr[pl.ds(c, 1), :]
        b_col = jnp.broadcast_to(jnp.sum(jnp.where(diag, b_row, 0.0), axis=-1, keepdims=True), (L, dm))
        g_sum = b_row[:, L - 1:L]
        dmat = jnp.where(causal, (wide(b_col) - b_row) + log_i, NEG)
        m_loc = jnp.broadcast_to(jnp.max(dmat, axis=-1, keepdims=True), (L, dm))
        p = jnp.dot(qc, ktc.astype(bf16), preferred_element_type=f32) * jnp.exp(dmat - wide(m_loc))
        intra = jnp.dot(p.astype(bf16), vaug, preferred_element_type=f32)
        m_inter = b_col + m_prev
        m_out = jnp.maximum(m_inter, m_loc)
        xo = (wide(jnp.exp(m_inter - m_out)) * jnp.dot(qc, ct.astype(bf16), preferred_element_type=f32)
              + wide(jnp.exp(m_loc - m_out)) * intra)
        num = xo[:, 0:dm]
        den = xo[:, dm:2 * dm]
        hh = num / jnp.maximum(jnp.abs(den), jnp.exp(-m_out))
        hh = hh * jax.nn.sigmoid(op_ref[0, pl.ds(r0, L), :])
        ssq = jnp.dot((hh * hh).astype(bf16), ones_sq, preferred_element_type=f32)
        hh = hh * lax.rsqrt(ssq * (1.0 / dm) + NORM_EPS) * ng_ref[0]
        y_ref[0, pl.ds(r0, L), :] = (hh + sk_ref[...] * uc_scr[pl.ds(r0, L), :]).astype(y_ref.dtype)

        w_end = (g_sum - b_row) + log_i
        m_new = jnp.maximum(g_sum + m_prev, jnp.max(w_end, axis=-1, keepdims=True))
        decay = jnp.exp(g_sum + m_prev - m_new)
        w = jnp.exp(w_end - m_new)
        return decay * ct + jnp.dot((ktc * w).astype(bf16), vaug, preferred_element_type=f32), m_new

    def chunk_group(i, carry):
        ct, m_prev = ct_scr[...], m_scr[0:1, 0:1]
        for j in range(MLSTM_UNROLL):
            ct, m_prev = chunk(i * MLSTM_UNROLL + j, ct, m_prev)
        ct_scr[...] = ct
        m_scr[...] = jnp.broadcast_to(m_prev, m_scr.shape)
        return carry

    lax.fori_loop(0, t // (L * MLSTM_UNROLL), chunk_group, 0)


def _mlstm(b_i, b_f, u3, v3, op3, gi4, gf4, cw, cb2, wq, wk, ng3, sk2):
    b, t, _ = u3.shape
    dm = MLSTM_HEAD_DIM
    seq = pl.BlockSpec((1, t, dm), lambda bi, hi: (bi, 0, hi))
    gate = pl.BlockSpec((1, 1, t // MLSTM_L, MLSTM_L), lambda bi, hi: (bi, hi, 0, 0))
    smem = pl.BlockSpec(memory_space=pltpu.SMEM)
    return pl.pallas_call(
        _mlstm_kernel,
        grid=(b, MLSTM_HEADS),
        in_specs=[
            smem, smem, seq, seq, seq, gate, gate,
            pl.BlockSpec((CONV_WIDTH, dm), lambda bi, hi: (0, hi)),
            pl.BlockSpec((1, dm), lambda bi, hi: (0, hi)),
            pl.BlockSpec((1, dm, dm), lambda bi, hi: (hi, 0, 0)),
            pl.BlockSpec((1, dm, dm), lambda bi, hi: (hi, 0, 0)),
            pl.BlockSpec((1, 1, dm), lambda bi, hi: (hi, 0, 0)),
            pl.BlockSpec((1, dm), lambda bi, hi: (0, hi)),
        ],
        out_specs=seq,
        out_shape=jax.ShapeDtypeStruct((b, t, MLSTM_WIDTH), bf16),
        scratch_shapes=[
            pltpu.VMEM((t, dm), f32),
            pltpu.VMEM((t, dm), bf16),
            pltpu.VMEM((dm, t), f32),
            pltpu.VMEM((dm, 2 * dm), f32),
            pltpu.VMEM((SUBLANES, LANES), f32),
            pltpu.VMEM((t // MLSTM_L, MLSTM_L), f32),
            pltpu.VMEM((t // MLSTM_L, MLSTM_L), f32),
            pltpu.VMEM((t + SUBLANES, dm), f32),
        ],
        compiler_params=_cparams(("parallel", "parallel")),
        name="mlstm",
    )(b_i, b_f, u3, v3, op3, gi4, gf4, cw, cb2, wq, wk, ng3, sk2)


def _ffn_kernel(x_ref, ya_ref, yb_ref, wo_ref, g2_ref, wg_ref, wu_ref, wd_ref, o_ref, act_scr):
    x1 = (x_ref[...]
          + jnp.dot(ya_ref[...], wo_ref[0:NSA_WIDTH, :], preferred_element_type=f32)
          + jnp.dot(yb_ref[...], wo_ref[NSA_WIDTH:NSA_WIDTH + MLSTM_WIDTH, :],
                    preferred_element_type=f32))
    h2 = x1 * lax.rsqrt(jnp.mean(x1 * x1, axis=-1, keepdims=True) + NORM_EPS) * g2_ref[...]
    h2b = h2.astype(bf16)
    for c in range(D_FF // FF_CHUNK):
        cols = slice(c * FF_CHUNK, (c + 1) * FF_CHUNK)
        gt = jnp.dot(h2b, wg_ref[:, cols], preferred_element_type=f32)
        up = jnp.dot(h2b, wu_ref[:, cols], preferred_element_type=f32)
        act_scr[:, cols] = (gt * jax.nn.sigmoid(gt) * up).astype(bf16)
    o_ref[...] = x1 + jnp.dot(act_scr[...], wd_ref[...], preferred_element_type=f32)


def _ffn(x2, ya, yb, wo, g2, wg, wu, wd):
    n = x2.shape[0]
    tm = TM_FFN

    def const(shape):
        return pl.BlockSpec(shape, lambda i: (0, 0), pipeline_mode=pl.Buffered(1))

    return pl.pallas_call(
        _ffn_kernel,
        grid=(n // tm,),
        in_specs=[
            pl.BlockSpec((tm, D_MODEL), lambda i: (i, 0)),
            pl.BlockSpec((tm, NSA_WIDTH), lambda i: (i, 0)),
            pl.BlockSpec((tm, MLSTM_WIDTH), lambda i: (i, 0)),
            const(wo.shape), const(g2.shape), const(wg.shape), const(wu.shape), const(wd.shape),
        ],
        out_specs=pl.BlockSpec((tm, D_MODEL), lambda i: (i, 0)),
        out_shape=jax.ShapeDtypeStruct((n, D_MODEL), f32),
        scratch_shapes=[pltpu.VMEM((tm, D_FF), bf16)],
        compiler_params=_cparams(("parallel",)),
        name="ffn",
    )(x2, ya, yb, wo, g2, wg, wu, wd)


def _overlap_t(ncp):
    nsel = ncp // (SEL_BLOCK // CMP_STRIDE)
    cs = np.arange(ncp) * CMP_STRIDE
    ss = np.arange(nsel) * SEL_BLOCK
    ov = ((cs[None, :] < ss[:, None] + SEL_BLOCK) & (cs[None, :] + CMP_BLOCK > ss[:, None])).astype(np.float32)
    ov[:, ncp - 1] = 0.0
    out = np.zeros((LANES, ncp), np.float32)
    out[:nsel] = ov
    return jnp.asarray(np.tile(out, (1, 3)), dtype=bf16)


def _layer(x, norm1_g, w_in, q_g, kc_g, ks_g, kw_g, cmp_pos, w_ck1, w_ck2, w_cv1, w_cv2, conv_w, conv_b,
           w_mq, w_mk, b_i, b_f, mlstm_norm_g, mlstm_skip, w_out, norm2_g, w_gate, w_up, w_down):
    b, t, d = x.shape
    n = b * t
    x2 = x.reshape(n, d)

    o_gate = NSA_WIDTH + 6 * KV_WIDTH
    o_u = o_gate + 3 * NSA_HEADS
    o_if = o_u + 3 * MLSTM_WIDTH
    w_perm = jnp.concatenate([
        w_in[:, :o_gate], w_in[:, o_u:o_if], w_in[:, o_gate:o_u], w_in[:, o_if:],
        jnp.zeros((d, LANES - 3 * NSA_HEADS - 2 * MLSTM_HEADS), w_in.dtype)], axis=1).astype(bf16)
    w_gates_t = jnp.concatenate([w_in[:, o_if:].T, jnp.zeros((GATE_ROWS - 2 * MLSTM_HEADS, d), w_in.dtype)],
                                axis=0).astype(bf16)
    kvc2, u2, vm2, op2, sm2, gates_t, ks, vs, kw, vw, q3 = _inproj(
        x2, norm1_g.reshape(1, d), w_perm, w_gates_t, jnp.tile(ks_g, 2).reshape(1, LANES),
        jnp.tile(kw_g, 2).reshape(1, LANES), jnp.tile(q_g, NSA_HEADS).reshape(1, NSA_WIDTH), b, t)

    kv3 = kvc2.reshape(b, t, 2 * KV_WIDTH)
    nseg = t // CMP_STRIDE
    pos8 = jnp.broadcast_to(cmp_pos.reshape(1, CMP_BLOCK * NSA_HEAD_DIM), (SUBLANES, CMP_BLOCK * NSA_HEAD_DIM))
    wk1s, wk2s = _compress_weights(w_ck1, w_ck2)
    wv1s, wv2s = _compress_weights(w_cv1, w_cv2)
    kcd, vcl = _compress(kv3, w_ck1, wk1s, wk2s, w_cv1, wv1s, wv2s, pos8, jnp.tile(kc_g, 2).reshape(1, LANES))
    slopes = jnp.exp2(-8.0 * (jnp.arange(NSA_HEADS, dtype=f32) + 1.0) / NSA_HEADS)
    k_gain = jnp.maximum(jnp.max(jnp.abs(kc_g)), jnp.maximum(jnp.max(jnp.abs(ks_g)), jnp.max(jnp.abs(kw_g))))
    score_cap = (NSA_HEAD_DIM ** 0.5 * math.log2(math.e)) * jnp.max(jnp.abs(q_g)) * k_gain
    nsa_args = (slopes, q3, sm2.reshape(b, t, LANES), kcd, vcl, _overlap_t(nseg), ks, vs, kw, vw)
    y_nsa = lax.cond(score_cap <= SCORE_BOUND, functools.partial(_nsa, bounded=True),
                     functools.partial(_nsa, bounded=False), *nsa_args)

    gi4 = gates_t[0:MLSTM_HEADS].reshape(MLSTM_HEADS, b, t).transpose(1, 0, 2)
    gf4 = gates_t[MLSTM_HEADS:2 * MLSTM_HEADS].reshape(MLSTM_HEADS, b, t).transpose(1, 0, 2)
    y_mem = _mlstm(b_i, b_f, u2.reshape(b, t, MLSTM_WIDTH), vm2.reshape(b, t, MLSTM_WIDTH),
                   op2.reshape(b, t, MLSTM_WIDTH), gi4.reshape(b, MLSTM_HEADS, t // MLSTM_L, MLSTM_L),
                   gf4.reshape(b, MLSTM_HEADS, t // MLSTM_L, MLSTM_L), conv_w, conv_b.reshape(1, MLSTM_WIDTH),
                   w_mq, w_mk,
                   mlstm_norm_g.reshape(MLSTM_HEADS, 1, MLSTM_HEAD_DIM), mlstm_skip.reshape(1, MLSTM_WIDTH))

    out = _ffn(x2, y_nsa.reshape(n, NSA_WIDTH), y_mem.reshape(n, MLSTM_WIDTH), w_out.astype(bf16),
               norm2_g.reshape(1, d), w_gate.astype(bf16), w_up.astype(bf16), w_down.astype(bf16))
    return out.reshape(b, t, d)


def kernel(x, norm1_g, w_in, q_norm_g, kc_norm_g, ks_norm_g, kw_norm_g, cmp_pos, w_ck1, w_ck2, w_cv1, w_cv2,
           conv_w, conv_b, w_mq, w_mk, b_i, b_f, mlstm_norm_g, mlstm_skip, w_out, norm2_g, w_gate, w_up, w_down):
    depth = norm1_g.shape[0]
    for l in range(depth):
        x = _layer(x, norm1_g[l], w_in[l], q_norm_g[l], kc_norm_g[l], ks_norm_g[l], kw_norm_g[l], cmp_pos[l],
                   w_ck1[l], w_ck2[l], w_cv1[l], w_cv2[l], conv_w[l], conv_b[l], w_mq[l], w_mk[l], b_i[l], b_f[l],
                   mlstm_norm_g[l], mlstm_skip[l], w_out[l], norm2_g[l], w_gate[l], w_up[l], w_down[l])
    return x
```
